```python
import math
import jax, jax.numpy as jnp
from jax import lax
import numpy as np

D_MODEL = 1024
BATCH = 2
SEQ = 16384
DEPTH = 2
DEC_BATCH = 32
DEC_SEQ = 64
PAST_LEN = 4096

CHUNK = 64
EPS = 1e-6
N_ADA = 9
D_FF = 2816
A_WIDTH = D_MODEL
A_GROUPS = 8
A_GROUP_DIM = A_WIDTH // A_GROUPS
A_CHUNK = 128
SSM_INNER = 2 * D_MODEL
SSM_HEAD_DIM = 64
SSM_HEADS = SSM_INNER // SSM_HEAD_DIM
SSM_GROUPS = 4
SSM_HPG = SSM_HEADS // SSM_GROUPS
SSM_STATE = 128
CONV_W = 4
CONV_DIM = SSM_INNER + 2 * SSM_GROUPS * SSM_STATE
C_HEADS = 8
C_KV_HEADS = 4
C_GROUP = C_HEADS // C_KV_HEADS
C_HEAD_DIM = 128
C_WIDTH = C_HEADS * C_HEAD_DIM
IDX_HEADS = 8
IDX_DIM = 64
TOPK_MAX = 256
Q_BLOCK = 128
N_BUCKETS = 32
MAX_DISTANCE = 128
A_COLS = 2 * A_WIDTH
B_COLS = SSM_INNER + CONV_DIM + SSM_HEADS
C_Q = C_HEADS * C_HEAD_DIM
C_KV = C_KV_HEADS * C_HEAD_DIM
C_QI = IDX_HEADS * IDX_DIM
C_COLS = C_Q + 2 * C_KV + C_QI + IDX_DIM + IDX_HEADS
GATE_COLS = 3 * D_MODEL
IN_COLS = A_COLS + B_COLS + C_COLS + GATE_COLS

kernel_name = "hybrid_streaming_encoder_step"


def rms_norm(x, g):
    xf = x.astype(jnp.float32)
    y = xf * lax.rsqrt(jnp.mean(xf * xf, axis=-1, keepdims=True) + EPS)
    return (y * g).astype(x.dtype)


def modulate(x, g, shift, scale):
    return rms_norm(x, g) * (1 + scale[:, None, :]) + shift[:, None, :]


def swiglu(h, w_in, w_out):
    gate, up = jnp.split(h @ w_in, 2, axis=-1)
    return (jax.nn.silu(gate) * up) @ w_out


def t5_bucket(rel):
    nb = N_BUCKETS // 2
    max_exact = nb // 2
    n = jnp.abs(rel)
    nf = jnp.maximum(n, 1).astype(jnp.float32)
    large = max_exact + (jnp.log(nf / max_exact) / math.log(MAX_DISTANCE / max_exact)
                         * (nb - max_exact)).astype(jnp.int32)
    large = jnp.minimum(large, nb - 1)
    return jnp.where(rel > 0, nb, 0) + jnp.where(n < max_exact, n, large)


def chunk_spatial_gate(u, v, w_s, b_s):
    b, T, _ = v.shape
    n = min(T, A_CHUNK)
    nc = T // n
    mask = jnp.tril(jnp.ones((n, n), bool))
    ws = jnp.where(mask[None], w_s[:, :n, :n], 0)
    vc = v.reshape(b, nc, n, A_GROUPS, A_GROUP_DIM)
    mixed = jnp.einsum('gts,bcsgd->bctgd', ws, vc) + b_s[:, :n].T[None, None, :, :, None]
    return u * mixed.reshape(b, T, A_WIDTH)


def causal_conv(xbc, buf, w, bias):
    xp = jnp.concatenate([buf, xbc], axis=1)
    out = lax.conv_general_dilated(xp, w[:, None, :], window_strides=(1,), padding='VALID',
                                   dimension_numbers=('NWC', 'WIO', 'NWC'),
                                   feature_group_count=CONV_DIM)
    return out + bias, xp[:, -(CONV_W - 1):]


def ssd_chunked(x, dt, A, Bm, Cm, h0, q):
    b, T = x.shape[:2]
    nc = T // q

    def to_chunks(t):
        return jnp.moveaxis(t.reshape((b, nc, q) + t.shape[2:]), 1, 0)

    mask = jnp.tril(jnp.ones((q, q), bool))[None, :, :, None, None]

    def step(h, inp):
        xc, dtc, bc, cc = inp
        acs = jnp.cumsum(dtc * A, axis=1)
        seg = acs[:, :, None] - acs[:, None, :]
        lmat = jnp.exp(jnp.where(mask, seg, -jnp.inf))
        cb = jnp.einsum('btgn,bsgn->btsg', cc, bc)
        y = jnp.einsum('btsg,btsgr,bsgr,bsgrp->btgrp', cb, lmat, dtc, xc)
        y = y + jnp.einsum('btgn,bgrpn,btgr->btgrp', cc, h, jnp.exp(acs))
        decay = jnp.exp(acs[:, -1:] - acs) * dtc
        h = jnp.exp(acs[:, -1])[..., None, None] * h + jnp.einsum('bsgn,bsgr,bsgrp->bgrpn', bc, decay, xc)
        return h, y

    h, ys = lax.scan(step, h0, (to_chunks(x), to_chunks(dt), to_chunks(Bm), to_chunks(Cm)))
    return jnp.moveaxis(ys, 0, 1).reshape(x.shape), h


def mamba2_branch(z, xbc, dt_raw, conv_buf, h0, w_conv, b_conv, dt_bias, a_log, d_skip, g_norm, q_len):
    b, T, _ = xbc.shape
    xc, new_buf = causal_conv(xbc, conv_buf, w_conv, b_conv)
    xc = jax.nn.silu(xc)
    xs, bm, cm = jnp.split(xc, [SSM_INNER, SSM_INNER + SSM_GROUPS * SSM_STATE], axis=-1)
    xs = xs.reshape(b, T, SSM_GROUPS, SSM_HPG, SSM_HEAD_DIM).astype(jnp.float32)
    bm = bm.reshape(b, T, SSM_GROUPS, SSM_STATE).astype(jnp.float32)
    cm = cm.reshape(b, T, SSM_GROUPS, SSM_STATE).astype(jnp.float32)
    dt = jax.nn.softplus(dt_raw.astype(jnp.float32) + dt_bias.astype(jnp.float32))
    dt = dt.reshape(b, T, SSM_GROUPS, SSM_HPG)
    A = -jnp.exp(a_log.astype(jnp.float32)).reshape(SSM_GROUPS, SSM_HPG)
    h0 = h0.astype(jnp.float32).reshape(b, SSM_GROUPS, SSM_HPG, SSM_HEAD_DIM, SSM_STATE)
    y, h = ssd_chunked(xs, dt, A, bm, cm, h0, q_len)
    y = y + d_skip.astype(jnp.float32).reshape(SSM_GROUPS, SSM_HPG)[:, :, None] * xs
    y = y.reshape(b, T, SSM_INNER).astype(z.dtype)
    y = rms_norm(y * jax.nn.silu(z), g_norm)
    return y, new_buf, h.reshape(b, SSM_HEADS, SSM_HEAD_DIM, SSM_STATE)


def dsa_attend(q, qi, wi, qpos, k_all, v_all, ki_all, rel_table, topk):
    b, Tq = q.shape[:2]
    L = k_all.shape[1]
    kpos = jnp.arange(L, dtype=jnp.int32)
    admissible = (kpos[None, :] // CHUNK) <= (qpos[:, None] // CHUNK)
    sc = jnp.einsum('bthd,bsd->bths', qi.astype(jnp.float32), ki_all.astype(jnp.float32)) * IDX_DIM ** -0.5
    iscore = jnp.einsum('bths,bth->bts', jax.nn.relu(sc), wi.astype(jnp.float32) * IDX_HEADS ** -0.5)
    iscore = jnp.where(admissible[None], iscore, -jnp.inf)
    _, idx = lax.top_k(iscore, topk)
    keep = (idx // CHUNK) <= (qpos[None, :, None] // CHUNK)
    gather = jax.vmap(lambda arr, ii: arr[ii])
    ks = gather(k_all, idx)
    vs = gather(v_all, idx)
    logits = jnp.einsum('btgrd,btkgd->btgrk', q, ks).astype(jnp.float32) * C_HEAD_DIM ** -0.5
    bias = rel_table[t5_bucket(idx - qpos[None, :, None])].astype(jnp.float32)
    bias = jnp.moveaxis(bias.reshape(b, Tq, topk, C_KV_HEADS, C_GROUP), 2, -1)
    logits = jnp.where(keep[:, :, None, None, :], logits + bias, -jnp.inf)
    p = jax.nn.softmax(logits, axis=-1)
    o = jnp.einsum('btgrk,btkgd->btgrd', p.astype(vs.dtype), vs)
    return o.reshape(b, Tq, C_WIDTH)


def dsa_prompt(q, qi, wi, k, v, ki, rel_table):
    b, T = q.shape[:2]
    topk = min(TOPK_MAX, T // 4)
    nblk = T // Q_BLOCK

    def blocks(t):
        return jnp.moveaxis(t.reshape((b, nblk, Q_BLOCK) + t.shape[2:]), 1, 0)

    qpos = jnp.arange(T, dtype=jnp.int32).reshape(nblk, Q_BLOCK)

    def one_block(inp):
        qb, qib, wib, pos = inp
        return dsa_attend(qb, qib, wib, pos, k, v, ki, rel_table, topk)

    out = lax.map(one_block, (blocks(q), blocks(qi), blocks(wi), qpos))
    return jnp.moveaxis(out, 0, 1).reshape(b, T, C_WIDTH)


def dsa_sample(q, qi, wi, k_new, v_new, ki_new, cache_k, cache_v, cache_ki, rel_table):
    T = q.shape[1]
    past = cache_k.shape[1]
    k_all = jnp.concatenate([cache_k, k_new], axis=1)
    v_all = jnp.concatenate([cache_v, v_new], axis=1)
    ki_all = jnp.concatenate([cache_ki, ki_new], axis=1)
    topk = min(TOPK_MAX, (past + T) // 4)
    qpos = past + jnp.arange(T, dtype=jnp.int32)

    def one_seq(inp):
        qs, qis, wis, ks, vs, kis = inp
        return dsa_attend(qs[None], qis[None], wis[None], qpos, ks[None], vs[None], kis[None],
                          rel_table, topk)[0]

    return lax.map(one_seq, (q, qi, wi, k_all, v_all, ki_all))


def token_mixers(h, lp, rel_table, past):
    b, T, _ = h.shape
    proj = h @ lp['w_in']
    pa, pb, pc, pg = jnp.split(proj, [A_COLS, A_COLS + B_COLS, A_COLS + B_COLS + C_COLS], axis=-1)
    u, va = jnp.split(jax.nn.gelu(pa), 2, axis=-1)
    va = rms_norm(va, lp['g_a'])
    o_a = chunk_spatial_gate(u, va, lp['w_s'], lp['b_s'])
    z, xbc, dt_raw = jnp.split(pb, [SSM_INNER, SSM_INNER + CONV_DIM], axis=-1)
    if past is None:
        conv_buf = jnp.zeros((b, CONV_W - 1, CONV_DIM), h.dtype)
        h0 = jnp.zeros((b, SSM_HEADS, SSM_HEAD_DIM, SSM_STATE), jnp.float32)
        q_len = CHUNK
    else:
        h0, conv_buf = past[3], past[4]
        q_len = T
    o_b, new_conv, new_ssm = mamba2_branch(z, xbc, dt_raw, conv_buf, h0, lp['w_conv'], lp['b_conv'],
                                           lp['dt_bias'], lp['a_log'], lp['d_skip'], lp['g_ssm'], q_len)
    q, k, v, qi, ki, wi = jnp.split(
        pc, [C_Q, C_Q + C_KV, C_Q + 2 * C_KV, C_Q + 2 * C_KV + C_QI, C_Q + 2 * C_KV + C_QI + IDX_DIM], axis=-1)
    q = q.reshape(b, T, C_KV_HEADS, C_GROUP, C_HEAD_DIM)
    k = k.reshape(b, T, C_KV_HEADS, C_HEAD_DIM)
    v = v.reshape(b, T, C_KV_HEADS, C_HEAD_DIM)
    qi = qi.reshape(b, T, IDX_HEADS, IDX_DIM)
    if past is None:
        o_c = dsa_prompt(q, qi, wi, k, v, ki, rel_table)
    else:
        o_c = dsa_sample(q, qi, wi, k, v, ki, past[0], past[1], past[2], rel_table)
    gates = jax.nn.sigmoid(pg).reshape(b, T, 3, D_MODEL)
    merged = (gates[:, :, 0] * (o_a @ lp['w_br_a'])
              + gates[:, :, 1] * (o_b @ lp['w_br_b'])
              + gates[:, :, 2] * (o_c @ lp['w_br_c']))
    return merged @ lp['w_out'], (k, v, ki, new_ssm, new_conv, va)


def hybrid_layer(x, c, lp, rel_table, past):
    ada = (jax.nn.silu(c) @ lp['w_ada'] + lp['b_ada']).reshape(c.shape[0], N_ADA, D_MODEL)
    h = modulate(x, lp['g_ffn1'], ada[:, 0], ada[:, 1])
    x = x + 0.5 * ada[:, 2][:, None] * swiglu(h, lp['w_ffn1_in'], lp['w_ffn1_out'])
    h = modulate(x, lp['g_mix'], ada[:, 3], ada[:, 4])
    mix, states = token_mixers(h, lp, rel_table, past)
    x = x + ada[:, 5][:, None] * mix
    h = modulate(x, lp['g_ffn2'], ada[:, 6], ada[:, 7])
    x = x + 0.5 * ada[:, 8][:, None] * swiglu(h, lp['w_ffn2_in'], lp['w_ffn2_out'])
    return x, states


def setup_inputs(seed: int = 0) -> dict:
    key = jax.random.key(seed)
    keys = iter(jax.random.split(key, 48))

    def nrm(shape, scale):
        return jax.random.normal(next(keys), shape, jnp.float32) * scale

    def gain(shape):
        return 1.0 + nrm(shape, 0.05)

    dt0 = jnp.exp(jax.random.uniform(next(keys), (DEPTH, SSM_HEADS), jnp.float32,
                                     math.log(0.001), math.log(0.1)))
    return {
        'x_prompt': nrm((BATCH, SEQ, D_MODEL), 1.0),
        'x_sample': nrm((DEC_BATCH, DEC_SEQ, D_MODEL), 1.0),
        'c_prompt': nrm((BATCH, D_MODEL), 1.0),
        'c_sample': nrm((DEC_BATCH, D_MODEL), 1.0),
        'cache_k': nrm((DEPTH, DEC_BATCH, PAST_LEN, C_KV_HEADS, C_HEAD_DIM), 1.0),
        'cache_v': nrm((DEPTH, DEC_BATCH, PAST_LEN, C_KV_HEADS, C_HEAD_DIM), 1.0),
        'cache_kidx': nrm((DEPTH, DEC_BATCH, PAST_LEN, IDX_DIM), 1.0),
        'state_ssm': nrm((DEPTH, DEC_BATCH, SSM_HEADS, SSM_HEAD_DIM, SSM_STATE), 0.3),
        'state_conv': nrm((DEPTH, DEC_BATCH, CONV_W - 1, CONV_DIM), 1.0),
        'rel_bias': nrm((N_BUCKETS, C_HEADS), 0.5),
        'w_ada': nrm((DEPTH, D_MODEL, N_ADA * D_MODEL), 0.5 * D_MODEL ** -0.5),
        'b_ada': nrm((DEPTH, N_ADA * D_MODEL), 0.01),
        'g_ffn1': gain((DEPTH, D_MODEL)),
        'w_ffn1_in': nrm((DEPTH, D_MODEL, 2 * D_FF), D_MODEL ** -0.5),
        'w_ffn1_out': nrm((DEPTH, D_FF, D_MODEL), D_FF ** -0.5),
        'g_mix': gain((DEPTH, D_MODEL)),
        'w_in': nrm((DEPTH, D_MODEL, IN_COLS), D_MODEL ** -0.5),
        'g_a': gain((DEPTH, A_WIDTH)),
        'w_s': nrm((DEPTH, A_GROUPS, A_CHUNK, A_CHUNK), A_CHUNK ** -0.5),
        'b_s': 1.0 + nrm((DEPTH, A_GROUPS, A_CHUNK), 0.02),
        'w_conv': nrm((DEPTH, CONV_W, CONV_DIM), CONV_W ** -0.5),
        'b_conv': nrm((DEPTH, CONV_DIM), 0.01),
        'dt_bias': dt0 + jnp.log(-jnp.expm1(-dt0)),
        'a_log': jnp.log(jax.random.uniform(next(keys), (DEPTH, SSM_HEADS), jnp.float32, 1.0, 16.0)),
        'd_skip': gain((DEPTH, SSM_HEADS)),
        'g_ssm': gain((DEPTH, SSM_INNER)),
        'w_br_a': nrm((DEPTH, A_WIDTH, D_MODEL), A_WIDTH ** -0.5),
        'w_br_b': nrm((DEPTH, SSM_INNER, D_MODEL), SSM_INNER ** -0.5),
        'w_br_c': nrm((DEPTH, C_WIDTH, D_MODEL), C_WIDTH ** -0.5),
        'w_out': nrm((DEPTH, D_MODEL, D_MODEL), D_MODEL ** -0.5),
        'g_ffn2': gain((DEPTH, D_MODEL)),
        'w_ffn2_in': nrm((DEPTH, D_MODEL, 2 * D_FF), D_MODEL ** -0.5),
        'w_ffn2_out': nrm((DEPTH, D_FF, D_MODEL), D_FF ** -0.5),
        'g_final': gain((D_MODEL,)),
    }


def reference(x_prompt, x_sample, c_prompt, c_sample, cache_k, cache_v, cache_kidx, state_ssm, state_conv,
              rel_bias, w_ada, b_ada, g_ffn1, w_ffn1_in, w_ffn1_out, g_mix, w_in, g_a, w_s, b_s,
              w_conv, b_conv, dt_bias, a_log, d_skip, g_ssm, w_br_a, w_br_b, w_br_c, w_out,
              g_ffn2, w_ffn2_in, w_ffn2_out, g_final):
    yp, ys = x_prompt, x_sample
    kp, vp, kip, sp, cp = [], [], [], [], []
    ks_, vs_, kis, ss, cs, gvs = [], [], [], [], [], []
    for i in range(DEPTH):
        lp = dict(w_ada=w_ada[i], b_ada=b_ada[i], g_ffn1=g_ffn1[i], w_ffn1_in=w_ffn1_in[i],
                  w_ffn1_out=w_ffn1_out[i], g_mix=g_mix[i], w_in=w_in[i], g_a=g_a[i], w_s=w_s[i],
                  b_s=b_s[i], w_conv=w_conv[i], b_conv=b_conv[i], dt_bias=dt_bias[i], a_log=a_log[i],
                  d_skip=d_skip[i], g_ssm=g_ssm[i], w_br_a=w_br_a[i], w_br_b=w_br_b[i], w_br_c=w_br_c[i],
                  w_out=w_out[i], g_ffn2=g_ffn2[i], w_ffn2_in=w_ffn2_in[i], w_ffn2_out=w_ffn2_out[i])
        yp, st_p = hybrid_layer(yp, c_prompt, lp, rel_bias, None)
        ys, st_s = hybrid_layer(ys, c_sample, lp, rel_bias,
                                (cache_k[i], cache_v[i], cache_kidx[i], state_ssm[i], state_conv[i]))
        kp.append(st_p[0]); vp.append(st_p[1]); kip.append(st_p[2]); sp.append(st_p[3]); cp.append(st_p[4])
        ks_.append(st_s[0]); vs_.append(st_s[1]); kis.append(st_s[2]); ss.append(st_s[3]); cs.append(st_s[4])
        gvs.append(st_s[5])
    y_prompt = rms_norm(yp, g_final)
    y_sample = rms_norm(ys, g_final)
    return (y_prompt, y_sample,
            jnp.stack(kp), jnp.stack(vp), jnp.stack(kip), jnp.stack(sp), jnp.stack(cp),
            jnp.stack(ks_), jnp.stack(vs_), jnp.stack(kis), jnp.stack(ss), jnp.stack(cs), jnp.stack(gvs))
```

```python
import functools
import math

import numpy as np
import jax
import jax.numpy as jnp
from jax import lax
from jax.experimental import pallas as pl
from jax.experimental.pallas import tpu as pltpu

f32 = jnp.float32
bf16 = jnp.bfloat16
i32 = jnp.int32

D_MODEL = 1024
CHUNK = 64
EPS = 1e-6
N_ADA = 9
D_FF = 2816
A_WIDTH = D_MODEL
A_GROUPS = 8
A_GROUP_DIM = A_WIDTH // A_GROUPS
A_CHUNK = 128
SSM_INNER = 2 * D_MODEL
SSM_HEAD_DIM = 64
SSM_HEADS = SSM_INNER // SSM_HEAD_DIM
SSM_GROUPS = 4
SSM_HPG = SSM_HEADS // SSM_GROUPS
SSM_STATE = 128
CONV_W = 4
CONV_DIM = SSM_INNER + 2 * SSM_GROUPS * SSM_STATE
C_HEADS = 8
C_KV_HEADS = 4
C_GROUP = C_HEADS // C_KV_HEADS
C_HEAD_DIM = 128
C_WIDTH = C_HEADS * C_HEAD_DIM
IDX_HEADS = 8
IDX_DIM = 64
TOPK_MAX = 256
Q_BLOCK = 128
N_BUCKETS = 32
MAX_DISTANCE = 128
A_COLS = 2 * A_WIDTH
B_COLS = SSM_INNER + CONV_DIM + SSM_HEADS
C_Q = C_HEADS * C_HEAD_DIM
C_KV = C_KV_HEADS * C_HEAD_DIM
C_QI = IDX_HEADS * IDX_DIM
GATE_COLS = 3 * D_MODEL

LANES = 128
VMEM_LIMIT = 56 * 1024 * 1024

OFF_GATE = 0
OFF_XBC = OFF_GATE + GATE_COLS
OFF_PA = OFF_XBC + CONV_DIM
OFF_Z = OFF_PA + A_COLS
OFF_Q = OFF_Z + SSM_INNER
OFF_K = OFF_Q + C_Q
OFF_V = OFF_K + C_KV
OFF_QI = OFF_V + C_KV
OFF_SMALL = OFF_QI + C_QI
SMALL_DT = 0
SMALL_WI = SSM_HEADS
SMALL_KI = LANES - IDX_DIM
PROJ_TILE = 1024
PROJ_W = ((OFF_SMALL + LANES + PROJ_TILE - 1) // PROJ_TILE) * PROJ_TILE

NEG_BIG = -1e30
NEG_INF_KEY = -2139095041
INT_MIN = -2147483648
INT_MAX = 2147483647
ATT_KB = 512
BIAS_WIN = 256
FAR_BUCKET = N_BUCKETS // 2 - 1


def _cp(sem, vmem=VMEM_LIMIT):
    return pltpu.CompilerParams(dimension_semantics=sem, vmem_limit_bytes=vmem)


def _rms(x, g):
    return x * lax.rsqrt(jnp.mean(x * x, axis=-1, keepdims=True) + EPS) * g


def _dot(a, b):
    return jnp.dot(a, b, preferred_element_type=f32)


def _dot_nt(a, b):
    return lax.dot_general(a, b, (((1,), (1,)), ((), ())), preferred_element_type=f32)


def _dot_tn(a, b):
    return lax.dot_general(a, b, (((0,), (0,)), ((), ())), preferred_element_type=f32)


def _dot_exact(a, b):
    return jnp.dot(a, b, precision=lax.Precision.HIGHEST, preferred_element_type=f32)


def _ada_kernel(c_ref, w_ref, b_ref, o_ref):
    s = jax.nn.silu(c_ref[...]).astype(bf16)
    o_ref[0] = _dot(s, w_ref[0]) + b_ref[0]


def _ada_all(c_all, w_ada16, b_ada):
    depth = w_ada16.shape[0]
    rows = c_all.shape[0]
    n_out = N_ADA * D_MODEL
    tn = 9 * LANES
    return pl.pallas_call(
        _ada_kernel,
        out_shape=jax.ShapeDtypeStruct((depth, rows, n_out), f32),
        grid=(depth, n_out // tn),
        in_specs=[pl.BlockSpec((rows, D_MODEL), lambda d, n: (0, 0)),
                  pl.BlockSpec((1, D_MODEL, tn), lambda d, n: (d, 0, n)),
                  pl.BlockSpec((1, 1, tn), lambda d, n: (d, 0, n))],
        out_specs=pl.BlockSpec((1, rows, tn), lambda d, n: (d, 0, n)),
        compiler_params=_cp(("arbitrary", "arbitrary")),
        name="ada",
    )(c_all, w_ada16, b_ada.reshape(depth, 1, n_out))


def _ffn_kernel(x_ref, ada_ref, g_ref, wg_ref, wu_ref, wo_ref, gf_ref, o_ref, h_scr, acc_scr, *, k0, final_norm):
    f = pl.program_id(2)
    bt, tt, d = x_ref.shape
    rows = bt * tt

    @pl.when(f == 0)
    def _():
        ada = ada_ref[...]
        h = _rms(x_ref[...], g_ref[...]) * (1.0 + ada[:, k0 + 1:k0 + 2, :]) + ada[:, k0:k0 + 1, :]
        h_scr[...] = h.reshape(rows, d).astype(bf16)
        acc_scr[...] = jnp.zeros_like(acc_scr)

    h = h_scr[...]
    gate = _dot(h, wg_ref[...])
    up = _dot(h, wu_ref[...])
    a = (jax.nn.silu(gate) * up).astype(bf16)
    acc_scr[...] += _dot(a, wo_ref[...])

    @pl.when(f == pl.num_programs(2) - 1)
    def _():
        ada = ada_ref[...]
        y = x_ref[...] + (0.5 * ada[:, k0 + 2:k0 + 3, :]) * acc_scr[...].reshape(bt, tt, d)
        if final_norm:
            y = _rms(y, gf_ref[...])
        o_ref[...] = y


def _token_tile(b, t, rows=512):
    if t >= rows:
        return 1, rows
    return min(b, rows // t), t


def _ffn(x, ada, g, w_in16, w_out16, g_final, *, k0, final_norm):
    b, t, d = x.shape
    bt, tt = _token_tile(b, t)
    tf = D_FF // 2
    nf = D_FF // tf
    kern = functools.partial(_ffn_kernel, k0=k0, final_norm=final_norm)
    return pl.pallas_call(
        kern,
        out_shape=jax.ShapeDtypeStruct((b, t, d), f32),
        grid=(b // bt, t // tt, nf),
        in_specs=[pl.BlockSpec((bt, tt, d), lambda i, j, f: (i, j, 0)),
                  pl.BlockSpec((bt, N_ADA, d), lambda i, j, f: (i, 0, 0)),
                  pl.BlockSpec((1, d), lambda i, j, f: (0, 0)),
                  pl.BlockSpec((d, tf), lambda i, j, f: (0, f)),
                  pl.BlockSpec((d, tf), lambda i, j, f: (0, f + nf)),
                  pl.BlockSpec((tf, d), lambda i, j, f: (f, 0)),
                  pl.BlockSpec((1, d), lambda i, j, f: (0, 0))],
        out_specs=pl.BlockSpec((bt, tt, d), lambda i, j, f: (i, j, 0)),
        scratch_shapes=[pltpu.VMEM((bt * tt, d), bf16), pltpu.VMEM((bt * tt, d), f32)],
        compiler_params=_cp(("arbitrary", "arbitrary", "arbitrary")),
        name="ffn",
    )(x, ada, g.reshape(1, d), w_in16, w_in16, w_out16, g_final.reshape(1, d))


def _inproj_kernel(x_ref, ada_ref, g_ref, w_ref, o_ref, h_scr, *, k0):
    n = pl.program_id(2)
    bt, tt, d = x_ref.shape

    @pl.when(n == 0)
    def _():
        ada = ada_ref[...]
        h = _rms(x_ref[...], g_ref[...]) * (1.0 + ada[:, k0 + 1:k0 + 2, :]) + ada[:, k0:k0 + 1, :]
        h_scr[...] = h.reshape(bt * tt, d).astype(bf16)

    o_ref[...] = _dot(h_scr[...], w_ref[...]).reshape(bt, tt, PROJ_TILE)


def _inproj(x, ada, g, w_proj16, *, k0):
    b, t, d = x.shape
    bt, tt = _token_tile(b, t)
    return pl.pallas_call(
        functools.partial(_inproj_kernel, k0=k0),
        out_shape=jax.ShapeDtypeStruct((b, t, PROJ_W), f32),
        grid=(b // bt, t // tt, PROJ_W // PROJ_TILE),
        in_specs=[pl.BlockSpec((bt, tt, d), lambda i, j, n: (i, j, 0)),
                  pl.BlockSpec((bt, N_ADA, d), lambda i, j, n: (i, 0, 0)),
                  pl.BlockSpec((1, d), lambda i, j, n: (0, 0)),
                  pl.BlockSpec((d, PROJ_TILE), lambda i, j, n: (0, n))],
        out_specs=pl.BlockSpec((bt, tt, PROJ_TILE), lambda i, j, n: (i, j, n)),
        scratch_shapes=[pltpu.VMEM((bt * tt, d), bf16)],
        compiler_params=_cp(("arbitrary", "arbitrary", "arbitrary")),
        name="inproj",
    )(x, ada, g.reshape(1, d), w_proj16)


def _layout_w_in(w_in):
    d = w_in.shape[0]
    o = 0
    pa = w_in[:, o:o + A_COLS]; o += A_COLS
    z = w_in[:, o:o + SSM_INNER]; o += SSM_INNER
    xbc = w_in[:, o:o + CONV_DIM]; o += CONV_DIM
    dt = w_in[:, o:o + SSM_HEADS]; o += SSM_HEADS
    q = w_in[:, o:o + C_Q]; o += C_Q
    k = w_in[:, o:o + C_KV]; o += C_KV
    v = w_in[:, o:o + C_KV]; o += C_KV
    qi = w_in[:, o:o + C_QI]; o += C_QI
    ki = w_in[:, o:o + IDX_DIM]; o += IDX_DIM
    wi = w_in[:, o:o + IDX_HEADS]; o += IDX_HEADS
    gates = w_in[:, o:o + GATE_COLS]
    small = jnp.concatenate([dt, wi, jnp.zeros((d, SMALL_KI - SMALL_WI - IDX_HEADS), w_in.dtype), ki], axis=1)
    pad = jnp.zeros((d, PROJ_W - OFF_SMALL - LANES), w_in.dtype)
    return jnp.concatenate([gates, xbc, pa, z, q, k, v, qi, small, pad], axis=1)


def _mixa_kernel(pa_ref, ga_ref, ws_ref, bs_ref, oa_ref, va_ref):
    n = pa_ref.shape[1]
    ge = jax.nn.gelu(pa_ref[0])
    u = ge[:, :A_WIDTH]
    v = _rms(ge[:, A_WIDTH:], ga_ref[...])
    va_ref[0] = v
    vb = v.astype(bf16)
    tri = lax.broadcasted_iota(i32, (n, n), 0) >= lax.broadcasted_iota(i32, (n, n), 1)
    bs = bs_ref[...]
    for g in range(A_GROUPS):
        sl = slice(g * A_GROUP_DIM, (g + 1) * A_GROUP_DIM)
        w = jnp.where(tri, ws_ref[g], 0.0).astype(bf16)
        mixed = _dot(w, vb[:, sl]) + bs[:, g:g + 1]
        oa_ref[0, :, sl] = (u[:, sl] * mixed).astype(bf16)


def _mixer_a(proj, g_a, w_s, b_s):
    b, t, _ = proj.shape
    n = min(t, A_CHUNK)
    ws = w_s[:, :n, :n]
    bs_t = b_s[:, :n].T
    return pl.pallas_call(
        _mixa_kernel,
        out_shape=(jax.ShapeDtypeStruct((b, t, A_WIDTH), bf16), jax.ShapeDtypeStruct((b, t, A_WIDTH), f32)),
        grid=(b, t // n),
        in_specs=[pl.BlockSpec((1, n, A_COLS), lambda i, j: (i, j, OFF_PA // A_COLS)),
                  pl.BlockSpec((1, A_WIDTH), lambda i, j: (0, 0)),
                  pl.BlockSpec((A_GROUPS, n, n), lambda i, j: (0, 0, 0)),
                  pl.BlockSpec((n, A_GROUPS), lambda i, j: (0, 0))],
        out_specs=(pl.BlockSpec((1, n, A_WIDTH), lambda i, j: (i, j, 0)),
                   pl.BlockSpec((1, n, A_WIDTH), lambda i, j: (i, j, 0))),
        compiler_params=_cp(("arbitrary", "arbitrary")),
        name="mixer_a",
    )(proj, g_a.reshape(1, A_WIDTH), ws, bs_t)


CONV_PAD = 8


def _ssd_kernel(z_ref, xbc_ref, sm_ref, cinit_ref, h0_ref, wc_ref, bc_ref, dtb_ref, alog_ref, dsk_ref, gs_ref,
                ex_ref, ob_ref, hs_ref, xpad, ht):
    c = pl.program_id(1)
    q = z_ref.shape[1]
    gw = SSM_HPG * SSM_HEAD_DIM
    tail = CONV_W - 1

    @pl.when(c == 0)
    def _():
        xpad[CONV_PAD - tail:CONV_PAD, :] = cinit_ref[0]
        for g in range(SSM_GROUPS):
            ht[g] = h0_ref[0, g * gw:(g + 1) * gw, :].T

    xpad[CONV_PAD:CONV_PAD + q, :] = xbc_ref[0]
    wc = wc_ref[...]
    conv = bc_ref[...] + wc[0:1, :] * xpad[pl.ds(CONV_PAD - 3, q), :]
    for k in range(1, CONV_W):
        conv = conv + wc[k:k + 1, :] * xpad[pl.ds(CONV_PAD - 3 + k, q), :]
    nxt = xpad[CONV_PAD + q - tail:CONV_PAD + q, :]
    xpad[CONV_PAD - tail:CONV_PAD, :] = nxt
    conv = jax.nn.silu(conv)
    xs = conv[:, :SSM_INNER]
    bm = conv[:, SSM_INNER:SSM_INNER + SSM_GROUPS * SSM_STATE]
    cm = conv[:, SSM_INNER + SSM_GROUPS * SSM_STATE:]

    lane = lax.broadcasted_iota(i32, (q, LANES), 1)
    head_lane = lane < SSM_HEADS
    dt = jnp.where(head_lane, jax.nn.softplus(sm_ref[0] + dtb_ref[...]), 0.0)
    da = dt * (-jnp.exp(alog_ref[...]))
    tri = lax.broadcasted_iota(i32, (q, q), 0) >= lax.broadcasted_iota(i32, (q, q), 1)
    acs = _dot_exact(tri.astype(f32), da)
    acs_t = acs.T
    dt_t = dt.T
    acs_last = acs[q - 1:q, :]
    e_acs = jnp.where(head_lane, jnp.exp(acs), 0.0)
    decay = jnp.where(head_lane, jnp.exp(acs_last - acs) * dt, 0.0)
    ex = ex_ref[...]
    e_acs_x = _dot_exact(e_acs, ex)
    decay_x = _dot_exact(decay, ex)
    e_last_x = e_acs_x[q - 1:q, :]

    xs16 = xs.astype(bf16)
    xdec16 = (xs * decay_x).astype(bf16)
    y_parts = []
    for g in range(SSM_GROUPS):
        bg = bm[:, g * SSM_STATE:(g + 1) * SSM_STATE].astype(bf16)
        cg = cm[:, g * SSM_STATE:(g + 1) * SSM_STATE].astype(bf16)
        cb = _dot_nt(cg, bg)
        htg = ht[g]
        y_inter = _dot(cg, htg.astype(bf16))
        intra = []
        for r in range(SSM_HPG):
            hd = g * SSM_HPG + r
            seg = acs[:, hd:hd + 1] - acs_t[hd:hd + 1, :]
            m = cb * jnp.exp(jnp.where(tri, seg, -jnp.inf)) * dt_t[hd:hd + 1, :]
            intra.append(_dot(m.astype(bf16), xs16[:, hd * SSM_HEAD_DIM:(hd + 1) * SSM_HEAD_DIM]))
        y_parts.append(jnp.concatenate(intra, axis=1) + y_inter * e_acs_x[:, g * gw:(g + 1) * gw])
        ht[g] = htg * e_last_x[:, g * gw:(g + 1) * gw] + _dot_tn(bg, xdec16[:, g * gw:(g + 1) * gw])
    y = jnp.concatenate(y_parts, axis=1) + dsk_ref[...] * xs
    y = _rms(y * jax.nn.silu(z_ref[0]), gs_ref[...])
    ob_ref[0] = y.astype(bf16)

    @pl.when(c == pl.num_programs(1) - 1)
    def _():
        for g in range(SSM_GROUPS):
            hs_ref[0, g * gw:(g + 1) * gw, :] = ht[g].T


def _mixer_b(proj, conv_init, h0, w_conv, b_conv, dt_bias, a_log, d_skip, g_ssm):
    b, t, _ = proj.shape
    q = CHUNK
    pad_row = lambda v: jnp.zeros((1, LANES), f32).at[0, :SSM_HEADS].set(v)
    expand = (jnp.arange(LANES)[:, None] == (jnp.arange(SSM_INNER) // SSM_HEAD_DIM)[None, :]).astype(f32)
    hrows = SSM_HEADS * SSM_HEAD_DIM
    return pl.pallas_call(
        _ssd_kernel,
        out_shape=(jax.ShapeDtypeStruct((b, t, SSM_INNER), bf16),
                   jax.ShapeDtypeStruct((b, hrows, SSM_STATE), f32)),
        grid=(b, t // q),
        in_specs=[pl.BlockSpec((1, q, SSM_INNER), lambda i, c: (i, c, OFF_Z // SSM_INNER)),
                  pl.BlockSpec((1, q, CONV_DIM), lambda i, c: (i, c, OFF_XBC // CONV_DIM)),
                  pl.BlockSpec((1, q, LANES), lambda i, c: (i, c, OFF_SMALL // LANES)),
                  pl.BlockSpec((1, CONV_W - 1, CONV_DIM), lambda i, c: (i, 0, 0)),
                  pl.BlockSpec((1, hrows, SSM_STATE), lambda i, c: (i, 0, 0)),
                  pl.BlockSpec((CONV_W, CONV_DIM), lambda i, c: (0, 0)),
                  pl.BlockSpec((1, CONV_DIM), lambda i, c: (0, 0)),
                  pl.BlockSpec((1, LANES), lambda i, c: (0, 0)),
                  pl.BlockSpec((1, LANES), lambda i, c: (0, 0)),
                  pl.BlockSpec((1, SSM_INNER), lambda i, c: (0, 0)),
                  pl.BlockSpec((1, SSM_INNER), lambda i, c: (0, 0)),
                  pl.BlockSpec((LANES, SSM_INNER), lambda i, c: (0, 0))],
        out_specs=(pl.BlockSpec((1, q, SSM_INNER), lambda i, c: (i, c, 0)),
                   pl.BlockSpec((1, hrows, SSM_STATE), lambda i, c: (i, 0, 0))),
        scratch_shapes=[pltpu.VMEM((CONV_PAD + q, CONV_DIM), f32),
                        pltpu.VMEM((SSM_GROUPS, SSM_STATE, SSM_HPG * SSM_HEAD_DIM), f32)],
        compiler_params=_cp(("arbitrary", "arbitrary")),
        name="mixer_b",
    )(proj, proj, proj, conv_init, h0.reshape(b, hrows, SSM_STATE), w_conv, b_conv.reshape(1, CONV_DIM),
      pad_row(dt_bias), pad_row(a_log), jnp.repeat(d_skip, SSM_HEAD_DIM).reshape(1, SSM_INNER),
      g_ssm.reshape(1, SSM_INNER), expand)


def _t5_bucket(rel):
    nb = N_BUCKETS // 2
    max_exact = nb // 2
    n = jnp.abs(rel)
    nf = jnp.maximum(n, 1).astype(jnp.float32)
    large = max_exact + (jnp.log(nf / max_exact) / math.log(MAX_DISTANCE / max_exact)
                         * (nb - max_exact)).astype(jnp.int32)
    large = jnp.minimum(large, nb - 1)
    return jnp.where(rel > 0, nb, 0) + jnp.where(n < max_exact, n, large)


def _bias_kernel(tab_ref, bk_ref, o_ref):
    h = pl.program_id(1)
    bk = bk_ref[0]
    far = tab_ref[FAR_BUCKET, h]
    acc = jnp.zeros(bk.shape, f32)
    for bucket in range(N_BUCKETS):
        acc = jnp.where(bk == bucket, tab_ref[bucket, h] - far, acc)
    o_ref[0, 0] = acc


def _bias_tables(rel_bias, qb, offsets):
    t = np.arange(qb)[:, None]
    cases = []
    for off in offsets:
        w = np.arange(ATT_KB)[None, :] - off
        valid = (w >= 0) & (w < BIAS_WIN)
        rel = w - BIAS_WIN + qb - t
        cases.append(jnp.where(jnp.asarray(valid), _t5_bucket(jnp.asarray(rel, dtype=jnp.int32)), -1))
    cases.append(jnp.full((qb, ATT_KB), -1, jnp.int32))
    buckets = jnp.stack(cases).astype(i32)
    nc = buckets.shape[0]
    return pl.pallas_call(
        _bias_kernel,
        out_shape=jax.ShapeDtypeStruct((nc, C_HEADS, qb, ATT_KB), f32),
        grid=(nc, C_HEADS),
        in_specs=[pl.BlockSpec(memory_space=pltpu.SMEM),
                  pl.BlockSpec((1, qb, ATT_KB), lambda c, h: (c, 0, 0))],
        out_specs=pl.BlockSpec((1, 1, qb, ATT_KB), lambda c, h: (c, h, 0, 0)),
        compiler_params=_cp(("arbitrary", "arbitrary")),
        name="t5_bias",
    )(rel_bias, buckets)


def _index_kernel(qi_ref, sm_ref, ki_ref, o_ref, keys, *, qoff, topk, idx_bits):
    j = pl.program_id(1)
    qb = qi_ref.shape[1]
    ltot = ki_ref.shape[1]
    kb = ATT_KB
    kmax = qoff + (j + 1) * qb
    nkb = lax.div(kmax + (kb - 1), kb)
    sub = kb // LANES

    qv = qi_ref[0].astype(bf16)
    stacked = jnp.concatenate([qv[:, h * IDX_DIM:(h + 1) * IDX_DIM] for h in range(IDX_HEADS)], axis=0)
    w = sm_ref[0][:, SMALL_WI:SMALL_WI + IDX_HEADS] * (IDX_HEADS ** -0.5) * (IDX_DIM ** -0.5)
    wb = [jnp.broadcast_to(w[:, h:h + 1], (qb, kb)) for h in range(IDX_HEADS)]
    qchunk = lax.shift_right_arithmetic(qoff + j * qb + lax.broadcasted_iota(i32, (qb, 1), 0), 6)
    lane_pos = lax.broadcasted_iota(i32, (1, kb), 1)

    def score_body(i, carry):
        off = pl.multiple_of(i * kb, kb)
        s = _dot_nt(stacked, ki_ref[0, pl.ds(off, kb), :])
        acc = jnp.maximum(s[0:qb], 0.0) * wb[0]
        for h in range(1, IDX_HEADS):
            acc = acc + jnp.maximum(s[h * qb:(h + 1) * qb], 0.0) * wb[h]
        adm = lax.shift_right_arithmetic(off + lane_pos, 6) <= qchunk
        x = jnp.where(adm, jnp.where(acc == 0.0, 0.0, acc), -jnp.inf)
        bits = lax.bitcast_convert_type(x, i32)
        keys[:, pl.ds(off, kb)] = bits ^ (lax.shift_right_arithmetic(bits, 31) & INT_MAX)
        return carry

    lax.fori_loop(0, nkb, score_body, 0)

    def count(pred):
        def body(i, acc):
            off = pl.multiple_of(i * kb, kb)
            blk = keys[:, pl.ds(off, kb)]
            m = pred(blk, off).astype(i32)
            for c in range(sub):
                acc = acc + m[:, c * LANES:(c + 1) * LANES]
            return acc
        acc = lax.fori_loop(0, nkb, body, jnp.zeros((qb, LANES), i32))
        return jnp.sum(acc, axis=1, keepdims=True)

    def count_ge(thr):
        tb = jnp.broadcast_to(thr, (qb, kb))
        return count(lambda blk, off: blk >= tb)

    def bis_body(_, c):
        lo, hi = c
        mid = lax.shift_right_arithmetic(lo, 1) + lax.shift_right_arithmetic(hi, 1) + (lo & hi & 1)
        ge = count_ge(mid) >= topk
        return jnp.where(ge, mid, lo), jnp.where(ge, hi, mid)

    lo0 = jnp.full((qb, 1), INT_MIN, i32)
    hi0 = jnp.full((qb, 1), INT_MAX, i32)
    thr, _ = lax.fori_loop(0, 32, bis_body, (lo0, hi0))
    n_ge = count_ge(thr)
    n_gt = count_ge(thr + 1)
    tie = (n_ge > topk) & (thr > NEG_INF_KEY)
    need = topk - n_gt

    def tie_cut(_):
        tb = jnp.broadcast_to(thr, (qb, kb))

        def body(_, c):
            lo, hi = c
            mid = lax.shift_right_arithmetic(lo + hi, 1)
            mb = jnp.broadcast_to(mid, (qb, kb))
            n = count(lambda blk, off: (blk == tb) & ((off + lane_pos) <= mb))
            ok = n >= need
            return jnp.where(ok, lo, mid), jnp.where(ok, mid, hi)

        lo_i = jnp.full((qb, 1), -1, i32)
        hi_i = jnp.full((qb, 1), 1, i32) * (nkb * kb - 1)
        _, cut = lax.fori_loop(0, idx_bits, body, (lo_i, hi_i))
        return jnp.where(tie, cut, INT_MAX)

    any_tie = jnp.max(tie.astype(i32)) > 0
    cut = lax.cond(any_tie, tie_cut, lambda _: jnp.full((qb, 1), INT_MAX, i32), 0)

    tb = jnp.broadcast_to(thr, (qb, kb))
    cb = jnp.broadcast_to(cut, (qb, kb))

    def out_body(i, carry):
        off = pl.multiple_of(i * kb, kb)
        blk = keys[:, pl.ds(off, kb)]
        sel = ((blk > tb) | ((blk == tb) & ((off + lane_pos) <= cb))) & (blk > NEG_INF_KEY)
        o_ref[0, :, pl.ds(off, kb)] = jnp.where(sel, 0.0, NEG_BIG).astype(bf16)
        return carry

    lax.fori_loop(0, nkb, out_body, 0)

    def fill_body(i, carry):
        off = pl.multiple_of(i * kb, kb)
        o_ref[0, :, pl.ds(off, kb)] = jnp.full((qb, kb), NEG_BIG, bf16)
        return carry

    lax.fori_loop(nkb, ltot // kb, fill_body, 0)


def _dsa_mask(proj, ki16, *, qb, qoff, topk):
    b, t, _ = proj.shape
    ltot = ki16.shape[1]
    idx_bits = int(math.ceil(math.log2(ltot))) + 1
    kern = functools.partial(_index_kernel, qoff=qoff, topk=topk, idx_bits=idx_bits)
    return pl.pallas_call(
        kern,
        out_shape=jax.ShapeDtypeStruct((b, t, ltot), bf16),
        grid=(b, t // qb),
        in_specs=[pl.BlockSpec((1, qb, C_QI), lambda i, j: (i, j, OFF_QI // C_QI)),
                  pl.BlockSpec((1, qb, LANES), lambda i, j: (i, j, OFF_SMALL // LANES)),
                  pl.BlockSpec((1, ltot, IDX_DIM), lambda i, j: (i, 0, 0))],
        out_specs=pl.BlockSpec((1, qb, ltot), lambda i, j: (i, j, 0)),
        scratch_shapes=[pltpu.VMEM((qb, ltot), i32)],
        compiler_params=_cp(("arbitrary", "arbitrary")),
        name="dsa_index",
    )(proj, proj, ki16)


def _attn_kernel(qidx_ref, kidx_ref, flag_ref, case_ref, q_ref, k_ref, v_ref, mb_ref, bias_ref, o_ref,
                 q_scr, m_scr, l_scr, acc_scr):
    s_id = pl.program_id(1)
    qb = q_ref.shape[1]
    flags = flag_ref[s_id]

    @pl.when((flags & 1) != 0)
    def _():
        qv = q_ref[0].astype(bf16)
        for g in range(C_KV_HEADS):
            for r in range(C_GROUP):
                hd = g * C_GROUP + r
                q_scr[g, r * qb:(r + 1) * qb, :] = qv[:, hd * C_HEAD_DIM:(hd + 1) * C_HEAD_DIM]
        m_scr[...] = jnp.full(m_scr.shape, NEG_BIG, f32)
        l_scr[...] = jnp.zeros_like(l_scr)
        acc_scr[...] = jnp.zeros_like(acc_scr)

    mb = mb_ref[0].astype(f32)
    kblk = k_ref[0]
    vblk = v_ref[0]
    scale = C_HEAD_DIM ** -0.5
    for g in range(C_KV_HEADS):
        sl = slice(g * C_HEAD_DIM, (g + 1) * C_HEAD_DIM)
        s = _dot_nt(q_scr[g], kblk[:, sl]) * scale
        add = jnp.concatenate([mb + bias_ref[0, g * C_GROUP + r] for r in range(C_GROUP)], axis=0)
        s = s + add
        m_old = m_scr[g]
        m_new = jnp.maximum(m_old, jnp.max(s, axis=1, keepdims=True))
        alpha = jnp.exp(m_old - m_new)
        p = jnp.exp(s - m_new)
        l_scr[g] = alpha * l_scr[g] + jnp.sum(p, axis=1, keepdims=True)
        acc_scr[g] = alpha * acc_scr[g] + _dot(p.astype(bf16), vblk[:, sl])
        m_scr[g] = m_new

    @pl.when((flags & 2) != 0)
    def _():
        for g in range(C_KV_HEADS):
            o = acc_scr[g] / l_scr[g]
            for r in range(C_GROUP):
                hd = g * C_GROUP + r
                o_ref[0, :, hd * C_HEAD_DIM:(hd + 1) * C_HEAD_DIM] = o[r * qb:(r + 1) * qb, :].astype(bf16)


def _attn_schedule(nq, qb, qoff, offsets):
    qidx, kidx, flags, case = [], [], [], []
    zero_case = len(offsets)
    for j in range(nq):
        kmax = qoff + (j + 1) * qb
        nkb = -(-kmax // ATT_KB)
        for kb in range(nkb):
            off = kmax - BIAS_WIN - kb * ATT_KB
            qidx.append(j)
            kidx.append(kb)
            flags.append((1 if kb == 0 else 0) | (2 if kb == nkb - 1 else 0))
            case.append(offsets.index(off) if off in offsets else zero_case)
            assert (off in offsets) == (-BIAS_WIN < off < ATT_KB)
    return [jnp.asarray(np.asarray(a, np.int32)) for a in (qidx, kidx, flags, case)]


def _dsa_attend(proj, k16, v16, maskbias, tables, *, qb, qoff, offsets):
    b, t, _ = proj.shape
    nq = t // qb
    qidx, kidx, flags, case = _attn_schedule(nq, qb, qoff, offsets)
    ns = int(qidx.shape[0])
    rows = C_GROUP * qb
    grid_spec = pltpu.PrefetchScalarGridSpec(
        num_scalar_prefetch=4,
        grid=(b, ns),
        in_specs=[pl.BlockSpec((1, qb, C_Q), lambda i, s, qi, ki, fl, ca: (i, qi[s], OFF_Q // C_Q)),
                  pl.BlockSpec((1, ATT_KB, C_KV), lambda i, s, qi, ki, fl, ca: (i, ki[s], 0)),
                  pl.BlockSpec((1, ATT_KB, C_KV), lambda i, s, qi, ki, fl, ca: (i, ki[s], 0)),
                  pl.BlockSpec((1, qb, ATT_KB), lambda i, s, qi, ki, fl, ca: (i, qi[s], ki[s])),
                  pl.BlockSpec((1, C_HEADS, qb, ATT_KB), lambda i, s, qi, ki, fl, ca: (ca[s], 0, 0, 0))],
        out_specs=pl.BlockSpec((1, qb, C_WIDTH), lambda i, s, qi, ki, fl, ca: (i, qi[s], 0)),
        scratch_shapes=[pltpu.VMEM((C_KV_HEADS, rows, C_HEAD_DIM), bf16),
                        pltpu.VMEM((C_KV_HEADS, rows, 1), f32),
                        pltpu.VMEM((C_KV_HEADS, rows, 1), f32),
                        pltpu.VMEM((C_KV_HEADS, rows, C_HEAD_DIM), f32)])
    return pl.pallas_call(
        _attn_kernel,
        out_shape=jax.ShapeDtypeStruct((b, t, C_WIDTH), bf16),
        grid_spec=grid_spec,
        compiler_params=_cp(("arbitrary", "arbitrary")),
        name="dsa_attend",
    )(qidx, kidx, flags, case, proj, k16, v16, maskbias, tables)


def _window_offsets(nq, qb, qoff):
    offs = set()
    for j in range(nq):
        kmax = qoff + (j + 1) * qb
        for kb in range(-(-kmax // ATT_KB)):
            off = kmax - BIAS_WIN - kb * ATT_KB
            if -BIAS_WIN < off < ATT_KB:
                offs.add(off)
    return sorted(offs)


def _mixer_c(proj, k_all16, v_all16, ki_all16, rel_bias, *, qb, qoff, true_len):
    b, t, _ = proj.shape
    topk = min(TOPK_MAX, true_len // 4)
    assert BIAS_WIN - qb + 1 > MAX_DISTANCE
    offsets = _window_offsets(t // qb, qb, qoff)
    tables = _bias_tables(rel_bias, qb, offsets)
    maskbias = _dsa_mask(proj, ki_all16, qb=qb, qoff=qoff, topk=topk)
    return _dsa_attend(proj, k_all16, v_all16, maskbias, tables, qb=qb, qoff=qoff, offsets=offsets)


def _merge_kernel(x_ref, ada_ref, gt_ref, oa_ref, ob_ref, oc_ref, wa_ref, wb_ref, wc_ref, wo_ref, o_ref):
    bt, tt, d = x_ref.shape
    rows = bt * tt
    gates = jax.nn.sigmoid(gt_ref[...].reshape(rows, GATE_COLS))
    merged = (gates[:, 0:d] * _dot(oa_ref[...].reshape(rows, A_WIDTH), wa_ref[...])
              + gates[:, d:2 * d] * _dot(ob_ref[...].reshape(rows, SSM_INNER), wb_ref[...])
              + gates[:, 2 * d:3 * d] * _dot(oc_ref[...].reshape(rows, C_WIDTH), wc_ref[...]))
    mix = _dot(merged.astype(bf16), wo_ref[...]).reshape(bt, tt, d)
    o_ref[...] = x_ref[...] + ada_ref[...][:, 5:6, :] * mix


def _merge(x, ada, proj, o_a, o_b, o_c, wa16, wb16, wc16, wo16):
    b, t, d = x.shape
    bt, tt = _token_tile(b, t, rows=256)
    tok = lambda w: pl.BlockSpec((bt, tt, w), lambda i, j: (i, j, 0))
    full = lambda r, c: pl.BlockSpec((r, c), lambda i, j: (0, 0))
    return pl.pallas_call(
        _merge_kernel,
        out_shape=jax.ShapeDtypeStruct((b, t, d), f32),
        grid=(b // bt, t // tt),
        in_specs=[tok(d), pl.BlockSpec((bt, N_ADA, d), lambda i, j: (i, 0, 0)),
                  pl.BlockSpec((bt, tt, GATE_COLS), lambda i, j: (i, j, OFF_GATE // GATE_COLS)),
                  tok(A_WIDTH), tok(SSM_INNER), tok(C_WIDTH),
                  full(A_WIDTH, d), full(SSM_INNER, d), full(C_WIDTH, d), full(d, d)],
        out_specs=tok(d),
        compiler_params=_cp(("arbitrary", "arbitrary")),
        name="merge",
    )(x, ada, proj, o_a, o_b, o_c, wa16, wb16, wc16, wo16)


def _pad_keys(x, mult):
    pad = (-x.shape[1]) % mult
    if pad:
        x = jnp.concatenate([x, jnp.zeros((x.shape[0], pad) + x.shape[2:], x.dtype)], axis=1)
    return x


def _layer(x, ada, lp, rel_bias, past, g_final, last):
    b, t, d = x.shape
    x = _ffn(x, ada, lp['g_ffn1'], lp['w_ffn1_in'], lp['w_ffn1_out'], g_final, k0=0, final_norm=False)
    proj = _inproj(x, ada, lp['g_mix'], lp['w_proj'], k0=3)
    k_new = proj[:, :, OFF_K:OFF_K + C_KV]
    v_new = proj[:, :, OFF_V:OFF_V + C_KV]
    ki_new = proj[:, :, OFF_SMALL + SMALL_KI:OFF_SMALL + LANES]
    new_conv = proj[:, t - (CONV_W - 1):, OFF_XBC:OFF_XBC + CONV_DIM]

    o_a, va = _mixer_a(proj, lp['g_a'], lp['w_s'], lp['b_s'])

    if past is None:
        conv_init = jnp.zeros((b, CONV_W - 1, CONV_DIM), f32)
        h0 = jnp.zeros((b, SSM_HEADS, SSM_HEAD_DIM, SSM_STATE), f32)
    else:
        h0, conv_init = past[3], past[4]
    o_b, new_ssm = _mixer_b(proj, conv_init, h0, lp['w_conv'], lp['b_conv'], lp['dt_bias'], lp['a_log'],
                            lp['d_skip'], lp['g_ssm'])
    new_ssm = new_ssm.reshape(b, SSM_HEADS, SSM_HEAD_DIM, SSM_STATE)

    if past is None:
        k_all, v_all, ki_all = k_new, v_new, ki_new
        qb, qoff, true_len = Q_BLOCK, 0, t
    else:
        plen = past[0].shape[1]
        k_all = jnp.concatenate([past[0].reshape(b, plen, C_KV).astype(bf16), k_new.astype(bf16)], axis=1)
        v_all = jnp.concatenate([past[1].reshape(b, plen, C_KV).astype(bf16), v_new.astype(bf16)], axis=1)
        ki_all = jnp.concatenate([past[2].astype(bf16), ki_new.astype(bf16)], axis=1)
        qb, qoff, true_len = t, plen, plen + t
    o_c = _mixer_c(proj, _pad_keys(k_all.astype(bf16), ATT_KB), _pad_keys(v_all.astype(bf16), ATT_KB),
                   _pad_keys(ki_all.astype(bf16), ATT_KB), rel_bias, qb=qb, qoff=qoff, true_len=true_len)

    x = _merge(x, ada, proj, o_a, o_b, o_c, lp['w_br_a'], lp['w_br_b'], lp['w_br_c'], lp['w_out'])
    x = _ffn(x, ada, lp['g_ffn2'], lp['w_ffn2_in'], lp['w_ffn2_out'], g_final, k0=6, final_norm=last)
    states = (k_new.reshape(b, t, C_KV_HEADS, C_HEAD_DIM), v_new.reshape(b, t, C_KV_HEADS, C_HEAD_DIM),
              ki_new, new_ssm, new_conv, va)
    return x, states


def kernel(x_prompt, x_sample, c_prompt, c_sample, cache_k, cache_v, cache_kidx, state_ssm, state_conv, rel_bias, w_ada, b_ada, g_ffn1, w_ffn1_in, w_ffn1_out, g_mix, w_in, g_a, w_s, b_s, w_conv, b_conv, dt_bias, a_log, d_skip, g_ssm, w_br_a, w_br_b, w_br_c, w_out, g_ffn2, w_ffn2_in, w_ffn2_out, g_final):
    depth = w_ada.shape[0]
    bp, bs = c_prompt.shape[0], c_sample.shape[0]
    assert x_prompt.shape[1] % ATT_KB == 0 and x_prompt.shape[1] % Q_BLOCK == 0
    assert x_sample.shape[1] == CHUNK

    rows = -(-(bp + bs) // 8) * 8
    c_all = jnp.concatenate([c_prompt, c_sample, jnp.zeros((rows - bp - bs, D_MODEL), f32)], axis=0)
    ada_all = _ada_all(c_all, w_ada.astype(bf16), b_ada)
    ada_all = ada_all.reshape(depth, rows, N_ADA, D_MODEL)

    yp, ys = x_prompt, x_sample
    st_p, st_s = [], []
    for i in range(depth):
        lp = dict(g_ffn1=g_ffn1[i], w_ffn1_in=w_ffn1_in[i].astype(bf16), w_ffn1_out=w_ffn1_out[i].astype(bf16),
                  g_mix=g_mix[i], w_proj=_layout_w_in(w_in[i]).astype(bf16), g_a=g_a[i], w_s=w_s[i], b_s=b_s[i],
                  w_conv=w_conv[i], b_conv=b_conv[i], dt_bias=dt_bias[i], a_log=a_log[i], d_skip=d_skip[i],
                  g_ssm=g_ssm[i], w_br_a=w_br_a[i].astype(bf16), w_br_b=w_br_b[i].astype(bf16),
                  w_br_c=w_br_c[i].astype(bf16), w_out=w_out[i].astype(bf16), g_ffn2=g_ffn2[i],
                  w_ffn2_in=w_ffn2_in[i].astype(bf16), w_ffn2_out=w_ffn2_out[i].astype(bf16))
        last = i == depth - 1
        yp, sp = _layer(yp, ada_all[i, :bp], lp, rel_bias, None, g_final, last)
        ys, ss = _layer(ys, ada_all[i, bp:bp + bs], lp, rel_bias,
                        (cache_k[i], cache_v[i], cache_kidx[i], state_ssm[i], state_conv[i]), g_final, last)
        st_p.append(sp)
        st_s.append(ss)

    stack = lambda sts, k: jnp.stack([s[k] for s in sts])
    return (yp, ys,
            stack(st_p, 0), stack(st_p, 1), stack(st_p, 2), stack(st_p, 3), stack(st_p, 4),
            stack(st_s, 0), stack(st_s, 1), stack(st_s, 2), stack(st_s, 3), stack(st_s, 4), stack(st_s, 5))
```

```python
import functools
import math

import numpy as np
import jax
import jax.numpy as jnp
from jax import lax
from jax.experimental import pallas as pl
from jax.experimental.pallas import tpu as pltpu

f32 = jnp.float32
bf16 = jnp.bfloat16
i32 = jnp.int32

D_MODEL = 1024
CHUNK = 64
EPS = 1e-6
N_ADA = 9
D_FF = 2816
A_WIDTH = D_MODEL
A_GROUPS = 8
A_GROUP_DIM = A_WIDTH // A_GROUPS
A_CHUNK = 128
SSM_INNER = 2 * D_MODEL
SSM_HEAD_DIM = 64
SSM_HEADS = SSM_INNER // SSM_HEAD_DIM
SSM_GROUPS = 4
SSM_HPG = SSM_HEADS // SSM_GROUPS
SSM_STATE = 128
CONV_W = 4
CONV_DIM = SSM_INNER + 2 * SSM_GROUPS * SSM_STATE
C_HEADS = 8
C_KV_HEADS = 4
C_GROUP = C_HEADS // C_KV_HEADS
C_HEAD_DIM = 128
C_WIDTH = C_HEADS * C_HEAD_DIM
IDX_HEADS = 8
IDX_DIM = 64
TOPK_MAX = 256
Q_BLOCK = 128
N_BUCKETS = 32
MAX_DISTANCE = 128
A_COLS = 2 * A_WIDTH
B_COLS = SSM_INNER + CONV_DIM + SSM_HEADS
C_Q = C_HEADS * C_HEAD_DIM
C_KV = C_KV_HEADS * C_HEAD_DIM
C_QI = IDX_HEADS * IDX_DIM
GATE_COLS = 3 * D_MODEL

LANES = 128
VMEM_LIMIT = 56 * 1024 * 1024

OFF_GATE = 0
OFF_XBC = OFF_GATE + GATE_COLS
OFF_PA = OFF_XBC + CONV_DIM
OFF_Z = OFF_PA + A_COLS
OFF_Q = OFF_Z + SSM_INNER
OFF_K = OFF_Q + C_Q
OFF_V = OFF_K + C_KV
OFF_QI = OFF_V + C_KV
OFF_SMALL = OFF_QI + C_QI
SMALL_DT = 0
SMALL_WI = SSM_HEADS
SMALL_KI = LANES - IDX_DIM
PROJ_TILE = 1024
PROJ_W = ((OFF_SMALL + LANES + PROJ_TILE - 1) // PROJ_TILE) * PROJ_TILE

NEG_BIG = -1e30
NEG_INF_KEY = -2139095041
INT_MIN = -2147483648
INT_MAX = 2147483647
ATT_KB = 512
ATT_RC = 8
LOG2E = math.log2(math.e)
BIAS_WIN = 256
FAR_BUCKET = N_BUCKETS // 2 - 1


def _cp(sem, vmem=VMEM_LIMIT):
    return pltpu.CompilerParams(dimension_semantics=sem, vmem_limit_bytes=vmem)


def _rms(x, g):
    return x * lax.rsqrt(jnp.mean(x * x, axis=-1, keepdims=True) + EPS) * g


def _dot(a, b):
    return jnp.dot(a, b, preferred_element_type=f32)


def _dot_nt(a, b):
    return lax.dot_general(a, b, (((1,), (1,)), ((), ())), preferred_element_type=f32)


def _dot_tn(a, b):
    return lax.dot_general(a, b, (((0,), (0,)), ((), ())), preferred_element_type=f32)


def _dot_exact(a, b):
    return jnp.dot(a, b, precision=lax.Precision.HIGHEST, preferred_element_type=f32)


def _ada_kernel(c_ref, w_ref, b_ref, o_ref):
    s = jax.nn.silu(c_ref[...]).astype(bf16)
    o_ref[0] = _dot(s, w_ref[0]) + b_ref[0]


def _ada_all(c_all, w_ada16, b_ada):
    depth = w_ada16.shape[0]
    rows = c_all.shape[0]
    n_out = N_ADA * D_MODEL
    tn = 9 * LANES
    return pl.pallas_call(
        _ada_kernel,
        out_shape=jax.ShapeDtypeStruct((depth, rows, n_out), f32),
        grid=(depth, n_out // tn),
        in_specs=[pl.BlockSpec((rows, D_MODEL), lambda d, n: (0, 0)),
                  pl.BlockSpec((1, D_MODEL, tn), lambda d, n: (d, 0, n)),
                  pl.BlockSpec((1, 1, tn), lambda d, n: (d, 0, n))],
        out_specs=pl.BlockSpec((1, rows, tn), lambda d, n: (d, 0, n)),
        compiler_params=_cp(("arbitrary", "arbitrary")),
        name="ada",
    )(c_all, w_ada16, b_ada.reshape(depth, 1, n_out))


def _ffn_kernel(x_ref, ada_ref, g_ref, wg_ref, wu_ref, wo_ref, gf_ref, o_ref, h_scr, acc_scr, *, k0, final_norm):
    f = pl.program_id(2)
    bt, tt, d = x_ref.shape
    rows = bt * tt

    @pl.when(f == 0)
    def _():
        ada = ada_ref[...]
        h = _rms(x_ref[...], g_ref[...]) * (1.0 + ada[:, k0 + 1:k0 + 2, :]) + ada[:, k0:k0 + 1, :]
        h_scr[...] = h.reshape(rows, d).astype(bf16)
        acc_scr[...] = jnp.zeros_like(acc_scr)

    h = h_scr[...]
    gate = _dot(h, wg_ref[...])
    up = _dot(h, wu_ref[...])
    a = (jax.nn.silu(gate) * up).astype(bf16)
    acc_scr[...] += _dot(a, wo_ref[...])

    @pl.when(f == pl.num_programs(2) - 1)
    def _():
        ada = ada_ref[...]
        y = x_ref[...] + (0.5 * ada[:, k0 + 2:k0 + 3, :]) * acc_scr[...].reshape(bt, tt, d)
        if final_norm:
            y = _rms(y, gf_ref[...])
        o_ref[...] = y


def _token_tile(b, t, rows=512):
    if t >= rows:
        return 1, rows
    return min(b, rows // t), t


def _ffn(x, ada, g, w_in16, w_out16, g_final, *, k0, final_norm):
    b, t, d = x.shape
    bt, tt = _token_tile(b, t)
    tf = D_FF // 2
    nf = D_FF // tf
    kern = functools.partial(_ffn_kernel, k0=k0, final_norm=final_norm)
    return pl.pallas_call(
        kern,
        out_shape=jax.ShapeDtypeStruct((b, t, d), f32),
        grid=(b // bt, t // tt, nf),
        in_specs=[pl.BlockSpec((bt, tt, d), lambda i, j, f: (i, j, 0)),
                  pl.BlockSpec((bt, N_ADA, d), lambda i, j, f: (i, 0, 0)),
                  pl.BlockSpec((1, d), lambda i, j, f: (0, 0)),
                  pl.BlockSpec((d, tf), lambda i, j, f: (0, f)),
                  pl.BlockSpec((d, tf), lambda i, j, f: (0, f + nf)),
                  pl.BlockSpec((tf, d), lambda i, j, f: (f, 0)),
                  pl.BlockSpec((1, d), lambda i, j, f: (0, 0))],
        out_specs=pl.BlockSpec((bt, tt, d), lambda i, j, f: (i, j, 0)),
        scratch_shapes=[pltpu.VMEM((bt * tt, d), bf16), pltpu.VMEM((bt * tt, d), f32)],
        compiler_params=_cp(("arbitrary", "arbitrary", "arbitrary")),
        name="ffn",
    )(x, ada, g.reshape(1, d), w_in16, w_in16, w_out16, g_final.reshape(1, d))


def _inproj_kernel(x_ref, ada_ref, g_ref, w_ref, o_ref, h_scr, *, k0):
    n = pl.program_id(2)
    bt, tt, d = x_ref.shape

    @pl.when(n == 0)
    def _():
        ada = ada_ref[...]
        h = _rms(x_ref[...], g_ref[...]) * (1.0 + ada[:, k0 + 1:k0 + 2, :]) + ada[:, k0:k0 + 1, :]
        h_scr[...] = h.reshape(bt * tt, d).astype(bf16)

    o_ref[...] = _dot(h_scr[...], w_ref[...]).reshape(bt, tt, PROJ_TILE)


def _inproj(x, ada, g, w_proj16, *, k0):
    b, t, d = x.shape
    bt, tt = _token_tile(b, t)
    return pl.pallas_call(
        functools.partial(_inproj_kernel, k0=k0),
        out_shape=jax.ShapeDtypeStruct((b, t, PROJ_W), f32),
        grid=(b // bt, t // tt, PROJ_W // PROJ_TILE),
        in_specs=[pl.BlockSpec((bt, tt, d), lambda i, j, n: (i, j, 0)),
                  pl.BlockSpec((bt, N_ADA, d), lambda i, j, n: (i, 0, 0)),
                  pl.BlockSpec((1, d), lambda i, j, n: (0, 0)),
                  pl.BlockSpec((d, PROJ_TILE), lambda i, j, n: (0, n))],
        out_specs=pl.BlockSpec((bt, tt, PROJ_TILE), lambda i, j, n: (i, j, n)),
        scratch_shapes=[pltpu.VMEM((bt * tt, d), bf16)],
        compiler_params=_cp(("arbitrary", "arbitrary", "arbitrary")),
        name="inproj",
    )(x, ada, g.reshape(1, d), w_proj16)


def _layout_w_in(w_in):
    d = w_in.shape[0]
    o = 0
    pa = w_in[:, o:o + A_COLS]; o += A_COLS
    z = w_in[:, o:o + SSM_INNER]; o += SSM_INNER
    xbc = w_in[:, o:o + CONV_DIM]; o += CONV_DIM
    dt = w_in[:, o:o + SSM_HEADS]; o += SSM_HEADS
    q = w_in[:, o:o + C_Q]; o += C_Q
    k = w_in[:, o:o + C_KV]; o += C_KV
    v = w_in[:, o:o + C_KV]; o += C_KV
    qi = w_in[:, o:o + C_QI]; o += C_QI
    ki = w_in[:, o:o + IDX_DIM]; o += IDX_DIM
    wi = w_in[:, o:o + IDX_HEADS]; o += IDX_HEADS
    gates = w_in[:, o:o + GATE_COLS]
    small = jnp.concatenate([dt, wi, jnp.zeros((d, SMALL_KI - SMALL_WI - IDX_HEADS), w_in.dtype), ki], axis=1)
    pad = jnp.zeros((d, PROJ_W - OFF_SMALL - LANES), w_in.dtype)
    return jnp.concatenate([gates, xbc, pa, z, q, k, v, qi, small, pad], axis=1)


def _mixa_kernel(pa_ref, ga_ref, ws_ref, bs_ref, oa_ref, va_ref):
    n = pa_ref.shape[1]
    ge = jax.nn.gelu(pa_ref[0])
    u = ge[:, :A_WIDTH]
    v = _rms(ge[:, A_WIDTH:], ga_ref[...])
    va_ref[0] = v
    vb = v.astype(bf16)
    tri = lax.broadcasted_iota(i32, (n, n), 0) >= lax.broadcasted_iota(i32, (n, n), 1)
    bs = bs_ref[...]
    for g in range(A_GROUPS):
        sl = slice(g * A_GROUP_DIM, (g + 1) * A_GROUP_DIM)
        w = jnp.where(tri, ws_ref[g], 0.0).astype(bf16)
        mixed = _dot(w, vb[:, sl]) + bs[:, g:g + 1]
        oa_ref[0, :, sl] = (u[:, sl] * mixed).astype(bf16)


def _mixer_a(proj, g_a, w_s, b_s):
    b, t, _ = proj.shape
    n = min(t, A_CHUNK)
    ws = w_s[:, :n, :n]
    bs_t = b_s[:, :n].T
    return pl.pallas_call(
        _mixa_kernel,
        out_shape=(jax.ShapeDtypeStruct((b, t, A_WIDTH), bf16), jax.ShapeDtypeStruct((b, t, A_WIDTH), f32)),
        grid=(b, t // n),
        in_specs=[pl.BlockSpec((1, n, A_COLS), lambda i, j: (i, j, OFF_PA // A_COLS)),
                  pl.BlockSpec((1, A_WIDTH), lambda i, j: (0, 0)),
                  pl.BlockSpec((A_GROUPS, n, n), lambda i, j: (0, 0, 0)),
                  pl.BlockSpec((n, A_GROUPS), lambda i, j: (0, 0))],
        out_specs=(pl.BlockSpec((1, n, A_WIDTH), lambda i, j: (i, j, 0)),
                   pl.BlockSpec((1, n, A_WIDTH), lambda i, j: (i, j, 0))),
        compiler_params=_cp(("arbitrary", "arbitrary")),
        name="mixer_a",
    )(proj, g_a.reshape(1, A_WIDTH), ws, bs_t)


CONV_PAD = 8


def _ssd_kernel(z_ref, xbc_ref, sm_ref, cinit_ref, h0_ref, wc_ref, bc_ref, dtb_ref, alog_ref, dsk_ref, gs_ref,
                ex_ref, ob_ref, hs_ref, xpad, ht):
    c = pl.program_id(1)
    q = z_ref.shape[1]
    gw = SSM_HPG * SSM_HEAD_DIM
    tail = CONV_W - 1

    @pl.when(c == 0)
    def _():
        xpad[CONV_PAD - tail:CONV_PAD, :] = cinit_ref[0]
        for g in range(SSM_GROUPS):
            ht[g] = h0_ref[0, g * gw:(g + 1) * gw, :].T

    xpad[CONV_PAD:CONV_PAD + q, :] = xbc_ref[0]
    wc = wc_ref[...]
    conv = bc_ref[...] + wc[0:1, :] * xpad[pl.ds(CONV_PAD - 3, q), :]
    for k in range(1, CONV_W):
        conv = conv + wc[k:k + 1, :] * xpad[pl.ds(CONV_PAD - 3 + k, q), :]
    nxt = xpad[CONV_PAD + q - tail:CONV_PAD + q, :]
    xpad[CONV_PAD - tail:CONV_PAD, :] = nxt
    conv = jax.nn.silu(conv)
    xs = conv[:, :SSM_INNER]
    bm = conv[:, SSM_INNER:SSM_INNER + SSM_GROUPS * SSM_STATE]
    cm = conv[:, SSM_INNER + SSM_GROUPS * SSM_STATE:]

    lane = lax.broadcasted_iota(i32, (q, LANES), 1)
    head_lane = lane < SSM_HEADS
    dt = jnp.where(head_lane, jax.nn.softplus(sm_ref[0] + dtb_ref[...]), 0.0)
    da = dt * (-jnp.exp(alog_ref[...]))
    tri = lax.broadcasted_iota(i32, (q, q), 0) >= lax.broadcasted_iota(i32, (q, q), 1)
    acs = _dot_exact(tri.astype(f32), da)
    acs_t = acs.T
    dt_t = dt.T
    acs_last = acs[q - 1:q, :]
    e_acs = jnp.where(head_lane, jnp.exp(acs), 0.0)
    decay = jnp.where(head_lane, jnp.exp(acs_last - acs) * dt, 0.0)
    ex = ex_ref[...]
    e_acs_x = _dot_exact(e_acs, ex)
    decay_x = _dot_exact(decay, ex)
    e_last_x = e_acs_x[q - 1:q, :]

    xs16 = xs.astype(bf16)
    xdec16 = (xs * decay_x).astype(bf16)
    y_parts = []
    for g in range(SSM_GROUPS):
        bg = bm[:, g * SSM_STATE:(g + 1) * SSM_STATE].astype(bf16)
        cg = cm[:, g * SSM_STATE:(g + 1) * SSM_STATE].astype(bf16)
        cb = _dot_nt(cg, bg)
        htg = ht[g]
        y_inter = _dot(cg, htg.astype(bf16))
        intra = []
        for r in range(SSM_HPG):
            hd = g * SSM_HPG + r
            seg = acs[:, hd:hd + 1] - acs_t[hd:hd + 1, :]
            m = cb * jnp.exp(jnp.where(tri, seg, -jnp.inf)) * dt_t[hd:hd + 1, :]
            intra.append(_dot(m.astype(bf16), xs16[:, hd * SSM_HEAD_DIM:(hd + 1) * SSM_HEAD_DIM]))
        y_parts.append(jnp.concatenate(intra, axis=1) + y_inter * e_acs_x[:, g * gw:(g + 1) * gw])
        ht[g] = htg * e_last_x[:, g * gw:(g + 1) * gw] + _dot_tn(bg, xdec16[:, g * gw:(g + 1) * gw])
    y = jnp.concatenate(y_parts, axis=1) + dsk_ref[...] * xs
    y = _rms(y * jax.nn.silu(z_ref[0]), gs_ref[...])
    ob_ref[0] = y.astype(bf16)

    @pl.when(c == pl.num_programs(1) - 1)
    def _():
        for g in range(SSM_GROUPS):
            hs_ref[0, g * gw:(g + 1) * gw, :] = ht[g].T


def _mixer_b(proj, conv_init, h0, w_conv, b_conv, dt_bias, a_log, d_skip, g_ssm):
    b, t, _ = proj.shape
    q = CHUNK
    pad_row = lambda v: jnp.zeros((1, LANES), f32).at[0, :SSM_HEADS].set(v)
    expand = (jnp.arange(LANES)[:, None] == (jnp.arange(SSM_INNER) // SSM_HEAD_DIM)[None, :]).astype(f32)
    hrows = SSM_HEADS * SSM_HEAD_DIM
    return pl.pallas_call(
        _ssd_kernel,
        out_shape=(jax.ShapeDtypeStruct((b, t, SSM_INNER), bf16),
                   jax.ShapeDtypeStruct((b, hrows, SSM_STATE), f32)),
        grid=(b, t // q),
        in_specs=[pl.BlockSpec((1, q, SSM_INNER), lambda i, c: (i, c, OFF_Z // SSM_INNER)),
                  pl.BlockSpec((1, q, CONV_DIM), lambda i, c: (i, c, OFF_XBC // CONV_DIM)),
                  pl.BlockSpec((1, q, LANES), lambda i, c: (i, c, OFF_SMALL // LANES)),
                  pl.BlockSpec((1, CONV_W - 1, CONV_DIM), lambda i, c: (i, 0, 0)),
                  pl.BlockSpec((1, hrows, SSM_STATE), lambda i, c: (i, 0, 0)),
                  pl.BlockSpec((CONV_W, CONV_DIM), lambda i, c: (0, 0)),
                  pl.BlockSpec((1, CONV_DIM), lambda i, c: (0, 0)),
                  pl.BlockSpec((1, LANES), lambda i, c: (0, 0)),
                  pl.BlockSpec((1, LANES), lambda i, c: (0, 0)),
                  pl.BlockSpec((1, SSM_INNER), lambda i, c: (0, 0)),
                  pl.BlockSpec((1, SSM_INNER), lambda i, c: (0, 0)),
                  pl.BlockSpec((LANES, SSM_INNER), lambda i, c: (0, 0))],
        out_specs=(pl.BlockSpec((1, q, SSM_INNER), lambda i, c: (i, c, 0)),
                   pl.BlockSpec((1, hrows, SSM_STATE), lambda i, c: (i, 0, 0))),
        scratch_shapes=[pltpu.VMEM((CONV_PAD + q, CONV_DIM), f32),
                        pltpu.VMEM((SSM_GROUPS, SSM_STATE, SSM_HPG * SSM_HEAD_DIM), f32)],
        compiler_params=_cp(("arbitrary", "arbitrary")),
        name="mixer_b",
    )(proj, proj, proj, conv_init, h0.reshape(b, hrows, SSM_STATE), w_conv, b_conv.reshape(1, CONV_DIM),
      pad_row(dt_bias), pad_row(a_log), jnp.repeat(d_skip, SSM_HEAD_DIM).reshape(1, SSM_INNER),
      g_ssm.reshape(1, SSM_INNER), expand)


def _t5_bucket(rel):
    nb = N_BUCKETS // 2
    max_exact = nb // 2
    n = jnp.abs(rel)
    nf = jnp.maximum(n, 1).astype(jnp.float32)
    large = max_exact + (jnp.log(nf / max_exact) / math.log(MAX_DISTANCE / max_exact)
                         * (nb - max_exact)).astype(jnp.int32)
    large = jnp.minimum(large, nb - 1)
    return jnp.where(rel > 0, nb, 0) + jnp.where(n < max_exact, n, large)


def _bias_kernel(tab_ref, bk_ref, o_ref):
    h = pl.program_id(1)
    bk = bk_ref[0]
    far = tab_ref[FAR_BUCKET, h]
    acc = jnp.zeros(bk.shape, f32)
    for bucket in range(N_BUCKETS):
        acc = jnp.where(bk == bucket, tab_ref[bucket, h] - far, acc)
    o_ref[0, 0] = acc * LOG2E


def _bias_tables(rel_bias, qb, offsets):
    t = np.arange(qb)[:, None]
    cases = []
    for off in offsets:
        w = np.arange(ATT_KB)[None, :] - off
        valid = (w >= 0) & (w < BIAS_WIN)
        rel = w - BIAS_WIN + qb - t
        cases.append(jnp.where(jnp.asarray(valid), _t5_bucket(jnp.asarray(rel, dtype=jnp.int32)), -1))
    cases.append(jnp.full((qb, ATT_KB), -1, jnp.int32))
    buckets = jnp.stack(cases).astype(i32)
    nc = buckets.shape[0]
    return pl.pallas_call(
        _bias_kernel,
        out_shape=jax.ShapeDtypeStruct((nc, C_HEADS, qb, ATT_KB), f32),
        grid=(nc, C_HEADS),
        in_specs=[pl.BlockSpec(memory_space=pltpu.SMEM),
                  pl.BlockSpec((1, qb, ATT_KB), lambda c, h: (c, 0, 0))],
        out_specs=pl.BlockSpec((1, 1, qb, ATT_KB), lambda c, h: (c, h, 0, 0)),
        compiler_params=_cp(("arbitrary", "arbitrary")),
        name="t5_bias",
    )(rel_bias, buckets)


def _index_kernel(qi_ref, sm_ref, ki_ref, o_ref, keys, *, qoff, topk, idx_bits):
    j = pl.program_id(1)
    qb = qi_ref.shape[1]
    ltot = ki_ref.shape[1]
    kb = ATT_KB
    kmax = qoff + (j + 1) * qb
    nkb = lax.div(kmax + (kb - 1), kb)
    sub = kb // LANES

    qv = qi_ref[0].astype(bf16)
    stacked = jnp.concatenate([qv[:, h * IDX_DIM:(h + 1) * IDX_DIM] for h in range(IDX_HEADS)], axis=0)
    w = sm_ref[0][:, SMALL_WI:SMALL_WI + IDX_HEADS] * (IDX_HEADS ** -0.5) * (IDX_DIM ** -0.5)
    wb = [jnp.broadcast_to(w[:, h:h + 1], (qb, kb)) for h in range(IDX_HEADS)]
    qchunk = lax.shift_right_arithmetic(qoff + j * qb + lax.broadcasted_iota(i32, (qb, 1), 0), 6)
    lane_pos = lax.broadcasted_iota(i32, (1, kb), 1)

    def score_body(i, carry):
        off = pl.multiple_of(i * kb, kb)
        s = _dot_nt(stacked, ki_ref[0, pl.ds(off, kb), :])
        acc = jnp.maximum(s[0:qb], 0.0) * wb[0]
        for h in range(1, IDX_HEADS):
            acc = acc + jnp.maximum(s[h * qb:(h + 1) * qb], 0.0) * wb[h]
        adm = lax.shift_right_arithmetic(off + lane_pos, 6) <= qchunk
        x = jnp.where(adm, jnp.where(acc == 0.0, 0.0, acc), -jnp.inf)
        bits = lax.bitcast_convert_type(x, i32)
        key = bits ^ (lax.shift_right_arithmetic(bits, 31) & INT_MAX)
        keys[:, pl.ds(off, kb)] = key
        cm = list(carry)
        for c in range(sub):
            cm[c % 2] = jnp.maximum(cm[c % 2], key[:, c * LANES:(c + 1) * LANES])
        return tuple(cm)

    cm_init = jnp.full((qb, LANES), NEG_INF_KEY, i32)
    cm_a, cm_b = lax.fori_loop(0, nkb, score_body, (cm_init, cm_init))

    def count(pred):
        def body(i, acc):
            off = pl.multiple_of(i * kb, kb)
            blk = keys[:, pl.ds(off, kb)]
            m = pred(blk, off).astype(i32)
            for c in range(sub):
                acc = acc + m[:, c * LANES:(c + 1) * LANES]
            return acc
        acc = lax.fori_loop(0, nkb, body, jnp.zeros((qb, LANES), i32))
        return jnp.sum(acc, axis=1, keepdims=True)

    def count_ge(thr):
        tb = jnp.broadcast_to(thr, (qb, kb))
        return count(lambda blk, off: blk >= tb)

    assert topk <= 2 * LANES
    lo0 = jnp.min(jnp.minimum(cm_a, cm_b), axis=1, keepdims=True)
    hi0 = jnp.max(jnp.maximum(cm_a, cm_b), axis=1, keepdims=True) + 1
    lo0 = jnp.where(hi0 - 1 > lo0, lo0, hi0 - 1)

    def bis_cond(c):
        it, open_rows, _, _ = c
        return (open_rows > 0) & (it < 40)

    def bis_body(c):
        it, _, lo, hi = c
        mid = lax.shift_right_arithmetic(lo, 1) + lax.shift_right_arithmetic(hi, 1) + (lo & hi & 1)
        n = count_ge(mid)
        ge = n >= topk
        lo_n = jnp.where(ge, mid, lo)
        hi_n = jnp.where(n == topk, mid + 1, jnp.where(ge, hi, mid))
        open_rows = jnp.max((lo_n < hi_n - 1).astype(i32))
        return it + 1, open_rows, lo_n, hi_n

    open0 = jnp.max((lo0 < hi0 - 1).astype(i32))
    _, _, thr, _ = lax.while_loop(bis_cond, bis_body, (jnp.int32(0), open0, lo0, hi0))
    n_ge = count_ge(thr)
    n_gt = count_ge(thr + 1)
    tie = (n_ge > topk) & (thr > NEG_INF_KEY)
    need = topk - n_gt

    def tie_cut(_):
        tb = jnp.broadcast_to(thr, (qb, kb))

        def body(_, c):
            lo, hi = c
            mid = lax.shift_right_arithmetic(lo + hi, 1)
            mb = jnp.broadcast_to(mid, (qb, kb))
            n = count(lambda blk, off: (blk == tb) & ((off + lane_pos) <= mb))
            ok = n >= need
            return jnp.where(ok, lo, mid), jnp.where(ok, mid, hi)

        lo_i = jnp.full((qb, 1), -1, i32)
        hi_i = jnp.full((qb, 1), 1, i32) * (nkb * kb - 1)
        _, cut = lax.fori_loop(0, idx_bits, body, (lo_i, hi_i))
        return jnp.where(tie, cut, INT_MAX)

    any_tie = jnp.max(tie.astype(i32)) > 0
    cut = lax.cond(any_tie, tie_cut, lambda _: jnp.full((qb, 1), INT_MAX, i32), 0)

    tb = jnp.broadcast_to(thr, (qb, kb))
    cb = jnp.broadcast_to(cut, (qb, kb))

    def out_body(i, carry):
        off = pl.multiple_of(i * kb, kb)
        blk = keys[:, pl.ds(off, kb)]
        sel = ((blk > tb) | ((blk == tb) & ((off + lane_pos) <= cb))) & (blk > NEG_INF_KEY)
        o_ref[0, :, pl.ds(off, kb)] = jnp.where(sel, 0.0, NEG_BIG).astype(bf16)
        return carry

    lax.fori_loop(0, nkb, out_body, 0)

    def fill_body(i, carry):
        off = pl.multiple_of(i * kb, kb)
        o_ref[0, :, pl.ds(off, kb)] = jnp.full((qb, kb), NEG_BIG, bf16)
        return carry

    lax.fori_loop(nkb, ltot // kb, fill_body, 0)


def _dsa_mask(proj, ki16, *, qb, qoff, topk):
    b, t, _ = proj.shape
    ltot = ki16.shape[1]
    idx_bits = int(math.ceil(math.log2(ltot))) + 1
    kern = functools.partial(_index_kernel, qoff=qoff, topk=topk, idx_bits=idx_bits)
    return pl.pallas_call(
        kern,
        out_shape=jax.ShapeDtypeStruct((b, t, ltot), bf16),
        grid=(b, t // qb),
        in_specs=[pl.BlockSpec((1, qb, C_QI), lambda i, j: (i, j, OFF_QI // C_QI)),
                  pl.BlockSpec((1, qb, LANES), lambda i, j: (i, j, OFF_SMALL // LANES)),
                  pl.BlockSpec((1, ltot, IDX_DIM), lambda i, j: (i, 0, 0))],
        out_specs=pl.BlockSpec((1, qb, ltot), lambda i, j: (i, j, 0)),
        scratch_shapes=[pltpu.VMEM((qb, ltot), i32)],
        compiler_params=_cp(("arbitrary", "arbitrary")),
        name="dsa_index",
    )(proj, proj, ki16)


def _attn_kernel(qidx_ref, kidx_ref, flag_ref, case_ref, q_ref, k_ref, v_ref, mb_ref, bias_ref, o_ref,
                 q_scr, m_scr, l_scr, acc_scr, s_scr, p_scr, a_scr, mb_scr):
    s_id = pl.program_id(1)
    qb = q_ref.shape[1]
    kb = k_ref.shape[1]
    rows = C_GROUP * qb
    nt = kb // LANES
    flags = flag_ref[s_id]

    @pl.when((flags & 1) != 0)
    def _():
        qv = (q_ref[0] * (C_HEAD_DIM ** -0.5 * LOG2E)).astype(bf16)
        for g in range(C_KV_HEADS):
            for r in range(C_GROUP):
                hd = g * C_GROUP + r
                q_scr[g, r * qb:(r + 1) * qb, :] = qv[:, hd * C_HEAD_DIM:(hd + 1) * C_HEAD_DIM]
        m_scr[...] = jnp.full(m_scr.shape, NEG_BIG, f32)
        l_scr[...] = jnp.zeros_like(l_scr)
        acc_scr[...] = jnp.zeros_like(acc_scr)

    def visit(with_bias):
        kblk = k_ref[0]
        vblk = v_ref[0]
        mb_scr[...] = mb_ref[0].astype(f32)
        for g in range(C_KV_HEADS):
            sl = slice(g * C_HEAD_DIM, (g + 1) * C_HEAD_DIM)
            s_scr[g] = _dot_nt(q_scr[g], kblk[:, sl])
            for c in range(rows // ATT_RC):
                r0 = c * ATT_RC
                q0 = r0 % qb
                rs = slice(r0, r0 + ATT_RC)
                s = s_scr[g, rs, :] + mb_scr[q0:q0 + ATT_RC, :]
                if with_bias:
                    s = s + bias_ref[0, g * C_GROUP + r0 // qb, q0:q0 + ATT_RC, :]
                t = s[:, 0:LANES]
                for i in range(1, nt):
                    t = jnp.maximum(t, s[:, i * LANES:(i + 1) * LANES])
                m_old = m_scr[g, rs, :]
                m_new = jnp.maximum(m_old, jnp.max(t, axis=1, keepdims=True))
                alpha = jnp.exp2(m_old - m_new)
                p = jnp.exp2(s - jnp.concatenate([m_new] * nt, axis=1))
                psum = p[:, 0:LANES]
                for i in range(1, nt):
                    psum = psum + p[:, i * LANES:(i + 1) * LANES]
                l_scr[g, rs, :] = alpha * l_scr[g, rs, :] + psum
                m_scr[g, rs, :] = m_new
                a_scr[g, rs, :] = alpha
                p_scr[g, rs, :] = p.astype(bf16)
            acc_scr[g] = a_scr[g] * acc_scr[g] + _dot(p_scr[g], vblk[:, sl])

    @pl.when((flags & 4) != 0)
    def _():
        visit(True)

    @pl.when((flags & 4) == 0)
    def _():
        visit(False)

    @pl.when((flags & 2) != 0)
    def _():
        for g in range(C_KV_HEADS):
            o = acc_scr[g] / jnp.sum(l_scr[g], axis=1, keepdims=True)
            for r in range(C_GROUP):
                hd = g * C_GROUP + r
                o_ref[0, :, hd * C_HEAD_DIM:(hd + 1) * C_HEAD_DIM] = o[r * qb:(r + 1) * qb, :].astype(bf16)


def _attn_schedule(nq, qb, qoff, offsets):
    qidx, kidx, flags, case = [], [], [], []
    zero_case = len(offsets)
    for j in range(nq):
        kmax = qoff + (j + 1) * qb
        nkb = -(-kmax // ATT_KB)
        for kb in range(nkb):
            off = kmax - BIAS_WIN - kb * ATT_KB
            qidx.append(j)
            kidx.append(kb)
            near = off in offsets
            flags.append((1 if kb == 0 else 0) | (2 if kb == nkb - 1 else 0) | (4 if near else 0))
            case.append(offsets.index(off) if near else zero_case)
            assert near == (-BIAS_WIN < off < ATT_KB)
    return [jnp.asarray(np.asarray(a, np.int32)) for a in (qidx, kidx, flags, case)]


def _dsa_attend(proj, k16, v16, maskbias, tables, *, qb, qoff, offsets):
    b, t, _ = proj.shape
    nq = t // qb
    qidx, kidx, flags, case = _attn_schedule(nq, qb, qoff, offsets)
    ns = int(qidx.shape[0])
    rows = C_GROUP * qb
    grid_spec = pltpu.PrefetchScalarGridSpec(
        num_scalar_prefetch=4,
        grid=(b, ns),
        in_specs=[pl.BlockSpec((1, qb, C_Q), lambda i, s, qi, ki, fl, ca: (i, qi[s], OFF_Q // C_Q)),
                  pl.BlockSpec((1, ATT_KB, C_KV), lambda i, s, qi, ki, fl, ca: (i, ki[s], 0)),
                  pl.BlockSpec((1, ATT_KB, C_KV), lambda i, s, qi, ki, fl, ca: (i, ki[s], 0)),
                  pl.BlockSpec((1, qb, ATT_KB), lambda i, s, qi, ki, fl, ca: (i, qi[s], ki[s])),
                  pl.BlockSpec((1, C_HEADS, qb, ATT_KB), lambda i, s, qi, ki, fl, ca: (ca[s], 0, 0, 0))],
        out_specs=pl.BlockSpec((1, qb, C_WIDTH), lambda i, s, qi, ki, fl, ca: (i, qi[s], 0)),
        scratch_shapes=[pltpu.VMEM((C_KV_HEADS, rows, C_HEAD_DIM), bf16),
                        pltpu.VMEM((C_KV_HEADS, rows, LANES), f32),
                        pltpu.VMEM((C_KV_HEADS, rows, LANES), f32),
                        pltpu.VMEM((C_KV_HEADS, rows, C_HEAD_DIM), f32),
                        pltpu.VMEM((C_KV_HEADS, rows, ATT_KB), f32),
                        pltpu.VMEM((C_KV_HEADS, rows, ATT_KB), bf16),
                        pltpu.VMEM((C_KV_HEADS, rows, LANES), f32),
                        pltpu.VMEM((qb, ATT_KB), f32)])
    return pl.pallas_call(
        _attn_kernel,
        out_shape=jax.ShapeDtypeStruct((b, t, C_WIDTH), bf16),
        grid_spec=grid_spec,
        compiler_params=_cp(("arbitrary", "arbitrary")),
        name="dsa_attend",
    )(qidx, kidx, flags, case, proj, k16, v16, maskbias, tables)


def _window_offsets(nq, qb, qoff):
    offs = set()
    for j in range(nq):
        kmax = qoff + (j + 1) * qb
        for kb in range(-(-kmax // ATT_KB)):
            off = kmax - BIAS_WIN - kb * ATT_KB
            if -BIAS_WIN < off < ATT_KB:
                offs.add(off)
    return sorted(offs)


def _mixer_c(proj, k_all16, v_all16, ki_all16, rel_bias, *, qb, qoff, true_len):
    b, t, _ = proj.shape
    topk = min(TOPK_MAX, true_len // 4)
    assert BIAS_WIN - qb + 1 > MAX_DISTANCE
    offsets = _window_offsets(t // qb, qb, qoff)
    tables = _bias_tables(rel_bias, qb, offsets)
    maskbias = _dsa_mask(proj, ki_all16, qb=qb, qoff=qoff, topk=topk)
    return _dsa_attend(proj, k_all16, v_all16, maskbias, tables, qb=qb, qoff=qoff, offsets=offsets)


def _merge_kernel(x_ref, ada_ref, gt_ref, oa_ref, ob_ref, oc_ref, wa_ref, wb_ref, wc_ref, wo_ref, o_ref):
    bt, tt, d = x_ref.shape
    rows = bt * tt
    gates = jax.nn.sigmoid(gt_ref[...].reshape(rows, GATE_COLS))
    merged = (gates[:, 0:d] * _dot(oa_ref[...].reshape(rows, A_WIDTH), wa_ref[...])
              + gates[:, d:2 * d] * _dot(ob_ref[...].reshape(rows, SSM_INNER), wb_ref[...])
              + gates[:, 2 * d:3 * d] * _dot(oc_ref[...].reshape(rows, C_WIDTH), wc_ref[...]))
    mix = _dot(merged.astype(bf16), wo_ref[...]).reshape(bt, tt, d)
    o_ref[...] = x_ref[...] + ada_ref[...][:, 5:6, :] * mix


def _merge(x, ada, proj, o_a, o_b, o_c, wa16, wb16, wc16, wo16):
    b, t, d = x.shape
    bt, tt = _token_tile(b, t, rows=256)
    tok = lambda w: pl.BlockSpec((bt, tt, w), lambda i, j: (i, j, 0))
    full = lambda r, c: pl.BlockSpec((r, c), lambda i, j: (0, 0))
    return pl.pallas_call(
        _merge_kernel,
        out_shape=jax.ShapeDtypeStruct((b, t, d), f32),
        grid=(b // bt, t // tt),
        in_specs=[tok(d), pl.BlockSpec((bt, N_ADA, d), lambda i, j: (i, 0, 0)),
                  pl.BlockSpec((bt, tt, GATE_COLS), lambda i, j: (i, j, OFF_GATE // GATE_COLS)),
                  tok(A_WIDTH), tok(SSM_INNER), tok(C_WIDTH),
                  full(A_WIDTH, d), full(SSM_INNER, d), full(C_WIDTH, d), full(d, d)],
        out_specs=tok(d),
        compiler_params=_cp(("arbitrary", "arbitrary")),
        name="merge",
    )(x, ada, proj, o_a, o_b, o_c, wa16, wb16, wc16, wo16)


def _pad_keys(x, mult):
    pad = (-x.shape[1]) % mult
    if pad:
        x = jnp.concatenate([x, jnp.zeros((x.shape[0], pad) + x.shape[2:], x.dtype)], axis=1)
    return x


def _layer(x, ada, lp, rel_bias, past, g_final, last):
    b, t, d = x.shape
    x = _ffn(x, ada, lp['g_ffn1'], lp['w_ffn1_in'], lp['w_ffn1_out'], g_final, k0=0, final_norm=False)
    proj = _inproj(x, ada, lp['g_mix'], lp['w_proj'], k0=3)
    k_new = proj[:, :, OFF_K:OFF_K + C_KV]
    v_new = proj[:, :, OFF_V:OFF_V + C_KV]
    ki_new = proj[:, :, OFF_SMALL + SMALL_KI:OFF_SMALL + LANES]
    new_conv = proj[:, t - (CONV_W - 1):, OFF_XBC:OFF_XBC + CONV_DIM]

    o_a, va = _mixer_a(proj, lp['g_a'], lp['w_s'], lp['b_s'])

    if past is None:
        conv_init = jnp.zeros((b, CONV_W - 1, CONV_DIM), f32)
        h0 = jnp.zeros((b, SSM_HEADS, SSM_HEAD_DIM, SSM_STATE), f32)
    else:
        h0, conv_init = past[3], past[4]
    o_b, new_ssm = _mixer_b(proj, conv_init, h0, lp['w_conv'], lp['b_conv'], lp['dt_bias'], lp['a_log'],
                            lp['d_skip'], lp['g_ssm'])
    new_ssm = new_ssm.reshape(b, SSM_HEADS, SSM_HEAD_DIM, SSM_STATE)

    if past is None:
        k_all, v_all, ki_all = k_new, v_new, ki_new
        qb, qoff, true_len = Q_BLOCK, 0, t
    else:
        plen = past[0].shape[1]
        k_all = jnp.concatenate([past[0].reshape(b, plen, C_KV).astype(bf16), k_new.astype(bf16)], axis=1)
        v_all = jnp.concatenate([past[1].reshape(b, plen, C_KV).astype(bf16), v_new.astype(bf16)], axis=1)
        ki_all = jnp.concatenate([past[2].astype(bf16), ki_new.astype(bf16)], axis=1)
        qb, qoff, true_len = t, plen, plen + t
    o_c = _mixer_c(proj, _pad_keys(k_all.astype(bf16), ATT_KB), _pad_keys(v_all.astype(bf16), ATT_KB),
                   _pad_keys(ki_all.astype(bf16), ATT_KB), rel_bias, qb=qb, qoff=qoff, true_len=true_len)

    x = _merge(x, ada, proj, o_a, o_b, o_c, lp['w_br_a'], lp['w_br_b'], lp['w_br_c'], lp['w_out'])
    x = _ffn(x, ada, lp['g_ffn2'], lp['w_ffn2_in'], lp['w_ffn2_out'], g_final, k0=6, final_norm=last)
    states = (k_new.reshape(b, t, C_KV_HEADS, C_HEAD_DIM), v_new.reshape(b, t, C_KV_HEADS, C_HEAD_DIM),
              ki_new, new_ssm, new_conv, va)
    return x, states


def kernel(x_prompt, x_sample, c_prompt, c_sample, cache_k, cache_v, cache_kidx, state_ssm, state_conv, rel_bias, w_ada, b_ada, g_ffn1, w_ffn1_in, w_ffn1_out, g_mix, w_in, g_a, w_s, b_s, w_conv, b_conv, dt_bias, a_log, d_skip, g_ssm, w_br_a, w_br_b, w_br_c, w_out, g_ffn2, w_ffn2_in, w_ffn2_out, g_final):
    depth = w_ada.shape[0]
    bp, bs = c_prompt.shape[0], c_sample.shape[0]
    assert x_prompt.shape[1] % ATT_KB == 0 and x_prompt.shape[1] % Q_BLOCK == 0
    assert x_sample.shape[1] == CHUNK

    rows = -(-(bp + bs) // 8) * 8
    c_all = jnp.concatenate([c_prompt, c_sample, jnp.zeros((rows - bp - bs, D_MODEL), f32)], axis=0)
    ada_all = _ada_all(c_all, w_ada.astype(bf16), b_ada)
    ada_all = ada_all.reshape(depth, rows, N_ADA, D_MODEL)

    yp, ys = x_prompt, x_sample
    st_p, st_s = [], []
    for i in range(depth):
        lp = dict(g_ffn1=g_ffn1[i], w_ffn1_in=w_ffn1_in[i].astype(bf16), w_ffn1_out=w_ffn1_out[i].astype(bf16),
                  g_mix=g_mix[i], w_proj=_layout_w_in(w_in[i]).astype(bf16), g_a=g_a[i], w_s=w_s[i], b_s=b_s[i],
                  w_conv=w_conv[i], b_conv=b_conv[i], dt_bias=dt_bias[i], a_log=a_log[i], d_skip=d_skip[i],
                  g_ssm=g_ssm[i], w_br_a=w_br_a[i].astype(bf16), w_br_b=w_br_b[i].astype(bf16),
                  w_br_c=w_br_c[i].astype(bf16), w_out=w_out[i].astype(bf16), g_ffn2=g_ffn2[i],
                  w_ffn2_in=w_ffn2_in[i].astype(bf16), w_ffn2_out=w_ffn2_out[i].astype(bf16))
        last = i == depth - 1
        yp, sp = _layer(yp, ada_all[i, :bp], lp, rel_bias, None, g_final, last)
        ys, ss = _layer(ys, ada_all[i, bp:bp + bs], lp, rel_bias,
                        (cache_k[i], cache_v[i], cache_kidx[i], state_ssm[i], state_conv[i]), g_final, last)
        st_p.append(sp)
        st_s.append(ss)

    stack = lambda sts, k: jnp.stack([s[k] for s in sts])
    return (yp, ys,
            stack(st_p, 0), stack(st_p, 1), stack(st_p, 2), stack(st_p, 3), stack(st_p, 4),
            stack(st_s, 0), stack(st_s, 1), stack(st_s, 2), stack(st_s, 3), stack(st_s, 4), stack(st_s, 5))
```

```python
import functools
import math

import numpy as np
import jax
import jax.numpy as jnp
from jax import lax
from jax.experimental import pallas as pl
from jax.experimental.pallas import tpu as pltpu

f32 = jnp.float32
bf16 = jnp.bfloat16
i32 = jnp.int32

D_MODEL = 1024
CHUNK = 64
EPS = 1e-6
N_ADA = 9
D_FF = 2816
A_WIDTH = D_MODEL
A_GROUPS = 8
A_GROUP_DIM = A_WIDTH // A_GROUPS
A_CHUNK = 128
SSM_INNER = 2 * D_MODEL
SSM_HEAD_DIM = 64
SSM_HEADS = SSM_INNER // SSM_HEAD_DIM
SSM_GROUPS = 4
SSM_HPG = SSM_HEADS // SSM_GROUPS
SSM_STATE = 128
CONV_W = 4
CONV_DIM = SSM_INNER + 2 * SSM_GROUPS * SSM_STATE
C_HEADS = 8
C_KV_HEADS = 4
C_GROUP = C_HEADS // C_KV_HEADS
C_HEAD_DIM = 128
C_WIDTH = C_HEADS * C_HEAD_DIM
IDX_HEADS = 8
IDX_DIM = 64
TOPK_MAX = 256
Q_BLOCK = 128
N_BUCKETS = 32
MAX_DISTANCE = 128
A_COLS = 2 * A_WIDTH
B_COLS = SSM_INNER + CONV_DIM + SSM_HEADS
C_Q = C_HEADS * C_HEAD_DIM
C_KV = C_KV_HEADS * C_HEAD_DIM
C_QI = IDX_HEADS * IDX_DIM
GATE_COLS = 3 * D_MODEL

LANES = 128
VMEM_LIMIT = 56 * 1024 * 1024

OFF_GATE = 0
OFF_XBC = OFF_GATE + GATE_COLS
OFF_PA = OFF_XBC + CONV_DIM
OFF_Z = OFF_PA + A_COLS
OFF_Q = OFF_Z + SSM_INNER
OFF_K = OFF_Q + C_Q
OFF_V = OFF_K + C_KV
OFF_QI = OFF_V + C_KV
OFF_SMALL = OFF_QI + C_QI
SMALL_DT = 0
SMALL_WI = SSM_HEADS
SMALL_KI = LANES - IDX_DIM
PROJ_TILE = 1024
PROJ_W = ((OFF_SMALL + LANES + PROJ_TILE - 1) // PROJ_TILE) * PROJ_TILE

NEG_BIG = -1e30
NEG_INF_KEY = -2139095041
INT_MIN = -2147483648
INT_MAX = 2147483647
ATT_KB = 1024
IDX_KB = 512
ATT_RC = 8
LOG2E = math.log2(math.e)
BIAS_WIN = 256
FAR_BUCKET = N_BUCKETS // 2 - 1


def _cp(sem, vmem=VMEM_LIMIT):
    return pltpu.CompilerParams(dimension_semantics=sem, vmem_limit_bytes=vmem)


def _rms(x, g):
    return x * lax.rsqrt(jnp.mean(x * x, axis=-1, keepdims=True) + EPS) * g


def _dot(a, b):
    return jnp.dot(a, b, preferred_element_type=f32)


def _dot_nt(a, b):
    return lax.dot_general(a, b, (((1,), (1,)), ((), ())), preferred_element_type=f32)


def _dot_tn(a, b):
    return lax.dot_general(a, b, (((0,), (0,)), ((), ())), preferred_element_type=f32)


def _dot_exact(a, b):
    return jnp.dot(a, b, precision=lax.Precision.HIGHEST, preferred_element_type=f32)


def _ada_kernel(c_ref, w_ref, b_ref, o_ref):
    s = jax.nn.silu(c_ref[...]).astype(bf16)
    o_ref[0] = _dot(s, w_ref[0]) + b_ref[0]


def _ada_all(c_all, w_ada16, b_ada):
    depth = w_ada16.shape[0]
    rows = c_all.shape[0]
    n_out = N_ADA * D_MODEL
    tn = 9 * LANES
    return pl.pallas_call(
        _ada_kernel,
        out_shape=jax.ShapeDtypeStruct((depth, rows, n_out), f32),
        grid=(depth, n_out // tn),
        in_specs=[pl.BlockSpec((rows, D_MODEL), lambda d, n: (0, 0)),
                  pl.BlockSpec((1, D_MODEL, tn), lambda d, n: (d, 0, n)),
                  pl.BlockSpec((1, 1, tn), lambda d, n: (d, 0, n))],
        out_specs=pl.BlockSpec((1, rows, tn), lambda d, n: (d, 0, n)),
        compiler_params=_cp(("arbitrary", "arbitrary")),
        name="ada",
    )(c_all, w_ada16, b_ada.reshape(depth, 1, n_out))


def _ffn_kernel(x_ref, ada_ref, g_ref, wg_ref, wu_ref, wo_ref, gf_ref, o_ref, h_scr, acc_scr, *, k0, final_norm):
    f = pl.program_id(2)
    bt, tt, d = x_ref.shape
    rows = bt * tt

    @pl.when(f == 0)
    def _():
        ada = ada_ref[...]
        h = _rms(x_ref[...], g_ref[...]) * (1.0 + ada[:, k0 + 1:k0 + 2, :]) + ada[:, k0:k0 + 1, :]
        h_scr[...] = h.reshape(rows, d).astype(bf16)
        acc_scr[...] = jnp.zeros_like(acc_scr)

    h = h_scr[...]
    gate = _dot(h, wg_ref[...])
    up = _dot(h, wu_ref[...])
    a = (jax.nn.silu(gate) * up).astype(bf16)
    acc_scr[...] += _dot(a, wo_ref[...])

    @pl.when(f == pl.num_programs(2) - 1)
    def _():
        ada = ada_ref[...]
        y = x_ref[...] + (0.5 * ada[:, k0 + 2:k0 + 3, :]) * acc_scr[...].reshape(bt, tt, d)
        if final_norm:
            y = _rms(y, gf_ref[...])
        o_ref[...] = y


def _token_tile(b, t, rows=512):
    if t >= rows:
        return 1, rows
    return min(b, rows // t), t


def _ffn(x, ada, g, w_in16, w_out16, g_final, *, k0, final_norm):
    b, t, d = x.shape
    bt, tt = _token_tile(b, t)
    tf = D_FF // 2
    nf = D_FF // tf
    kern = functools.partial(_ffn_kernel, k0=k0, final_norm=final_norm)
    return pl.pallas_call(
        kern,
        out_shape=jax.ShapeDtypeStruct((b, t, d), f32),
        grid=(b // bt, t // tt, nf),
        in_specs=[pl.BlockSpec((bt, tt, d), lambda i, j, f: (i, j, 0)),
                  pl.BlockSpec((bt, N_ADA, d), lambda i, j, f: (i, 0, 0)),
                  pl.BlockSpec((1, d), lambda i, j, f: (0, 0)),
                  pl.BlockSpec((d, tf), lambda i, j, f: (0, f)),
                  pl.BlockSpec((d, tf), lambda i, j, f: (0, f + nf)),
                  pl.BlockSpec((tf, d), lambda i, j, f: (f, 0)),
                  pl.BlockSpec((1, d), lambda i, j, f: (0, 0))],
        out_specs=pl.BlockSpec((bt, tt, d), lambda i, j, f: (i, j, 0)),
        scratch_shapes=[pltpu.VMEM((bt * tt, d), bf16), pltpu.VMEM((bt * tt, d), f32)],
        compiler_params=_cp(("arbitrary", "arbitrary", "arbitrary")),
        name="ffn",
    )(x, ada, g.reshape(1, d), w_in16, w_in16, w_out16, g_final.reshape(1, d))


def _inproj_kernel(x_ref, ada_ref, g_ref, w_ref, o_ref, k16_ref, v16_ref, ki16_ref, h_scr, *, k0):
    n = pl.program_id(2)
    bt, tt, d = x_ref.shape

    @pl.when(n == 0)
    def _():
        ada = ada_ref[...]
        h = _rms(x_ref[...], g_ref[...]) * (1.0 + ada[:, k0 + 1:k0 + 2, :]) + ada[:, k0:k0 + 1, :]
        h_scr[...] = h.reshape(bt * tt, d).astype(bf16)

    res = _dot(h_scr[...], w_ref[...])
    o_ref[...] = res.reshape(bt, tt, PROJ_TILE)

    @pl.when(n == OFF_K // PROJ_TILE)
    def _():
        k16_ref[...] = res[:, :C_KV].astype(bf16).reshape(bt, tt, C_KV)
        v16_ref[...] = res[:, C_KV:2 * C_KV].astype(bf16).reshape(bt, tt, C_KV)

    @pl.when(n == OFF_SMALL // PROJ_TILE)
    def _():
        lo = OFF_SMALL % PROJ_TILE + SMALL_KI
        ki16_ref[...] = res[:, lo:lo + IDX_DIM].astype(bf16).reshape(bt, tt, IDX_DIM)


def _inproj(x, ada, g, w_proj16, *, k0):
    b, t, d = x.shape
    bt, tt = _token_tile(b, t)
    assert OFF_K % PROJ_TILE == 0 and OFF_V == OFF_K + C_KV and 2 * C_KV <= PROJ_TILE
    tok = lambda w: pl.BlockSpec((bt, tt, w), lambda i, j, n: (i, j, 0))
    return pl.pallas_call(
        functools.partial(_inproj_kernel, k0=k0),
        out_shape=(jax.ShapeDtypeStruct((b, t, PROJ_W), f32), jax.ShapeDtypeStruct((b, t, C_KV), bf16),
                   jax.ShapeDtypeStruct((b, t, C_KV), bf16), jax.ShapeDtypeStruct((b, t, IDX_DIM), bf16)),
        grid=(b // bt, t // tt, PROJ_W // PROJ_TILE),
        in_specs=[pl.BlockSpec((bt, tt, d), lambda i, j, n: (i, j, 0)),
                  pl.BlockSpec((bt, N_ADA, d), lambda i, j, n: (i, 0, 0)),
                  pl.BlockSpec((1, d), lambda i, j, n: (0, 0)),
                  pl.BlockSpec((d, PROJ_TILE), lambda i, j, n: (0, n))],
        out_specs=(pl.BlockSpec((bt, tt, PROJ_TILE), lambda i, j, n: (i, j, n)), tok(C_KV), tok(C_KV), tok(IDX_DIM)),
        scratch_shapes=[pltpu.VMEM((bt * tt, d), bf16)],
        compiler_params=_cp(("arbitrary", "arbitrary", "arbitrary")),
        name="inproj",
    )(x, ada, g.reshape(1, d), w_proj16)


def _layout_w_in(w_in):
    d = w_in.shape[0]
    o = 0
    pa = w_in[:, o:o + A_COLS]; o += A_COLS
    z = w_in[:, o:o + SSM_INNER]; o += SSM_INNER
    xbc = w_in[:, o:o + CONV_DIM]; o += CONV_DIM
    dt = w_in[:, o:o + SSM_HEADS]; o += SSM_HEADS
    q = w_in[:, o:o + C_Q]; o += C_Q
    k = w_in[:, o:o + C_KV]; o += C_KV
    v = w_in[:, o:o + C_KV]; o += C_KV
    qi = w_in[:, o:o + C_QI]; o += C_QI
    ki = w_in[:, o:o + IDX_DIM]; o += IDX_DIM
    wi = w_in[:, o:o + IDX_HEADS]; o += IDX_HEADS
    gates = w_in[:, o:o + GATE_COLS]
    small = jnp.concatenate([dt, wi, jnp.zeros((d, SMALL_KI - SMALL_WI - IDX_HEADS), w_in.dtype), ki], axis=1)
    pad = jnp.zeros((d, PROJ_W - OFF_SMALL - LANES), w_in.dtype)
    return jnp.concatenate([gates, xbc, pa, z, q, k, v, qi, small, pad], axis=1)


def _mixa_kernel(pa_ref, ga_ref, ws_ref, bs_ref, oa_ref, va_ref):
    n = pa_ref.shape[1]
    ge = jax.nn.gelu(pa_ref[0])
    u = ge[:, :A_WIDTH]
    v = _rms(ge[:, A_WIDTH:], ga_ref[...])
    va_ref[0] = v
    vb = v.astype(bf16)
    tri = lax.broadcasted_iota(i32, (n, n), 0) >= lax.broadcasted_iota(i32, (n, n), 1)
    bs = bs_ref[...]
    for g in range(A_GROUPS):
        sl = slice(g * A_GROUP_DIM, (g + 1) * A_GROUP_DIM)
        w = jnp.where(tri, ws_ref[g], 0.0).astype(bf16)
        mixed = _dot(w, vb[:, sl]) + bs[:, g:g + 1]
        oa_ref[0, :, sl] = (u[:, sl] * mixed).astype(bf16)


def _mixer_a(proj, g_a, w_s, b_s):
    b, t, _ = proj.shape
    n = min(t, A_CHUNK)
    ws = w_s[:, :n, :n]
    bs_t = b_s[:, :n].T
    return pl.pallas_call(
        _mixa_kernel,
        out_shape=(jax.ShapeDtypeStruct((b, t, A_WIDTH), bf16), jax.ShapeDtypeStruct((b, t, A_WIDTH), f32)),
        grid=(b, t // n),
        in_specs=[pl.BlockSpec((1, n, A_COLS), lambda i, j: (i, j, OFF_PA // A_COLS)),
                  pl.BlockSpec((1, A_WIDTH), lambda i, j: (0, 0)),
                  pl.BlockSpec((A_GROUPS, n, n), lambda i, j: (0, 0, 0)),
                  pl.BlockSpec((n, A_GROUPS), lambda i, j: (0, 0))],
        out_specs=(pl.BlockSpec((1, n, A_WIDTH), lambda i, j: (i, j, 0)),
                   pl.BlockSpec((1, n, A_WIDTH), lambda i, j: (i, j, 0))),
        compiler_params=_cp(("arbitrary", "arbitrary")),
        name="mixer_a",
    )(proj, g_a.reshape(1, A_WIDTH), ws, bs_t)


CONV_PAD = 8


def _ssd_kernel(z_ref, xbc_ref, sm_ref, cinit_ref, h0_ref, wc_ref, bc_ref, dtb_ref, alog_ref, dsk_ref, gs_ref,
                ex_ref, ob_ref, hs_ref, xpad, ht):
    c = pl.program_id(1)
    q = z_ref.shape[1]
    gw = SSM_HPG * SSM_HEAD_DIM
    tail = CONV_W - 1

    @pl.when(c == 0)
    def _():
        xpad[CONV_PAD - tail:CONV_PAD, :] = cinit_ref[0]
        for g in range(SSM_GROUPS):
            ht[g] = h0_ref[0, g * gw:(g + 1) * gw, :].T

    xpad[CONV_PAD:CONV_PAD + q, :] = xbc_ref[0]
    wc = wc_ref[...]
    conv = bc_ref[...] + wc[0:1, :] * xpad[pl.ds(CONV_PAD - 3, q), :]
    for k in range(1, CONV_W):
        conv = conv + wc[k:k + 1, :] * xpad[pl.ds(CONV_PAD - 3 + k, q), :]
    nxt = xpad[CONV_PAD + q - tail:CONV_PAD + q, :]
    xpad[CONV_PAD - tail:CONV_PAD, :] = nxt
    conv = jax.nn.silu(conv)
    xs = conv[:, :SSM_INNER]
    bm = conv[:, SSM_INNER:SSM_INNER + SSM_GROUPS * SSM_STATE]
    cm = conv[:, SSM_INNER + SSM_GROUPS * SSM_STATE:]

    lane = lax.broadcasted_iota(i32, (q, LANES), 1)
    head_lane = lane < SSM_HEADS
    dt = jnp.where(head_lane, jax.nn.softplus(sm_ref[0] + dtb_ref[...]), 0.0)
    da = dt * (-jnp.exp(alog_ref[...]))
    tri = lax.broadcasted_iota(i32, (q, q), 0) >= lax.broadcasted_iota(i32, (q, q), 1)
    acs = _dot_exact(tri.astype(f32), da)
    acs_t = acs.T
    dt_t = dt.T
    acs_last = acs[q - 1:q, :]
    e_acs = jnp.where(head_lane, jnp.exp(acs), 0.0)
    decay = jnp.where(head_lane, jnp.exp(acs_last - acs) * dt, 0.0)
    ex = ex_ref[...]
    e_acs_x = _dot_exact(e_acs, ex)
    decay_x = _dot_exact(decay, ex)
    e_last_x = e_acs_x[q - 1:q, :]

    xs16 = xs.astype(bf16)
    xdec16 = (xs * decay_x).astype(bf16)
    y_parts = []
    for g in range(SSM_GROUPS):
        bg = bm[:, g * SSM_STATE:(g + 1) * SSM_STATE].astype(bf16)
        cg = cm[:, g * SSM_STATE:(g + 1) * SSM_STATE].astype(bf16)
        cb = _dot_nt(cg, bg)
        htg = ht[g]
        y_inter = _dot(cg, htg.astype(bf16))
        intra = []
        for r in range(SSM_HPG):
            hd = g * SSM_HPG + r
            seg = acs[:, hd:hd + 1] - acs_t[hd:hd + 1, :]
            m = cb * jnp.exp(jnp.where(tri, seg, -jnp.inf)) * dt_t[hd:hd + 1, :]
            intra.append(_dot(m.astype(bf16), xs16[:, hd * SSM_HEAD_DIM:(hd + 1) * SSM_HEAD_DIM]))
        y_parts.append(jnp.concatenate(intra, axis=1) + y_inter * e_acs_x[:, g * gw:(g + 1) * gw])
        ht[g] = htg * e_last_x[:, g * gw:(g + 1) * gw] + _dot_tn(bg, xdec16[:, g * gw:(g + 1) * gw])
    y = jnp.concatenate(y_parts, axis=1) + dsk_ref[...] * xs
    y = _rms(y * jax.nn.silu(z_ref[0]), gs_ref[...])
    ob_ref[0] = y.astype(bf16)

    @pl.when(c == pl.num_programs(1) - 1)
    def _():
        for g in range(SSM_GROUPS):
            hs_ref[0, g * gw:(g + 1) * gw, :] = ht[g].T


def _mixer_b(proj, conv_init, h0, w_conv, b_conv, dt_bias, a_log, d_skip, g_ssm):
    b, t, _ = proj.shape
    q = CHUNK
    pad_row = lambda v: jnp.zeros((1, LANES), f32).at[0, :SSM_HEADS].set(v)
    expand = (jnp.arange(LANES)[:, None] == (jnp.arange(SSM_INNER) // SSM_HEAD_DIM)[None, :]).astype(f32)
    hrows = SSM_HEADS * SSM_HEAD_DIM
    return pl.pallas_call(
        _ssd_kernel,
        out_shape=(jax.ShapeDtypeStruct((b, t, SSM_INNER), bf16),
                   jax.ShapeDtypeStruct((b, hrows, SSM_STATE), f32)),
        grid=(b, t // q),
        in_specs=[pl.BlockSpec((1, q, SSM_INNER), lambda i, c: (i, c, OFF_Z // SSM_INNER)),
                  pl.BlockSpec((1, q, CONV_DIM), lambda i, c: (i, c, OFF_XBC // CONV_DIM)),
                  pl.BlockSpec((1, q, LANES), lambda i, c: (i, c, OFF_SMALL // LANES)),
                  pl.BlockSpec((1, CONV_W - 1, CONV_DIM), lambda i, c: (i, 0, 0)),
                  pl.BlockSpec((1, hrows, SSM_STATE), lambda i, c: (i, 0, 0)),
                  pl.BlockSpec((CONV_W, CONV_DIM), lambda i, c: (0, 0)),
                  pl.BlockSpec((1, CONV_DIM), lambda i, c: (0, 0)),
                  pl.BlockSpec((1, LANES), lambda i, c: (0, 0)),
                  pl.BlockSpec((1, LANES), lambda i, c: (0, 0)),
                  pl.BlockSpec((1, SSM_INNER), lambda i, c: (0, 0)),
                  pl.BlockSpec((1, SSM_INNER), lambda i, c: (0, 0)),
                  pl.BlockSpec((LANES, SSM_INNER), lambda i, c: (0, 0))],
        out_specs=(pl.BlockSpec((1, q, SSM_INNER), lambda i, c: (i, c, 0)),
                   pl.BlockSpec((1, hrows, SSM_STATE), lambda i, c: (i, 0, 0))),
        scratch_shapes=[pltpu.VMEM((CONV_PAD + q, CONV_DIM), f32),
                        pltpu.VMEM((SSM_GROUPS, SSM_STATE, SSM_HPG * SSM_HEAD_DIM), f32)],
        compiler_params=_cp(("arbitrary", "arbitrary")),
        name="mixer_b",
    )(proj, proj, proj, conv_init, h0.reshape(b, hrows, SSM_STATE), w_conv, b_conv.reshape(1, CONV_DIM),
      pad_row(dt_bias), pad_row(a_log), jnp.repeat(d_skip, SSM_HEAD_DIM).reshape(1, SSM_INNER),
      g_ssm.reshape(1, SSM_INNER), expand)


def _t5_bucket(rel):
    nb = N_BUCKETS // 2
    max_exact = nb // 2
    n = jnp.abs(rel)
    nf = jnp.maximum(n, 1).astype(jnp.float32)
    large = max_exact + (jnp.log(nf / max_exact) / math.log(MAX_DISTANCE / max_exact)
                         * (nb - max_exact)).astype(jnp.int32)
    large = jnp.minimum(large, nb - 1)
    return jnp.where(rel > 0, nb, 0) + jnp.where(n < max_exact, n, large)


def _bias_kernel(tab_ref, bk_ref, o_ref):
    h = pl.program_id(1)
    bk = bk_ref[0]
    far = tab_ref[FAR_BUCKET, h]
    acc = jnp.zeros(bk.shape, f32)
    for bucket in range(N_BUCKETS):
        acc = jnp.where(bk == bucket, tab_ref[bucket, h] - far, acc)
    o_ref[0, 0] = acc * LOG2E


def _bias_tables(rel_bias, qb, offsets):
    t = np.arange(qb)[:, None]
    cases = []
    for off in offsets:
        w = np.arange(ATT_KB)[None, :] - off
        valid = (w >= 0) & (w < BIAS_WIN)
        rel = w - BIAS_WIN + qb - t
        cases.append(jnp.where(jnp.asarray(valid), _t5_bucket(jnp.asarray(rel, dtype=jnp.int32)), -1))
    cases.append(jnp.full((qb, ATT_KB), -1, jnp.int32))
    buckets = jnp.stack(cases).astype(i32)
    nc = buckets.shape[0]
    return pl.pallas_call(
        _bias_kernel,
        out_shape=jax.ShapeDtypeStruct((nc, C_HEADS, qb, ATT_KB), f32),
        grid=(nc, C_HEADS),
        in_specs=[pl.BlockSpec(memory_space=pltpu.SMEM),
                  pl.BlockSpec((1, qb, ATT_KB), lambda c, h: (c, 0, 0))],
        out_specs=pl.BlockSpec((1, 1, qb, ATT_KB), lambda c, h: (c, h, 0, 0)),
        compiler_params=_cp(("arbitrary", "arbitrary")),
        name="t5_bias",
    )(rel_bias, buckets)


def _index_kernel(qi_ref, sm_ref, ki_ref, o_ref, keys, *, qoff, topk, idx_bits):
    j = pl.program_id(1)
    qb = qi_ref.shape[1]
    ltot = ki_ref.shape[1]
    kb = IDX_KB
    kmax = qoff + (j + 1) * qb
    nkb = lax.div(kmax + (kb - 1), kb)
    sub = kb // LANES

    qv = qi_ref[0].astype(bf16)
    stacked = jnp.concatenate([qv[:, h * IDX_DIM:(h + 1) * IDX_DIM] for h in range(IDX_HEADS)], axis=0)
    w = sm_ref[0][:, SMALL_WI:SMALL_WI + IDX_HEADS] * (IDX_HEADS ** -0.5) * (IDX_DIM ** -0.5)
    wb = [jnp.broadcast_to(w[:, h:h + 1], (qb, kb)) for h in range(IDX_HEADS)]
    qchunk = lax.shift_right_arithmetic(qoff + j * qb + lax.broadcasted_iota(i32, (qb, 1), 0), 6)
    lane_pos = lax.broadcasted_iota(i32, (1, kb), 1)

    def score_body(i, carry):
        off = pl.multiple_of(i * kb, kb)
        s = _dot_nt(stacked, ki_ref[0, pl.ds(off, kb), :])
        acc = jnp.maximum(s[0:qb], 0.0) * wb[0]
        for h in range(1, IDX_HEADS):
            acc = acc + jnp.maximum(s[h * qb:(h + 1) * qb], 0.0) * wb[h]
        adm = lax.shift_right_arithmetic(off + lane_pos, 6) <= qchunk
        x = jnp.where(adm, jnp.where(acc == 0.0, 0.0, acc), -jnp.inf)
        bits = lax.bitcast_convert_type(x, i32)
        key = bits ^ (lax.shift_right_arithmetic(bits, 31) & INT_MAX)
        keys[:, pl.ds(off, kb)] = key
        cm = list(carry)
        for c in range(sub):
            cm[c % 2] = jnp.maximum(cm[c % 2], key[:, c * LANES:(c + 1) * LANES])
        return tuple(cm)

    cm_init = jnp.full((qb, LANES), NEG_INF_KEY, i32)
    cm_a, cm_b = lax.fori_loop(0, nkb, score_body, (cm_init, cm_init))

    def count(pred):
        def body(i, acc):
            off = pl.multiple_of(i * kb, kb)
            blk = keys[:, pl.ds(off, kb)]
            m = pred(blk, off).astype(i32)
            for c in range(sub):
                acc = acc + m[:, c * LANES:(c + 1) * LANES]
            return acc
        acc = lax.fori_loop(0, nkb, body, jnp.zeros((qb, LANES), i32))
        return jnp.sum(acc, axis=1, keepdims=True)

    def count_ge(thr):
        tb = jnp.broadcast_to(thr, (qb, kb))
        return count(lambda blk, off: blk >= tb)

    assert topk <= 2 * LANES
    lo0 = jnp.min(jnp.minimum(cm_a, cm_b), axis=1, keepdims=True)
    hi0 = jnp.max(jnp.maximum(cm_a, cm_b), axis=1, keepdims=True) + 1
    lo0 = jnp.where(hi0 - 1 > lo0, lo0, hi0 - 1)

    def bis_cond(c):
        it, open_rows, _, _ = c
        return (open_rows > 0) & (it < 40)

    def bis_body(c):
        it, _, lo, hi = c
        mid = lax.shift_right_arithmetic(lo, 1) + lax.shift_right_arithmetic(hi, 1) + (lo & hi & 1)
        n = count_ge(mid)
        ge = n >= topk
        lo_n = jnp.where(ge, mid, lo)
        hi_n = jnp.where(n == topk, mid + 1, jnp.where(ge, hi, mid))
        open_rows = jnp.max((lo_n < hi_n - 1).astype(i32))
        return it + 1, open_rows, lo_n, hi_n

    open0 = jnp.max((lo0 < hi0 - 1).astype(i32))
    _, _, thr, _ = lax.while_loop(bis_cond, bis_body, (jnp.int32(0), open0, lo0, hi0))
    n_ge = count_ge(thr)
    n_gt = count_ge(thr + 1)
    tie = (n_ge > topk) & (thr > NEG_INF_KEY)
    need = topk - n_gt

    def tie_cut(_):
        tb = jnp.broadcast_to(thr, (qb, kb))

        def body(_, c):
            lo, hi = c
            mid = lax.shift_right_arithmetic(lo + hi, 1)
            mb = jnp.broadcast_to(mid, (qb, kb))
            n = count(lambda blk, off: (blk == tb) & ((off + lane_pos) <= mb))
            ok = n >= need
            return jnp.where(ok, lo, mid), jnp.where(ok, mid, hi)

        lo_i = jnp.full((qb, 1), -1, i32)
        hi_i = jnp.full((qb, 1), 1, i32) * (nkb * kb - 1)
        _, cut = lax.fori_loop(0, idx_bits, body, (lo_i, hi_i))
        return jnp.where(tie, cut, INT_MAX)

    any_tie = jnp.max(tie.astype(i32)) > 0
    cut = lax.cond(any_tie, tie_cut, lambda _: jnp.full((qb, 1), INT_MAX, i32), 0)

    tb = jnp.broadcast_to(thr, (qb, kb))
    cb = jnp.broadcast_to(cut, (qb, kb))

    def out_body(i, carry):
        off = pl.multiple_of(i * kb, kb)
        blk = keys[:, pl.ds(off, kb)]
        sel = ((blk > tb) | ((blk == tb) & ((off + lane_pos) <= cb))) & (blk > NEG_INF_KEY)
        o_ref[0, :, pl.ds(off, kb)] = jnp.where(sel, 0.0, NEG_BIG).astype(bf16)
        return carry

    lax.fori_loop(0, nkb, out_body, 0)

    def fill_body(i, carry):
        off = pl.multiple_of(i * kb, kb)
        o_ref[0, :, pl.ds(off, kb)] = jnp.full((qb, kb), NEG_BIG, bf16)
        return carry

    lax.fori_loop(nkb, ltot // kb, fill_body, 0)


def _dsa_mask(proj, ki16, *, qb, qoff, topk):
    b, t, _ = proj.shape
    ltot = ki16.shape[1]
    idx_bits = int(math.ceil(math.log2(ltot))) + 1
    kern = functools.partial(_index_kernel, qoff=qoff, topk=topk, idx_bits=idx_bits)
    return pl.pallas_call(
        kern,
        out_shape=jax.ShapeDtypeStruct((b, t, ltot), bf16),
        grid=(b, t // qb),
        in_specs=[pl.BlockSpec((1, qb, C_QI), lambda i, j: (i, j, OFF_QI // C_QI)),
                  pl.BlockSpec((1, qb, LANES), lambda i, j: (i, j, OFF_SMALL // LANES)),
                  pl.BlockSpec((1, ltot, IDX_DIM), lambda i, j: (i, 0, 0))],
        out_specs=pl.BlockSpec((1, qb, ltot), lambda i, j: (i, j, 0)),
        scratch_shapes=[pltpu.VMEM((qb, ltot), i32)],
        compiler_params=_cp(("arbitrary", "arbitrary")),
        name="dsa_index",
    )(proj, proj, ki16)


def _attn_kernel(qidx_ref, kidx_ref, flag_ref, case_ref, q_ref, k_ref, v_ref, mb_ref, bias_ref, o_ref,
                 q_scr, m_scr, l_scr, acc_scr, s_scr, p_scr, a_scr, mb_scr):
    s_id = pl.program_id(1)
    qb = q_ref.shape[1]
    kb = k_ref.shape[1]
    rows = C_GROUP * qb
    nt = kb // LANES
    flags = flag_ref[s_id]

    @pl.when((flags & 1) != 0)
    def _():
        qv = (q_ref[0] * (C_HEAD_DIM ** -0.5 * LOG2E)).astype(bf16)
        for g in range(C_KV_HEADS):
            for r in range(C_GROUP):
                hd = g * C_GROUP + r
                q_scr[g, r * qb:(r + 1) * qb, :] = qv[:, hd * C_HEAD_DIM:(hd + 1) * C_HEAD_DIM]
        m_scr[...] = jnp.full(m_scr.shape, NEG_BIG, f32)
        l_scr[...] = jnp.zeros_like(l_scr)
        acc_scr[...] = jnp.zeros_like(acc_scr)

    def visit(with_bias):
        kblk = k_ref[0]
        vblk = v_ref[0]
        mb_scr[...] = mb_ref[0].astype(f32)
        units = [(g, r) for g in range(C_KV_HEADS) for r in range(C_GROUP)]

        def qk(u):
            g, r = units[u]
            hs = slice(r * qb, (r + 1) * qb)
            s_scr[g, hs, :] = _dot_nt(q_scr[g, hs, :], kblk[:, g * C_HEAD_DIM:(g + 1) * C_HEAD_DIM])

        def soft(u):
            g, r = units[u]
            for c in range(qb // ATT_RC):
                q0 = c * ATT_RC
                rs = slice(r * qb + q0, r * qb + q0 + ATT_RC)
                s = s_scr[g, rs, :] + mb_scr[q0:q0 + ATT_RC, :]
                if with_bias:
                    s = s + bias_ref[0, g * C_GROUP + r, q0:q0 + ATT_RC, :]
                t = s[:, 0:LANES]
                for i in range(1, nt):
                    t = jnp.maximum(t, s[:, i * LANES:(i + 1) * LANES])
                m_old = m_scr[g, rs, :]
                m_new = jnp.maximum(m_old, jnp.max(t, axis=1, keepdims=True))
                alpha = jnp.exp2(m_old - m_new)
                p = jnp.exp2(s - jnp.concatenate([m_new] * nt, axis=1))
                psum = p[:, 0:LANES]
                for i in range(1, nt):
                    psum = psum + p[:, i * LANES:(i + 1) * LANES]
                l_scr[g, rs, :] = alpha * l_scr[g, rs, :] + psum
                m_scr[g, rs, :] = m_new
                a_scr[g, rs, :] = alpha
                p_scr[g, rs, :] = p.astype(bf16)

        def pv(u):
            g, r = units[u]
            hs = slice(r * qb, (r + 1) * qb)
            acc_scr[g, hs, :] = a_scr[g, hs, :] * acc_scr[g, hs, :] + _dot(
                p_scr[g, hs, :], vblk[:, g * C_HEAD_DIM:(g + 1) * C_HEAD_DIM])

        qk(0)
        for u in range(len(units)):
            if u + 1 < len(units):
                qk(u + 1)
            soft(u)
            pv(u)

    @pl.when((flags & 4) != 0)
    def _():
        visit(True)

    @pl.when((flags & 4) == 0)
    def _():
        visit(False)

    @pl.when((flags & 2) != 0)
    def _():
        for g in range(C_KV_HEADS):
            o = acc_scr[g] / jnp.sum(l_scr[g], axis=1, keepdims=True)
            for r in range(C_GROUP):
                hd = g * C_GROUP + r
                o_ref[0, :, hd * C_HEAD_DIM:(hd + 1) * C_HEAD_DIM] = o[r * qb:(r + 1) * qb, :].astype(bf16)


def _attn_schedule(nq, qb, qoff, offsets):
    qidx, kidx, flags, case = [], [], [], []
    zero_case = len(offsets)
    for j in range(nq):
        kmax = qoff + (j + 1) * qb
        nkb = -(-kmax // ATT_KB)
        for kb in range(nkb):
            off = kmax - BIAS_WIN - kb * ATT_KB
            qidx.append(j)
            kidx.append(kb)
            near = off in offsets
            flags.append((1 if kb == 0 else 0) | (2 if kb == nkb - 1 else 0) | (4 if near else 0))
            case.append(offsets.index(off) if near else zero_case)
            assert near == (-BIAS_WIN < off < ATT_KB)
    return [jnp.asarray(np.asarray(a, np.int32)) for a in (qidx, kidx, flags, case)]


def _dsa_attend(proj, k16, v16, maskbias, tables, *, qb, qoff, offsets):
    b, t, _ = proj.shape
    nq = t // qb
    qidx, kidx, flags, case = _attn_schedule(nq, qb, qoff, offsets)
    ns = int(qidx.shape[0])
    rows = C_GROUP * qb
    grid_spec = pltpu.PrefetchScalarGridSpec(
        num_scalar_prefetch=4,
        grid=(b, ns),
        in_specs=[pl.BlockSpec((1, qb, C_Q), lambda i, s, qi, ki, fl, ca: (i, qi[s], OFF_Q // C_Q)),
                  pl.BlockSpec((1, ATT_KB, C_KV), lambda i, s, qi, ki, fl, ca: (i, ki[s], 0)),
                  pl.BlockSpec((1, ATT_KB, C_KV), lambda i, s, qi, ki, fl, ca: (i, ki[s], 0)),
                  pl.BlockSpec((1, qb, ATT_KB), lambda i, s, qi, ki, fl, ca: (i, qi[s], ki[s])),
                  pl.BlockSpec((1, C_HEADS, qb, ATT_KB), lambda i, s, qi, ki, fl, ca: (ca[s], 0, 0, 0))],
        out_specs=pl.BlockSpec((1, qb, C_WIDTH), lambda i, s, qi, ki, fl, ca: (i, qi[s], 0)),
        scratch_shapes=[pltpu.VMEM((C_KV_HEADS, rows, C_HEAD_DIM), bf16),
                        pltpu.VMEM((C_KV_HEADS, rows, LANES), f32),
                        pltpu.VMEM((C_KV_HEADS, rows, LANES), f32),
                        pltpu.VMEM((C_KV_HEADS, rows, C_HEAD_DIM), f32),
                        pltpu.VMEM((C_KV_HEADS, rows, ATT_KB), f32),
                        pltpu.VMEM((C_KV_HEADS, rows, ATT_KB), bf16),
                        pltpu.VMEM((C_KV_HEADS, rows, LANES), f32),
                        pltpu.VMEM((qb, ATT_KB), f32)])
    return pl.pallas_call(
        _attn_kernel,
        out_shape=jax.ShapeDtypeStruct((b, t, C_WIDTH), bf16),
        grid_spec=grid_spec,
        compiler_params=_cp(("arbitrary", "arbitrary")),
        name="dsa_attend",
    )(qidx, kidx, flags, case, proj, k16, v16, maskbias, tables)


def _window_offsets(nq, qb, qoff):
    offs = set()
    for j in range(nq):
        kmax = qoff + (j + 1) * qb
        for kb in range(-(-kmax // ATT_KB)):
            off = kmax - BIAS_WIN - kb * ATT_KB
            if -BIAS_WIN < off < ATT_KB:
                offs.add(off)
    return sorted(offs)


def _mixer_c(proj, k_all16, v_all16, ki_all16, rel_bias, *, qb, qoff, true_len):
    b, t, _ = proj.shape
    topk = min(TOPK_MAX, true_len // 4)
    assert BIAS_WIN - qb + 1 > MAX_DISTANCE
    offsets = _window_offsets(t // qb, qb, qoff)
    tables = _bias_tables(rel_bias, qb, offsets)
    maskbias = _dsa_mask(proj, ki_all16, qb=qb, qoff=qoff, topk=topk)
    return _dsa_attend(proj, k_all16, v_all16, maskbias, tables, qb=qb, qoff=qoff, offsets=offsets)


def _merge_kernel(x_ref, ada_ref, gt_ref, oa_ref, ob_ref, oc_ref, wa_ref, wb_ref, wc_ref, wo_ref, o_ref):
    bt, tt, d = x_ref.shape
    rows = bt * tt
    gates = jax.nn.sigmoid(gt_ref[...].reshape(rows, GATE_COLS))
    merged = (gates[:, 0:d] * _dot(oa_ref[...].reshape(rows, A_WIDTH), wa_ref[...])
              + gates[:, d:2 * d] * _dot(ob_ref[...].reshape(rows, SSM_INNER), wb_ref[...])
              + gates[:, 2 * d:3 * d] * _dot(oc_ref[...].reshape(rows, C_WIDTH), wc_ref[...]))
    mix = _dot(merged.astype(bf16), wo_ref[...]).reshape(bt, tt, d)
    o_ref[...] = x_ref[...] + ada_ref[...][:, 5:6, :] * mix


def _merge(x, ada, proj, o_a, o_b, o_c, wa16, wb16, wc16, wo16):
    b, t, d = x.shape
    bt, tt = _token_tile(b, t, rows=256)
    tok = lambda w: pl.BlockSpec((bt, tt, w), lambda i, j: (i, j, 0))
    full = lambda r, c: pl.BlockSpec((r, c), lambda i, j: (0, 0))
    return pl.pallas_call(
        _merge_kernel,
        out_shape=jax.ShapeDtypeStruct((b, t, d), f32),
        grid=(b // bt, t // tt),
        in_specs=[tok(d), pl.BlockSpec((bt, N_ADA, d), lambda i, j: (i, 0, 0)),
                  pl.BlockSpec((bt, tt, GATE_COLS), lambda i, j: (i, j, OFF_GATE // GATE_COLS)),
                  tok(A_WIDTH), tok(SSM_INNER), tok(C_WIDTH),
                  full(A_WIDTH, d), full(SSM_INNER, d), full(C_WIDTH, d), full(d, d)],
        out_specs=tok(d),
        compiler_params=_cp(("arbitrary", "arbitrary")),
        name="merge",
    )(x, ada, proj, o_a, o_b, o_c, wa16, wb16, wc16, wo16)


def _pad_keys(x, mult):
    pad = (-x.shape[1]) % mult
    if pad:
        x = jnp.concatenate([x, jnp.zeros((x.shape[0], pad) + x.shape[2:], x.dtype)], axis=1)
    return x


def _layer(x, ada, lp, rel_bias, past, g_final, last):
    b, t, d = x.shape
    x = _ffn(x, ada, lp['g_ffn1'], lp['w_ffn1_in'], lp['w_ffn1_out'], g_final, k0=0, final_norm=False)
    proj, k16, v16, ki16 = _inproj(x, ada, lp['g_mix'], lp['w_proj'], k0=3)
    k_new = proj[:, :, OFF_K:OFF_K + C_KV]
    v_new = proj[:, :, OFF_V:OFF_V + C_KV]
    ki_new = proj[:, :, OFF_SMALL + SMALL_KI:OFF_SMALL + LANES]
    new_conv = proj[:, t - (CONV_W - 1):, OFF_XBC:OFF_XBC + CONV_DIM]

    o_a, va = _mixer_a(proj, lp['g_a'], lp['w_s'], lp['b_s'])

    if past is None:
        conv_init = jnp.zeros((b, CONV_W - 1, CONV_DIM), f32)
        h0 = jnp.zeros((b, SSM_HEADS, SSM_HEAD_DIM, SSM_STATE), f32)
    else:
        h0, conv_init = past[3], past[4]
    o_b, new_ssm = _mixer_b(proj, conv_init, h0, lp['w_conv'], lp['b_conv'], lp['dt_bias'], lp['a_log'],
                            lp['d_skip'], lp['g_ssm'])
    new_ssm = new_ssm.reshape(b, SSM_HEADS, SSM_HEAD_DIM, SSM_STATE)

    if past is None:
        k_all, v_all, ki_all = k16, v16, ki16
        qb, qoff, true_len = Q_BLOCK, 0, t
    else:
        plen = past[0].shape[1]
        k_all = jnp.concatenate([past[0].reshape(b, plen, C_KV).astype(bf16), k16], axis=1)
        v_all = jnp.concatenate([past[1].reshape(b, plen, C_KV).astype(bf16), v16], axis=1)
        ki_all = jnp.concatenate([past[2].astype(bf16), ki16], axis=1)
        qb, qoff, true_len = t, plen, plen + t
    o_c = _mixer_c(proj, _pad_keys(k_all, ATT_KB), _pad_keys(v_all, ATT_KB), _pad_keys(ki_all, ATT_KB), rel_bias,
                   qb=qb, qoff=qoff, true_len=true_len)

    x = _merge(x, ada, proj, o_a, o_b, o_c, lp['w_br_a'], lp['w_br_b'], lp['w_br_c'], lp['w_out'])
    x = _ffn(x, ada, lp['g_ffn2'], lp['w_ffn2_in'], lp['w_ffn2_out'], g_final, k0=6, final_norm=last)
    states = (k_new.reshape(b, t, C_KV_HEADS, C_HEAD_DIM), v_new.reshape(b, t, C_KV_HEADS, C_HEAD_DIM),
              ki_new, new_ssm, new_conv, va)
    return x, states


def kernel(x_prompt, x_sample, c_prompt, c_sample, cache_k, cache_v, cache_kidx, state_ssm, state_conv, rel_bias, w_ada, b_ada, g_ffn1, w_ffn1_in, w_ffn1_out, g_mix, w_in, g_a, w_s, b_s, w_conv, b_conv, dt_bias, a_log, d_skip, g_ssm, w_br_a, w_br_b, w_br_c, w_out, g_ffn2, w_ffn2_in, w_ffn2_out, g_final):
    depth = w_ada.shape[0]
    bp, bs = c_prompt.shape[0], c_sample.shape[0]
    assert x_prompt.shape[1] % ATT_KB == 0 and x_prompt.shape[1] % Q_BLOCK == 0
    assert x_sample.shape[1] == CHUNK

    rows = -(-(bp + bs) // 8) * 8
    c_all = jnp.concatenate([c_prompt, c_sample, jnp.zeros((rows - bp - bs, D_MODEL), f32)], axis=0)
    ada_all = _ada_all(c_all, w_ada.astype(bf16), b_ada)
    ada_all = ada_all.reshape(depth, rows, N_ADA, D_MODEL)

    yp, ys = x_prompt, x_sample
    st_p, st_s = [], []
    for i in range(depth):
        lp = dict(g_ffn1=g_ffn1[i], w_ffn1_in=w_ffn1_in[i].astype(bf16), w_ffn1_out=w_ffn1_out[i].astype(bf16),
                  g_mix=g_mix[i], w_proj=_layout_w_in(w_in[i]).astype(bf16), g_a=g_a[i], w_s=w_s[i], b_s=b_s[i],
                  w_conv=w_conv[i], b_conv=b_conv[i], dt_bias=dt_bias[i], a_log=a_log[i], d_skip=d_skip[i],
                  g_ssm=g_ssm[i], w_br_a=w_br_a[i].astype(bf16), w_br_b=w_br_b[i].astype(bf16),
                  w_br_c=w_br_c[i].astype(bf16), w_out=w_out[i].astype(bf16), g_ffn2=g_ffn2[i],
                  w_ffn2_in=w_ffn2_in[i].astype(bf16), w_ffn2_out=w_ffn2_out[i].astype(bf16))
        last = i == depth - 1
        yp, sp = _layer(yp, ada_all[i, :bp], lp, rel_bias, None, g_final, last)
        ys, ss = _layer(ys, ada_all[i, bp:bp + bs], lp, rel_bias,
                        (cache_k[i], cache_v[i], cache_kidx[i], state_ssm[i], state_conv[i]), g_final, last)
        st_p.append(sp)
        st_s.append(ss)

    stack = lambda sts, k: jnp.stack([s[k] for s in sts])
    return (yp, ys,
            stack(st_p, 0), stack(st_p, 1), stack(st_p, 2), stack(st_p, 3), stack(st_p, 4),
            stack(st_s, 0), stack(st_s, 1), stack(st_s, 2), stack(st_s, 3), stack(st_s, 4), stack(st_s, 5))
```

```python
import functools
import math

import numpy as np
import jax
import jax.numpy as jnp
from jax import lax
from jax.experimental import pallas as pl
from jax.experimental.pallas import tpu as pltpu

f32 = jnp.float32
bf16 = jnp.bfloat16
i32 = jnp.int32

D_MODEL = 1024
CHUNK = 64
EPS = 1e-6
N_ADA = 9
D_FF = 2816
A_WIDTH = D_MODEL
A_GROUPS = 8
A_GROUP_DIM = A_WIDTH // A_GROUPS
A_CHUNK = 128
SSM_INNER = 2 * D_MODEL
SSM_HEAD_DIM = 64
SSM_HEADS = SSM_INNER // SSM_HEAD_DIM
SSM_GROUPS = 4
SSM_HPG = SSM_HEADS // SSM_GROUPS
SSM_STATE = 128
CONV_W = 4
CONV_DIM = SSM_INNER + 2 * SSM_GROUPS * SSM_STATE
C_HEADS = 8
C_KV_HEADS = 4
C_GROUP = C_HEADS // C_KV_HEADS
C_HEAD_DIM = 128
C_WIDTH = C_HEADS * C_HEAD_DIM
IDX_HEADS = 8
IDX_DIM = 64
TOPK_MAX = 256
Q_BLOCK = 128
N_BUCKETS = 32
MAX_DISTANCE = 128
A_COLS = 2 * A_WIDTH
B_COLS = SSM_INNER + CONV_DIM + SSM_HEADS
C_Q = C_HEADS * C_HEAD_DIM
C_KV = C_KV_HEADS * C_HEAD_DIM
C_QI = IDX_HEADS * IDX_DIM
GATE_COLS = 3 * D_MODEL

LANES = 128
SUBLANES = 8
VMEM_LIMIT = 56 * 1024 * 1024

OFF_GATE = 0
OFF_XBC = OFF_GATE + GATE_COLS
OFF_PA = OFF_XBC + CONV_DIM
OFF_Z = OFF_PA + A_COLS
OFF_Q = OFF_Z + SSM_INNER
OFF_K = OFF_Q + C_Q
OFF_V = OFF_K + C_KV
OFF_QI = OFF_V + C_KV
OFF_SMALL = OFF_QI + C_QI
SMALL_DT = 0
SMALL_WI = SSM_HEADS
SMALL_KI = LANES - IDX_DIM
PROJ_TILE = 1024
PROJ_W = ((OFF_SMALL + LANES + PROJ_TILE - 1) // PROJ_TILE) * PROJ_TILE

NEG_BIG = -1e30
SUBNORMAL_SPAN = 0x007FFFFF
NEG_INF_KEY = -2139095041 + SUBNORMAL_SPAN + 1
INT_MIN = -2147483648
INT_MAX = 2147483647
ATT_KB = 1024
IDX_KB = 512
TOP_R = 16
IDX_QB = 256
ATT_RC = 8
LOG2E = math.log2(math.e)
BIAS_WIN = 256
FAR_BUCKET = N_BUCKETS // 2 - 1


def _cp(sem, vmem=VMEM_LIMIT):
    return pltpu.CompilerParams(dimension_semantics=sem, vmem_limit_bytes=vmem)


def _rms(x, g):
    return x * lax.rsqrt(jnp.mean(x * x, axis=-1, keepdims=True) + EPS) * g


def _dot(a, b):
    return jnp.dot(a, b, preferred_element_type=f32)


def _dot_nt(a, b):
    return lax.dot_general(a, b, (((1,), (1,)), ((), ())), preferred_element_type=f32)


def _dot_tn(a, b):
    return lax.dot_general(a, b, (((0,), (0,)), ((), ())), preferred_element_type=f32)


def _dot_exact(a, b):
    return jnp.dot(a, b, precision=lax.Precision.HIGHEST, preferred_element_type=f32)


def _ada_kernel(c_ref, w_ref, b_ref, o_ref):
    s = jax.nn.silu(c_ref[...]).astype(bf16)
    o_ref[0] = _dot(s, w_ref[0]) + b_ref[0]


def _ada_all(c_all, w_ada16, b_ada):
    depth = w_ada16.shape[0]
    rows = c_all.shape[0]
    n_out = N_ADA * D_MODEL
    tn = 9 * LANES
    return pl.pallas_call(
        _ada_kernel,
        out_shape=jax.ShapeDtypeStruct((depth, rows, n_out), f32),
        grid=(depth, n_out // tn),
        in_specs=[pl.BlockSpec((rows, D_MODEL), lambda d, n: (0, 0)),
                  pl.BlockSpec((1, D_MODEL, tn), lambda d, n: (d, 0, n)),
                  pl.BlockSpec((1, 1, tn), lambda d, n: (d, 0, n))],
        out_specs=pl.BlockSpec((1, rows, tn), lambda d, n: (d, 0, n)),
        compiler_params=_cp(("arbitrary", "arbitrary")),
        name="ada",
    )(c_all, w_ada16, b_ada.reshape(depth, 1, n_out))


def _ffn_kernel(x_ref, ada_ref, g_ref, wg_ref, wu_ref, wo_ref, gf_ref, o_ref, h_scr, acc_scr, *, k0, final_norm):
    f = pl.program_id(2)
    bt, tt, d = x_ref.shape
    rows = bt * tt

    @pl.when(f == 0)
    def _():
        ada = ada_ref[...]
        h = _rms(x_ref[...], g_ref[...]) * (1.0 + ada[:, k0 + 1:k0 + 2, :]) + ada[:, k0:k0 + 1, :]
        h_scr[...] = h.reshape(rows, d).astype(bf16)
        acc_scr[...] = jnp.zeros_like(acc_scr)

    h = h_scr[...]
    gate = _dot(h, wg_ref[...])
    up = _dot(h, wu_ref[...])
    a = (jax.nn.silu(gate) * up).astype(bf16)
    acc_scr[...] += _dot(a, wo_ref[...])

    @pl.when(f == pl.num_programs(2) - 1)
    def _():
        ada = ada_ref[...]
        y = x_ref[...] + (0.5 * ada[:, k0 + 2:k0 + 3, :]) * acc_scr[...].reshape(bt, tt, d)
        if final_norm:
            y = _rms(y, gf_ref[...])
        o_ref[...] = y


def _token_tile(b, t, rows=512):
    if t >= rows:
        return 1, rows
    return min(b, rows // t), t


def _ffn(x, ada, g, w_in16, w_out16, g_final, *, k0, final_norm):
    b, t, d = x.shape
    bt, tt = _token_tile(b, t)
    tf = D_FF // 2
    nf = D_FF // tf
    kern = functools.partial(_ffn_kernel, k0=k0, final_norm=final_norm)
    return pl.pallas_call(
        kern,
        out_shape=jax.ShapeDtypeStruct((b, t, d), f32),
        grid=(b // bt, t // tt, nf),
        in_specs=[pl.BlockSpec((bt, tt, d), lambda i, j, f: (i, j, 0)),
                  pl.BlockSpec((bt, N_ADA, d), lambda i, j, f: (i, 0, 0)),
                  pl.BlockSpec((1, d), lambda i, j, f: (0, 0)),
                  pl.BlockSpec((d, tf), lambda i, j, f: (0, f)),
                  pl.BlockSpec((d, tf), lambda i, j, f: (0, f + nf)),
                  pl.BlockSpec((tf, d), lambda i, j, f: (f, 0)),
                  pl.BlockSpec((1, d), lambda i, j, f: (0, 0))],
        out_specs=pl.BlockSpec((bt, tt, d), lambda i, j, f: (i, j, 0)),
        scratch_shapes=[pltpu.VMEM((bt * tt, d), bf16), pltpu.VMEM((bt * tt, d), f32)],
        compiler_params=_cp(("arbitrary", "arbitrary", "arbitrary")),
        name="ffn",
    )(x, ada, g.reshape(1, d), w_in16, w_in16, w_out16, g_final.reshape(1, d))


def _inproj_kernel(x_ref, ada_ref, g_ref, w_ref, o_ref, k16_ref, v16_ref, ki16_ref, h_scr, *, k0):
    n = pl.program_id(2)
    bt, tt, d = x_ref.shape

    @pl.when(n == 0)
    def _():
        ada = ada_ref[...]
        h = _rms(x_ref[...], g_ref[...]) * (1.0 + ada[:, k0 + 1:k0 + 2, :]) + ada[:, k0:k0 + 1, :]
        h_scr[...] = h.reshape(bt * tt, d).astype(bf16)

    res = _dot(h_scr[...], w_ref[...])
    o_ref[...] = res.reshape(bt, tt, PROJ_TILE)

    @pl.when(n == OFF_K // PROJ_TILE)
    def _():
        k16_ref[...] = res[:, :C_KV].astype(bf16).reshape(bt, tt, C_KV)
        v16_ref[...] = res[:, C_KV:2 * C_KV].astype(bf16).reshape(bt, tt, C_KV)

    @pl.when(n == OFF_SMALL // PROJ_TILE)
    def _():
        lo = OFF_SMALL % PROJ_TILE + SMALL_KI
        ki16_ref[...] = res[:, lo:lo + IDX_DIM].astype(bf16).reshape(bt, tt, IDX_DIM)


def _inproj(x, ada, g, w_proj16, *, k0):
    b, t, d = x.shape
    bt, tt = _token_tile(b, t)
    assert OFF_K % PROJ_TILE == 0 and OFF_V == OFF_K + C_KV and 2 * C_KV <= PROJ_TILE
    tok = lambda w: pl.BlockSpec((bt, tt, w), lambda i, j, n: (i, j, 0))
    return pl.pallas_call(
        functools.partial(_inproj_kernel, k0=k0),
        out_shape=(jax.ShapeDtypeStruct((b, t, PROJ_W), f32), jax.ShapeDtypeStruct((b, t, C_KV), bf16),
                   jax.ShapeDtypeStruct((b, t, C_KV), bf16), jax.ShapeDtypeStruct((b, t, IDX_DIM), bf16)),
        grid=(b // bt, t // tt, PROJ_W // PROJ_TILE),
        in_specs=[pl.BlockSpec((bt, tt, d), lambda i, j, n: (i, j, 0)),
                  pl.BlockSpec((bt, N_ADA, d), lambda i, j, n: (i, 0, 0)),
                  pl.BlockSpec((1, d), lambda i, j, n: (0, 0)),
                  pl.BlockSpec((d, PROJ_TILE), lambda i, j, n: (0, n))],
        out_specs=(pl.BlockSpec((bt, tt, PROJ_TILE), lambda i, j, n: (i, j, n)), tok(C_KV), tok(C_KV), tok(IDX_DIM)),
        scratch_shapes=[pltpu.VMEM((bt * tt, d), bf16)],
        compiler_params=_cp(("arbitrary", "arbitrary", "arbitrary")),
        name="inproj",
    )(x, ada, g.reshape(1, d), w_proj16)


def _layout_w_in(w_in):
    d = w_in.shape[0]
    o = 0
    pa = w_in[:, o:o + A_COLS]; o += A_COLS
    z = w_in[:, o:o + SSM_INNER]; o += SSM_INNER
    xbc = w_in[:, o:o + CONV_DIM]; o += CONV_DIM
    dt = w_in[:, o:o + SSM_HEADS]; o += SSM_HEADS
    q = w_in[:, o:o + C_Q]; o += C_Q
    k = w_in[:, o:o + C_KV]; o += C_KV
    v = w_in[:, o:o + C_KV]; o += C_KV
    qi = w_in[:, o:o + C_QI]; o += C_QI
    ki = w_in[:, o:o + IDX_DIM]; o += IDX_DIM
    wi = w_in[:, o:o + IDX_HEADS]; o += IDX_HEADS
    gates = w_in[:, o:o + GATE_COLS]
    small = jnp.concatenate([dt, wi, jnp.zeros((d, SMALL_KI - SMALL_WI - IDX_HEADS), w_in.dtype), ki], axis=1)
    pad = jnp.zeros((d, PROJ_W - OFF_SMALL - LANES), w_in.dtype)
    return jnp.concatenate([gates, xbc, pa, z, q, k, v, qi, small, pad], axis=1)


def _mixa_kernel(pa_ref, ga_ref, ws_ref, bs_ref, oa_ref, va_ref):
    n = pa_ref.shape[1]
    ge = jax.nn.gelu(pa_ref[0])
    u = ge[:, :A_WIDTH]
    v = _rms(ge[:, A_WIDTH:], ga_ref[...])
    va_ref[0] = v
    vb = v.astype(bf16)
    tri = lax.broadcasted_iota(i32, (n, n), 0) >= lax.broadcasted_iota(i32, (n, n), 1)
    bs = bs_ref[...]
    for g in range(A_GROUPS):
        sl = slice(g * A_GROUP_DIM, (g + 1) * A_GROUP_DIM)
        w = jnp.where(tri, ws_ref[g], 0.0).astype(bf16)
        mixed = _dot(w, vb[:, sl]) + bs[:, g:g + 1]
        oa_ref[0, :, sl] = (u[:, sl] * mixed).astype(bf16)


def _mixer_a(proj, g_a, w_s, b_s):
    b, t, _ = proj.shape
    n = min(t, A_CHUNK)
    ws = w_s[:, :n, :n]
    bs_t = b_s[:, :n].T
    return pl.pallas_call(
        _mixa_kernel,
        out_shape=(jax.ShapeDtypeStruct((b, t, A_WIDTH), bf16), jax.ShapeDtypeStruct((b, t, A_WIDTH), f32)),
        grid=(b, t // n),
        in_specs=[pl.BlockSpec((1, n, A_COLS), lambda i, j: (i, j, OFF_PA // A_COLS)),
                  pl.BlockSpec((1, A_WIDTH), lambda i, j: (0, 0)),
                  pl.BlockSpec((A_GROUPS, n, n), lambda i, j: (0, 0, 0)),
                  pl.BlockSpec((n, A_GROUPS), lambda i, j: (0, 0))],
        out_specs=(pl.BlockSpec((1, n, A_WIDTH), lambda i, j: (i, j, 0)),
                   pl.BlockSpec((1, n, A_WIDTH), lambda i, j: (i, j, 0))),
        compiler_params=_cp(("arbitrary", "arbitrary")),
        name="mixer_a",
    )(proj, g_a.reshape(1, A_WIDTH), ws, bs_t)


CONV_PAD = 8


def _ssd_kernel(z_ref, xbc_ref, sm_ref, cinit_ref, h0_ref, wc_ref, bc_ref, dtb_ref, alog_ref, dsk_ref, gs_ref,
                ex_ref, ob_ref, hs_ref, xpad, ht):
    c = pl.program_id(1)
    q = z_ref.shape[1]
    gw = SSM_HPG * SSM_HEAD_DIM
    tail = CONV_W - 1

    @pl.when(c == 0)
    def _():
        xpad[CONV_PAD - tail:CONV_PAD, :] = cinit_ref[0]
        for g in range(SSM_GROUPS):
            ht[g] = h0_ref[0, g * gw:(g + 1) * gw, :].T

    xpad[CONV_PAD:CONV_PAD + q, :] = xbc_ref[0]
    wc = wc_ref[...]
    conv = bc_ref[...] + wc[0:1, :] * xpad[pl.ds(CONV_PAD - 3, q), :]
    for k in range(1, CONV_W):
        conv = conv + wc[k:k + 1, :] * xpad[pl.ds(CONV_PAD - 3 + k, q), :]
    nxt = xpad[CONV_PAD + q - tail:CONV_PAD + q, :]
    xpad[CONV_PAD - tail:CONV_PAD, :] = nxt
    conv = jax.nn.silu(conv)
    xs = conv[:, :SSM_INNER]
    bm = conv[:, SSM_INNER:SSM_INNER + SSM_GROUPS * SSM_STATE]
    cm = conv[:, SSM_INNER + SSM_GROUPS * SSM_STATE:]

    lane = lax.broadcasted_iota(i32, (q, LANES), 1)
    head_lane = lane < SSM_HEADS
    dt = jnp.where(head_lane, jax.nn.softplus(sm_ref[0] + dtb_ref[...]), 0.0)
    da = dt * (-jnp.exp(alog_ref[...]))
    tri = lax.broadcasted_iota(i32, (q, q), 0) >= lax.broadcasted_iota(i32, (q, q), 1)
    acs = _dot_exact(tri.astype(f32), da)
    acs_t = acs.T
    dt_t = dt.T
    acs_last = acs[q - 1:q, :]
    e_acs = jnp.where(head_lane, jnp.exp(acs), 0.0)
    decay = jnp.where(head_lane, jnp.exp(acs_last - acs) * dt, 0.0)
    ex = ex_ref[...]
    e_acs_x = _dot_exact(e_acs, ex)
    decay_x = _dot_exact(decay, ex)
    e_last_x = e_acs_x[q - 1:q, :]

    xs16 = xs.astype(bf16)
    xdec16 = (xs * decay_x).astype(bf16)
    y_parts = []
    for g in range(SSM_GROUPS):
        bg = bm[:, g * SSM_STATE:(g + 1) * SSM_STATE].astype(bf16)
        cg = cm[:, g * SSM_STATE:(g + 1) * SSM_STATE].astype(bf16)
        cb = _dot_nt(cg, bg)
        htg = ht[g]
        y_inter = _dot(cg, htg.astype(bf16))
        intra = []
        for r in range(SSM_HPG):
            hd = g * SSM_HPG + r
            seg = acs[:, hd:hd + 1] - acs_t[hd:hd + 1, :]
            m = cb * jnp.exp(jnp.where(tri, seg, -jnp.inf)) * dt_t[hd:hd + 1, :]
            intra.append(_dot(m.astype(bf16), xs16[:, hd * SSM_HEAD_DIM:(hd + 1) * SSM_HEAD_DIM]))
        y_parts.append(jnp.concatenate(intra, axis=1) + y_inter * e_acs_x[:, g * gw:(g + 1) * gw])
        ht[g] = htg * e_last_x[:, g * gw:(g + 1) * gw] + _dot_tn(bg, xdec16[:, g * gw:(g + 1) * gw])
    y = jnp.concatenate(y_parts, axis=1) + dsk_ref[...] * xs
    y = _rms(y * jax.nn.silu(z_ref[0]), gs_ref[...])
    ob_ref[0] = y.astype(bf16)

    @pl.when(c == pl.num_programs(1) - 1)
    def _():
        for g in range(SSM_GROUPS):
            hs_ref[0, g * gw:(g + 1) * gw, :] = ht[g].T


def _mixer_b(proj, conv_init, h0, w_conv, b_conv, dt_bias, a_log, d_skip, g_ssm):
    b, t, _ = proj.shape
    q = CHUNK
    pad_row = lambda v: jnp.zeros((1, LANES), f32).at[0, :SSM_HEADS].set(v)
    expand = (jnp.arange(LANES)[:, None] == (jnp.arange(SSM_INNER) // SSM_HEAD_DIM)[None, :]).astype(f32)
    hrows = SSM_HEADS * SSM_HEAD_DIM
    return pl.pallas_call(
        _ssd_kernel,
        out_shape=(jax.ShapeDtypeStruct((b, t, SSM_INNER), bf16),
                   jax.ShapeDtypeStruct((b, hrows, SSM_STATE), f32)),
        grid=(b, t // q),
        in_specs=[pl.BlockSpec((1, q, SSM_INNER), lambda i, c: (i, c, OFF_Z // SSM_INNER)),
                  pl.BlockSpec((1, q, CONV_DIM), lambda i, c: (i, c, OFF_XBC // CONV_DIM)),
                  pl.BlockSpec((1, q, LANES), lambda i, c: (i, c, OFF_SMALL // LANES)),
                  pl.BlockSpec((1, CONV_W - 1, CONV_DIM), lambda i, c: (i, 0, 0)),
                  pl.BlockSpec((1, hrows, SSM_STATE), lambda i, c: (i, 0, 0)),
                  pl.BlockSpec((CONV_W, CONV_DIM), lambda i, c: (0, 0)),
                  pl.BlockSpec((1, CONV_DIM), lambda i, c: (0, 0)),
                  pl.BlockSpec((1, LANES), lambda i, c: (0, 0)),
                  pl.BlockSpec((1, LANES), lambda i, c: (0, 0)),
                  pl.BlockSpec((1, SSM_INNER), lambda i, c: (0, 0)),
                  pl.BlockSpec((1, SSM_INNER), lambda i, c: (0, 0)),
                  pl.BlockSpec((LANES, SSM_INNER), lambda i, c: (0, 0))],
        out_specs=(pl.BlockSpec((1, q, SSM_INNER), lambda i, c: (i, c, 0)),
                   pl.BlockSpec((1, hrows, SSM_STATE), lambda i, c: (i, 0, 0))),
        scratch_shapes=[pltpu.VMEM((CONV_PAD + q, CONV_DIM), f32),
                        pltpu.VMEM((SSM_GROUPS, SSM_STATE, SSM_HPG * SSM_HEAD_DIM), f32)],
        compiler_params=_cp(("arbitrary", "arbitrary")),
        name="mixer_b",
    )(proj, proj, proj, conv_init, h0.reshape(b, hrows, SSM_STATE), w_conv, b_conv.reshape(1, CONV_DIM),
      pad_row(dt_bias), pad_row(a_log), jnp.repeat(d_skip, SSM_HEAD_DIM).reshape(1, SSM_INNER),
      g_ssm.reshape(1, SSM_INNER), expand)


def _t5_bucket(rel):
    nb = N_BUCKETS // 2
    max_exact = nb // 2
    n = jnp.abs(rel)
    nf = jnp.maximum(n, 1).astype(jnp.float32)
    large = max_exact + (jnp.log(nf / max_exact) / math.log(MAX_DISTANCE / max_exact)
                         * (nb - max_exact)).astype(jnp.int32)
    large = jnp.minimum(large, nb - 1)
    return jnp.where(rel > 0, nb, 0) + jnp.where(n < max_exact, n, large)


def _bias_kernel(tab_ref, bk_ref, o_ref):
    h = pl.program_id(1)
    bk = bk_ref[0]
    far = tab_ref[FAR_BUCKET, h]
    acc = jnp.zeros(bk.shape, f32)
    for bucket in range(N_BUCKETS):
        acc = jnp.where(bk == bucket, tab_ref[bucket, h] - far, acc)
    o_ref[0, 0] = acc * LOG2E


def _bias_tables(rel_bias, qb, offsets):
    t = np.arange(qb)[:, None]
    cases = []
    for off in offsets:
        w = np.arange(ATT_KB)[None, :] - off
        valid = (w >= 0) & (w < BIAS_WIN)
        rel = w - BIAS_WIN + qb - t
        cases.append(jnp.where(jnp.asarray(valid), _t5_bucket(jnp.asarray(rel, dtype=jnp.int32)), -1))
    cases.append(jnp.full((qb, ATT_KB), -1, jnp.int32))
    buckets = jnp.stack(cases).astype(i32)
    nc = buckets.shape[0]
    return pl.pallas_call(
        _bias_kernel,
        out_shape=jax.ShapeDtypeStruct((nc, C_HEADS, qb, ATT_KB), f32),
        grid=(nc, C_HEADS),
        in_specs=[pl.BlockSpec(memory_space=pltpu.SMEM),
                  pl.BlockSpec((1, qb, ATT_KB), lambda c, h: (c, 0, 0))],
        out_specs=pl.BlockSpec((1, 1, qb, ATT_KB), lambda c, h: (c, h, 0, 0)),
        compiler_params=_cp(("arbitrary", "arbitrary")),
        name="t5_bias",
    )(rel_bias, buckets)


def _oddeven_merge_sort_pairs(n):
    pairs = []
    p = 1
    while p < n:
        k = p
        while k >= 1:
            for j in range(k % p, n - k, 2 * k):
                for i in range(min(k, n - j - k)):
                    if (i + j) // (2 * p) == (i + j + k) // (2 * p):
                        pairs.append((i + j, i + j + k))
            k //= 2
        p *= 2
    return pairs


def _compare_exchange(v, i, j):
    a, b = v[i], v[j]
    v[i] = jnp.maximum(a, b)
    v[j] = jnp.minimum(a, b)


def _sort_desc(v):
    for i, j in _oddeven_merge_sort_pairs(len(v)):
        _compare_exchange(v, i, j)


def _bitonic_merge_desc(v):
    d = len(v) // 2
    while d >= 1:
        for i in range(len(v)):
            if (i & d) == 0:
                _compare_exchange(v, i, i + d)
        d //= 2


def _index_kernel(qi_ref, sm_ref, ki_ref, o_ref, keys, top_scr, *, qoff, topk, idx_bits):
    j = pl.program_id(1)
    qb = qi_ref.shape[1]
    ltot = ki_ref.shape[1]
    kb = IDX_KB
    kmax = qoff + (j + 1) * qb
    nkb = lax.div(kmax + (kb - 1), kb)
    sub = kb // LANES

    qv = qi_ref[0].astype(bf16)
    stacked = jnp.concatenate([qv[:, h * IDX_DIM:(h + 1) * IDX_DIM] for h in range(IDX_HEADS)], axis=0)
    w = sm_ref[0][:, SMALL_WI:SMALL_WI + IDX_HEADS] * (IDX_HEADS ** -0.5) * (IDX_DIM ** -0.5)
    wb = [jnp.broadcast_to(w[:, h:h + 1], (qb, kb)) for h in range(IDX_HEADS)]
    qchunk = lax.shift_right_arithmetic(qoff + j * qb + lax.broadcasted_iota(i32, (qb, 1), 0), 6)
    lane_pos = lax.broadcasted_iota(i32, (1, kb), 1)

    def score_body(i, carry):
        off = pl.multiple_of(i * kb, kb)
        s = _dot_nt(stacked, ki_ref[0, pl.ds(off, kb), :])
        acc = jnp.maximum(s[0:qb], 0.0) * wb[0]
        for h in range(1, IDX_HEADS):
            acc = acc + jnp.maximum(s[h * qb:(h + 1) * qb], 0.0) * wb[h]
        adm = lax.shift_right_arithmetic(off + lane_pos, 6) <= qchunk
        keys[:, pl.ds(off, kb)] = jnp.where(adm, jnp.where(acc == 0.0, 0.0, acc), -jnp.inf)
        return carry

    lax.fori_loop(0, nkb, score_body, 0)

    group = TOP_R * LANES
    ngrp = lax.div(kmax + (group - 1), group)

    def pad_body(i, carry):
        keys[:, pl.ds(pl.multiple_of(i * kb, kb), kb)] = jnp.full((qb, kb), -jnp.inf, f32)
        return carry

    lax.fori_loop(nkb, ngrp * (group // kb), pad_body, 0)
    top_scr[...] = jnp.full(top_scr.shape, -jnp.inf, f32)

    def top_body(gidx, carry):
        base = gidx * group
        for s in range(qb // SUBLANES):
            rs = slice(s * SUBLANES, (s + 1) * SUBLANES)
            v = [keys[rs, pl.ds(pl.multiple_of(base + i * LANES, LANES), LANES)] for i in range(TOP_R)]
            _sort_desc(v)
            merged = [jnp.maximum(top_scr[i, rs, :], v[TOP_R - 1 - i]) for i in range(TOP_R)]
            _bitonic_merge_desc(merged)
            for i in range(TOP_R):
                top_scr[i, rs, :] = merged[i]
        return carry

    lax.fori_loop(0, ngrp, top_body, 0)

    def flip(bits):
        return bits ^ (lax.shift_right_arithmetic(bits, 31) & INT_MAX)

    def to_key(x):
        k = flip(lax.bitcast_convert_type(x, i32))
        return jnp.where(k > 0, k - SUBNORMAL_SPAN, jnp.where(k < 0, k + SUBNORMAL_SPAN + 1, 0))

    def to_score(key):
        k = jnp.where(key > 0, key + SUBNORMAL_SPAN, jnp.where(key < 0, key - SUBNORMAL_SPAN - 1, 0))
        return lax.bitcast_convert_type(flip(k), f32)

    def count_top(thr):
        tb = jnp.broadcast_to(to_score(thr), (qb, LANES))
        acc = (top_scr[0] >= tb).astype(i32)
        for i in range(1, TOP_R):
            acc = acc + (top_scr[i] >= tb).astype(i32)
        return jnp.sum(acc, axis=1, keepdims=True)

    def count(pred):
        def body(i, acc):
            off = pl.multiple_of(i * kb, kb)
            blk = keys[:, pl.ds(off, kb)]
            m = pred(blk, off).astype(i32)
            for c in range(sub):
                acc = acc + m[:, c * LANES:(c + 1) * LANES]
            return acc
        acc = lax.fori_loop(0, nkb, body, jnp.zeros((qb, LANES), i32))
        return jnp.sum(acc, axis=1, keepdims=True)

    def count_ge(thr):
        tb = jnp.broadcast_to(to_score(thr), (qb, kb))
        return count(lambda blk, off: blk >= tb)

    assert topk <= 2 * LANES and TOP_R >= 2
    hi0 = to_key(jnp.max(top_scr[0], axis=1, keepdims=True)) + 1
    lo0 = jnp.minimum(to_key(jnp.min(top_scr[1], axis=1, keepdims=True)), hi0 - 1)

    def search(count_fn, lo_start):
        def cond(c):
            it, open_rows, _, _ = c
            return (open_rows > 0) & (it < 40)

        def body(c):
            it, _, lo, hi = c
            mid = lax.shift_right_arithmetic(lo, 1) + lax.shift_right_arithmetic(hi, 1) + (lo & hi & 1)
            n = count_fn(mid)
            ge = n >= topk
            lo_n = jnp.where(ge, mid, lo)
            hi_n = jnp.where(n == topk, mid + 1, jnp.where(ge, hi, mid))
            open_rows = jnp.max((lo_n < hi_n - 1).astype(i32))
            return it + 1, open_rows, lo_n, hi_n

        open0 = jnp.max((lo_start < hi0 - 1).astype(i32))
        return lax.while_loop(cond, body, (jnp.int32(0), open0, lo_start, hi0))[2]

    thr_c = search(count_top, lo0)
    n_ge_c = count_ge(thr_c)
    bad = (n_ge_c != count_top(thr_c)) & (thr_c > NEG_INF_KEY)

    def full_search(_):
        t = search(count_ge, lo0)
        return t, count_ge(t)

    thr, n_ge = lax.cond(jnp.max(bad.astype(i32)) > 0, full_search, lambda _: (thr_c, n_ge_c), 0)
    n_gt = count_ge(thr + 1)
    tie = (n_ge > topk) & (thr > NEG_INF_KEY)
    need = topk - n_gt

    def tie_cut(_):
        tb = jnp.broadcast_to(to_score(thr), (qb, kb))

        def body(_, c):
            lo, hi = c
            mid = lax.shift_right_arithmetic(lo + hi, 1)
            mb = jnp.broadcast_to(mid, (qb, kb))
            n = count(lambda blk, off: (blk == tb) & ((off + lane_pos) <= mb))
            ok = n >= need
            return jnp.where(ok, lo, mid), jnp.where(ok, mid, hi)

        lo_i = jnp.full((qb, 1), -1, i32)
        hi_i = jnp.full((qb, 1), 1, i32) * (nkb * kb - 1)
        _, cut = lax.fori_loop(0, idx_bits, body, (lo_i, hi_i))
        return jnp.where(tie, cut, INT_MAX)

    any_tie = jnp.max(tie.astype(i32)) > 0
    cut = lax.cond(any_tie, tie_cut, lambda _: jnp.full((qb, 1), INT_MAX, i32), 0)

    tb = jnp.broadcast_to(to_score(thr), (qb, kb))
    cb = jnp.broadcast_to(cut, (qb, kb))

    def out_body(i, carry):
        off = pl.multiple_of(i * kb, kb)
        blk = keys[:, pl.ds(off, kb)]
        sel = ((blk > tb) | ((blk == tb) & ((off + lane_pos) <= cb))) & (blk > -jnp.inf)
        o_ref[0, :, pl.ds(off, kb)] = jnp.where(sel, 0.0, NEG_BIG).astype(bf16)
        return carry

    lax.fori_loop(0, nkb, out_body, 0)

    def fill_body(i, carry):
        off = pl.multiple_of(i * kb, kb)
        o_ref[0, :, pl.ds(off, kb)] = jnp.full((qb, kb), NEG_BIG, bf16)
        return carry

    lax.fori_loop(nkb, ltot // kb, fill_body, 0)


def _dsa_mask(proj, ki16, *, qb, qoff, topk):
    b, t, _ = proj.shape
    ltot = ki16.shape[1]
    idx_bits = int(math.ceil(math.log2(ltot))) + 1
    kern = functools.partial(_index_kernel, qoff=qoff, topk=topk, idx_bits=idx_bits)
    return pl.pallas_call(
        kern,
        out_shape=jax.ShapeDtypeStruct((b, t, ltot), bf16),
        grid=(b, t // qb),
        in_specs=[pl.BlockSpec((1, qb, C_QI), lambda i, j: (i, j, OFF_QI // C_QI)),
                  pl.BlockSpec((1, qb, LANES), lambda i, j: (i, j, OFF_SMALL // LANES)),
                  pl.BlockSpec((1, ltot, IDX_DIM), lambda i, j: (i, 0, 0))],
        out_specs=pl.BlockSpec((1, qb, ltot), lambda i, j: (i, j, 0)),
        scratch_shapes=[pltpu.VMEM((qb, -(-ltot // (TOP_R * LANES)) * TOP_R * LANES), f32),
                        pltpu.VMEM((TOP_R, qb, LANES), f32)],
        compiler_params=_cp(("arbitrary", "arbitrary")),
        name="dsa_index",
    )(proj, proj, ki16)


def _attn_kernel(qidx_ref, kidx_ref, flag_ref, case_ref, q_ref, k_ref, v_ref, mb_ref, bias_ref, o_ref,
                 q_scr, m_scr, l_scr, acc_scr, s_scr, p_scr, a_scr, mb_scr):
    s_id = pl.program_id(1)
    qb = q_ref.shape[1]
    kb = k_ref.shape[1]
    rows = C_GROUP * qb
    nt = kb // LANES
    flags = flag_ref[s_id]

    @pl.when((flags & 1) != 0)
    def _():
        qv = (q_ref[0] * (C_HEAD_DIM ** -0.5 * LOG2E)).astype(bf16)
        for g in range(C_KV_HEADS):
            for r in range(C_GROUP):
                hd = g * C_GROUP + r
                q_scr[g, r * qb:(r + 1) * qb, :] = qv[:, hd * C_HEAD_DIM:(hd + 1) * C_HEAD_DIM]
        m_scr[...] = jnp.full(m_scr.shape, NEG_BIG, f32)
        l_scr[...] = jnp.zeros_like(l_scr)
        acc_scr[...] = jnp.zeros_like(acc_scr)

    def visit(with_bias):
        kblk = k_ref[0]
        vblk = v_ref[0]
        mb_scr[...] = mb_ref[0].astype(f32)
        units = [(g, r) for g in range(C_KV_HEADS) for r in range(C_GROUP)]

        def qk(u):
            g, r = units[u]
            hs = slice(r * qb, (r + 1) * qb)
            s_scr[g, hs, :] = _dot_nt(q_scr[g, hs, :], kblk[:, g * C_HEAD_DIM:(g + 1) * C_HEAD_DIM])

        def soft(u):
            g, r = units[u]
            for c in range(qb // ATT_RC):
                q0 = c * ATT_RC
                rs = slice(r * qb + q0, r * qb + q0 + ATT_RC)
                s = s_scr[g, rs, :] + mb_scr[q0:q0 + ATT_RC, :]
                if with_bias:
                    s = s + bias_ref[0, g * C_GROUP + r, q0:q0 + ATT_RC, :]
                t = s[:, 0:LANES]
                for i in range(1, nt):
                    t = jnp.maximum(t, s[:, i * LANES:(i + 1) * LANES])
                m_old = m_scr[g, rs, :]
                m_new = jnp.maximum(m_old, jnp.max(t, axis=1, keepdims=True))
                alpha = jnp.exp2(m_old - m_new)
                p = jnp.exp2(s - jnp.concatenate([m_new] * nt, axis=1))
                psum = p[:, 0:LANES]
                for i in range(1, nt):
                    psum = psum + p[:, i * LANES:(i + 1) * LANES]
                l_scr[g, rs, :] = alpha * l_scr[g, rs, :] + psum
                m_scr[g, rs, :] = m_new
                a_scr[g, rs, :] = alpha
                p_scr[g, rs, :] = p.astype(bf16)

        def pv(u):
            g, r = units[u]
            hs = slice(r * qb, (r + 1) * qb)
            acc_scr[g, hs, :] = a_scr[g, hs, :] * acc_scr[g, hs, :] + _dot(
                p_scr[g, hs, :], vblk[:, g * C_HEAD_DIM:(g + 1) * C_HEAD_DIM])

        qk(0)
        for u in range(len(units)):
            if u + 1 < len(units):
                qk(u + 1)
            soft(u)
            pv(u)

    @pl.when((flags & 4) != 0)
    def _():
        visit(True)

    @pl.when((flags & 4) == 0)
    def _():
        visit(False)

    @pl.when((flags & 2) != 0)
    def _():
        for g in range(C_KV_HEADS):
            o = acc_scr[g] / jnp.sum(l_scr[g], axis=1, keepdims=True)
            for r in range(C_GROUP):
                hd = g * C_GROUP + r
                o_ref[0, :, hd * C_HEAD_DIM:(hd + 1) * C_HEAD_DIM] = o[r * qb:(r + 1) * qb, :].astype(bf16)


def _attn_schedule(nq, qb, qoff, offsets):
    qidx, kidx, flags, case = [], [], [], []
    zero_case = len(offsets)
    for j in range(nq):
        kmax = qoff + (j + 1) * qb
        nkb = -(-kmax // ATT_KB)
        for kb in range(nkb):
            off = kmax - BIAS_WIN - kb * ATT_KB
            qidx.append(j)
            kidx.append(kb)
            near = off in offsets
            flags.append((1 if kb == 0 else 0) | (2 if kb == nkb - 1 else 0) | (4 if near else 0))
            case.append(offsets.index(off) if near else zero_case)
            assert near == (-BIAS_WIN < off < ATT_KB)
    return [jnp.asarray(np.asarray(a, np.int32)) for a in (qidx, kidx, flags, case)]


def _dsa_attend(proj, k16, v16, maskbias, tables, *, qb, qoff, offsets):
    b, t, _ = proj.shape
    nq = t // qb
    qidx, kidx, flags, case = _attn_schedule(nq, qb, qoff, offsets)
    ns = int(qidx.shape[0])
    rows = C_GROUP * qb
    grid_spec = pltpu.PrefetchScalarGridSpec(
        num_scalar_prefetch=4,
        grid=(b, ns),
        in_specs=[pl.BlockSpec((1, qb, C_Q), lambda i, s, qi, ki, fl, ca: (i, qi[s], OFF_Q // C_Q)),
                  pl.BlockSpec((1, ATT_KB, C_KV), lambda i, s, qi, ki, fl, ca: (i, ki[s], 0)),
                  pl.BlockSpec((1, ATT_KB, C_KV), lambda i, s, qi, ki, fl, ca: (i, ki[s], 0)),
                  pl.BlockSpec((1, qb, ATT_KB), lambda i, s, qi, ki, fl, ca: (i, qi[s], ki[s])),
                  pl.BlockSpec((1, C_HEADS, qb, ATT_KB), lambda i, s, qi, ki, fl, ca: (ca[s], 0, 0, 0))],
        out_specs=pl.BlockSpec((1, qb, C_WIDTH), lambda i, s, qi, ki, fl, ca: (i, qi[s], 0)),
        scratch_shapes=[pltpu.VMEM((C_KV_HEADS, rows, C_HEAD_DIM), bf16),
                        pltpu.VMEM((C_KV_HEADS, rows, LANES), f32),
                        pltpu.VMEM((C_KV_HEADS, rows, LANES), f32),
                        pltpu.VMEM((C_KV_HEADS, rows, C_HEAD_DIM), f32),
                        pltpu.VMEM((C_KV_HEADS, rows, ATT_KB), f32),
                        pltpu.VMEM((C_KV_HEADS, rows, ATT_KB), bf16),
                        pltpu.VMEM((C_KV_HEADS, rows, LANES), f32),
                        pltpu.VMEM((qb, ATT_KB), f32)])
    return pl.pallas_call(
        _attn_kernel,
        out_shape=jax.ShapeDtypeStruct((b, t, C_WIDTH), bf16),
        grid_spec=grid_spec,
        compiler_params=_cp(("arbitrary", "arbitrary")),
        name="dsa_attend",
    )(qidx, kidx, flags, case, proj, k16, v16, maskbias, tables)


def _window_offsets(nq, qb, qoff):
    offs = set()
    for j in range(nq):
        kmax = qoff + (j + 1) * qb
        for kb in range(-(-kmax // ATT_KB)):
            off = kmax - BIAS_WIN - kb * ATT_KB
            if -BIAS_WIN < off < ATT_KB:
                offs.add(off)
    return sorted(offs)


def _mixer_c(proj, k_all16, v_all16, ki_all16, rel_bias, *, qb, qoff, true_len):
    b, t, _ = proj.shape
    topk = min(TOPK_MAX, true_len // 4)
    assert BIAS_WIN - qb + 1 > MAX_DISTANCE
    offsets = _window_offsets(t // qb, qb, qoff)
    tables = _bias_tables(rel_bias, qb, offsets)
    idx_qb = IDX_QB if t % IDX_QB == 0 else qb
    maskbias = _dsa_mask(proj, ki_all16, qb=idx_qb, qoff=qoff, topk=topk)
    return _dsa_attend(proj, k_all16, v_all16, maskbias, tables, qb=qb, qoff=qoff, offsets=offsets)


def _merge_kernel(x_ref, ada_ref, gt_ref, oa_ref, ob_ref, oc_ref, wa_ref, wb_ref, wc_ref, wo_ref, o_ref):
    bt, tt, d = x_ref.shape
    rows = bt * tt
    gates = jax.nn.sigmoid(gt_ref[...].reshape(rows, GATE_COLS))
    merged = (gates[:, 0:d] * _dot(oa_ref[...].reshape(rows, A_WIDTH), wa_ref[...])
              + gates[:, d:2 * d] * _dot(ob_ref[...].reshape(rows, SSM_INNER), wb_ref[...])
              + gates[:, 2 * d:3 * d] * _dot(oc_ref[...].reshape(rows, C_WIDTH), wc_ref[...]))
    mix = _dot(merged.astype(bf16), wo_ref[...]).reshape(bt, tt, d)
    o_ref[...] = x_ref[...] + ada_ref[...][:, 5:6, :] * mix


def _merge(x, ada, proj, o_a, o_b, o_c, wa16, wb16, wc16, wo16):
    b, t, d = x.shape
    bt, tt = _token_tile(b, t, rows=256)
    tok = lambda w: pl.BlockSpec((bt, tt, w), lambda i, j: (i, j, 0))
    full = lambda r, c: pl.BlockSpec((r, c), lambda i, j: (0, 0))
    return pl.pallas_call(
        _merge_kernel,
        out_shape=jax.ShapeDtypeStruct((b, t, d), f32),
        grid=(b // bt, t // tt),
        in_specs=[tok(d), pl.BlockSpec((bt, N_ADA, d), lambda i, j: (i, 0, 0)),
                  pl.BlockSpec((bt, tt, GATE_COLS), lambda i, j: (i, j, OFF_GATE // GATE_COLS)),
                  tok(A_WIDTH), tok(SSM_INNER), tok(C_WIDTH),
                  full(A_WIDTH, d), full(SSM_INNER, d), full(C_WIDTH, d), full(d, d)],
        out_specs=tok(d),
        compiler_params=_cp(("arbitrary", "arbitrary")),
        name="merge",
    )(x, ada, proj, o_a, o_b, o_c, wa16, wb16, wc16, wo16)


def _pad_keys(x, mult):
    pad = (-x.shape[1]) % mult
    if pad:
        x = jnp.concatenate([x, jnp.zeros((x.shape[0], pad) + x.shape[2:], x.dtype)], axis=1)
    return x


def _layer(x, ada, lp, rel_bias, past, g_final, last):
    b, t, d = x.shape
    x = _ffn(x, ada, lp['g_ffn1'], lp['w_ffn1_in'], lp['w_ffn1_out'], g_final, k0=0, final_norm=False)
    proj, k16, v16, ki16 = _inproj(x, ada, lp['g_mix'], lp['w_proj'], k0=3)
    k_new = proj[:, :, OFF_K:OFF_K + C_KV]
    v_new = proj[:, :, OFF_V:OFF_V + C_KV]
    ki_new = proj[:, :, OFF_SMALL + SMALL_KI:OFF_SMALL + LANES]
    new_conv = proj[:, t - (CONV_W - 1):, OFF_XBC:OFF_XBC + CONV_DIM]

    o_a, va = _mixer_a(proj, lp['g_a'], lp['w_s'], lp['b_s'])

    if past is None:
        conv_init = jnp.zeros((b, CONV_W - 1, CONV_DIM), f32)
        h0 = jnp.zeros((b, SSM_HEADS, SSM_HEAD_DIM, SSM_STATE), f32)
    else:
        h0, conv_init = past[3], past[4]
    o_b, new_ssm = _mixer_b(proj, conv_init, h0, lp['w_conv'], lp['b_conv'], lp['dt_bias'], lp['a_log'],
                            lp['d_skip'], lp['g_ssm'])
    new_ssm = new_ssm.reshape(b, SSM_HEADS, SSM_HEAD_DIM, SSM_STATE)

    if past is None:
        k_all, v_all, ki_all = k16, v16, ki16
        qb, qoff, true_len = Q_BLOCK, 0, t
    else:
        plen = past[0].shape[1]
        k_all = jnp.concatenate([past[0].reshape(b, plen, C_KV).astype(bf16), k16], axis=1)
        v_all = jnp.concatenate([past[1].reshape(b, plen, C_KV).astype(bf16), v16], axis=1)
        ki_all = jnp.concatenate([past[2].astype(bf16), ki16], axis=1)
        qb, qoff, true_len = t, plen, plen + t
    o_c = _mixer_c(proj, _pad_keys(k_all, ATT_KB), _pad_keys(v_all, ATT_KB), _pad_keys(ki_all, ATT_KB), rel_bias,
                   qb=qb, qoff=qoff, true_len=true_len)

    x = _merge(x, ada, proj, o_a, o_b, o_c, lp['w_br_a'], lp['w_br_b'], lp['w_br_c'], lp['w_out'])
    x = _ffn(x, ada, lp['g_ffn2'], lp['w_ffn2_in'], lp['w_ffn2_out'], g_final, k0=6, final_norm=last)
    states = (k_new.reshape(b, t, C_KV_HEADS, C_HEAD_DIM), v_new.reshape(b, t, C_KV_HEADS, C_HEAD_DIM),
              ki_new, new_ssm, new_conv, va)
    return x, states


def kernel(x_prompt, x_sample, c_prompt, c_sample, cache_k, cache_v, cache_kidx, state_ssm, state_conv, rel_bias, w_ada, b_ada, g_ffn1, w_ffn1_in, w_ffn1_out, g_mix, w_in, g_a, w_s, b_s, w_conv, b_conv, dt_bias, a_log, d_skip, g_ssm, w_br_a, w_br_b, w_br_c, w_out, g_ffn2, w_ffn2_in, w_ffn2_out, g_final):
    depth = w_ada.shape[0]
    bp, bs = c_prompt.shape[0], c_sample.shape[0]
    assert x_prompt.shape[1] % ATT_KB == 0 and x_prompt.shape[1] % Q_BLOCK == 0
    assert x_sample.shape[1] == CHUNK

    rows = -(-(bp + bs) // 8) * 8
    c_all = jnp.concatenate([c_prompt, c_sample, jnp.zeros((rows - bp - bs, D_MODEL), f32)], axis=0)
    ada_all = _ada_all(c_all, w_ada.astype(bf16), b_ada)
    ada_all = ada_all.reshape(depth, rows, N_ADA, D_MODEL)

    yp, ys = x_prompt, x_sample
    st_p, st_s = [], []
    for i in range(depth):
        lp = dict(g_ffn1=g_ffn1[i], w_ffn1_in=w_ffn1_in[i].astype(bf16), w_ffn1_out=w_ffn1_out[i].astype(bf16),
                  g_mix=g_mix[i], w_proj=_layout_w_in(w_in[i]).astype(bf16), g_a=g_a[i], w_s=w_s[i], b_s=b_s[i],
                  w_conv=w_conv[i], b_conv=b_conv[i], dt_bias=dt_bias[i], a_log=a_log[i], d_skip=d_skip[i],
                  g_ssm=g_ssm[i], w_br_a=w_br_a[i].astype(bf16), w_br_b=w_br_b[i].astype(bf16),
                  w_br_c=w_br_c[i].astype(bf16), w_out=w_out[i].astype(bf16), g_ffn2=g_ffn2[i],
                  w_ffn2_in=w_ffn2_in[i].astype(bf16), w_ffn2_out=w_ffn2_out[i].astype(bf16))
        last = i == depth - 1
        yp, sp = _layer(yp, ada_all[i, :bp], lp, rel_bias, None, g_final, last)
        ys, ss = _layer(ys, ada_all[i, bp:bp + bs], lp, rel_bias,
                        (cache_k[i], cache_v[i], cache_kidx[i], state_ssm[i], state_conv[i]), g_final, last)
        st_p.append(sp)
        st_s.append(ss)

    stack = lambda sts, k: jnp.stack([s[k] for s in sts])
    return (yp, ys,
            stack(st_p, 0), stack(st_p, 1), stack(st_p, 2), stack(st_p, 3), stack(st_p, 4),
            stack(st_s, 0), stack(st_s, 1), stack(st_s, 2), stack(st_s, 3), stack(st_s, 4), stack(st_s, 5))
```

```python
import functools
import math

import numpy as np
import jax
import jax.numpy as jnp
from jax import lax
from jax.experimental import pallas as pl
from jax.experimental.pallas import tpu as pltpu

f32 = jnp.float32
bf16 = jnp.bfloat16
i32 = jnp.int32

D_MODEL = 1024
CHUNK = 64
EPS = 1e-6
N_ADA = 9
D_FF = 2816
A_WIDTH = D_MODEL
A_GROUPS = 8
A_GROUP_DIM = A_WIDTH // A_GROUPS
A_CHUNK = 128
SSM_INNER = 2 * D_MODEL
SSM_HEAD_DIM = 64
SSM_HEADS = SSM_INNER // SSM_HEAD_DIM
SSM_GROUPS = 4
SSM_HPG = SSM_HEADS // SSM_GROUPS
SSM_STATE = 128
CONV_W = 4
CONV_DIM = SSM_INNER + 2 * SSM_GROUPS * SSM_STATE
C_HEADS = 8
C_KV_HEADS = 4
C_GROUP = C_HEADS // C_KV_HEADS
C_HEAD_DIM = 128
C_WIDTH = C_HEADS * C_HEAD_DIM
IDX_HEADS = 8
IDX_DIM = 64
TOPK_MAX = 256
Q_BLOCK = 128
N_BUCKETS = 32
MAX_DISTANCE = 128
A_COLS = 2 * A_WIDTH
B_COLS = SSM_INNER + CONV_DIM + SSM_HEADS
C_Q = C_HEADS * C_HEAD_DIM
C_KV = C_KV_HEADS * C_HEAD_DIM
C_QI = IDX_HEADS * IDX_DIM
GATE_COLS = 3 * D_MODEL

LANES = 128
SUBLANES = 8
VMEM_LIMIT = 56 * 1024 * 1024

OFF_GATE = 0
OFF_XBC = OFF_GATE + GATE_COLS
OFF_PA = OFF_XBC + CONV_DIM
OFF_Z = OFF_PA + A_COLS
OFF_Q = OFF_Z + SSM_INNER
OFF_K = OFF_Q + C_Q
OFF_V = OFF_K + C_KV
OFF_QI = OFF_V + C_KV
OFF_SMALL = OFF_QI + C_QI
SMALL_DT = 0
SMALL_WI = SSM_HEADS
SMALL_KI = LANES - IDX_DIM
PROJ_TILE = 1024
PROJ_W = ((OFF_SMALL + LANES + PROJ_TILE - 1) // PROJ_TILE) * PROJ_TILE

NEG_BIG = -1e30
SUBNORMAL_SPAN = 0x007FFFFF
NEG_INF_KEY = -2139095041 + SUBNORMAL_SPAN + 1
INT_MIN = -2147483648
INT_MAX = 2147483647
ATT_KB = 1024
IDX_KB = 512
TOP_R = 16
IDX_QB = 256
ATT_RC = 8
LOG2E = math.log2(math.e)
ATT_QB = 256
BIAS_WIN = ATT_QB + MAX_DISTANCE
FAR_BUCKET = N_BUCKETS // 2 - 1


def _cp(sem, vmem=VMEM_LIMIT):
    return pltpu.CompilerParams(dimension_semantics=sem, vmem_limit_bytes=vmem)


def _rms(x, g):
    return x * lax.rsqrt(jnp.mean(x * x, axis=-1, keepdims=True) + EPS) * g


def _dot(a, b):
    return jnp.dot(a, b, preferred_element_type=f32)


def _dot_nt(a, b):
    return lax.dot_general(a, b, (((1,), (1,)), ((), ())), preferred_element_type=f32)


def _dot_tn(a, b):
    return lax.dot_general(a, b, (((0,), (0,)), ((), ())), preferred_element_type=f32)


def _dot_exact(a, b):
    return jnp.dot(a, b, precision=lax.Precision.HIGHEST, preferred_element_type=f32)


def _ada_kernel(c_ref, w_ref, b_ref, o_ref):
    s = jax.nn.silu(c_ref[...]).astype(bf16)
    o_ref[0] = _dot(s, w_ref[0]) + b_ref[0]


def _ada_all(c_all, w_ada16, b_ada):
    depth = w_ada16.shape[0]
    rows = c_all.shape[0]
    n_out = N_ADA * D_MODEL
    tn = 9 * LANES
    return pl.pallas_call(
        _ada_kernel,
        out_shape=jax.ShapeDtypeStruct((depth, rows, n_out), f32),
        grid=(depth, n_out // tn),
        in_specs=[pl.BlockSpec((rows, D_MODEL), lambda d, n: (0, 0)),
                  pl.BlockSpec((1, D_MODEL, tn), lambda d, n: (d, 0, n)),
                  pl.BlockSpec((1, 1, tn), lambda d, n: (d, 0, n))],
        out_specs=pl.BlockSpec((1, rows, tn), lambda d, n: (d, 0, n)),
        compiler_params=_cp(("arbitrary", "arbitrary")),
        name="ada",
    )(c_all, w_ada16, b_ada.reshape(depth, 1, n_out))


def _ffn_kernel(x_ref, ada_ref, g_ref, wg_ref, wu_ref, wo_ref, gf_ref, o_ref, h_scr, acc_scr, *, k0, final_norm):
    f = pl.program_id(2)
    bt, tt, d = x_ref.shape
    rows = bt * tt

    @pl.when(f == 0)
    def _():
        ada = ada_ref[...]
        h = _rms(x_ref[...], g_ref[...]) * (1.0 + ada[:, k0 + 1:k0 + 2, :]) + ada[:, k0:k0 + 1, :]
        h_scr[...] = h.reshape(rows, d).astype(bf16)
        acc_scr[...] = jnp.zeros_like(acc_scr)

    h = h_scr[...]
    gate = _dot(h, wg_ref[...])
    up = _dot(h, wu_ref[...])
    a = (jax.nn.silu(gate) * up).astype(bf16)
    acc_scr[...] += _dot(a, wo_ref[...])

    @pl.when(f == pl.num_programs(2) - 1)
    def _():
        ada = ada_ref[...]
        y = x_ref[...] + (0.5 * ada[:, k0 + 2:k0 + 3, :]) * acc_scr[...].reshape(bt, tt, d)
        if final_norm:
            y = _rms(y, gf_ref[...])
        o_ref[...] = y


def _token_tile(b, t, rows=512):
    if t >= rows:
        return 1, rows
    return min(b, rows // t), t


def _ffn(x, ada, g, w_in16, w_out16, g_final, *, k0, final_norm):
    b, t, d = x.shape
    bt, tt = _token_tile(b, t)
    tf = D_FF // 2
    nf = D_FF // tf
    kern = functools.partial(_ffn_kernel, k0=k0, final_norm=final_norm)
    return pl.pallas_call(
        kern,
        out_shape=jax.ShapeDtypeStruct((b, t, d), f32),
        grid=(b // bt, t // tt, nf),
        in_specs=[pl.BlockSpec((bt, tt, d), lambda i, j, f: (i, j, 0)),
                  pl.BlockSpec((bt, N_ADA, d), lambda i, j, f: (i, 0, 0)),
                  pl.BlockSpec((1, d), lambda i, j, f: (0, 0)),
                  pl.BlockSpec((d, tf), lambda i, j, f: (0, f)),
                  pl.BlockSpec((d, tf), lambda i, j, f: (0, f + nf)),
                  pl.BlockSpec((tf, d), lambda i, j, f: (f, 0)),
                  pl.BlockSpec((1, d), lambda i, j, f: (0, 0))],
        out_specs=pl.BlockSpec((bt, tt, d), lambda i, j, f: (i, j, 0)),
        scratch_shapes=[pltpu.VMEM((bt * tt, d), bf16), pltpu.VMEM((bt * tt, d), f32)],
        compiler_params=_cp(("arbitrary", "arbitrary", "arbitrary")),
        name="ffn",
    )(x, ada, g.reshape(1, d), w_in16, w_in16, w_out16, g_final.reshape(1, d))


def _inproj_kernel(x_ref, ada_ref, g_ref, w_ref, o_ref, k16_ref, v16_ref, ki16_ref, h_scr, *, k0):
    n = pl.program_id(2)
    bt, tt, d = x_ref.shape

    @pl.when(n == 0)
    def _():
        ada = ada_ref[...]
        h = _rms(x_ref[...], g_ref[...]) * (1.0 + ada[:, k0 + 1:k0 + 2, :]) + ada[:, k0:k0 + 1, :]
        h_scr[...] = h.reshape(bt * tt, d).astype(bf16)

    res = _dot(h_scr[...], w_ref[...])
    o_ref[...] = res.reshape(bt, tt, PROJ_TILE)

    @pl.when(n == OFF_K // PROJ_TILE)
    def _():
        k16_ref[...] = res[:, :C_KV].astype(bf16).reshape(bt, tt, C_KV)
        v16_ref[...] = res[:, C_KV:2 * C_KV].astype(bf16).reshape(bt, tt, C_KV)

    @pl.when(n == OFF_SMALL // PROJ_TILE)
    def _():
        lo = OFF_SMALL % PROJ_TILE + SMALL_KI
        ki16_ref[...] = res[:, lo:lo + IDX_DIM].astype(bf16).reshape(bt, tt, IDX_DIM)


def _inproj(x, ada, g, w_proj16, *, k0):
    b, t, d = x.shape
    bt, tt = _token_tile(b, t, rows=1024)
    assert OFF_K % PROJ_TILE == 0 and OFF_V == OFF_K + C_KV and 2 * C_KV <= PROJ_TILE
    tok = lambda w: pl.BlockSpec((bt, tt, w), lambda i, j, n: (i, j, 0))
    return pl.pallas_call(
        functools.partial(_inproj_kernel, k0=k0),
        out_shape=(jax.ShapeDtypeStruct((b, t, PROJ_W), f32), jax.ShapeDtypeStruct((b, t, C_KV), bf16),
                   jax.ShapeDtypeStruct((b, t, C_KV), bf16), jax.ShapeDtypeStruct((b, t, IDX_DIM), bf16)),
        grid=(b // bt, t // tt, PROJ_W // PROJ_TILE),
        in_specs=[pl.BlockSpec((bt, tt, d), lambda i, j, n: (i, j, 0)),
                  pl.BlockSpec((bt, N_ADA, d), lambda i, j, n: (i, 0, 0)),
                  pl.BlockSpec((1, d), lambda i, j, n: (0, 0)),
                  pl.BlockSpec((d, PROJ_TILE), lambda i, j, n: (0, n))],
        out_specs=(pl.BlockSpec((bt, tt, PROJ_TILE), lambda i, j, n: (i, j, n)), tok(C_KV), tok(C_KV), tok(IDX_DIM)),
        scratch_shapes=[pltpu.VMEM((bt * tt, d), bf16)],
        compiler_params=_cp(("arbitrary", "arbitrary", "arbitrary")),
        name="inproj",
    )(x, ada, g.reshape(1, d), w_proj16)


def _layout_w_in(w_in):
    d = w_in.shape[0]
    o = 0
    pa = w_in[:, o:o + A_COLS]; o += A_COLS
    z = w_in[:, o:o + SSM_INNER]; o += SSM_INNER
    xbc = w_in[:, o:o + CONV_DIM]; o += CONV_DIM
    dt = w_in[:, o:o + SSM_HEADS]; o += SSM_HEADS
    q = w_in[:, o:o + C_Q]; o += C_Q
    k = w_in[:, o:o + C_KV]; o += C_KV
    v = w_in[:, o:o + C_KV]; o += C_KV
    qi = w_in[:, o:o + C_QI]; o += C_QI
    ki = w_in[:, o:o + IDX_DIM]; o += IDX_DIM
    wi = w_in[:, o:o + IDX_HEADS]; o += IDX_HEADS
    gates = w_in[:, o:o + GATE_COLS]
    small = jnp.concatenate([dt, wi, jnp.zeros((d, SMALL_KI - SMALL_WI - IDX_HEADS), w_in.dtype), ki], axis=1)
    pad = jnp.zeros((d, PROJ_W - OFF_SMALL - LANES), w_in.dtype)
    return jnp.concatenate([gates, xbc, pa, z, q, k, v, qi, small, pad], axis=1)


def _mixa_kernel(pa_ref, ga_ref, ws_ref, bs_ref, oa_ref, va_ref):
    n = pa_ref.shape[1]
    ge = jax.nn.gelu(pa_ref[0])
    u = ge[:, :A_WIDTH]
    v = _rms(ge[:, A_WIDTH:], ga_ref[...])
    va_ref[0] = v
    vb = v.astype(bf16)
    tri = lax.broadcasted_iota(i32, (n, n), 0) >= lax.broadcasted_iota(i32, (n, n), 1)
    bs = bs_ref[...]
    for g in range(A_GROUPS):
        sl = slice(g * A_GROUP_DIM, (g + 1) * A_GROUP_DIM)
        w = jnp.where(tri, ws_ref[g], 0.0).astype(bf16)
        mixed = _dot(w, vb[:, sl]) + bs[:, g:g + 1]
        oa_ref[0, :, sl] = (u[:, sl] * mixed).astype(bf16)


def _mixer_a(proj, g_a, w_s, b_s):
    b, t, _ = proj.shape
    n = min(t, A_CHUNK)
    ws = w_s[:, :n, :n]
    bs_t = b_s[:, :n].T
    return pl.pallas_call(
        _mixa_kernel,
        out_shape=(jax.ShapeDtypeStruct((b, t, A_WIDTH), bf16), jax.ShapeDtypeStruct((b, t, A_WIDTH), f32)),
        grid=(b, t // n),
        in_specs=[pl.BlockSpec((1, n, A_COLS), lambda i, j: (i, j, OFF_PA // A_COLS)),
                  pl.BlockSpec((1, A_WIDTH), lambda i, j: (0, 0)),
                  pl.BlockSpec((A_GROUPS, n, n), lambda i, j: (0, 0, 0)),
                  pl.BlockSpec((n, A_GROUPS), lambda i, j: (0, 0))],
        out_specs=(pl.BlockSpec((1, n, A_WIDTH), lambda i, j: (i, j, 0)),
                   pl.BlockSpec((1, n, A_WIDTH), lambda i, j: (i, j, 0))),
        compiler_params=_cp(("arbitrary", "arbitrary")),
        name="mixer_a",
    )(proj, g_a.reshape(1, A_WIDTH), ws, bs_t)


CONV_PAD = 8


def _ssd_kernel(z_ref, xbc_ref, sm_ref, cinit_ref, h0_ref, wc_ref, bc_ref, dtb_ref, alog_ref, dsk_ref, gs_ref,
                ex_ref, ob_ref, hs_ref, xpad, ht):
    c = pl.program_id(1)
    q = z_ref.shape[1]
    gw = SSM_HPG * SSM_HEAD_DIM
    tail = CONV_W - 1

    @pl.when(c == 0)
    def _():
        xpad[CONV_PAD - tail:CONV_PAD, :] = cinit_ref[0]
        for g in range(SSM_GROUPS):
            ht[g] = h0_ref[0, g * gw:(g + 1) * gw, :].T

    xpad[CONV_PAD:CONV_PAD + q, :] = xbc_ref[0]
    wc = wc_ref[...]
    conv = bc_ref[...] + wc[0:1, :] * xpad[pl.ds(CONV_PAD - 3, q), :]
    for k in range(1, CONV_W):
        conv = conv + wc[k:k + 1, :] * xpad[pl.ds(CONV_PAD - 3 + k, q), :]
    nxt = xpad[CONV_PAD + q - tail:CONV_PAD + q, :]
    xpad[CONV_PAD - tail:CONV_PAD, :] = nxt
    conv = jax.nn.silu(conv)
    xs = conv[:, :SSM_INNER]
    bm = conv[:, SSM_INNER:SSM_INNER + SSM_GROUPS * SSM_STATE]
    cm = conv[:, SSM_INNER + SSM_GROUPS * SSM_STATE:]

    lane = lax.broadcasted_iota(i32, (q, LANES), 1)
    head_lane = lane < SSM_HEADS
    dt = jnp.where(head_lane, jax.nn.softplus(sm_ref[0] + dtb_ref[...]), 0.0)
    da = dt * (-jnp.exp(alog_ref[...]))
    tri = lax.broadcasted_iota(i32, (q, q), 0) >= lax.broadcasted_iota(i32, (q, q), 1)
    acs = _dot_exact(tri.astype(f32), da)
    acs_t = acs.T
    dt_t = dt.T
    acs_last = acs[q - 1:q, :]
    e_acs = jnp.where(head_lane, jnp.exp(acs), 0.0)
    decay = jnp.where(head_lane, jnp.exp(acs_last - acs) * dt, 0.0)
    ex = ex_ref[...]
    e_acs_x = _dot_exact(e_acs, ex)
    decay_x = _dot_exact(decay, ex)
    e_last_x = e_acs_x[q - 1:q, :]

    xs16 = xs.astype(bf16)
    xdec16 = (xs * decay_x).astype(bf16)
    y_parts = []
    for g in range(SSM_GROUPS):
        bg = bm[:, g * SSM_STATE:(g + 1) * SSM_STATE].astype(bf16)
        cg = cm[:, g * SSM_STATE:(g + 1) * SSM_STATE].astype(bf16)
        cb = _dot_nt(cg, bg)
        htg = ht[g]
        y_inter = _dot(cg, htg.astype(bf16))
        intra = []
        for r in range(SSM_HPG):
            hd = g * SSM_HPG + r
            seg = acs[:, hd:hd + 1] - acs_t[hd:hd + 1, :]
            m = cb * jnp.exp(jnp.where(tri, seg, -jnp.inf)) * dt_t[hd:hd + 1, :]
            intra.append(_dot(m.astype(bf16), xs16[:, hd * SSM_HEAD_DIM:(hd + 1) * SSM_HEAD_DIM]))
        y_parts.append(jnp.concatenate(intra, axis=1) + y_inter * e_acs_x[:, g * gw:(g + 1) * gw])
        ht[g] = htg * e_last_x[:, g * gw:(g + 1) * gw] + _dot_tn(bg, xdec16[:, g * gw:(g + 1) * gw])
    y = jnp.concatenate(y_parts, axis=1) + dsk_ref[...] * xs
    y = _rms(y * jax.nn.silu(z_ref[0]), gs_ref[...])
    ob_ref[0] = y.astype(bf16)

    @pl.when(c == pl.num_programs(1) - 1)
    def _():
        for g in range(SSM_GROUPS):
            hs_ref[0, g * gw:(g + 1) * gw, :] = ht[g].T


def _mixer_b(proj, conv_init, h0, w_conv, b_conv, dt_bias, a_log, d_skip, g_ssm):
    b, t, _ = proj.shape
    q = CHUNK
    pad_row = lambda v: jnp.zeros((1, LANES), f32).at[0, :SSM_HEADS].set(v)
    expand = (jnp.arange(LANES)[:, None] == (jnp.arange(SSM_INNER) // SSM_HEAD_DIM)[None, :]).astype(f32)
    hrows = SSM_HEADS * SSM_HEAD_DIM
    return pl.pallas_call(
        _ssd_kernel,
        out_shape=(jax.ShapeDtypeStruct((b, t, SSM_INNER), bf16),
                   jax.ShapeDtypeStruct((b, hrows, SSM_STATE), f32)),
        grid=(b, t // q),
        in_specs=[pl.BlockSpec((1, q, SSM_INNER), lambda i, c: (i, c, OFF_Z // SSM_INNER)),
                  pl.BlockSpec((1, q, CONV_DIM), lambda i, c: (i, c, OFF_XBC // CONV_DIM)),
                  pl.BlockSpec((1, q, LANES), lambda i, c: (i, c, OFF_SMALL // LANES)),
                  pl.BlockSpec((1, CONV_W - 1, CONV_DIM), lambda i, c: (i, 0, 0)),
                  pl.BlockSpec((1, hrows, SSM_STATE), lambda i, c: (i, 0, 0)),
                  pl.BlockSpec((CONV_W, CONV_DIM), lambda i, c: (0, 0)),
                  pl.BlockSpec((1, CONV_DIM), lambda i, c: (0, 0)),
                  pl.BlockSpec((1, LANES), lambda i, c: (0, 0)),
                  pl.BlockSpec((1, LANES), lambda i, c: (0, 0)),
                  pl.BlockSpec((1, SSM_INNER), lambda i, c: (0, 0)),
                  pl.BlockSpec((1, SSM_INNER), lambda i, c: (0, 0)),
                  pl.BlockSpec((LANES, SSM_INNER), lambda i, c: (0, 0))],
        out_specs=(pl.BlockSpec((1, q, SSM_INNER), lambda i, c: (i, c, 0)),
                   pl.BlockSpec((1, hrows, SSM_STATE), lambda i, c: (i, 0, 0))),
        scratch_shapes=[pltpu.VMEM((CONV_PAD + q, CONV_DIM), f32),
                        pltpu.VMEM((SSM_GROUPS, SSM_STATE, SSM_HPG * SSM_HEAD_DIM), f32)],
        compiler_params=_cp(("arbitrary", "arbitrary")),
        name="mixer_b",
    )(proj, proj, proj, conv_init, h0.reshape(b, hrows, SSM_STATE), w_conv, b_conv.reshape(1, CONV_DIM),
      pad_row(dt_bias), pad_row(a_log), jnp.repeat(d_skip, SSM_HEAD_DIM).reshape(1, SSM_INNER),
      g_ssm.reshape(1, SSM_INNER), expand)


def _t5_bucket(rel):
    nb = N_BUCKETS // 2
    max_exact = nb // 2
    n = jnp.abs(rel)
    nf = jnp.maximum(n, 1).astype(jnp.float32)
    large = max_exact + (jnp.log(nf / max_exact) / math.log(MAX_DISTANCE / max_exact)
                         * (nb - max_exact)).astype(jnp.int32)
    large = jnp.minimum(large, nb - 1)
    return jnp.where(rel > 0, nb, 0) + jnp.where(n < max_exact, n, large)


def _bias_kernel(tab_ref, bk_ref, o_ref):
    h = pl.program_id(1)
    bk = bk_ref[0]
    far = tab_ref[FAR_BUCKET, h]
    acc = jnp.zeros(bk.shape, f32)
    for bucket in range(N_BUCKETS):
        acc = jnp.where(bk == bucket, tab_ref[bucket, h] - far, acc)
    o_ref[0, 0] = acc * LOG2E


def _bias_tables(rel_bias, qb, offsets):
    t = np.arange(qb)[:, None]
    cases = []
    for off in offsets:
        w = np.arange(ATT_KB)[None, :] - off
        valid = (w >= 0) & (w < BIAS_WIN)
        rel = w - BIAS_WIN + qb - t
        cases.append(jnp.where(jnp.asarray(valid), _t5_bucket(jnp.asarray(rel, dtype=jnp.int32)), -1))
    cases.append(jnp.full((qb, ATT_KB), -1, jnp.int32))
    buckets = jnp.stack(cases).astype(i32)
    nc = buckets.shape[0]
    return pl.pallas_call(
        _bias_kernel,
        out_shape=jax.ShapeDtypeStruct((nc, C_HEADS, qb, ATT_KB), f32),
        grid=(nc, C_HEADS),
        in_specs=[pl.BlockSpec(memory_space=pltpu.SMEM),
                  pl.BlockSpec((1, qb, ATT_KB), lambda c, h: (c, 0, 0))],
        out_specs=pl.BlockSpec((1, 1, qb, ATT_KB), lambda c, h: (c, h, 0, 0)),
        compiler_params=_cp(("arbitrary", "arbitrary")),
        name="t5_bias",
    )(rel_bias, buckets)


def _oddeven_merge_sort_pairs(n):
    pairs = []
    p = 1
    while p < n:
        k = p
        while k >= 1:
            for j in range(k % p, n - k, 2 * k):
                for i in range(min(k, n - j - k)):
                    if (i + j) // (2 * p) == (i + j + k) // (2 * p):
                        pairs.append((i + j, i + j + k))
            k //= 2
        p *= 2
    return pairs


def _compare_exchange(v, i, j):
    a, b = v[i], v[j]
    v[i] = jnp.maximum(a, b)
    v[j] = jnp.minimum(a, b)


def _sort_desc(v):
    for i, j in _oddeven_merge_sort_pairs(len(v)):
        _compare_exchange(v, i, j)


def _bitonic_merge_desc(v):
    d = len(v) // 2
    while d >= 1:
        for i in range(len(v)):
            if (i & d) == 0:
                _compare_exchange(v, i, i + d)
        d //= 2


def _index_kernel(qi_ref, sm_ref, ki_ref, o_ref, keys, top_scr, *, qoff, topk, idx_bits):
    j = pl.program_id(1)
    qb = qi_ref.shape[1]
    ltot = ki_ref.shape[1]
    kb = IDX_KB
    kmax = qoff + (j + 1) * qb
    nkb = lax.div(kmax + (kb - 1), kb)
    sub = kb // LANES

    qv = qi_ref[0].astype(bf16)
    stacked = jnp.concatenate([qv[:, h * IDX_DIM:(h + 1) * IDX_DIM] for h in range(IDX_HEADS)], axis=0)
    w = sm_ref[0][:, SMALL_WI:SMALL_WI + IDX_HEADS] * (IDX_HEADS ** -0.5) * (IDX_DIM ** -0.5)
    wb = [jnp.broadcast_to(w[:, h:h + 1], (qb, kb)) for h in range(IDX_HEADS)]
    qchunk = lax.shift_right_arithmetic(qoff + j * qb + lax.broadcasted_iota(i32, (qb, 1), 0), 6)
    lane_pos = lax.broadcasted_iota(i32, (1, kb), 1)

    def score_body(i, carry):
        off = pl.multiple_of(i * kb, kb)
        s = _dot_nt(stacked, ki_ref[0, pl.ds(off, kb), :])
        acc = jnp.maximum(s[0:qb], 0.0) * wb[0]
        for h in range(1, IDX_HEADS):
            acc = acc + jnp.maximum(s[h * qb:(h + 1) * qb], 0.0) * wb[h]
        adm = lax.shift_right_arithmetic(off + lane_pos, 6) <= qchunk
        keys[:, pl.ds(off, kb)] = jnp.where(adm, jnp.where(acc == 0.0, 0.0, acc), -jnp.inf)
        return carry

    lax.fori_loop(0, nkb, score_body, 0)

    group = TOP_R * LANES
    ngrp = lax.div(kmax + (group - 1), group)

    def pad_body(i, carry):
        keys[:, pl.ds(pl.multiple_of(i * kb, kb), kb)] = jnp.full((qb, kb), -jnp.inf, f32)
        return carry

    lax.fori_loop(nkb, ngrp * (group // kb), pad_body, 0)
    top_scr[...] = jnp.full(top_scr.shape, -jnp.inf, f32)

    def top_body(gidx, carry):
        base = gidx * group
        for s in range(qb // SUBLANES):
            rs = slice(s * SUBLANES, (s + 1) * SUBLANES)
            v = [keys[rs, pl.ds(pl.multiple_of(base + i * LANES, LANES), LANES)] for i in range(TOP_R)]
            _sort_desc(v)
            merged = [jnp.maximum(top_scr[i, rs, :], v[TOP_R - 1 - i]) for i in range(TOP_R)]
            _bitonic_merge_desc(merged)
            for i in range(TOP_R):
                top_scr[i, rs, :] = merged[i]
        return carry

    lax.fori_loop(0, ngrp, top_body, 0)

    def flip(bits):
        return bits ^ (lax.shift_right_arithmetic(bits, 31) & INT_MAX)

    def to_key(x):
        k = flip(lax.bitcast_convert_type(x, i32))
        return jnp.where(k > 0, k - SUBNORMAL_SPAN, jnp.where(k < 0, k + SUBNORMAL_SPAN + 1, 0))

    def to_score(key):
        k = jnp.where(key > 0, key + SUBNORMAL_SPAN, jnp.where(key < 0, key - SUBNORMAL_SPAN - 1, 0))
        return lax.bitcast_convert_type(flip(k), f32)

    def count_top(thr):
        tb = jnp.broadcast_to(to_score(thr), (qb, LANES))
        acc = (top_scr[0] >= tb).astype(i32)
        for i in range(1, TOP_R):
            acc = acc + (top_scr[i] >= tb).astype(i32)
        return jnp.sum(acc, axis=1, keepdims=True)

    def count(pred):
        def body(i, acc):
            off = pl.multiple_of(i * kb, kb)
            blk = keys[:, pl.ds(off, kb)]
            m = pred(blk, off).astype(i32)
            for c in range(sub):
                acc = acc + m[:, c * LANES:(c + 1) * LANES]
            return acc
        acc = lax.fori_loop(0, nkb, body, jnp.zeros((qb, LANES), i32))
        return jnp.sum(acc, axis=1, keepdims=True)

    def count_ge(thr):
        tb = jnp.broadcast_to(to_score(thr), (qb, kb))
        return count(lambda blk, off: blk >= tb)

    assert topk <= 2 * LANES and TOP_R >= 2
    hi0 = to_key(jnp.max(top_scr[0], axis=1, keepdims=True)) + 1
    lo0 = jnp.minimum(to_key(jnp.min(top_scr[1], axis=1, keepdims=True)), hi0 - 1)

    def search(count_fn, lo_start):
        def cond(c):
            it, open_rows, _, _ = c
            return (open_rows > 0) & (it < 40)

        def body(c):
            it, _, lo, hi = c
            mid = lax.shift_right_arithmetic(lo, 1) + lax.shift_right_arithmetic(hi, 1) + (lo & hi & 1)
            n = count_fn(mid)
            ge = n >= topk
            lo_n = jnp.where(ge, mid, lo)
            hi_n = jnp.where(n == topk, mid + 1, jnp.where(ge, hi, mid))
            open_rows = jnp.max((lo_n < hi_n - 1).astype(i32))
            return it + 1, open_rows, lo_n, hi_n

        open0 = jnp.max((lo_start < hi0 - 1).astype(i32))
        return lax.while_loop(cond, body, (jnp.int32(0), open0, lo_start, hi0))[2]

    thr_c = search(count_top, lo0)
    n_ge_c = count_ge(thr_c)
    bad = (n_ge_c != count_top(thr_c)) & (thr_c > NEG_INF_KEY)

    def full_search(_):
        t = search(count_ge, lo0)
        return t, count_ge(t)

    thr, n_ge = lax.cond(jnp.max(bad.astype(i32)) > 0, full_search, lambda _: (thr_c, n_ge_c), 0)
    n_gt = count_ge(thr + 1)
    tie = (n_ge > topk) & (thr > NEG_INF_KEY)
    need = topk - n_gt

    def tie_cut(_):
        tb = jnp.broadcast_to(to_score(thr), (qb, kb))

        def body(_, c):
            lo, hi = c
            mid = lax.shift_right_arithmetic(lo + hi, 1)
            mb = jnp.broadcast_to(mid, (qb, kb))
            n = count(lambda blk, off: (blk == tb) & ((off + lane_pos) <= mb))
            ok = n >= need
            return jnp.where(ok, lo, mid), jnp.where(ok, mid, hi)

        lo_i = jnp.full((qb, 1), -1, i32)
        hi_i = jnp.full((qb, 1), 1, i32) * (nkb * kb - 1)
        _, cut = lax.fori_loop(0, idx_bits, body, (lo_i, hi_i))
        return jnp.where(tie, cut, INT_MAX)

    any_tie = jnp.max(tie.astype(i32)) > 0
    cut = lax.cond(any_tie, tie_cut, lambda _: jnp.full((qb, 1), INT_MAX, i32), 0)

    tb = jnp.broadcast_to(to_score(thr), (qb, kb))
    cb = jnp.broadcast_to(cut, (qb, kb))

    def out_body(i, carry):
        off = pl.multiple_of(i * kb, kb)
        blk = keys[:, pl.ds(off, kb)]
        sel = ((blk > tb) | ((blk == tb) & ((off + lane_pos) <= cb))) & (blk > -jnp.inf)
        o_ref[0, :, pl.ds(off, kb)] = jnp.where(sel, 0.0, NEG_BIG).astype(bf16)
        return carry

    lax.fori_loop(0, nkb, out_body, 0)

    def fill_body(i, carry):
        off = pl.multiple_of(i * kb, kb)
        o_ref[0, :, pl.ds(off, kb)] = jnp.full((qb, kb), NEG_BIG, bf16)
        return carry

    lax.fori_loop(nkb, ltot // kb, fill_body, 0)


def _dsa_mask(proj, ki16, *, qb, qoff, topk):
    b, t, _ = proj.shape
    ltot = ki16.shape[1]
    idx_bits = int(math.ceil(math.log2(ltot))) + 1
    kern = functools.partial(_index_kernel, qoff=qoff, topk=topk, idx_bits=idx_bits)
    return pl.pallas_call(
        kern,
        out_shape=jax.ShapeDtypeStruct((b, t, ltot), bf16),
        grid=(b, t // qb),
        in_specs=[pl.BlockSpec((1, qb, C_QI), lambda i, j: (i, j, OFF_QI // C_QI)),
                  pl.BlockSpec((1, qb, LANES), lambda i, j: (i, j, OFF_SMALL // LANES)),
                  pl.BlockSpec((1, ltot, IDX_DIM), lambda i, j: (i, 0, 0))],
        out_specs=pl.BlockSpec((1, qb, ltot), lambda i, j: (i, j, 0)),
        scratch_shapes=[pltpu.VMEM((qb, -(-ltot // (TOP_R * LANES)) * TOP_R * LANES), f32),
                        pltpu.VMEM((TOP_R, qb, LANES), f32)],
        compiler_params=_cp(("arbitrary", "arbitrary")),
        name="dsa_index",
    )(proj, proj, ki16)


def _attn_kernel(qidx_ref, kidx_ref, flag_ref, case_ref, q_ref, *refs, has_cache):
    if has_cache:
        (ck_ref, cv_ref, kn_ref, vn_ref, mb_ref, bias_ref, o_ref,
         q_scr, m_scr, l_scr, acc_scr, s_scr, p_scr, a_scr, mb_scr, k_ref, v_ref) = refs
    else:
        (k_ref, v_ref, mb_ref, bias_ref, o_ref,
         q_scr, m_scr, l_scr, acc_scr, s_scr, p_scr, a_scr, mb_scr) = refs
    s_id = pl.program_id(1)
    qb = q_ref.shape[1]
    kb = k_ref.shape[1]
    rows = C_GROUP * qb
    nt = kb // LANES
    flags = flag_ref[s_id]

    if has_cache:
        @pl.when((flags & 8) != 0)
        def _():
            k_ref[0] = kn_ref[0]
            v_ref[0] = vn_ref[0]

        @pl.when((flags & 8) == 0)
        def _():
            k_ref[0] = ck_ref[0].astype(bf16)
            v_ref[0] = cv_ref[0].astype(bf16)

    @pl.when((flags & 1) != 0)
    def _():
        qv = (q_ref[0] * (C_HEAD_DIM ** -0.5 * LOG2E)).astype(bf16)
        for g in range(C_KV_HEADS):
            for r in range(C_GROUP):
                hd = g * C_GROUP + r
                q_scr[g, r * qb:(r + 1) * qb, :] = qv[:, hd * C_HEAD_DIM:(hd + 1) * C_HEAD_DIM]
        m_scr[...] = jnp.full(m_scr.shape, NEG_BIG, f32)
        l_scr[...] = jnp.zeros_like(l_scr)
        acc_scr[...] = jnp.zeros_like(acc_scr)

    def visit(with_bias):
        kblk = k_ref[0]
        vblk = v_ref[0]
        mb_scr[...] = mb_ref[0].astype(f32)
        units = [(g, r) for g in range(C_KV_HEADS) for r in range(C_GROUP)]

        def qk(u):
            g, r = units[u]
            hs = slice(r * qb, (r + 1) * qb)
            s_scr[g, hs, :] = _dot_nt(q_scr[g, hs, :], kblk[:, g * C_HEAD_DIM:(g + 1) * C_HEAD_DIM])

        def soft(u):
            g, r = units[u]
            for c in range(qb // ATT_RC):
                q0 = c * ATT_RC
                rs = slice(r * qb + q0, r * qb + q0 + ATT_RC)
                s = s_scr[g, rs, :] + mb_scr[q0:q0 + ATT_RC, :]
                if with_bias:
                    s = s + bias_ref[0, g * C_GROUP + r, q0:q0 + ATT_RC, :]
                t = s[:, 0:LANES]
                for i in range(1, nt):
                    t = jnp.maximum(t, s[:, i * LANES:(i + 1) * LANES])
                m_old = m_scr[g, rs, :]
                m_new = jnp.maximum(m_old, jnp.max(t, axis=1, keepdims=True))
                alpha = jnp.exp2(m_old - m_new)
                p = jnp.exp2(s - jnp.concatenate([m_new] * nt, axis=1))
                psum = p[:, 0:LANES]
                for i in range(1, nt):
                    psum = psum + p[:, i * LANES:(i + 1) * LANES]
                l_scr[g, rs, :] = alpha * l_scr[g, rs, :] + psum
                m_scr[g, rs, :] = m_new
                a_scr[g, rs, :] = alpha
                p_scr[g, rs, :] = p.astype(bf16)

        def pv(u):
            g, r = units[u]
            hs = slice(r * qb, (r + 1) * qb)
            acc_scr[g, hs, :] = a_scr[g, hs, :] * acc_scr[g, hs, :] + _dot(
                p_scr[g, hs, :], vblk[:, g * C_HEAD_DIM:(g + 1) * C_HEAD_DIM])

        qk(0)
        for u in range(len(units)):
            if u + 1 < len(units):
                qk(u + 1)
            soft(u)
            pv(u)

    @pl.when((flags & 4) != 0)
    def _():
        visit(True)

    @pl.when((flags & 4) == 0)
    def _():
        visit(False)

    @pl.when((flags & 2) != 0)
    def _():
        for g in range(C_KV_HEADS):
            o = acc_scr[g] / jnp.sum(l_scr[g], axis=1, keepdims=True)
            for r in range(C_GROUP):
                hd = g * C_GROUP + r
                o_ref[0, :, hd * C_HEAD_DIM:(hd + 1) * C_HEAD_DIM] = o[r * qb:(r + 1) * qb, :].astype(bf16)


def _attn_schedule(nq, qb, qoff, offsets):
    qidx, kidx, flags, case = [], [], [], []
    zero_case = len(offsets)
    for j in range(nq):
        kmax = qoff + (j + 1) * qb
        nkb = -(-kmax // ATT_KB)
        for kb in range(nkb):
            off = kmax - BIAS_WIN - kb * ATT_KB
            qidx.append(j)
            kidx.append(kb)
            near = off in offsets
            flags.append((1 if kb == 0 else 0) | (2 if kb == nkb - 1 else 0) | (4 if near else 0)
                         | (8 if kb * ATT_KB >= qoff else 0))
            case.append(offsets.index(off) if near else zero_case)
            assert near == (-BIAS_WIN < off < ATT_KB)
    return [jnp.asarray(np.asarray(a, np.int32)) for a in (qidx, kidx, flags, case)]


def _dsa_attend(proj, k16, v16, maskbias, tables, *, qb, qoff, offsets, cache=None):
    b, t, _ = proj.shape
    nq = t // qb
    qidx, kidx, flags, case = _attn_schedule(nq, qb, qoff, offsets)
    ns = int(qidx.shape[0])
    rows = C_GROUP * qb
    kv_block = lambda f: pl.BlockSpec((1, ATT_KB, C_KV), f)
    if cache is None:
        kv_specs = [kv_block(lambda i, s, qi, ki, fl, ca: (i, ki[s], 0))] * 2
        kv_args = (k16, v16)
        kv_scratch = []
    else:
        ncache = cache[0].shape[1] // ATT_KB
        assert cache[0].shape[1] == qoff and qoff % ATT_KB == 0 and nq == 1 and k16.shape[1] == ATT_KB
        kv_specs = ([kv_block(lambda i, s, qi, ki, fl, ca: (i, jnp.minimum(ki[s], ncache - 1), 0))] * 2
                    + [kv_block(lambda i, s, qi, ki, fl, ca: (i, 0, 0))] * 2)
        kv_args = (cache[0], cache[1], k16, v16)
        kv_scratch = [pltpu.VMEM((1, ATT_KB, C_KV), bf16)] * 2
    grid_spec = pltpu.PrefetchScalarGridSpec(
        num_scalar_prefetch=4,
        grid=(b, ns),
        in_specs=([pl.BlockSpec((1, qb, C_Q), lambda i, s, qi, ki, fl, ca: (i, qi[s], OFF_Q // C_Q))] + kv_specs
                  + [pl.BlockSpec((1, qb, ATT_KB), lambda i, s, qi, ki, fl, ca: (i, qi[s], ki[s])),
                     pl.BlockSpec((1, C_HEADS, qb, ATT_KB), lambda i, s, qi, ki, fl, ca: (ca[s], 0, 0, 0))]),
        out_specs=pl.BlockSpec((1, qb, C_WIDTH), lambda i, s, qi, ki, fl, ca: (i, qi[s], 0)),
        scratch_shapes=[pltpu.VMEM((C_KV_HEADS, rows, C_HEAD_DIM), bf16),
                        pltpu.VMEM((C_KV_HEADS, rows, LANES), f32),
                        pltpu.VMEM((C_KV_HEADS, rows, LANES), f32),
                        pltpu.VMEM((C_KV_HEADS, rows, C_HEAD_DIM), f32),
                        pltpu.VMEM((C_KV_HEADS, rows, ATT_KB), f32),
                        pltpu.VMEM((C_KV_HEADS, rows, ATT_KB), bf16),
                        pltpu.VMEM((C_KV_HEADS, rows, LANES), f32),
                        pltpu.VMEM((qb, ATT_KB), f32)] + kv_scratch)
    return pl.pallas_call(
        functools.partial(_attn_kernel, has_cache=cache is not None),
        out_shape=jax.ShapeDtypeStruct((b, t, C_WIDTH), bf16),
        grid_spec=grid_spec,
        compiler_params=_cp(("arbitrary", "arbitrary")),
        name="dsa_attend",
    )(qidx, kidx, flags, case, proj, *kv_args, maskbias, tables)


def _window_offsets(nq, qb, qoff):
    offs = set()
    for j in range(nq):
        kmax = qoff + (j + 1) * qb
        for kb in range(-(-kmax // ATT_KB)):
            off = kmax - BIAS_WIN - kb * ATT_KB
            if -BIAS_WIN < off < ATT_KB:
                offs.add(off)
    return sorted(offs)


def _mixer_c(proj, k_all16, v_all16, ki_all16, rel_bias, *, qb, qoff, true_len, cache=None):
    b, t, _ = proj.shape
    topk = min(TOPK_MAX, true_len // 4)
    assert BIAS_WIN - qb + 1 > MAX_DISTANCE
    offsets = _window_offsets(t // qb, qb, qoff)
    tables = _bias_tables(rel_bias, qb, offsets)
    idx_qb = IDX_QB if t % IDX_QB == 0 else qb
    maskbias = _dsa_mask(proj, ki_all16, qb=idx_qb, qoff=qoff, topk=topk)
    return _dsa_attend(proj, k_all16, v_all16, maskbias, tables, qb=qb, qoff=qoff, offsets=offsets, cache=cache)


def _merge_kernel(x_ref, ada_ref, gt_ref, oa_ref, ob_ref, oc_ref, wa_ref, wb_ref, wc_ref, wo_ref, o_ref):
    bt, tt, d = x_ref.shape
    rows = bt * tt
    gates = jax.nn.sigmoid(gt_ref[...].reshape(rows, GATE_COLS))
    merged = (gates[:, 0:d] * _dot(oa_ref[...].reshape(rows, A_WIDTH), wa_ref[...])
              + gates[:, d:2 * d] * _dot(ob_ref[...].reshape(rows, SSM_INNER), wb_ref[...])
              + gates[:, 2 * d:3 * d] * _dot(oc_ref[...].reshape(rows, C_WIDTH), wc_ref[...]))
    mix = _dot(merged.astype(bf16), wo_ref[...]).reshape(bt, tt, d)
    o_ref[...] = x_ref[...] + ada_ref[...][:, 5:6, :] * mix


def _merge(x, ada, proj, o_a, o_b, o_c, wa16, wb16, wc16, wo16):
    b, t, d = x.shape
    bt, tt = _token_tile(b, t, rows=256)
    tok = lambda w: pl.BlockSpec((bt, tt, w), lambda i, j: (i, j, 0))
    full = lambda r, c: pl.BlockSpec((r, c), lambda i, j: (0, 0))
    return pl.pallas_call(
        _merge_kernel,
        out_shape=jax.ShapeDtypeStruct((b, t, d), f32),
        grid=(b // bt, t // tt),
        in_specs=[tok(d), pl.BlockSpec((bt, N_ADA, d), lambda i, j: (i, 0, 0)),
                  pl.BlockSpec((bt, tt, GATE_COLS), lambda i, j: (i, j, OFF_GATE // GATE_COLS)),
                  tok(A_WIDTH), tok(SSM_INNER), tok(C_WIDTH),
                  full(A_WIDTH, d), full(SSM_INNER, d), full(C_WIDTH, d), full(d, d)],
        out_specs=tok(d),
        compiler_params=_cp(("arbitrary", "arbitrary")),
        name="merge",
    )(x, ada, proj, o_a, o_b, o_c, wa16, wb16, wc16, wo16)


def _pad_keys(x, mult):
    pad = (-x.shape[1]) % mult
    if pad:
        x = jnp.concatenate([x, jnp.zeros((x.shape[0], pad) + x.shape[2:], x.dtype)], axis=1)
    return x


def _layer(x, ada, lp, rel_bias, past, g_final, last):
    b, t, d = x.shape
    x = _ffn(x, ada, lp['g_ffn1'], lp['w_ffn1_in'], lp['w_ffn1_out'], g_final, k0=0, final_norm=False)
    proj, k16, v16, ki16 = _inproj(x, ada, lp['g_mix'], lp['w_proj'], k0=3)
    k_new = proj[:, :, OFF_K:OFF_K + C_KV]
    v_new = proj[:, :, OFF_V:OFF_V + C_KV]
    ki_new = proj[:, :, OFF_SMALL + SMALL_KI:OFF_SMALL + LANES]
    new_conv = proj[:, t - (CONV_W - 1):, OFF_XBC:OFF_XBC + CONV_DIM]

    o_a, va = _mixer_a(proj, lp['g_a'], lp['w_s'], lp['b_s'])

    if past is None:
        conv_init = jnp.zeros((b, CONV_W - 1, CONV_DIM), f32)
        h0 = jnp.zeros((b, SSM_HEADS, SSM_HEAD_DIM, SSM_STATE), f32)
    else:
        h0, conv_init = past[3], past[4]
    o_b, new_ssm = _mixer_b(proj, conv_init, h0, lp['w_conv'], lp['b_conv'], lp['dt_bias'], lp['a_log'],
                            lp['d_skip'], lp['g_ssm'])
    new_ssm = new_ssm.reshape(b, SSM_HEADS, SSM_HEAD_DIM, SSM_STATE)

    if past is None:
        k_all, v_all, ki_all, cache = k16, v16, ki16, None
        qb, qoff, true_len = (ATT_QB if t % ATT_QB == 0 else Q_BLOCK), 0, t
    else:
        plen = past[0].shape[1]
        ck, cv = past[0].reshape(b, plen, C_KV), past[1].reshape(b, plen, C_KV)
        if plen % ATT_KB == 0:
            k_all, v_all, cache = k16, v16, (ck, cv)
        else:
            k_all = jnp.concatenate([ck.astype(bf16), k16], axis=1)
            v_all = jnp.concatenate([cv.astype(bf16), v16], axis=1)
            cache = None
        ki_all = jnp.concatenate([past[2].astype(bf16), ki16], axis=1)
        qb, qoff, true_len = t, plen, plen + t
    o_c = _mixer_c(proj, _pad_keys(k_all, ATT_KB), _pad_keys(v_all, ATT_KB), _pad_keys(ki_all, ATT_KB), rel_bias,
                   qb=qb, qoff=qoff, true_len=true_len, cache=cache)

    x = _merge(x, ada, proj, o_a, o_b, o_c, lp['w_br_a'], lp['w_br_b'], lp['w_br_c'], lp['w_out'])
    x = _ffn(x, ada, lp['g_ffn2'], lp['w_ffn2_in'], lp['w_ffn2_out'], g_final, k0=6, final_norm=last)
    states = (k_new.reshape(b, t, C_KV_HEADS, C_HEAD_DIM), v_new.reshape(b, t, C_KV_HEADS, C_HEAD_DIM),
              ki_new, new_ssm, new_conv, va)
    return x, states


def kernel(x_prompt, x_sample, c_prompt, c_sample, cache_k, cache_v, cache_kidx, state_ssm, state_conv, rel_bias, w_ada, b_ada, g_ffn1, w_ffn1_in, w_ffn1_out, g_mix, w_in, g_a, w_s, b_s, w_conv, b_conv, dt_bias, a_log, d_skip, g_ssm, w_br_a, w_br_b, w_br_c, w_out, g_ffn2, w_ffn2_in, w_ffn2_out, g_final):
    depth = w_ada.shape[0]
    bp, bs = c_prompt.shape[0], c_sample.shape[0]
    assert x_prompt.shape[1] % ATT_KB == 0 and x_prompt.shape[1] % Q_BLOCK == 0
    assert x_sample.shape[1] == CHUNK

    rows = -(-(bp + bs) // 8) * 8
    c_all = jnp.concatenate([c_prompt, c_sample, jnp.zeros((rows - bp - bs, D_MODEL), f32)], axis=0)
    ada_all = _ada_all(c_all, w_ada.astype(bf16), b_ada)
    ada_all = ada_all.reshape(depth, rows, N_ADA, D_MODEL)

    yp, ys = x_prompt, x_sample
    st_p, st_s = [], []
    for i in range(depth):
        lp = dict(g_ffn1=g_ffn1[i], w_ffn1_in=w_ffn1_in[i].astype(bf16), w_ffn1_out=w_ffn1_out[i].astype(bf16),
                  g_mix=g_mix[i], w_proj=_layout_w_in(w_in[i]).astype(bf16), g_a=g_a[i], w_s=w_s[i], b_s=b_s[i],
                  w_conv=w_conv[i], b_conv=b_conv[i], dt_bias=dt_bias[i], a_log=a_log[i], d_skip=d_skip[i],
                  g_ssm=g_ssm[i], w_br_a=w_br_a[i].astype(bf16), w_br_b=w_br_b[i].astype(bf16),
                  w_br_c=w_br_c[i].astype(bf16), w_out=w_out[i].astype(bf16), g_ffn2=g_ffn2[i],
                  w_ffn2_in=w_ffn2_in[i].astype(bf16), w_ffn2_out=w_ffn2_out[i].astype(bf16))
        last = i == depth - 1
        yp, sp = _layer(yp, ada_all[i, :bp], lp, rel_bias, None, g_final, last)
        ys, ss = _layer(ys, ada_all[i, bp:bp + bs], lp, rel_bias,
                        (cache_k[i], cache_v[i], cache_kidx[i], state_ssm[i], state_conv[i]), g_final, last)
        st_p.append(sp)
        st_s.append(ss)

    stack = lambda sts, k: jnp.stack([s[k] for s in sts])
    return (yp, ys,
            stack(st_p, 0), stack(st_p, 1), stack(st_p, 2), stack(st_p, 3), stack(st_p, 4),
            stack(st_s, 0), stack(st_s, 1), stack(st_s, 2), stack(st_s, 3), stack(st_s, 4), stack(st_s, 5))
```

```python
import functools
import math

import numpy as np
import jax
import jax.numpy as jnp
from jax import lax
from jax.experimental import pallas as pl
from jax.experimental.pallas import tpu as pltpu

f32 = jnp.float32
bf16 = jnp.bfloat16
i32 = jnp.int32

D_MODEL = 1024
CHUNK = 64
EPS = 1e-6
N_ADA = 9
D_FF = 2816
A_WIDTH = D_MODEL
A_GROUPS = 8
A_GROUP_DIM = A_WIDTH // A_GROUPS
A_CHUNK = 128
SSM_INNER = 2 * D_MODEL
SSM_HEAD_DIM = 64
SSM_HEADS = SSM_INNER // SSM_HEAD_DIM
SSM_GROUPS = 4
SSM_HPG = SSM_HEADS // SSM_GROUPS
SSM_STATE = 128
CONV_W = 4
CONV_DIM = SSM_INNER + 2 * SSM_GROUPS * SSM_STATE
C_HEADS = 8
C_KV_HEADS = 4
C_GROUP = C_HEADS // C_KV_HEADS
C_HEAD_DIM = 128
C_WIDTH = C_HEADS * C_HEAD_DIM
IDX_HEADS = 8
IDX_DIM = 64
TOPK_MAX = 256
Q_BLOCK = 128
N_BUCKETS = 32
MAX_DISTANCE = 128
A_COLS = 2 * A_WIDTH
B_COLS = SSM_INNER + CONV_DIM + SSM_HEADS
C_Q = C_HEADS * C_HEAD_DIM
C_KV = C_KV_HEADS * C_HEAD_DIM
C_QI = IDX_HEADS * IDX_DIM
GATE_COLS = 3 * D_MODEL

LANES = 128
SUBLANES = 8
VMEM_LIMIT = 56 * 1024 * 1024

OFF_GATE = 0
OFF_XBC = OFF_GATE + GATE_COLS
OFF_PA = OFF_XBC + CONV_DIM
OFF_Z = OFF_PA + A_COLS
OFF_Q = OFF_Z + SSM_INNER
OFF_K = OFF_Q + C_Q
OFF_V = OFF_K + C_KV
OFF_QI = OFF_V + C_KV
OFF_SMALL = OFF_QI + C_QI
SMALL_DT = 0
SMALL_WI = SSM_HEADS
SMALL_KI = LANES - IDX_DIM
PROJ_TILE = 1024
PROJ_W = ((OFF_SMALL + LANES + PROJ_TILE - 1) // PROJ_TILE) * PROJ_TILE

NEG_BIG = -1e30
SUBNORMAL_SPAN = 0x007FFFFF
NEG_INF_KEY = -2139095041 + SUBNORMAL_SPAN + 1
INT_MIN = -2147483648
INT_MAX = 2147483647
ATT_KB = 1024
IDX_KB = 512
TOP_R = 16
IDX_QB = 256
ATT_RC = 8
LOG2E = math.log2(math.e)
ATT_QB = 256
BIAS_WIN = ATT_QB + MAX_DISTANCE
FAR_BUCKET = N_BUCKETS // 2 - 1


def _cp(sem, vmem=VMEM_LIMIT):
    return pltpu.CompilerParams(dimension_semantics=sem, vmem_limit_bytes=vmem)


def _rms(x, g):
    return x * lax.rsqrt(jnp.mean(x * x, axis=-1, keepdims=True) + EPS) * g


def _dot(a, b):
    return jnp.dot(a, b, preferred_element_type=f32)


def _dot_nt(a, b):
    return lax.dot_general(a, b, (((1,), (1,)), ((), ())), preferred_element_type=f32)


def _dot_tn(a, b):
    return lax.dot_general(a, b, (((0,), (0,)), ((), ())), preferred_element_type=f32)


def _bf16_pieces(x, pieces):
    out = []
    for _ in range(pieces):
        p = x.astype(bf16)
        out.append(p)
        x = x - p.astype(f32)
    return out


def _dot_zero_one_lhs(sel16, x, pieces):
    parts = [_dot(sel16, p) for p in _bf16_pieces(x, pieces)]
    return functools.reduce(lambda a, b: a + b, parts)


def _dot_zero_one_rhs(x, sel16, pieces):
    parts = [_dot(p, sel16) for p in _bf16_pieces(x, pieces)]
    return functools.reduce(lambda a, b: a + b, parts)


def _ada_kernel(c_ref, w_ref, b_ref, o_ref):
    s = jax.nn.silu(c_ref[...]).astype(bf16)
    o_ref[0] = _dot(s, w_ref[0]) + b_ref[0]


def _ada_all(c_all, w_ada16, b_ada):
    depth = w_ada16.shape[0]
    rows = c_all.shape[0]
    n_out = N_ADA * D_MODEL
    tn = 9 * LANES
    return pl.pallas_call(
        _ada_kernel,
        out_shape=jax.ShapeDtypeStruct((depth, rows, n_out), f32),
        grid=(depth, n_out // tn),
        in_specs=[pl.BlockSpec((rows, D_MODEL), lambda d, n: (0, 0)),
                  pl.BlockSpec((1, D_MODEL, tn), lambda d, n: (d, 0, n)),
                  pl.BlockSpec((1, 1, tn), lambda d, n: (d, 0, n))],
        out_specs=pl.BlockSpec((1, rows, tn), lambda d, n: (d, 0, n)),
        compiler_params=_cp(("arbitrary", "arbitrary")),
        name="ada",
    )(c_all, w_ada16, b_ada.reshape(depth, 1, n_out))


def _ffn_kernel(x_ref, ada_ref, g_ref, wg_ref, wu_ref, wo_ref, gf_ref, o_ref, h_scr, acc_scr, *, k0, final_norm):
    f = pl.program_id(2)
    bt, tt, d = x_ref.shape
    rows = bt * tt

    @pl.when(f == 0)
    def _():
        ada = ada_ref[...]
        h = _rms(x_ref[...], g_ref[...]) * (1.0 + ada[:, k0 + 1:k0 + 2, :]) + ada[:, k0:k0 + 1, :]
        h_scr[...] = h.reshape(rows, d).astype(bf16)
        acc_scr[...] = jnp.zeros_like(acc_scr)

    h = h_scr[...]
    gate = _dot(h, wg_ref[...])
    up = _dot(h, wu_ref[...])
    a = (jax.nn.silu(gate) * up).astype(bf16)
    acc_scr[...] += _dot(a, wo_ref[...])

    @pl.when(f == pl.num_programs(2) - 1)
    def _():
        ada = ada_ref[...]
        y = x_ref[...] + (0.5 * ada[:, k0 + 2:k0 + 3, :]) * acc_scr[...].reshape(bt, tt, d)
        if final_norm:
            y = _rms(y, gf_ref[...])
        o_ref[...] = y


def _token_tile(b, t, rows=512):
    if t >= rows:
        return 1, rows
    return min(b, rows // t), t


def _ffn(x, ada, g, w_in16, w_out16, g_final, *, k0, final_norm):
    b, t, d = x.shape
    bt, tt = _token_tile(b, t)
    tf = D_FF // 2
    nf = D_FF // tf
    kern = functools.partial(_ffn_kernel, k0=k0, final_norm=final_norm)
    return pl.pallas_call(
        kern,
        out_shape=jax.ShapeDtypeStruct((b, t, d), f32),
        grid=(b // bt, t // tt, nf),
        in_specs=[pl.BlockSpec((bt, tt, d), lambda i, j, f: (i, j, 0)),
                  pl.BlockSpec((bt, N_ADA, d), lambda i, j, f: (i, 0, 0)),
                  pl.BlockSpec((1, d), lambda i, j, f: (0, 0)),
                  pl.BlockSpec((d, tf), lambda i, j, f: (0, f)),
                  pl.BlockSpec((d, tf), lambda i, j, f: (0, f + nf)),
                  pl.BlockSpec((tf, d), lambda i, j, f: (f, 0)),
                  pl.BlockSpec((1, d), lambda i, j, f: (0, 0))],
        out_specs=pl.BlockSpec((bt, tt, d), lambda i, j, f: (i, j, 0)),
        scratch_shapes=[pltpu.VMEM((bt * tt, d), bf16), pltpu.VMEM((bt * tt, d), f32)],
        compiler_params=_cp(("arbitrary", "arbitrary", "arbitrary")),
        name="ffn",
    )(x, ada, g.reshape(1, d), w_in16, w_in16, w_out16, g_final.reshape(1, d))


def _inproj_kernel(x_ref, ada_ref, g_ref, w_ref, o_ref, k16_ref, v16_ref, ki16_ref, k32_ref, v32_ref, ki32_ref, h_scr, *, k0):
    n = pl.program_id(2)
    bt, tt, d = x_ref.shape

    @pl.when(n == 0)
    def _():
        ada = ada_ref[...]
        h = _rms(x_ref[...], g_ref[...]) * (1.0 + ada[:, k0 + 1:k0 + 2, :]) + ada[:, k0:k0 + 1, :]
        h_scr[...] = h.reshape(bt * tt, d).astype(bf16)

    res = _dot(h_scr[...], w_ref[...])
    o_ref[...] = res.reshape(bt, tt, PROJ_TILE)

    @pl.when(n == OFF_K // PROJ_TILE)
    def _():
        k = res[:, :C_KV]
        v = res[:, C_KV:2 * C_KV]
        k16_ref[...] = k.astype(bf16).reshape(bt, tt, C_KV)
        v16_ref[...] = v.astype(bf16).reshape(bt, tt, C_KV)
        k32_ref[...] = k.reshape(bt, tt, C_KV_HEADS, C_HEAD_DIM)
        v32_ref[...] = v.reshape(bt, tt, C_KV_HEADS, C_HEAD_DIM)

    @pl.when(n == OFF_SMALL // PROJ_TILE)
    def _():
        lo = OFF_SMALL % PROJ_TILE + SMALL_KI
        ki = res[:, lo:lo + IDX_DIM]
        ki16_ref[...] = ki.astype(bf16).reshape(bt, tt, IDX_DIM)
        ki32_ref[...] = ki.reshape(bt, tt, IDX_DIM)


def _inproj(x, ada, g, w_proj16, *, k0):
    b, t, d = x.shape
    bt, tt = _token_tile(b, t, rows=1024)
    assert OFF_K % PROJ_TILE == 0 and OFF_V == OFF_K + C_KV and 2 * C_KV <= PROJ_TILE
    tok = lambda w: pl.BlockSpec((bt, tt, w), lambda i, j, n: (i, j, 0))
    heads = pl.BlockSpec((bt, tt, C_KV_HEADS, C_HEAD_DIM), lambda i, j, n: (i, j, 0, 0))
    kv32 = jax.ShapeDtypeStruct((b, t, C_KV_HEADS, C_HEAD_DIM), f32)
    return pl.pallas_call(
        functools.partial(_inproj_kernel, k0=k0),
        out_shape=(jax.ShapeDtypeStruct((b, t, PROJ_W), f32), jax.ShapeDtypeStruct((b, t, C_KV), bf16),
                   jax.ShapeDtypeStruct((b, t, C_KV), bf16), jax.ShapeDtypeStruct((b, t, IDX_DIM), bf16),
                   kv32, kv32, jax.ShapeDtypeStruct((b, t, IDX_DIM), f32)),
        grid=(b // bt, t // tt, PROJ_W // PROJ_TILE),
        in_specs=[pl.BlockSpec((bt, tt, d), lambda i, j, n: (i, j, 0)),
                  pl.BlockSpec((bt, N_ADA, d), lambda i, j, n: (i, 0, 0)),
                  pl.BlockSpec((1, d), lambda i, j, n: (0, 0)),
                  pl.BlockSpec((d, PROJ_TILE), lambda i, j, n: (0, n))],
        out_specs=(pl.BlockSpec((bt, tt, PROJ_TILE), lambda i, j, n: (i, j, n)), tok(C_KV), tok(C_KV), tok(IDX_DIM),
                   heads, heads, tok(IDX_DIM)),
        scratch_shapes=[pltpu.VMEM((bt * tt, d), bf16)],
        compiler_params=_cp(("arbitrary", "arbitrary", "arbitrary")),
        name="inproj",
    )(x, ada, g.reshape(1, d), w_proj16)


def _layout_w_in(w_in):
    d = w_in.shape[0]
    o = 0
    pa = w_in[:, o:o + A_COLS]; o += A_COLS
    z = w_in[:, o:o + SSM_INNER]; o += SSM_INNER
    xbc = w_in[:, o:o + CONV_DIM]; o += CONV_DIM
    dt = w_in[:, o:o + SSM_HEADS]; o += SSM_HEADS
    q = w_in[:, o:o + C_Q]; o += C_Q
    k = w_in[:, o:o + C_KV]; o += C_KV
    v = w_in[:, o:o + C_KV]; o += C_KV
    qi = w_in[:, o:o + C_QI]; o += C_QI
    ki = w_in[:, o:o + IDX_DIM]; o += IDX_DIM
    wi = w_in[:, o:o + IDX_HEADS]; o += IDX_HEADS
    gates = w_in[:, o:o + GATE_COLS]
    small = jnp.concatenate([dt, wi, jnp.zeros((d, SMALL_KI - SMALL_WI - IDX_HEADS), w_in.dtype), ki], axis=1)
    pad = jnp.zeros((d, PROJ_W - OFF_SMALL - LANES), w_in.dtype)
    return jnp.concatenate([gates, xbc, pa, z, q, k, v, qi, small, pad], axis=1)


def _mixa_kernel(pa_ref, ga_ref, ws_ref, bs_ref, oa_ref, va_ref):
    n = pa_ref.shape[1]
    ge = jax.nn.gelu(pa_ref[0])
    u = ge[:, :A_WIDTH]
    v = _rms(ge[:, A_WIDTH:], ga_ref[...])
    va_ref[0] = v
    vb = v.astype(bf16)
    tri = lax.broadcasted_iota(i32, (n, n), 0) >= lax.broadcasted_iota(i32, (n, n), 1)
    bs = bs_ref[...]
    for g in range(A_GROUPS):
        sl = slice(g * A_GROUP_DIM, (g + 1) * A_GROUP_DIM)
        w = jnp.where(tri, ws_ref[g], 0.0).astype(bf16)
        mixed = _dot(w, vb[:, sl]) + bs[:, g:g + 1]
        oa_ref[0, :, sl] = (u[:, sl] * mixed).astype(bf16)


def _mixer_a(proj, g_a, w_s, b_s):
    b, t, _ = proj.shape
    n = min(t, A_CHUNK)
    ws = w_s[:, :n, :n]
    bs_t = b_s[:, :n].T
    return pl.pallas_call(
        _mixa_kernel,
        out_shape=(jax.ShapeDtypeStruct((b, t, A_WIDTH), bf16), jax.ShapeDtypeStruct((b, t, A_WIDTH), f32)),
        grid=(b, t // n),
        in_specs=[pl.BlockSpec((1, n, A_COLS), lambda i, j: (i, j, OFF_PA // A_COLS)),
                  pl.BlockSpec((1, A_WIDTH), lambda i, j: (0, 0)),
                  pl.BlockSpec((A_GROUPS, n, n), lambda i, j: (0, 0, 0)),
                  pl.BlockSpec((n, A_GROUPS), lambda i, j: (0, 0))],
        out_specs=(pl.BlockSpec((1, n, A_WIDTH), lambda i, j: (i, j, 0)),
                   pl.BlockSpec((1, n, A_WIDTH), lambda i, j: (i, j, 0))),
        compiler_params=_cp(("arbitrary", "arbitrary")),
        name="mixer_a",
    )(proj, g_a.reshape(1, A_WIDTH), ws, bs_t)


CONV_PAD = 8


def _ssd_kernel(z_ref, xbc_ref, sm_ref, cinit_ref, h0_ref, wc_ref, bc_ref, dtb_ref, alog_ref, dsk_ref, gs_ref,
                ex_ref, ob_ref, hs_ref, xpad, ht):
    c = pl.program_id(1)
    q = z_ref.shape[1]
    gw = SSM_HPG * SSM_HEAD_DIM
    tail = CONV_W - 1

    @pl.when(c == 0)
    def _():
        xpad[CONV_PAD - tail:CONV_PAD, :] = cinit_ref[0]
        for g in range(SSM_GROUPS):
            ht[g] = h0_ref[0, g * gw:(g + 1) * gw, :].T

    xpad[CONV_PAD:CONV_PAD + q, :] = xbc_ref[0]
    wc = wc_ref[...]
    conv = bc_ref[...] + wc[0:1, :] * xpad[pl.ds(CONV_PAD - 3, q), :]
    for k in range(1, CONV_W):
        conv = conv + wc[k:k + 1, :] * xpad[pl.ds(CONV_PAD - 3 + k, q), :]
    nxt = xpad[CONV_PAD + q - tail:CONV_PAD + q, :]
    xpad[CONV_PAD - tail:CONV_PAD, :] = nxt
    conv = jax.nn.silu(conv)
    xs = conv[:, :SSM_INNER]
    bm = conv[:, SSM_INNER:SSM_INNER + SSM_GROUPS * SSM_STATE]
    cm = conv[:, SSM_INNER + SSM_GROUPS * SSM_STATE:]

    lane = lax.broadcasted_iota(i32, (q, LANES), 1)
    head_lane = lane < SSM_HEADS
    dt = jnp.where(head_lane, jax.nn.softplus(sm_ref[0] + dtb_ref[...]), 0.0)
    da = dt * (-jnp.exp(alog_ref[...]))
    tri = lax.broadcasted_iota(i32, (q, q), 0) >= lax.broadcasted_iota(i32, (q, q), 1)
    acs = _dot_zero_one_lhs(tri.astype(bf16), da, 3)
    acs_t = acs.T
    dt_t = dt.T
    acs_last = acs[q - 1:q, :]
    e_acs = jnp.where(head_lane, jnp.exp(acs), 0.0)
    decay = jnp.where(head_lane, jnp.exp(acs_last - acs) * dt, 0.0)
    ex = ex_ref[...]
    e_acs_x = _dot_zero_one_rhs(e_acs, ex, 3)
    decay_x = _dot_zero_one_rhs(decay, ex, 3)
    e_last_x = e_acs_x[q - 1:q, :]

    xs16 = xs.astype(bf16)
    xdec16 = (xs * decay_x).astype(bf16)
    y_parts = []
    for g in range(SSM_GROUPS):
        bg = bm[:, g * SSM_STATE:(g + 1) * SSM_STATE].astype(bf16)
        cg = cm[:, g * SSM_STATE:(g + 1) * SSM_STATE].astype(bf16)
        cb = _dot_nt(cg, bg)
        htg = ht[g]
        y_inter = _dot(cg, htg.astype(bf16))
        intra = []
        for r in range(SSM_HPG):
            hd = g * SSM_HPG + r
            seg = acs[:, hd:hd + 1] - acs_t[hd:hd + 1, :]
            m = cb * jnp.exp(jnp.where(tri, seg, -jnp.inf)) * dt_t[hd:hd + 1, :]
            intra.append(_dot(m.astype(bf16), xs16[:, hd * SSM_HEAD_DIM:(hd + 1) * SSM_HEAD_DIM]))
        y_parts.append(jnp.concatenate(intra, axis=1) + y_inter * e_acs_x[:, g * gw:(g + 1) * gw])
        ht[g] = htg * e_last_x[:, g * gw:(g + 1) * gw] + _dot_tn(bg, xdec16[:, g * gw:(g + 1) * gw])
    y = jnp.concatenate(y_parts, axis=1) + dsk_ref[...] * xs
    y = _rms(y * jax.nn.silu(z_ref[0]), gs_ref[...])
    ob_ref[0] = y.astype(bf16)

    @pl.when(c == pl.num_programs(1) - 1)
    def _():
        for g in range(SSM_GROUPS):
            hs_ref[0, g * gw:(g + 1) * gw, :] = ht[g].T


def _mixer_b(proj, conv_init, h0, w_conv, b_conv, dt_bias, a_log, d_skip, g_ssm):
    b, t, _ = proj.shape
    q = CHUNK
    pad_row = lambda v: jnp.zeros((1, LANES), f32).at[0, :SSM_HEADS].set(v)
    expand = (jnp.arange(LANES)[:, None] == (jnp.arange(SSM_INNER) // SSM_HEAD_DIM)[None, :]).astype(bf16)
    hrows = SSM_HEADS * SSM_HEAD_DIM
    return pl.pallas_call(
        _ssd_kernel,
        out_shape=(jax.ShapeDtypeStruct((b, t, SSM_INNER), bf16),
                   jax.ShapeDtypeStruct((b, hrows, SSM_STATE), f32)),
        grid=(b, t // q),
        in_specs=[pl.BlockSpec((1, q, SSM_INNER), lambda i, c: (i, c, OFF_Z // SSM_INNER)),
                  pl.BlockSpec((1, q, CONV_DIM), lambda i, c: (i, c, OFF_XBC // CONV_DIM)),
                  pl.BlockSpec((1, q, LANES), lambda i, c: (i, c, OFF_SMALL // LANES)),
                  pl.BlockSpec((1, CONV_W - 1, CONV_DIM), lambda i, c: (i, 0, 0)),
                  pl.BlockSpec((1, hrows, SSM_STATE), lambda i, c: (i, 0, 0)),
                  pl.BlockSpec((CONV_W, CONV_DIM), lambda i, c: (0, 0)),
                  pl.BlockSpec((1, CONV_DIM), lambda i, c: (0, 0)),
                  pl.BlockSpec((1, LANES), lambda i, c: (0, 0)),
                  pl.BlockSpec((1, LANES), lambda i, c: (0, 0)),
                  pl.BlockSpec((1, SSM_INNER), lambda i, c: (0, 0)),
                  pl.BlockSpec((1, SSM_INNER), lambda i, c: (0, 0)),
                  pl.BlockSpec((LANES, SSM_INNER), lambda i, c: (0, 0))],
        out_specs=(pl.BlockSpec((1, q, SSM_INNER), lambda i, c: (i, c, 0)),
                   pl.BlockSpec((1, hrows, SSM_STATE), lambda i, c: (i, 0, 0))),
        scratch_shapes=[pltpu.VMEM((CONV_PAD + q, CONV_DIM), f32),
                        pltpu.VMEM((SSM_GROUPS, SSM_STATE, SSM_HPG * SSM_HEAD_DIM), f32)],
        compiler_params=_cp(("arbitrary", "arbitrary")),
        name="mixer_b",
    )(proj, proj, proj, conv_init, h0.reshape(b, hrows, SSM_STATE), w_conv, b_conv.reshape(1, CONV_DIM),
      pad_row(dt_bias), pad_row(a_log), jnp.repeat(d_skip, SSM_HEAD_DIM).reshape(1, SSM_INNER),
      g_ssm.reshape(1, SSM_INNER), expand)


def _t5_bucket(rel):
    nb = N_BUCKETS // 2
    max_exact = nb // 2
    n = jnp.abs(rel)
    nf = jnp.maximum(n, 1).astype(jnp.float32)
    large = max_exact + (jnp.log(nf / max_exact) / math.log(MAX_DISTANCE / max_exact)
                         * (nb - max_exact)).astype(jnp.int32)
    large = jnp.minimum(large, nb - 1)
    return jnp.where(rel > 0, nb, 0) + jnp.where(n < max_exact, n, large)


def _bias_kernel(tab_ref, bk_ref, o_ref):
    h = pl.program_id(1)
    bk = bk_ref[0]
    far = tab_ref[FAR_BUCKET, h]
    acc = jnp.zeros(bk.shape, f32)
    for bucket in range(N_BUCKETS):
        acc = jnp.where(bk == bucket, tab_ref[bucket, h] - far, acc)
    o_ref[0, 0] = acc * LOG2E


def _bias_tables(rel_bias, qb, offsets):
    t = np.arange(qb)[:, None]
    cases = []
    for off in offsets:
        w = np.arange(ATT_KB)[None, :] - off
        valid = (w >= 0) & (w < BIAS_WIN)
        rel = w - BIAS_WIN + qb - t
        cases.append(jnp.where(jnp.asarray(valid), _t5_bucket(jnp.asarray(rel, dtype=jnp.int32)), -1))
    cases.append(jnp.full((qb, ATT_KB), -1, jnp.int32))
    buckets = jnp.stack(cases).astype(i32)
    nc = buckets.shape[0]
    return pl.pallas_call(
        _bias_kernel,
        out_shape=jax.ShapeDtypeStruct((nc, C_HEADS, qb, ATT_KB), f32),
        grid=(nc, C_HEADS),
        in_specs=[pl.BlockSpec(memory_space=pltpu.SMEM),
                  pl.BlockSpec((1, qb, ATT_KB), lambda c, h: (c, 0, 0))],
        out_specs=pl.BlockSpec((1, 1, qb, ATT_KB), lambda c, h: (c, h, 0, 0)),
        compiler_params=_cp(("arbitrary", "arbitrary")),
        name="t5_bias",
    )(rel_bias, buckets)


def _oddeven_merge_sort_pairs(n):
    pairs = []
    p = 1
    while p < n:
        k = p
        while k >= 1:
            for j in range(k % p, n - k, 2 * k):
                for i in range(min(k, n - j - k)):
                    if (i + j) // (2 * p) == (i + j + k) // (2 * p):
                        pairs.append((i + j, i + j + k))
            k //= 2
        p *= 2
    return pairs


def _compare_exchange(v, i, j):
    a, b = v[i], v[j]
    v[i] = jnp.maximum(a, b)
    v[j] = jnp.minimum(a, b)


def _sort_desc(v):
    for i, j in _oddeven_merge_sort_pairs(len(v)):
        _compare_exchange(v, i, j)


def _bitonic_merge_desc(v):
    d = len(v) // 2
    while d >= 1:
        for i in range(len(v)):
            if (i & d) == 0:
                _compare_exchange(v, i, i + d)
        d //= 2


def _index_kernel(qi_ref, sm_ref, ki_ref, o_ref, keys, top_scr, *, qoff, topk, idx_bits):
    j = pl.program_id(1)
    qb = qi_ref.shape[1]
    ltot = ki_ref.shape[1]
    kb = IDX_KB
    kmax = qoff + (j + 1) * qb
    nkb = lax.div(kmax + (kb - 1), kb)
    sub = kb // LANES

    qv = qi_ref[0].astype(bf16)
    stacked = jnp.concatenate([qv[:, h * IDX_DIM:(h + 1) * IDX_DIM] for h in range(IDX_HEADS)], axis=0)
    w = sm_ref[0][:, SMALL_WI:SMALL_WI + IDX_HEADS] * (IDX_HEADS ** -0.5) * (IDX_DIM ** -0.5)
    wb = [jnp.broadcast_to(w[:, h:h + 1], (qb, kb)) for h in range(IDX_HEADS)]
    qchunk = lax.shift_right_arithmetic(qoff + j * qb + lax.broadcasted_iota(i32, (qb, 1), 0), 6)
    lane_pos = lax.broadcasted_iota(i32, (1, kb), 1)

    def score_body(i, carry):
        off = pl.multiple_of(i * kb, kb)
        s = _dot_nt(stacked, ki_ref[0, pl.ds(off, kb), :])
        acc = jnp.maximum(s[0:qb], 0.0) * wb[0]
        for h in range(1, IDX_HEADS):
            acc = acc + jnp.maximum(s[h * qb:(h + 1) * qb], 0.0) * wb[h]
        adm = lax.shift_right_arithmetic(off + lane_pos, 6) <= qchunk
        keys[:, pl.ds(off, kb)] = jnp.where(adm, jnp.where(acc == 0.0, 0.0, acc), -jnp.inf)
        return carry

    lax.fori_loop(0, nkb, score_body, 0)

    group = TOP_R * LANES
    ngrp = lax.div(kmax + (group - 1), group)

    def pad_body(i, carry):
        keys[:, pl.ds(pl.multiple_of(i * kb, kb), kb)] = jnp.full((qb, kb), -jnp.inf, f32)
        return carry

    lax.fori_loop(nkb, ngrp * (group // kb), pad_body, 0)
    top_scr[...] = jnp.full(top_scr.shape, -jnp.inf, f32)

    def top_body(gidx, carry):
        base = gidx * group
        for s in range(qb // SUBLANES):
            rs = slice(s * SUBLANES, (s + 1) * SUBLANES)
            v = [keys[rs, pl.ds(pl.multiple_of(base + i * LANES, LANES), LANES)] for i in range(TOP_R)]
            _sort_desc(v)
            merged = [jnp.maximum(top_scr[i, rs, :], v[TOP_R - 1 - i]) for i in range(TOP_R)]
            _bitonic_merge_desc(merged)
            for i in range(TOP_R):
                top_scr[i, rs, :] = merged[i]
        return carry

    lax.fori_loop(0, ngrp, top_body, 0)

    def flip(bits):
        return bits ^ (lax.shift_right_arithmetic(bits, 31) & INT_MAX)

    def to_key(x):
        k = flip(lax.bitcast_convert_type(x, i32))
        return jnp.where(k > 0, k - SUBNORMAL_SPAN, jnp.where(k < 0, k + SUBNORMAL_SPAN + 1, 0))

    def to_score(key):
        k = jnp.where(key > 0, key + SUBNORMAL_SPAN, jnp.where(key < 0, key - SUBNORMAL_SPAN - 1, 0))
        return lax.bitcast_convert_type(flip(k), f32)

    def count_top(thr):
        tb = jnp.broadcast_to(to_score(thr), (qb, LANES))
        acc = (top_scr[0] >= tb).astype(i32)
        for i in range(1, TOP_R):
            acc = acc + (top_scr[i] >= tb).astype(i32)
        return jnp.sum(acc, axis=1, keepdims=True)

    def count(pred):
        def body(i, acc):
            off = pl.multiple_of(i * kb, kb)
            blk = keys[:, pl.ds(off, kb)]
            m = pred(blk, off).astype(i32)
            for c in range(sub):
                acc = acc + m[:, c * LANES:(c + 1) * LANES]
            return acc
        acc = lax.fori_loop(0, nkb, body, jnp.zeros((qb, LANES), i32))
        return jnp.sum(acc, axis=1, keepdims=True)

    def count_ge(thr):
        tb = jnp.broadcast_to(to_score(thr), (qb, kb))
        return count(lambda blk, off: blk >= tb)

    assert topk <= 2 * LANES and TOP_R >= 2
    hi0 = to_key(jnp.max(top_scr[0], axis=1, keepdims=True)) + 1
    lo0 = jnp.minimum(to_key(jnp.min(top_scr[1], axis=1, keepdims=True)), hi0 - 1)

    def search(count_fn, lo_start):
        def cond(c):
            it, open_rows, _, _ = c
            return (open_rows > 0) & (it < 40)

        def body(c):
            it, _, lo, hi = c
            mid = lax.shift_right_arithmetic(lo, 1) + lax.shift_right_arithmetic(hi, 1) + (lo & hi & 1)
            n = count_fn(mid)
            ge = n >= topk
            lo_n = jnp.where(ge, mid, lo)
            hi_n = jnp.where(n == topk, mid + 1, jnp.where(ge, hi, mid))
            open_rows = jnp.max((lo_n < hi_n - 1).astype(i32))
            return it + 1, open_rows, lo_n, hi_n

        open0 = jnp.max((lo_start < hi0 - 1).astype(i32))
        return lax.while_loop(cond, body, (jnp.int32(0), open0, lo_start, hi0))[2]

    thr_c = search(count_top, lo0)
    n_ge_c = count_ge(thr_c)
    bad = (n_ge_c != count_top(thr_c)) & (thr_c > NEG_INF_KEY)

    def full_search(_):
        t = search(count_ge, lo0)
        return t, count_ge(t)

    thr, n_ge = lax.cond(jnp.max(bad.astype(i32)) > 0, full_search, lambda _: (thr_c, n_ge_c), 0)
    n_gt = count_ge(thr + 1)
    tie = (n_ge > topk) & (thr > NEG_INF_KEY)
    need = topk - n_gt

    def tie_cut(_):
        tb = jnp.broadcast_to(to_score(thr), (qb, kb))

        def body(_, c):
            lo, hi = c
            mid = lax.shift_right_arithmetic(lo + hi, 1)
            mb = jnp.broadcast_to(mid, (qb, kb))
            n = count(lambda blk, off: (blk == tb) & ((off + lane_pos) <= mb))
            ok = n >= need
            return jnp.where(ok, lo, mid), jnp.where(ok, mid, hi)

        lo_i = jnp.full((qb, 1), -1, i32)
        hi_i = jnp.full((qb, 1), 1, i32) * (nkb * kb - 1)
        _, cut = lax.fori_loop(0, idx_bits, body, (lo_i, hi_i))
        return jnp.where(tie, cut, INT_MAX)

    any_tie = jnp.max(tie.astype(i32)) > 0
    cut = lax.cond(any_tie, tie_cut, lambda _: jnp.full((qb, 1), INT_MAX, i32), 0)

    tb = jnp.broadcast_to(to_score(thr), (qb, kb))
    cb = jnp.broadcast_to(cut, (qb, kb))

    def out_body(i, carry):
        off = pl.multiple_of(i * kb, kb)
        blk = keys[:, pl.ds(off, kb)]
        sel = ((blk > tb) | ((blk == tb) & ((off + lane_pos) <= cb))) & (blk > -jnp.inf)
        o_ref[0, :, pl.ds(off, kb)] = jnp.where(sel, 0.0, NEG_BIG).astype(bf16)
        return carry

    lax.fori_loop(0, nkb, out_body, 0)

    def fill_body(i, carry):
        off = pl.multiple_of(i * kb, kb)
        o_ref[0, :, pl.ds(off, kb)] = jnp.full((qb, kb), NEG_BIG, bf16)
        return carry

    lax.fori_loop(nkb, ltot // kb, fill_body, 0)


def _dsa_mask(proj, ki16, *, qb, qoff, topk):
    b, t, _ = proj.shape
    ltot = ki16.shape[1]
    idx_bits = int(math.ceil(math.log2(ltot))) + 1
    kern = functools.partial(_index_kernel, qoff=qoff, topk=topk, idx_bits=idx_bits)
    return pl.pallas_call(
        kern,
        out_shape=jax.ShapeDtypeStruct((b, t, ltot), bf16),
        grid=(b, t // qb),
        in_specs=[pl.BlockSpec((1, qb, C_QI), lambda i, j: (i, j, OFF_QI // C_QI)),
                  pl.BlockSpec((1, qb, LANES), lambda i, j: (i, j, OFF_SMALL // LANES)),
                  pl.BlockSpec((1, ltot, IDX_DIM), lambda i, j: (i, 0, 0))],
        out_specs=pl.BlockSpec((1, qb, ltot), lambda i, j: (i, j, 0)),
        scratch_shapes=[pltpu.VMEM((qb, -(-ltot // (TOP_R * LANES)) * TOP_R * LANES), f32),
                        pltpu.VMEM((TOP_R, qb, LANES), f32)],
        compiler_params=_cp(("arbitrary", "arbitrary")),
        name="dsa_index",
    )(proj, proj, ki16)


def _attn_kernel(qidx_ref, kidx_ref, flag_ref, case_ref, q_ref, *refs, has_cache):
    if has_cache:
        (ck_ref, cv_ref, kn_ref, vn_ref, mb_ref, bias_ref, o_ref,
         q_scr, m_scr, l_scr, acc_scr, s_scr, p_scr, a_scr, mb_scr, k_ref, v_ref) = refs
    else:
        (k_ref, v_ref, mb_ref, bias_ref, o_ref,
         q_scr, m_scr, l_scr, acc_scr, s_scr, p_scr, a_scr, mb_scr) = refs
    s_id = pl.program_id(1)
    qb = q_ref.shape[1]
    kb = k_ref.shape[1]
    rows = C_GROUP * qb
    nt = kb // LANES
    flags = flag_ref[s_id]

    if has_cache:
        @pl.when((flags & 8) != 0)
        def _():
            k_ref[0] = kn_ref[0]
            v_ref[0] = vn_ref[0]

        @pl.when((flags & 8) == 0)
        def _():
            k_ref[0] = ck_ref[0, 0].reshape(kb, C_KV).astype(bf16)
            v_ref[0] = cv_ref[0, 0].reshape(kb, C_KV).astype(bf16)

    @pl.when((flags & 1) != 0)
    def _():
        qv = (q_ref[0] * (C_HEAD_DIM ** -0.5 * LOG2E)).astype(bf16)
        for g in range(C_KV_HEADS):
            for r in range(C_GROUP):
                hd = g * C_GROUP + r
                q_scr[g, r * qb:(r + 1) * qb, :] = qv[:, hd * C_HEAD_DIM:(hd + 1) * C_HEAD_DIM]
        m_scr[...] = jnp.full(m_scr.shape, NEG_BIG, f32)
        l_scr[...] = jnp.zeros_like(l_scr)
        acc_scr[...] = jnp.zeros_like(acc_scr)

    def visit(with_bias):
        kblk = k_ref[0]
        vblk = v_ref[0]
        mb_scr[...] = mb_ref[0].astype(f32)
        units = [(g, r) for g in range(C_KV_HEADS) for r in range(C_GROUP)]

        def qk(u):
            g, r = units[u]
            hs = slice(r * qb, (r + 1) * qb)
            s_scr[g, hs, :] = _dot_nt(q_scr[g, hs, :], kblk[:, g * C_HEAD_DIM:(g + 1) * C_HEAD_DIM])

        def soft(u):
            g, r = units[u]
            for c in range(qb // ATT_RC):
                q0 = c * ATT_RC
                rs = slice(r * qb + q0, r * qb + q0 + ATT_RC)
                s = s_scr[g, rs, :] + mb_scr[q0:q0 + ATT_RC, :]
                if with_bias:
                    s = s + bias_ref[0, g * C_GROUP + r, q0:q0 + ATT_RC, :]
                t = s[:, 0:LANES]
                for i in range(1, nt):
                    t = jnp.maximum(t, s[:, i * LANES:(i + 1) * LANES])
                m_old = m_scr[g, rs, :]
                m_new = jnp.maximum(m_old, jnp.max(t, axis=1, keepdims=True))
                alpha = jnp.exp2(m_old - m_new)
                p = jnp.exp2(s - jnp.concatenate([m_new] * nt, axis=1))
                psum = p[:, 0:LANES]
                for i in range(1, nt):
                    psum = psum + p[:, i * LANES:(i + 1) * LANES]
                l_scr[g, rs, :] = alpha * l_scr[g, rs, :] + psum
                m_scr[g, rs, :] = m_new
                a_scr[g, rs, :] = alpha
                p_scr[g, rs, :] = p.astype(bf16)

        def pv(u):
            g, r = units[u]
            hs = slice(r * qb, (r + 1) * qb)
            acc_scr[g, hs, :] = a_scr[g, hs, :] * acc_scr[g, hs, :] + _dot(
                p_scr[g, hs, :], vblk[:, g * C_HEAD_DIM:(g + 1) * C_HEAD_DIM])

        qk(0)
        for u in range(len(units)):
            if u + 1 < len(units):
                qk(u + 1)
            soft(u)
            pv(u)

    @pl.when((flags & 4) != 0)
    def _():
        visit(True)

    @pl.when((flags & 4) == 0)
    def _():
        visit(False)

    @pl.when((flags & 2) != 0)
    def _():
        for g in range(C_KV_HEADS):
            o = acc_scr[g] / jnp.sum(l_scr[g], axis=1, keepdims=True)
            for r in range(C_GROUP):
                hd = g * C_GROUP + r
                o_ref[0, :, hd * C_HEAD_DIM:(hd + 1) * C_HEAD_DIM] = o[r * qb:(r + 1) * qb, :].astype(bf16)


def _attn_schedule(nq, qb, qoff, offsets):
    qidx, kidx, flags, case = [], [], [], []
    zero_case = len(offsets)
    for j in range(nq):
        kmax = qoff + (j + 1) * qb
        nkb = -(-kmax // ATT_KB)
        for kb in range(nkb):
            off = kmax - BIAS_WIN - kb * ATT_KB
            qidx.append(j)
            kidx.append(kb)
            near = off in offsets
            flags.append((1 if kb == 0 else 0) | (2 if kb == nkb - 1 else 0) | (4 if near else 0)
                         | (8 if kb * ATT_KB >= qoff else 0))
            case.append(offsets.index(off) if near else zero_case)
            assert near == (-BIAS_WIN < off < ATT_KB)
    return [jnp.asarray(np.asarray(a, np.int32)) for a in (qidx, kidx, flags, case)]


def _dsa_attend(proj, k16, v16, maskbias, tables, *, qb, qoff, offsets, cache=None):
    b, t, _ = proj.shape
    nq = t // qb
    qidx, kidx, flags, case = _attn_schedule(nq, qb, qoff, offsets)
    ns = int(qidx.shape[0])
    rows = C_GROUP * qb
    kv_block = lambda f: pl.BlockSpec((1, ATT_KB, C_KV), f)
    if cache is None:
        kv_specs = [kv_block(lambda i, s, qi, ki, fl, ca: (i, ki[s], 0))] * 2
        kv_args = (k16, v16)
        kv_scratch = []
    else:
        layer = cache[2]
        ncache = cache[0].shape[2] // ATT_KB
        assert cache[0].shape[2] == qoff and qoff % ATT_KB == 0 and nq == 1 and k16.shape[1] == ATT_KB
        cache_block = pl.BlockSpec((1, 1, ATT_KB, C_KV_HEADS, C_HEAD_DIM),
                                   lambda i, s, qi, ki, fl, ca: (layer, i, jnp.minimum(ki[s], ncache - 1), 0, 0))
        kv_specs = [cache_block] * 2 + [kv_block(lambda i, s, qi, ki, fl, ca: (i, 0, 0))] * 2
        kv_args = (cache[0], cache[1], k16, v16)
        kv_scratch = [pltpu.VMEM((1, ATT_KB, C_KV), bf16)] * 2
    grid_spec = pltpu.PrefetchScalarGridSpec(
        num_scalar_prefetch=4,
        grid=(b, ns),
        in_specs=([pl.BlockSpec((1, qb, C_Q), lambda i, s, qi, ki, fl, ca: (i, qi[s], OFF_Q // C_Q))] + kv_specs
                  + [pl.BlockSpec((1, qb, ATT_KB), lambda i, s, qi, ki, fl, ca: (i, qi[s], ki[s])),
                     pl.BlockSpec((1, C_HEADS, qb, ATT_KB), lambda i, s, qi, ki, fl, ca: (ca[s], 0, 0, 0))]),
        out_specs=pl.BlockSpec((1, qb, C_WIDTH), lambda i, s, qi, ki, fl, ca: (i, qi[s], 0)),
        scratch_shapes=[pltpu.VMEM((C_KV_HEADS, rows, C_HEAD_DIM), bf16),
                        pltpu.VMEM((C_KV_HEADS, rows, LANES), f32),
                        pltpu.VMEM((C_KV_HEADS, rows, LANES), f32),
                        pltpu.VMEM((C_KV_HEADS, rows, C_HEAD_DIM), f32),
                        pltpu.VMEM((C_KV_HEADS, rows, ATT_KB), f32),
                        pltpu.VMEM((C_KV_HEADS, rows, ATT_KB), bf16),
                        pltpu.VMEM((C_KV_HEADS, rows, LANES), f32),
                        pltpu.VMEM((qb, ATT_KB), f32)] + kv_scratch)
    return pl.pallas_call(
        functools.partial(_attn_kernel, has_cache=cache is not None),
        out_shape=jax.ShapeDtypeStruct((b, t, C_WIDTH), bf16),
        grid_spec=grid_spec,
        compiler_params=_cp(("arbitrary", "arbitrary")),
        name="dsa_attend",
    )(qidx, kidx, flags, case, proj, *kv_args, maskbias, tables)


def _window_offsets(nq, qb, qoff):
    offs = set()
    for j in range(nq):
        kmax = qoff + (j + 1) * qb
        for kb in range(-(-kmax // ATT_KB)):
            off = kmax - BIAS_WIN - kb * ATT_KB
            if -BIAS_WIN < off < ATT_KB:
                offs.add(off)
    return sorted(offs)


def _mixer_c(proj, k_all16, v_all16, ki_all16, rel_bias, *, qb, qoff, true_len, cache=None):
    b, t, _ = proj.shape
    topk = min(TOPK_MAX, true_len // 4)
    assert BIAS_WIN - qb + 1 > MAX_DISTANCE
    offsets = _window_offsets(t // qb, qb, qoff)
    tables = _bias_tables(rel_bias, qb, offsets)
    idx_qb = IDX_QB if t % IDX_QB == 0 else qb
    maskbias = _dsa_mask(proj, ki_all16, qb=idx_qb, qoff=qoff, topk=topk)
    return _dsa_attend(proj, k_all16, v_all16, maskbias, tables, qb=qb, qoff=qoff, offsets=offsets, cache=cache)


def _merge_kernel(x_ref, ada_ref, gt_ref, oa_ref, ob_ref, oc_ref, wa_ref, wb_ref, wc_ref, wo_ref, o_ref):
    bt, tt, d = x_ref.shape
    rows = bt * tt
    gates = jax.nn.sigmoid(gt_ref[...].reshape(rows, GATE_COLS))
    merged = (gates[:, 0:d] * _dot(oa_ref[...].reshape(rows, A_WIDTH), wa_ref[...])
              + gates[:, d:2 * d] * _dot(ob_ref[...].reshape(rows, SSM_INNER), wb_ref[...])
              + gates[:, 2 * d:3 * d] * _dot(oc_ref[...].reshape(rows, C_WIDTH), wc_ref[...]))
    mix = _dot(merged.astype(bf16), wo_ref[...]).reshape(bt, tt, d)
    o_ref[...] = x_ref[...] + ada_ref[...][:, 5:6, :] * mix


def _merge(x, ada, proj, o_a, o_b, o_c, wa16, wb16, wc16, wo16):
    b, t, d = x.shape
    bt, tt = _token_tile(b, t, rows=256)
    tok = lambda w: pl.BlockSpec((bt, tt, w), lambda i, j: (i, j, 0))
    full = lambda r, c: pl.BlockSpec((r, c), lambda i, j: (0, 0))
    return pl.pallas_call(
        _merge_kernel,
        out_shape=jax.ShapeDtypeStruct((b, t, d), f32),
        grid=(b // bt, t // tt),
        in_specs=[tok(d), pl.BlockSpec((bt, N_ADA, d), lambda i, j: (i, 0, 0)),
                  pl.BlockSpec((bt, tt, GATE_COLS), lambda i, j: (i, j, OFF_GATE // GATE_COLS)),
                  tok(A_WIDTH), tok(SSM_INNER), tok(C_WIDTH),
                  full(A_WIDTH, d), full(SSM_INNER, d), full(C_WIDTH, d), full(d, d)],
        out_specs=tok(d),
        compiler_params=_cp(("arbitrary", "arbitrary")),
        name="merge",
    )(x, ada, proj, o_a, o_b, o_c, wa16, wb16, wc16, wo16)


def _pad_keys(x, mult):
    pad = (-x.shape[1]) % mult
    if pad:
        x = jnp.concatenate([x, jnp.zeros((x.shape[0], pad) + x.shape[2:], x.dtype)], axis=1)
    return x


def _layer(x, ada, lp, rel_bias, past, g_final, last, layer):
    b, t, d = x.shape
    x = _ffn(x, ada, lp['g_ffn1'], lp['w_ffn1_in'], lp['w_ffn1_out'], g_final, k0=0, final_norm=False)
    proj, k16, v16, ki16, k_new, v_new, ki_new = _inproj(x, ada, lp['g_mix'], lp['w_proj'], k0=3)
    new_conv = proj[:, t - (CONV_W - 1):, OFF_XBC:OFF_XBC + CONV_DIM]

    o_a, va = _mixer_a(proj, lp['g_a'], lp['w_s'], lp['b_s'])

    if past is None:
        conv_init = jnp.zeros((b, CONV_W - 1, CONV_DIM), f32)
        h0 = jnp.zeros((b, SSM_HEADS, SSM_HEAD_DIM, SSM_STATE), f32)
    else:
        h0, conv_init = past[3], past[4]
    o_b, new_ssm = _mixer_b(proj, conv_init, h0, lp['w_conv'], lp['b_conv'], lp['dt_bias'], lp['a_log'],
                            lp['d_skip'], lp['g_ssm'])
    new_ssm = new_ssm.reshape(b, SSM_HEADS, SSM_HEAD_DIM, SSM_STATE)

    if past is None:
        k_all, v_all, ki_all, cache = k16, v16, ki16, None
        qb, qoff, true_len = (ATT_QB if t % ATT_QB == 0 else Q_BLOCK), 0, t
    else:
        plen = past[0].shape[2]
        if plen % ATT_KB == 0:
            k_all, v_all, cache = k16, v16, (past[0], past[1], layer)
        else:
            k_all = jnp.concatenate([past[0][layer].reshape(b, plen, C_KV).astype(bf16), k16], axis=1)
            v_all = jnp.concatenate([past[1][layer].reshape(b, plen, C_KV).astype(bf16), v16], axis=1)
            cache = None
        ki_all = jnp.concatenate([past[2].astype(bf16), ki16], axis=1)
        qb, qoff, true_len = t, plen, plen + t
    o_c = _mixer_c(proj, _pad_keys(k_all, ATT_KB), _pad_keys(v_all, ATT_KB), _pad_keys(ki_all, ATT_KB), rel_bias,
                   qb=qb, qoff=qoff, true_len=true_len, cache=cache)

    x = _merge(x, ada, proj, o_a, o_b, o_c, lp['w_br_a'], lp['w_br_b'], lp['w_br_c'], lp['w_out'])
    x = _ffn(x, ada, lp['g_ffn2'], lp['w_ffn2_in'], lp['w_ffn2_out'], g_final, k0=6, final_norm=last)
    states = (k_new, v_new, ki_new, new_ssm, new_conv, va)
    return x, states


def kernel(x_prompt, x_sample, c_prompt, c_sample, cache_k, cache_v, cache_kidx, state_ssm, state_conv, rel_bias, w_ada, b_ada, g_ffn1, w_ffn1_in, w_ffn1_out, g_mix, w_in, g_a, w_s, b_s, w_conv, b_conv, dt_bias, a_log, d_skip, g_ssm, w_br_a, w_br_b, w_br_c, w_out, g_ffn2, w_ffn2_in, w_ffn2_out, g_final):
    depth = w_ada.shape[0]
    bp, bs = c_prompt.shape[0], c_sample.shape[0]
    assert x_prompt.shape[1] % ATT_KB == 0 and x_prompt.shape[1] % Q_BLOCK == 0
    assert x_sample.shape[1] == CHUNK

    rows = -(-(bp + bs) // 8) * 8
    c_all = jnp.concatenate([c_prompt, c_sample, jnp.zeros((rows - bp - bs, D_MODEL), f32)], axis=0)
    ada_all = _ada_all(c_all, w_ada.astype(bf16), b_ada)
    ada_all = ada_all.reshape(depth, rows, N_ADA, D_MODEL)

    yp, ys = x_prompt, x_sample
    st_p, st_s = [], []
    for i in range(depth):
        lp = dict(g_ffn1=g_ffn1[i], w_ffn1_in=w_ffn1_in[i].astype(bf16), w_ffn1_out=w_ffn1_out[i].astype(bf16),
                  g_mix=g_mix[i], w_proj=_layout_w_in(w_in[i]).astype(bf16), g_a=g_a[i], w_s=w_s[i], b_s=b_s[i],
                  w_conv=w_conv[i], b_conv=b_conv[i], dt_bias=dt_bias[i], a_log=a_log[i], d_skip=d_skip[i],
                  g_ssm=g_ssm[i], w_br_a=w_br_a[i].astype(bf16), w_br_b=w_br_b[i].astype(bf16),
                  w_br_c=w_br_c[i].astype(bf16), w_out=w_out[i].astype(bf16), g_ffn2=g_ffn2[i],
                  w_ffn2_in=w_ffn2_in[i].astype(bf16), w_ffn2_out=w_ffn2_out[i].astype(bf16))
        last = i == depth - 1
        yp, sp = _layer(yp, ada_all[i, :bp], lp, rel_bias, None, g_final, last, i)
        ys, ss = _layer(ys, ada_all[i, bp:bp + bs], lp, rel_bias,
                        (cache_k, cache_v, cache_kidx[i], state_ssm[i], state_conv[i]), g_final, last, i)
        st_p.append(sp)
        st_s.append(ss)

    stack = lambda sts, k: jnp.stack([s[k] for s in sts])
    return (yp, ys,
            stack(st_p, 0), stack(st_p, 1), stack(st_p, 2), stack(st_p, 3), stack(st_p, 4),
            stack(st_s, 0), stack(st_s, 1), stack(st_s, 2), stack(st_s, 3), stack(st_s, 4), stack(st_s, 5))
```

```python
import functools
import math

import numpy as np
import jax
import jax.numpy as jnp
from jax import lax
from jax.experimental import pallas as pl
from jax.experimental.pallas import tpu as pltpu

f32 = jnp.float32
bf16 = jnp.bfloat16
i32 = jnp.int32

D_MODEL = 1024
CHUNK = 64
EPS = 1e-6
N_ADA = 9
D_FF = 2816
A_WIDTH = D_MODEL
A_GROUPS = 8
A_GROUP_DIM = A_WIDTH // A_GROUPS
A_CHUNK = 128
SSM_INNER = 2 * D_MODEL
SSM_HEAD_DIM = 64
SSM_HEADS = SSM_INNER // SSM_HEAD_DIM
SSM_GROUPS = 4
SSM_HPG = SSM_HEADS // SSM_GROUPS
SSM_STATE = 128
CONV_W = 4
CONV_DIM = SSM_INNER + 2 * SSM_GROUPS * SSM_STATE
C_HEADS = 8
C_KV_HEADS = 4
C_GROUP = C_HEADS // C_KV_HEADS
C_HEAD_DIM = 128
C_WIDTH = C_HEADS * C_HEAD_DIM
IDX_HEADS = 8
IDX_DIM = 64
TOPK_MAX = 256
Q_BLOCK = 128
N_BUCKETS = 32
MAX_DISTANCE = 128
A_COLS = 2 * A_WIDTH
B_COLS = SSM_INNER + CONV_DIM + SSM_HEADS
C_Q = C_HEADS * C_HEAD_DIM
C_KV = C_KV_HEADS * C_HEAD_DIM
C_QI = IDX_HEADS * IDX_DIM
GATE_COLS = 3 * D_MODEL

LANES = 128
SUBLANES = 8
VMEM_LIMIT = 56 * 1024 * 1024

OFF_GATE = 0
OFF_XBC = OFF_GATE + GATE_COLS
OFF_PA = OFF_XBC + CONV_DIM
OFF_Z = OFF_PA + A_COLS
OFF_Q = OFF_Z + SSM_INNER
OFF_K = OFF_Q + C_Q
OFF_V = OFF_K + C_KV
OFF_QI = OFF_V + C_KV
OFF_SMALL = OFF_QI + C_QI
SMALL_DT = 0
SMALL_WI = SSM_HEADS
SMALL_KI = LANES - IDX_DIM
PROJ_TILE = 1024
PROJ_W = ((OFF_SMALL + LANES + PROJ_TILE - 1) // PROJ_TILE) * PROJ_TILE

NEG_BIG = -1e30
SUBNORMAL_SPAN = 0x007FFFFF
NEG_INF_KEY = -2139095041 + SUBNORMAL_SPAN + 1
INT_MIN = -2147483648
INT_MAX = 2147483647
ATT_KB = 1024
IDX_KB = 512
TOP_R = 16
IDX_QB = 256
ATT_RC = 8
LOG2E = math.log2(math.e)
ATT_QB = 256
BIAS_WIN = ATT_QB + MAX_DISTANCE
FAR_BUCKET = N_BUCKETS // 2 - 1


def _cp(sem, vmem=VMEM_LIMIT):
    return pltpu.CompilerParams(dimension_semantics=sem, vmem_limit_bytes=vmem)


def _rms(x, g):
    return x * lax.rsqrt(jnp.mean(x * x, axis=-1, keepdims=True) + EPS) * g


def _dot(a, b):
    return jnp.dot(a, b, preferred_element_type=f32)


def _dot_nt(a, b):
    return lax.dot_general(a, b, (((1,), (1,)), ((), ())), preferred_element_type=f32)


def _dot_tn(a, b):
    return lax.dot_general(a, b, (((0,), (0,)), ((), ())), preferred_element_type=f32)


def _bf16_pieces(x, pieces):
    out = []
    for _ in range(pieces):
        p = x.astype(bf16)
        out.append(p)
        x = x - p.astype(f32)
    return out


def _dot_zero_one_lhs(sel16, x, pieces):
    parts = [_dot(sel16, p) for p in _bf16_pieces(x, pieces)]
    return functools.reduce(lambda a, b: a + b, parts)


def _dot_zero_one_rhs(x, sel16, pieces):
    parts = [_dot(p, sel16) for p in _bf16_pieces(x, pieces)]
    return functools.reduce(lambda a, b: a + b, parts)


def _ada_kernel(c_ref, w_ref, b_ref, o_ref):
    s = jax.nn.silu(c_ref[...]).astype(bf16)
    o_ref[0] = _dot(s, w_ref[0]) + b_ref[0]


def _ada_all(c_all, w_ada16, b_ada):
    depth = w_ada16.shape[0]
    rows = c_all.shape[0]
    n_out = N_ADA * D_MODEL
    tn = 9 * LANES
    return pl.pallas_call(
        _ada_kernel,
        out_shape=jax.ShapeDtypeStruct((depth, rows, n_out), f32),
        grid=(depth, n_out // tn),
        in_specs=[pl.BlockSpec((rows, D_MODEL), lambda d, n: (0, 0)),
                  pl.BlockSpec((1, D_MODEL, tn), lambda d, n: (d, 0, n)),
                  pl.BlockSpec((1, 1, tn), lambda d, n: (d, 0, n))],
        out_specs=pl.BlockSpec((1, rows, tn), lambda d, n: (d, 0, n)),
        compiler_params=_cp(("arbitrary", "arbitrary")),
        name="ada",
    )(c_all, w_ada16, b_ada.reshape(depth, 1, n_out))


def _ffn_kernel(x_ref, ada_ref, g_ref, wg_ref, wu_ref, wo_ref, gf_ref, o_ref, h_scr, acc_scr, *, k0, final_norm):
    f = pl.program_id(2)
    bt, tt, d = x_ref.shape
    rows = bt * tt

    @pl.when(f == 0)
    def _():
        ada = ada_ref[...]
        h = _rms(x_ref[...], g_ref[...]) * (1.0 + ada[:, k0 + 1:k0 + 2, :]) + ada[:, k0:k0 + 1, :]
        h_scr[...] = h.reshape(rows, d).astype(bf16)
        acc_scr[...] = jnp.zeros_like(acc_scr)

    h = h_scr[...]
    gate = _dot(h, wg_ref[...])
    up = _dot(h, wu_ref[...])
    a = (jax.nn.silu(gate) * up).astype(bf16)
    acc_scr[...] += _dot(a, wo_ref[...])

    @pl.when(f == pl.num_programs(2) - 1)
    def _():
        ada = ada_ref[...]
        y = x_ref[...] + (0.5 * ada[:, k0 + 2:k0 + 3, :]) * acc_scr[...].reshape(bt, tt, d)
        if final_norm:
            y = _rms(y, gf_ref[...])
        o_ref[...] = y


def _token_tile(b, t, rows=512):
    if t >= rows:
        return 1, rows
    return min(b, rows // t), t


def _ffn(x, ada, g, w_in16, w_out16, g_final, *, k0, final_norm):
    b, t, d = x.shape
    bt, tt = _token_tile(b, t)
    tf = D_FF // 2
    nf = D_FF // tf
    kern = functools.partial(_ffn_kernel, k0=k0, final_norm=final_norm)
    return pl.pallas_call(
        kern,
        out_shape=jax.ShapeDtypeStruct((b, t, d), f32),
        grid=(b // bt, t // tt, nf),
        in_specs=[pl.BlockSpec((bt, tt, d), lambda i, j, f: (i, j, 0)),
                  pl.BlockSpec((bt, N_ADA, d), lambda i, j, f: (i, 0, 0)),
                  pl.BlockSpec((1, d), lambda i, j, f: (0, 0)),
                  pl.BlockSpec((d, tf), lambda i, j, f: (0, f)),
                  pl.BlockSpec((d, tf), lambda i, j, f: (0, f + nf)),
                  pl.BlockSpec((tf, d), lambda i, j, f: (f, 0)),
                  pl.BlockSpec((1, d), lambda i, j, f: (0, 0))],
        out_specs=pl.BlockSpec((bt, tt, d), lambda i, j, f: (i, j, 0)),
        scratch_shapes=[pltpu.VMEM((bt * tt, d), bf16), pltpu.VMEM((bt * tt, d), f32)],
        compiler_params=_cp(("arbitrary", "arbitrary", "arbitrary")),
        name="ffn",
    )(x, ada, g.reshape(1, d), w_in16, w_in16, w_out16, g_final.reshape(1, d))


def _inproj_kernel(x_ref, ada_ref, g_ref, w_ref, o_ref, k16_ref, v16_ref, ki16_ref, k32_ref, v32_ref, ki32_ref, h_scr, *, k0):
    n = pl.program_id(2)
    bt, tt, d = x_ref.shape

    @pl.when(n == 0)
    def _():
        ada = ada_ref[...]
        h = _rms(x_ref[...], g_ref[...]) * (1.0 + ada[:, k0 + 1:k0 + 2, :]) + ada[:, k0:k0 + 1, :]
        h_scr[...] = h.reshape(bt * tt, d).astype(bf16)

    res = _dot(h_scr[...], w_ref[...])
    o_ref[...] = res.reshape(bt, tt, PROJ_TILE)

    @pl.when(n == OFF_K // PROJ_TILE)
    def _():
        k = res[:, :C_KV]
        v = res[:, C_KV:2 * C_KV]
        k16_ref[...] = k.astype(bf16).reshape(bt, tt, C_KV)
        v16_ref[...] = v.astype(bf16).reshape(bt, tt, C_KV)
        k32_ref[...] = k.reshape(bt, tt, C_KV_HEADS, C_HEAD_DIM)
        v32_ref[...] = v.reshape(bt, tt, C_KV_HEADS, C_HEAD_DIM)

    @pl.when(n == OFF_SMALL // PROJ_TILE)
    def _():
        lo = OFF_SMALL % PROJ_TILE + SMALL_KI
        ki = res[:, lo:lo + IDX_DIM]
        ki16_ref[...] = ki.astype(bf16).reshape(bt, tt, IDX_DIM)
        ki32_ref[...] = ki.reshape(bt, tt, IDX_DIM)


def _inproj(x, ada, g, w_proj16, *, k0):
    b, t, d = x.shape
    bt, tt = _token_tile(b, t, rows=1024)
    assert OFF_K % PROJ_TILE == 0 and OFF_V == OFF_K + C_KV and 2 * C_KV <= PROJ_TILE
    tok = lambda w: pl.BlockSpec((bt, tt, w), lambda i, j, n: (i, j, 0))
    heads = pl.BlockSpec((bt, tt, C_KV_HEADS, C_HEAD_DIM), lambda i, j, n: (i, j, 0, 0))
    kv32 = jax.ShapeDtypeStruct((b, t, C_KV_HEADS, C_HEAD_DIM), f32)
    return pl.pallas_call(
        functools.partial(_inproj_kernel, k0=k0),
        out_shape=(jax.ShapeDtypeStruct((b, t, PROJ_W), f32), jax.ShapeDtypeStruct((b, t, C_KV), bf16),
                   jax.ShapeDtypeStruct((b, t, C_KV), bf16), jax.ShapeDtypeStruct((b, t, IDX_DIM), bf16),
                   kv32, kv32, jax.ShapeDtypeStruct((b, t, IDX_DIM), f32)),
        grid=(b // bt, t // tt, PROJ_W // PROJ_TILE),
        in_specs=[pl.BlockSpec((bt, tt, d), lambda i, j, n: (i, j, 0)),
                  pl.BlockSpec((bt, N_ADA, d), lambda i, j, n: (i, 0, 0)),
                  pl.BlockSpec((1, d), lambda i, j, n: (0, 0)),
                  pl.BlockSpec((d, PROJ_TILE), lambda i, j, n: (0, n))],
        out_specs=(pl.BlockSpec((bt, tt, PROJ_TILE), lambda i, j, n: (i, j, n)), tok(C_KV), tok(C_KV), tok(IDX_DIM),
                   heads, heads, tok(IDX_DIM)),
        scratch_shapes=[pltpu.VMEM((bt * tt, d), bf16)],
        compiler_params=_cp(("arbitrary", "arbitrary", "arbitrary")),
        name="inproj",
    )(x, ada, g.reshape(1, d), w_proj16)


def _layout_w_in(w_in):
    d = w_in.shape[0]
    o = 0
    pa = w_in[:, o:o + A_COLS]; o += A_COLS
    z = w_in[:, o:o + SSM_INNER]; o += SSM_INNER
    xbc = w_in[:, o:o + CONV_DIM]; o += CONV_DIM
    dt = w_in[:, o:o + SSM_HEADS]; o += SSM_HEADS
    q = w_in[:, o:o + C_Q]; o += C_Q
    k = w_in[:, o:o + C_KV]; o += C_KV
    v = w_in[:, o:o + C_KV]; o += C_KV
    qi = w_in[:, o:o + C_QI]; o += C_QI
    ki = w_in[:, o:o + IDX_DIM]; o += IDX_DIM
    wi = w_in[:, o:o + IDX_HEADS]; o += IDX_HEADS
    gates = w_in[:, o:o + GATE_COLS]
    small = jnp.concatenate([dt, wi, jnp.zeros((d, SMALL_KI - SMALL_WI - IDX_HEADS), w_in.dtype), ki], axis=1)
    pad = jnp.zeros((d, PROJ_W - OFF_SMALL - LANES), w_in.dtype)
    return jnp.concatenate([gates, xbc, pa, z, q, k, v, qi, small, pad], axis=1)


def _mixa_kernel(pa_ref, ga_ref, ws_ref, bs_ref, oa_ref, va_ref):
    n = pa_ref.shape[1]
    ge = jax.nn.gelu(pa_ref[0])
    u = ge[:, :A_WIDTH]
    v = _rms(ge[:, A_WIDTH:], ga_ref[...])
    va_ref[0] = v
    vb = v.astype(bf16)
    tri = lax.broadcasted_iota(i32, (n, n), 0) >= lax.broadcasted_iota(i32, (n, n), 1)
    bs = bs_ref[...]
    for g in range(A_GROUPS):
        sl = slice(g * A_GROUP_DIM, (g + 1) * A_GROUP_DIM)
        w = jnp.where(tri, ws_ref[g], 0.0).astype(bf16)
        mixed = _dot(w, vb[:, sl]) + bs[:, g:g + 1]
        oa_ref[0, :, sl] = (u[:, sl] * mixed).astype(bf16)


def _mixer_a(proj, g_a, w_s, b_s):
    b, t, _ = proj.shape
    n = min(t, A_CHUNK)
    ws = w_s[:, :n, :n]
    bs_t = b_s[:, :n].T
    return pl.pallas_call(
        _mixa_kernel,
        out_shape=(jax.ShapeDtypeStruct((b, t, A_WIDTH), bf16), jax.ShapeDtypeStruct((b, t, A_WIDTH), f32)),
        grid=(b, t // n),
        in_specs=[pl.BlockSpec((1, n, A_COLS), lambda i, j: (i, j, OFF_PA // A_COLS)),
                  pl.BlockSpec((1, A_WIDTH), lambda i, j: (0, 0)),
                  pl.BlockSpec((A_GROUPS, n, n), lambda i, j: (0, 0, 0)),
                  pl.BlockSpec((n, A_GROUPS), lambda i, j: (0, 0))],
        out_specs=(pl.BlockSpec((1, n, A_WIDTH), lambda i, j: (i, j, 0)),
                   pl.BlockSpec((1, n, A_WIDTH), lambda i, j: (i, j, 0))),
        compiler_params=_cp(("arbitrary", "arbitrary")),
        name="mixer_a",
    )(proj, g_a.reshape(1, A_WIDTH), ws, bs_t)


CONV_PAD = 8


def _ssd_kernel(z_ref, xbc_ref, sm_ref, cinit_ref, h0_ref, wc_ref, bc_ref, dtb_ref, alog_ref, dsk_ref, gs_ref,
                ex_ref, ob_ref, hs_ref, xpad, ht):
    c = pl.program_id(1)
    q = z_ref.shape[1]
    gw = SSM_HPG * SSM_HEAD_DIM
    tail = CONV_W - 1

    @pl.when(c == 0)
    def _():
        xpad[CONV_PAD - tail:CONV_PAD, :] = cinit_ref[0]
        for g in range(SSM_GROUPS):
            ht[g] = h0_ref[0, g * gw:(g + 1) * gw, :].T

    xpad[CONV_PAD:CONV_PAD + q, :] = xbc_ref[0]
    wc = wc_ref[...]
    conv = bc_ref[...] + wc[0:1, :] * xpad[pl.ds(CONV_PAD - 3, q), :]
    for k in range(1, CONV_W):
        conv = conv + wc[k:k + 1, :] * xpad[pl.ds(CONV_PAD - 3 + k, q), :]
    nxt = xpad[CONV_PAD + q - tail:CONV_PAD + q, :]
    xpad[CONV_PAD - tail:CONV_PAD, :] = nxt
    conv = jax.nn.silu(conv)
    xs = conv[:, :SSM_INNER]
    bm = conv[:, SSM_INNER:SSM_INNER + SSM_GROUPS * SSM_STATE]
    cm = conv[:, SSM_INNER + SSM_GROUPS * SSM_STATE:]

    lane = lax.broadcasted_iota(i32, (q, LANES), 1)
    head_lane = lane < SSM_HEADS
    dt = jnp.where(head_lane, jax.nn.softplus(sm_ref[0] + dtb_ref[...]), 0.0)
    da = dt * (-jnp.exp(alog_ref[...]))
    tri = lax.broadcasted_iota(i32, (q, q), 0) >= lax.broadcasted_iota(i32, (q, q), 1)
    acs = _dot_zero_one_lhs(tri.astype(bf16), da, 3)
    acs_t = acs.T
    dt_t = dt.T
    acs_last = acs[q - 1:q, :]
    e_acs = jnp.where(head_lane, jnp.exp(acs), 0.0)
    decay = jnp.where(head_lane, jnp.exp(acs_last - acs) * dt, 0.0)
    ex = ex_ref[...]
    e_acs_x = _dot_zero_one_rhs(e_acs, ex, 3)
    decay_x = _dot_zero_one_rhs(decay, ex, 3)
    e_last_x = e_acs_x[q - 1:q, :]

    xs16 = xs.astype(bf16)
    xdec16 = (xs * decay_x).astype(bf16)
    y_parts = []
    for g in range(SSM_GROUPS):
        bg = bm[:, g * SSM_STATE:(g + 1) * SSM_STATE].astype(bf16)
        cg = cm[:, g * SSM_STATE:(g + 1) * SSM_STATE].astype(bf16)
        cb = _dot_nt(cg, bg)
        htg = ht[g]
        y_inter = _dot(cg, htg.astype(bf16))
        intra = []
        for r in range(SSM_HPG):
            hd = g * SSM_HPG + r
            seg = acs[:, hd:hd + 1] - acs_t[hd:hd + 1, :]
            m = cb * jnp.exp(jnp.where(tri, seg, -jnp.inf)) * dt_t[hd:hd + 1, :]
            intra.append(_dot(m.astype(bf16), xs16[:, hd * SSM_HEAD_DIM:(hd + 1) * SSM_HEAD_DIM]))
        y_parts.append(jnp.concatenate(intra, axis=1) + y_inter * e_acs_x[:, g * gw:(g + 1) * gw])
        ht[g] = htg * e_last_x[:, g * gw:(g + 1) * gw] + _dot_tn(bg, xdec16[:, g * gw:(g + 1) * gw])
    y = jnp.concatenate(y_parts, axis=1) + dsk_ref[...] * xs
    y = _rms(y * jax.nn.silu(z_ref[0]), gs_ref[...])
    ob_ref[0] = y.astype(bf16)

    @pl.when(c == pl.num_programs(1) - 1)
    def _():
        for g in range(SSM_GROUPS):
            hs_ref[0, g * gw:(g + 1) * gw, :] = ht[g].T


def _mixer_b(proj, conv_init, h0, w_conv, b_conv, dt_bias, a_log, d_skip, g_ssm):
    b, t, _ = proj.shape
    q = CHUNK
    pad_row = lambda v: jnp.zeros((1, LANES), f32).at[0, :SSM_HEADS].set(v)
    expand = (jnp.arange(LANES)[:, None] == (jnp.arange(SSM_INNER) // SSM_HEAD_DIM)[None, :]).astype(bf16)
    hrows = SSM_HEADS * SSM_HEAD_DIM
    return pl.pallas_call(
        _ssd_kernel,
        out_shape=(jax.ShapeDtypeStruct((b, t, SSM_INNER), bf16),
                   jax.ShapeDtypeStruct((b, hrows, SSM_STATE), f32)),
        grid=(b, t // q),
        in_specs=[pl.BlockSpec((1, q, SSM_INNER), lambda i, c: (i, c, OFF_Z // SSM_INNER)),
                  pl.BlockSpec((1, q, CONV_DIM), lambda i, c: (i, c, OFF_XBC // CONV_DIM)),
                  pl.BlockSpec((1, q, LANES), lambda i, c: (i, c, OFF_SMALL // LANES)),
                  pl.BlockSpec((1, CONV_W - 1, CONV_DIM), lambda i, c: (i, 0, 0)),
                  pl.BlockSpec((1, hrows, SSM_STATE), lambda i, c: (i, 0, 0)),
                  pl.BlockSpec((CONV_W, CONV_DIM), lambda i, c: (0, 0)),
                  pl.BlockSpec((1, CONV_DIM), lambda i, c: (0, 0)),
                  pl.BlockSpec((1, LANES), lambda i, c: (0, 0)),
                  pl.BlockSpec((1, LANES), lambda i, c: (0, 0)),
                  pl.BlockSpec((1, SSM_INNER), lambda i, c: (0, 0)),
                  pl.BlockSpec((1, SSM_INNER), lambda i, c: (0, 0)),
                  pl.BlockSpec((LANES, SSM_INNER), lambda i, c: (0, 0))],
        out_specs=(pl.BlockSpec((1, q, SSM_INNER), lambda i, c: (i, c, 0)),
                   pl.BlockSpec((1, hrows, SSM_STATE), lambda i, c: (i, 0, 0))),
        scratch_shapes=[pltpu.VMEM((CONV_PAD + q, CONV_DIM), f32),
                        pltpu.VMEM((SSM_GROUPS, SSM_STATE, SSM_HPG * SSM_HEAD_DIM), f32)],
        compiler_params=_cp(("arbitrary", "arbitrary")),
        name="mixer_b",
    )(proj, proj, proj, conv_init, h0.reshape(b, hrows, SSM_STATE), w_conv, b_conv.reshape(1, CONV_DIM),
      pad_row(dt_bias), pad_row(a_log), jnp.repeat(d_skip, SSM_HEAD_DIM).reshape(1, SSM_INNER),
      g_ssm.reshape(1, SSM_INNER), expand)


def _t5_bucket(rel):
    nb = N_BUCKETS // 2
    max_exact = nb // 2
    n = jnp.abs(rel)
    nf = jnp.maximum(n, 1).astype(jnp.float32)
    large = max_exact + (jnp.log(nf / max_exact) / math.log(MAX_DISTANCE / max_exact)
                         * (nb - max_exact)).astype(jnp.int32)
    large = jnp.minimum(large, nb - 1)
    return jnp.where(rel > 0, nb, 0) + jnp.where(n < max_exact, n, large)


def _bias_kernel(tab_ref, bk_ref, o_ref):
    h = pl.program_id(1)
    bk = bk_ref[0]
    far = tab_ref[FAR_BUCKET, h]
    acc = jnp.zeros(bk.shape, f32)
    for bucket in range(N_BUCKETS):
        acc = jnp.where(bk == bucket, tab_ref[bucket, h] - far, acc)
    o_ref[0, 0] = acc * LOG2E


def _bias_tables(rel_bias, qb, offsets):
    t = np.arange(qb)[:, None]
    cases = []
    for off in offsets:
        w = np.arange(ATT_KB)[None, :] - off
        valid = (w >= 0) & (w < BIAS_WIN)
        rel = w - BIAS_WIN + qb - t
        cases.append(jnp.where(jnp.asarray(valid), _t5_bucket(jnp.asarray(rel, dtype=jnp.int32)), -1))
    cases.append(jnp.full((qb, ATT_KB), -1, jnp.int32))
    buckets = jnp.stack(cases).astype(i32)
    nc = buckets.shape[0]
    return pl.pallas_call(
        _bias_kernel,
        out_shape=jax.ShapeDtypeStruct((nc, C_HEADS, qb, ATT_KB), f32),
        grid=(nc, C_HEADS),
        in_specs=[pl.BlockSpec(memory_space=pltpu.SMEM),
                  pl.BlockSpec((1, qb, ATT_KB), lambda c, h: (c, 0, 0))],
        out_specs=pl.BlockSpec((1, 1, qb, ATT_KB), lambda c, h: (c, h, 0, 0)),
        compiler_params=_cp(("arbitrary", "arbitrary")),
        name="t5_bias",
    )(rel_bias, buckets)


def _oddeven_merge_sort_pairs(n):
    pairs = []
    p = 1
    while p < n:
        k = p
        while k >= 1:
            for j in range(k % p, n - k, 2 * k):
                for i in range(min(k, n - j - k)):
                    if (i + j) // (2 * p) == (i + j + k) // (2 * p):
                        pairs.append((i + j, i + j + k))
            k //= 2
        p *= 2
    return pairs


def _compare_exchange(v, i, j):
    a, b = v[i], v[j]
    v[i] = jnp.maximum(a, b)
    v[j] = jnp.minimum(a, b)


def _sort_desc(v):
    for i, j in _oddeven_merge_sort_pairs(len(v)):
        _compare_exchange(v, i, j)


def _bitonic_merge_desc(v):
    d = len(v) // 2
    while d >= 1:
        for i in range(len(v)):
            if (i & d) == 0:
                _compare_exchange(v, i, i + d)
        d //= 2


def _index_kernel(qi_ref, sm_ref, ki_ref, o_ref, keys, top_scr, *, qoff, topk, idx_bits):
    j = pl.program_id(1)
    qb = qi_ref.shape[1]
    ltot = ki_ref.shape[1]
    kb = IDX_KB
    kmax = qoff + (j + 1) * qb
    nkb = lax.div(kmax + (kb - 1), kb)
    sub = kb // LANES

    qv = qi_ref[0].astype(bf16)
    stacked = jnp.concatenate([qv[:, h * IDX_DIM:(h + 1) * IDX_DIM] for h in range(IDX_HEADS)], axis=0)
    w = sm_ref[0][:, SMALL_WI:SMALL_WI + IDX_HEADS] * (IDX_HEADS ** -0.5) * (IDX_DIM ** -0.5)
    wb = [jnp.broadcast_to(w[:, h:h + 1], (qb, kb)) for h in range(IDX_HEADS)]
    qchunk = lax.shift_right_arithmetic(qoff + j * qb + lax.broadcasted_iota(i32, (qb, 1), 0), 6)
    lane_pos = lax.broadcasted_iota(i32, (1, kb), 1)

    def score_body(i, carry):
        off = pl.multiple_of(i * kb, kb)
        s = _dot_nt(stacked, ki_ref[0, pl.ds(off, kb), :])
        acc = jnp.maximum(s[0:qb], 0.0) * wb[0]
        for h in range(1, IDX_HEADS):
            acc = acc + jnp.maximum(s[h * qb:(h + 1) * qb], 0.0) * wb[h]
        adm = lax.shift_right_arithmetic(off + lane_pos, 6) <= qchunk
        keys[:, pl.ds(off, kb)] = jnp.where(adm, jnp.where(acc == 0.0, 0.0, acc), -jnp.inf)
        return carry

    lax.fori_loop(0, nkb, score_body, 0)

    group = TOP_R * LANES
    ngrp = lax.div(kmax + (group - 1), group)

    def pad_body(i, carry):
        keys[:, pl.ds(pl.multiple_of(i * kb, kb), kb)] = jnp.full((qb, kb), -jnp.inf, f32)
        return carry

    lax.fori_loop(nkb, ngrp * (group // kb), pad_body, 0)
    top_scr[...] = jnp.full(top_scr.shape, -jnp.inf, f32)

    def top_body(gidx, carry):
        base = gidx * group
        for s in range(qb // SUBLANES):
            rs = slice(s * SUBLANES, (s + 1) * SUBLANES)
            v = [keys[rs, pl.ds(pl.multiple_of(base + i * LANES, LANES), LANES)] for i in range(TOP_R)]
            _sort_desc(v)
            merged = [jnp.maximum(top_scr[i, rs, :], v[TOP_R - 1 - i]) for i in range(TOP_R)]
            _bitonic_merge_desc(merged)
            for i in range(TOP_R):
                top_scr[i, rs, :] = merged[i]
        return carry

    lax.fori_loop(0, ngrp, top_body, 0)

    def flip(bits):
        return bits ^ (lax.shift_right_arithmetic(bits, 31) & INT_MAX)

    def to_key(x):
        k = flip(lax.bitcast_convert_type(x, i32))
        return jnp.where(k > 0, k - SUBNORMAL_SPAN, jnp.where(k < 0, k + SUBNORMAL_SPAN + 1, 0))

    def to_score(key):
        k = jnp.where(key > 0, key + SUBNORMAL_SPAN, jnp.where(key < 0, key - SUBNORMAL_SPAN - 1, 0))
        return lax.bitcast_convert_type(flip(k), f32)

    def count_top(thr):
        tb = jnp.broadcast_to(to_score(thr), (qb, LANES))
        acc = (top_scr[0] >= tb).astype(i32)
        for i in range(1, TOP_R):
            acc = acc + (top_scr[i] >= tb).astype(i32)
        return jnp.sum(acc, axis=1, keepdims=True)

    def count(pred):
        def body(i, acc):
            off = pl.multiple_of(i * kb, kb)
            blk = keys[:, pl.ds(off, kb)]
            m = pred(blk, off).astype(i32)
            for c in range(sub):
                acc = acc + m[:, c * LANES:(c + 1) * LANES]
            return acc
        acc = lax.fori_loop(0, nkb, body, jnp.zeros((qb, LANES), i32))
        return jnp.sum(acc, axis=1, keepdims=True)

    def count_ge(thr):
        tb = jnp.broadcast_to(to_score(thr), (qb, kb))
        return count(lambda blk, off: blk >= tb)

    assert topk <= 2 * LANES and TOP_R >= 2
    hi0 = to_key(jnp.max(top_scr[1 if topk > LANES else 0], axis=1, keepdims=True)) + 1
    lo0 = jnp.minimum(to_key(jnp.min(top_scr[1], axis=1, keepdims=True)), hi0 - 1)

    def search(count_fn, lo_start):
        def cond(c):
            it, open_rows, _, _ = c
            return (open_rows > 0) & (it < 40)

        def body(c):
            it, _, lo, hi = c
            mid = lax.shift_right_arithmetic(lo, 1) + lax.shift_right_arithmetic(hi, 1) + (lo & hi & 1)
            n = count_fn(mid)
            ge = n >= topk
            lo_n = jnp.where(ge, mid, lo)
            hi_n = jnp.where(n == topk, mid + 1, jnp.where(ge, hi, mid))
            open_rows = jnp.max((lo_n < hi_n - 1).astype(i32))
            return it + 1, open_rows, lo_n, hi_n

        open0 = jnp.max((lo_start < hi0 - 1).astype(i32))
        return lax.while_loop(cond, body, (jnp.int32(0), open0, lo_start, hi0))[2]

    thr_c = search(count_top, lo0)
    n_ge_c = count_top(thr_c)
    smallest = to_key(jnp.max(top_scr[TOP_R - 1], axis=1, keepdims=True))
    bad = (smallest >= thr_c) & (thr_c > NEG_INF_KEY)

    def full_search(_):
        t = search(count_ge, lo0)
        return t, count_ge(t)

    thr, n_ge = lax.cond(jnp.max(bad.astype(i32)) > 0, full_search, lambda _: (thr_c, n_ge_c), 0)
    tie = (n_ge > topk) & (thr > NEG_INF_KEY)

    def tie_cut(_):
        tb = jnp.broadcast_to(to_score(thr), (qb, kb))
        need = topk - count_ge(thr + 1)

        def body(_, c):
            lo, hi = c
            mid = lax.shift_right_arithmetic(lo + hi, 1)
            mb = jnp.broadcast_to(mid, (qb, kb))
            n = count(lambda blk, off: (blk == tb) & ((off + lane_pos) <= mb))
            ok = n >= need
            return jnp.where(ok, lo, mid), jnp.where(ok, mid, hi)

        lo_i = jnp.full((qb, 1), -1, i32)
        hi_i = jnp.full((qb, 1), 1, i32) * (nkb * kb - 1)
        _, cut = lax.fori_loop(0, idx_bits, body, (lo_i, hi_i))
        return jnp.where(tie, cut, INT_MAX)

    any_tie = jnp.max(tie.astype(i32)) > 0
    cut = lax.cond(any_tie, tie_cut, lambda _: jnp.full((qb, 1), INT_MAX, i32), 0)

    tb = jnp.broadcast_to(to_score(thr), (qb, kb))
    cb = jnp.broadcast_to(cut, (qb, kb))

    def out_body(i, carry):
        off = pl.multiple_of(i * kb, kb)
        blk = keys[:, pl.ds(off, kb)]
        sel = ((blk > tb) | ((blk == tb) & ((off + lane_pos) <= cb))) & (blk > -jnp.inf)
        o_ref[0, :, pl.ds(off, kb)] = jnp.where(sel, 0.0, NEG_BIG).astype(bf16)
        return carry

    lax.fori_loop(0, nkb, out_body, 0)

    def fill_body(i, carry):
        off = pl.multiple_of(i * kb, kb)
        o_ref[0, :, pl.ds(off, kb)] = jnp.full((qb, kb), NEG_BIG, bf16)
        return carry

    lax.fori_loop(nkb, ltot // kb, fill_body, 0)


def _dsa_mask(proj, ki16, *, qb, qoff, topk):
    b, t, _ = proj.shape
    ltot = ki16.shape[1]
    idx_bits = int(math.ceil(math.log2(ltot))) + 1
    kern = functools.partial(_index_kernel, qoff=qoff, topk=topk, idx_bits=idx_bits)
    return pl.pallas_call(
        kern,
        out_shape=jax.ShapeDtypeStruct((b, t, ltot), bf16),
        grid=(b, t // qb),
        in_specs=[pl.BlockSpec((1, qb, C_QI), lambda i, j: (i, j, OFF_QI // C_QI)),
                  pl.BlockSpec((1, qb, LANES), lambda i, j: (i, j, OFF_SMALL // LANES)),
                  pl.BlockSpec((1, ltot, IDX_DIM), lambda i, j: (i, 0, 0))],
        out_specs=pl.BlockSpec((1, qb, ltot), lambda i, j: (i, j, 0)),
        scratch_shapes=[pltpu.VMEM((qb, -(-ltot // (TOP_R * LANES)) * TOP_R * LANES), f32),
                        pltpu.VMEM((TOP_R, qb, LANES), f32)],
        compiler_params=_cp(("arbitrary", "arbitrary")),
        name="dsa_index",
    )(proj, proj, ki16)


def _attn_kernel(qidx_ref, kidx_ref, flag_ref, case_ref, q_ref, *refs, has_cache):
    if has_cache:
        (ck_ref, cv_ref, kn_ref, vn_ref, mb_ref, bias_ref, o_ref,
         q_scr, m_scr, l_scr, acc_scr, s_scr, p_scr, a_scr, mb_scr, k_ref, v_ref) = refs
    else:
        (k_ref, v_ref, mb_ref, bias_ref, o_ref,
         q_scr, m_scr, l_scr, acc_scr, s_scr, p_scr, a_scr, mb_scr) = refs
    s_id = pl.program_id(1)
    qb = q_ref.shape[1]
    kb = k_ref.shape[1]
    rows = C_GROUP * qb
    nt = kb // LANES
    flags = flag_ref[s_id]

    if has_cache:
        @pl.when((flags & 8) != 0)
        def _():
            k_ref[0] = kn_ref[0]
            v_ref[0] = vn_ref[0]

        @pl.when((flags & 8) == 0)
        def _():
            k_ref[0] = ck_ref[0, 0].reshape(kb, C_KV).astype(bf16)
            v_ref[0] = cv_ref[0, 0].reshape(kb, C_KV).astype(bf16)

    @pl.when((flags & 1) != 0)
    def _():
        qv = (q_ref[0] * (C_HEAD_DIM ** -0.5 * LOG2E)).astype(bf16)
        for g in range(C_KV_HEADS):
            for r in range(C_GROUP):
                hd = g * C_GROUP + r
                q_scr[g, r * qb:(r + 1) * qb, :] = qv[:, hd * C_HEAD_DIM:(hd + 1) * C_HEAD_DIM]
        m_scr[...] = jnp.full(m_scr.shape, NEG_BIG, f32)
        l_scr[...] = jnp.zeros_like(l_scr)
        acc_scr[...] = jnp.zeros_like(acc_scr)

    def visit(with_bias):
        kblk = k_ref[0]
        vblk = v_ref[0]
        mb_scr[...] = mb_ref[0].astype(f32)
        units = [(g, r) for g in range(C_KV_HEADS) for r in range(C_GROUP)]

        def qk(u):
            g, r = units[u]
            hs = slice(r * qb, (r + 1) * qb)
            s_scr[g, hs, :] = _dot_nt(q_scr[g, hs, :], kblk[:, g * C_HEAD_DIM:(g + 1) * C_HEAD_DIM])

        def soft(u):
            g, r = units[u]
            for c in range(qb // ATT_RC):
                q0 = c * ATT_RC
                rs = slice(r * qb + q0, r * qb + q0 + ATT_RC)
                s = s_scr[g, rs, :] + mb_scr[q0:q0 + ATT_RC, :]
                if with_bias:
                    s = s + bias_ref[0, g * C_GROUP + r, q0:q0 + ATT_RC, :]
                t = s[:, 0:LANES]
                for i in range(1, nt):
                    t = jnp.maximum(t, s[:, i * LANES:(i + 1) * LANES])
                m_old = m_scr[g, rs, :]
                m_new = jnp.maximum(m_old, jnp.max(t, axis=1, keepdims=True))
                alpha = jnp.exp2(m_old - m_new)
                p = jnp.exp2(s - jnp.concatenate([m_new] * nt, axis=1))
                psum = p[:, 0:LANES]
                for i in range(1, nt):
                    psum = psum + p[:, i * LANES:(i + 1) * LANES]
                l_scr[g, rs, :] = alpha * l_scr[g, rs, :] + psum
                m_scr[g, rs, :] = m_new
                a_scr[g, rs, :] = alpha
                p_scr[g, rs, :] = p.astype(bf16)

        def pv(u):
            g, r = units[u]
            hs = slice(r * qb, (r + 1) * qb)
            acc_scr[g, hs, :] = a_scr[g, hs, :] * acc_scr[g, hs, :] + _dot(
                p_scr[g, hs, :], vblk[:, g * C_HEAD_DIM:(g + 1) * C_HEAD_DIM])

        qk(0)
        for u in range(len(units)):
            if u + 1 < len(units):
                qk(u + 1)
            soft(u)
            pv(u)

    @pl.when((flags & 4) != 0)
    def _():
        visit(True)

    @pl.when((flags & 4) == 0)
    def _():
        visit(False)

    @pl.when((flags & 2) != 0)
    def _():
        for g in range(C_KV_HEADS):
            o = acc_scr[g] / jnp.sum(l_scr[g], axis=1, keepdims=True)
            for r in range(C_GROUP):
                hd = g * C_GROUP + r
                o_ref[0, :, hd * C_HEAD_DIM:(hd + 1) * C_HEAD_DIM] = o[r * qb:(r + 1) * qb, :].astype(bf16)


def _attn_schedule(nq, qb, qoff, offsets):
    qidx, kidx, flags, case = [], [], [], []
    zero_case = len(offsets)
    for j in range(nq):
        kmax = qoff + (j + 1) * qb
        nkb = -(-kmax // ATT_KB)
        for kb in range(nkb):
            off = kmax - BIAS_WIN - kb * ATT_KB
            qidx.append(j)
            kidx.append(kb)
            near = off in offsets
            flags.append((1 if kb == 0 else 0) | (2 if kb == nkb - 1 else 0) | (4 if near else 0)
                         | (8 if kb * ATT_KB >= qoff else 0))
            case.append(offsets.index(off) if near else zero_case)
            assert near == (-BIAS_WIN < off < ATT_KB)
    return [jnp.asarray(np.asarray(a, np.int32)) for a in (qidx, kidx, flags, case)]


def _dsa_attend(proj, k16, v16, maskbias, tables, *, qb, qoff, offsets, cache=None):
    b, t, _ = proj.shape
    nq = t // qb
    qidx, kidx, flags, case = _attn_schedule(nq, qb, qoff, offsets)
    ns = int(qidx.shape[0])
    rows = C_GROUP * qb
    kv_block = lambda f: pl.BlockSpec((1, ATT_KB, C_KV), f)
    if cache is None:
        kv_specs = [kv_block(lambda i, s, qi, ki, fl, ca: (i, ki[s], 0))] * 2
        kv_args = (k16, v16)
        kv_scratch = []
    else:
        layer = cache[2]
        ncache = cache[0].shape[2] // ATT_KB
        assert cache[0].shape[2] == qoff and qoff % ATT_KB == 0 and nq == 1 and k16.shape[1] == ATT_KB
        cache_block = pl.BlockSpec((1, 1, ATT_KB, C_KV_HEADS, C_HEAD_DIM),
                                   lambda i, s, qi, ki, fl, ca: (layer, i, jnp.minimum(ki[s], ncache - 1), 0, 0))
        kv_specs = [cache_block] * 2 + [kv_block(lambda i, s, qi, ki, fl, ca: (i, 0, 0))] * 2
        kv_args = (cache[0], cache[1], k16, v16)
        kv_scratch = [pltpu.VMEM((1, ATT_KB, C_KV), bf16)] * 2
    grid_spec = pltpu.PrefetchScalarGridSpec(
        num_scalar_prefetch=4,
        grid=(b, ns),
        in_specs=([pl.BlockSpec((1, qb, C_Q), lambda i, s, qi, ki, fl, ca: (i, qi[s], OFF_Q // C_Q))] + kv_specs
                  + [pl.BlockSpec((1, qb, ATT_KB), lambda i, s, qi, ki, fl, ca: (i, qi[s], ki[s])),
                     pl.BlockSpec((1, C_HEADS, qb, ATT_KB), lambda i, s, qi, ki, fl, ca: (ca[s], 0, 0, 0))]),
        out_specs=pl.BlockSpec((1, qb, C_WIDTH), lambda i, s, qi, ki, fl, ca: (i, qi[s], 0)),
        scratch_shapes=[pltpu.VMEM((C_KV_HEADS, rows, C_HEAD_DIM), bf16),
                        pltpu.VMEM((C_KV_HEADS, rows, LANES), f32),
                        pltpu.VMEM((C_KV_HEADS, rows, LANES), f32),
                        pltpu.VMEM((C_KV_HEADS, rows, C_HEAD_DIM), f32),
                        pltpu.VMEM((C_KV_HEADS, rows, ATT_KB), f32),
                        pltpu.VMEM((C_KV_HEADS, rows, ATT_KB), bf16),
                        pltpu.VMEM((C_KV_HEADS, rows, LANES), f32),
                        pltpu.VMEM((qb, ATT_KB), f32)] + kv_scratch)
    return pl.pallas_call(
        functools.partial(_attn_kernel, has_cache=cache is not None),
        out_shape=jax.ShapeDtypeStruct((b, t, C_WIDTH), bf16),
        grid_spec=grid_spec,
        compiler_params=_cp(("arbitrary", "arbitrary")),
        name="dsa_attend",
    )(qidx, kidx, flags, case, proj, *kv_args, maskbias, tables)


def _window_offsets(nq, qb, qoff):
    offs = set()
    for j in range(nq):
        kmax = qoff + (j + 1) * qb
        for kb in range(-(-kmax // ATT_KB)):
            off = kmax - BIAS_WIN - kb * ATT_KB
            if -BIAS_WIN < off < ATT_KB:
                offs.add(off)
    return sorted(offs)


def _mixer_c(proj, k_all16, v_all16, ki_all16, rel_bias, tables_memo, *, qb, qoff, true_len, cache=None):
    b, t, _ = proj.shape
    topk = min(TOPK_MAX, true_len // 4)
    assert BIAS_WIN - qb + 1 > MAX_DISTANCE
    offsets = _window_offsets(t // qb, qb, qoff)
    memo_key = (qb, qoff, t)
    if memo_key not in tables_memo:
        tables_memo[memo_key] = _bias_tables(rel_bias, qb, offsets)
    tables = tables_memo[memo_key]
    idx_qb = IDX_QB if t % IDX_QB == 0 else qb
    maskbias = _dsa_mask(proj, ki_all16, qb=idx_qb, qoff=qoff, topk=topk)
    return _dsa_attend(proj, k_all16, v_all16, maskbias, tables, qb=qb, qoff=qoff, offsets=offsets, cache=cache)


def _merge_kernel(x_ref, ada_ref, gt_ref, oa_ref, ob_ref, oc_ref, wa_ref, wb_ref, wc_ref, wo_ref, o_ref):
    bt, tt, d = x_ref.shape
    rows = bt * tt
    gates = jax.nn.sigmoid(gt_ref[...].reshape(rows, GATE_COLS))
    merged = (gates[:, 0:d] * _dot(oa_ref[...].reshape(rows, A_WIDTH), wa_ref[...])
              + gates[:, d:2 * d] * _dot(ob_ref[...].reshape(rows, SSM_INNER), wb_ref[...])
              + gates[:, 2 * d:3 * d] * _dot(oc_ref[...].reshape(rows, C_WIDTH), wc_ref[...]))
    mix = _dot(merged.astype(bf16), wo_ref[...]).reshape(bt, tt, d)
    o_ref[...] = x_ref[...] + ada_ref[...][:, 5:6, :] * mix


def _merge(x, ada, proj, o_a, o_b, o_c, wa16, wb16, wc16, wo16):
    b, t, d = x.shape
    bt, tt = _token_tile(b, t, rows=256)
    tok = lambda w: pl.BlockSpec((bt, tt, w), lambda i, j: (i, j, 0))
    full = lambda r, c: pl.BlockSpec((r, c), lambda i, j: (0, 0))
    return pl.pallas_call(
        _merge_kernel,
        out_shape=jax.ShapeDtypeStruct((b, t, d), f32),
        grid=(b // bt, t // tt),
        in_specs=[tok(d), pl.BlockSpec((bt, N_ADA, d), lambda i, j: (i, 0, 0)),
                  pl.BlockSpec((bt, tt, GATE_COLS), lambda i, j: (i, j, OFF_GATE // GATE_COLS)),
                  tok(A_WIDTH), tok(SSM_INNER), tok(C_WIDTH),
                  full(A_WIDTH, d), full(SSM_INNER, d), full(C_WIDTH, d), full(d, d)],
        out_specs=tok(d),
        compiler_params=_cp(("arbitrary", "arbitrary")),
        name="merge",
    )(x, ada, proj, o_a, o_b, o_c, wa16, wb16, wc16, wo16)


def _pad_keys(x, mult):
    pad = (-x.shape[1]) % mult
    if pad:
        x = jnp.concatenate([x, jnp.zeros((x.shape[0], pad) + x.shape[2:], x.dtype)], axis=1)
    return x


def _layer(x, ada, lp, rel_bias, tables_memo, past, g_final, last, layer):
    b, t, d = x.shape
    x = _ffn(x, ada, lp['g_ffn1'], lp['w_ffn1_in'], lp['w_ffn1_out'], g_final, k0=0, final_norm=False)
    proj, k16, v16, ki16, k_new, v_new, ki_new = _inproj(x, ada, lp['g_mix'], lp['w_proj'], k0=3)
    new_conv = proj[:, t - (CONV_W - 1):, OFF_XBC:OFF_XBC + CONV_DIM]

    o_a, va = _mixer_a(proj, lp['g_a'], lp['w_s'], lp['b_s'])

    if past is None:
        conv_init = jnp.zeros((b, CONV_W - 1, CONV_DIM), f32)
        h0 = jnp.zeros((b, SSM_HEADS, SSM_HEAD_DIM, SSM_STATE), f32)
    else:
        h0, conv_init = past[3], past[4]
    o_b, new_ssm = _mixer_b(proj, conv_init, h0, lp['w_conv'], lp['b_conv'], lp['dt_bias'], lp['a_log'],
                            lp['d_skip'], lp['g_ssm'])
    new_ssm = new_ssm.reshape(b, SSM_HEADS, SSM_HEAD_DIM, SSM_STATE)

    if past is None:
        k_all, v_all, ki_all, cache = k16, v16, ki16, None
        qb, qoff, true_len = (ATT_QB if t % ATT_QB == 0 else Q_BLOCK), 0, t
    else:
        plen = past[0].shape[2]
        if plen % ATT_KB == 0:
            k_all, v_all, cache = k16, v16, (past[0], past[1], layer)
        else:
            k_all = jnp.concatenate([past[0][layer].reshape(b, plen, C_KV).astype(bf16), k16], axis=1)
            v_all = jnp.concatenate([past[1][layer].reshape(b, plen, C_KV).astype(bf16), v16], axis=1)
            cache = None
        ki_all = jnp.concatenate([past[2].astype(bf16), ki16], axis=1)
        qb, qoff, true_len = t, plen, plen + t
    o_c = _mixer_c(proj, _pad_keys(k_all, ATT_KB), _pad_keys(v_all, ATT_KB), _pad_keys(ki_all, ATT_KB), rel_bias, tables_memo,
                   qb=qb, qoff=qoff, true_len=true_len, cache=cache)

    x = _merge(x, ada, proj, o_a, o_b, o_c, lp['w_br_a'], lp['w_br_b'], lp['w_br_c'], lp['w_out'])
    x = _ffn(x, ada, lp['g_ffn2'], lp['w_ffn2_in'], lp['w_ffn2_out'], g_final, k0=6, final_norm=last)
    states = (k_new, v_new, ki_new, new_ssm, new_conv, va)
    return x, states


def kernel(x_prompt, x_sample, c_prompt, c_sample, cache_k, cache_v, cache_kidx, state_ssm, state_conv, rel_bias, w_ada, b_ada, g_ffn1, w_ffn1_in, w_ffn1_out, g_mix, w_in, g_a, w_s, b_s, w_conv, b_conv, dt_bias, a_log, d_skip, g_ssm, w_br_a, w_br_b, w_br_c, w_out, g_ffn2, w_ffn2_in, w_ffn2_out, g_final):
    depth = w_ada.shape[0]
    bp, bs = c_prompt.shape[0], c_sample.shape[0]
    assert x_prompt.shape[1] % ATT_KB == 0 and x_prompt.shape[1] % Q_BLOCK == 0
    assert x_sample.shape[1] == CHUNK

    rows = -(-(bp + bs) // 8) * 8
    c_all = jnp.concatenate([c_prompt, c_sample, jnp.zeros((rows - bp - bs, D_MODEL), f32)], axis=0)
    ada_all = _ada_all(c_all, w_ada.astype(bf16), b_ada)
    ada_all = ada_all.reshape(depth, rows, N_ADA, D_MODEL)

    yp, ys = x_prompt, x_sample
    tables_memo = {}
    st_p, st_s = [], []
    for i in range(depth):
        lp = dict(g_ffn1=g_ffn1[i], w_ffn1_in=w_ffn1_in[i].astype(bf16), w_ffn1_out=w_ffn1_out[i].astype(bf16),
                  g_mix=g_mix[i], w_proj=_layout_w_in(w_in[i]).astype(bf16), g_a=g_a[i], w_s=w_s[i], b_s=b_s[i],
                  w_conv=w_conv[i], b_conv=b_conv[i], dt_bias=dt_bias[i], a_log=a_log[i], d_skip=d_skip[i],
                  g_ssm=g_ssm[i], w_br_a=w_br_a[i].astype(bf16), w_br_b=w_br_b[i].astype(bf16),
                  w_br_c=w_br_c[i].astype(bf16), w_out=w_out[i].astype(bf16), g_ffn2=g_ffn2[i],
                  w_ffn2_in=w_ffn2_in[i].astype(bf16), w_ffn2_out=w_ffn2_out[i].astype(bf16))
        last = i == depth - 1
        yp, sp = _layer(yp, ada_all[i, :bp], lp, rel_bias, tables_memo, None, g_final, last, i)
        ys, ss = _layer(ys, ada_all[i, bp:bp + bs], lp, rel_bias, tables_memo,
                        (cache_k, cache_v, cache_kidx[i], state_ssm[i], state_conv[i]), g_final, last, i)
        st_p.append(sp)
        st_s.append(ss)

    stack = lambda sts, k: jnp.stack([s[k] for s in sts])
    return (yp, ys,
            stack(st_p, 0), stack(st_p, 1), stack(st_p, 2), stack(st_p, 3), stack(st_p, 4),
            stack(st_s, 0), stack(st_s, 1), stack(st_s, 2), stack(st_s, 3), stack(st_s, 4), stack(st_s, 5))
```

```python
import functools
import math

import numpy as np
import jax
import jax.numpy as jnp
from jax import lax
from jax.experimental import pallas as pl
from jax.experimental.pallas import tpu as pltpu

f32 = jnp.float32
bf16 = jnp.bfloat16
i32 = jnp.int32

D_MODEL = 1024
CHUNK = 64
EPS = 1e-6
N_ADA = 9
D_FF = 2816
A_WIDTH = D_MODEL
A_GROUPS = 8
A_GROUP_DIM = A_WIDTH // A_GROUPS
A_CHUNK = 128
SSM_INNER = 2 * D_MODEL
SSM_HEAD_DIM = 64
SSM_HEADS = SSM_INNER // SSM_HEAD_DIM
SSM_GROUPS = 4
SSM_HPG = SSM_HEADS // SSM_GROUPS
SSM_STATE = 128
CONV_W = 4
CONV_DIM = SSM_INNER + 2 * SSM_GROUPS * SSM_STATE
C_HEADS = 8
C_KV_HEADS = 4
C_GROUP = C_HEADS // C_KV_HEADS
C_HEAD_DIM = 128
C_WIDTH = C_HEADS * C_HEAD_DIM
IDX_HEADS = 8
IDX_DIM = 64
TOPK_MAX = 256
Q_BLOCK = 128
N_BUCKETS = 32
MAX_DISTANCE = 128
A_COLS = 2 * A_WIDTH
C_Q = C_HEADS * C_HEAD_DIM
C_KV = C_KV_HEADS * C_HEAD_DIM
C_QI = IDX_HEADS * IDX_DIM
GATE_COLS = 3 * D_MODEL

LANES = 128
SUBLANES = 8
VMEM_LIMIT = 56 * 1024 * 1024

OFF_GATE = 0
OFF_XBC = OFF_GATE + GATE_COLS
OFF_PA = OFF_XBC + CONV_DIM
OFF_Z = OFF_PA + A_COLS
OFF_Q = OFF_Z + SSM_INNER
OFF_K = OFF_Q + C_Q
OFF_V = OFF_K + C_KV
OFF_QI = OFF_V + C_KV
OFF_SMALL = OFF_QI + C_QI
SMALL_WI = SSM_HEADS
SMALL_KI = LANES - IDX_DIM
PROJ_TILE = 1024
PROJ_W = ((OFF_SMALL + LANES + PROJ_TILE - 1) // PROJ_TILE) * PROJ_TILE

NEG_BIG = -1e30
SUBNORMAL_SPAN = 0x007FFFFF
NEG_INF_KEY = -2139095041 + SUBNORMAL_SPAN + 1
INT_MAX = 2147483647
ATT_KB = 1024
IDX_KB = 512
TOP_R = 16
IDX_QB = 256
ATT_RC = 32
LOG2E = math.log2(math.e)
ATT_QB = 256
BIAS_WIN = ATT_QB + MAX_DISTANCE
FAR_BUCKET = N_BUCKETS // 2 - 1


def _cp(sem, vmem=VMEM_LIMIT):
    return pltpu.CompilerParams(dimension_semantics=sem, vmem_limit_bytes=vmem)


def _rms(x, g):
    return x * lax.rsqrt(jnp.mean(x * x, axis=-1, keepdims=True) + EPS) * g


def _dot(a, b):
    return jnp.dot(a, b, preferred_element_type=f32)


def _dot_nt(a, b):
    return lax.dot_general(a, b, (((1,), (1,)), ((), ())), preferred_element_type=f32)


def _dot_tn(a, b):
    return lax.dot_general(a, b, (((0,), (0,)), ((), ())), preferred_element_type=f32)


def _bf16_pieces(x, pieces):
    out = []
    for _ in range(pieces):
        p = x.astype(bf16)
        out.append(p)
        x = x - p.astype(f32)
    return out


def _dot_zero_one_lhs(sel16, x, pieces):
    parts = [_dot(sel16, p) for p in _bf16_pieces(x, pieces)]
    return functools.reduce(lambda a, b: a + b, parts)


def _dot_zero_one_rhs(x, sel16, pieces):
    parts = [_dot(p, sel16) for p in _bf16_pieces(x, pieces)]
    return functools.reduce(lambda a, b: a + b, parts)


def _ada_kernel(c_ref, w_ref, b_ref, o_ref):
    s = jax.nn.silu(c_ref[...]).astype(bf16)
    o_ref[0] = _dot(s, w_ref[0]) + b_ref[0]


def _ada_all(c_all, w_ada16, b_ada):
    depth = w_ada16.shape[0]
    rows = c_all.shape[0]
    n_out = N_ADA * D_MODEL
    tn = 9 * LANES
    return pl.pallas_call(
        _ada_kernel,
        out_shape=jax.ShapeDtypeStruct((depth, rows, n_out), f32),
        grid=(depth, n_out // tn),
        in_specs=[pl.BlockSpec((rows, D_MODEL), lambda d, n: (0, 0)),
                  pl.BlockSpec((1, D_MODEL, tn), lambda d, n: (d, 0, n)),
                  pl.BlockSpec((1, 1, tn), lambda d, n: (d, 0, n))],
        out_specs=pl.BlockSpec((1, rows, tn), lambda d, n: (d, 0, n)),
        compiler_params=_cp(("arbitrary", "arbitrary")),
        name="ada",
    )(c_all, w_ada16, b_ada.reshape(depth, 1, n_out))


def _ffn_kernel(x_ref, ada_ref, g_ref, wg_ref, wu_ref, wo_ref, gf_ref, o_ref, h_scr, acc_scr, *, k0, final_norm):
    f = pl.program_id(2)
    bt, tt, d = x_ref.shape
    rows = bt * tt

    @pl.when(f == 0)
    def _():
        ada = ada_ref[...]
        h = _rms(x_ref[...], g_ref[...]) * (1.0 + ada[:, k0 + 1:k0 + 2, :]) + ada[:, k0:k0 + 1, :]
        h_scr[...] = h.reshape(rows, d).astype(bf16)
        acc_scr[...] = jnp.zeros_like(acc_scr)

    h = h_scr[...]
    gate = _dot(h, wg_ref[...])
    up = _dot(h, wu_ref[...])
    a = (jax.nn.silu(gate) * up).astype(bf16)
    acc_scr[...] += _dot(a, wo_ref[...])

    @pl.when(f == pl.num_programs(2) - 1)
    def _():
        ada = ada_ref[...]
        y = x_ref[...] + (0.5 * ada[:, k0 + 2:k0 + 3, :]) * acc_scr[...].reshape(bt, tt, d)
        if final_norm:
            y = _rms(y, gf_ref[...])
        o_ref[...] = y


def _token_tile(b, t, rows=512):
    if t >= rows:
        return 1, rows
    return min(b, rows // t), t


def _ffn(x, ada, g, w_in16, w_out16, g_final, *, k0, final_norm):
    b, t, d = x.shape
    bt, tt = _token_tile(b, t)
    tf = D_FF // 2
    nf = D_FF // tf
    kern = functools.partial(_ffn_kernel, k0=k0, final_norm=final_norm)
    return pl.pallas_call(
        kern,
        out_shape=jax.ShapeDtypeStruct((b, t, d), f32),
        grid=(b // bt, t // tt, nf),
        in_specs=[pl.BlockSpec((bt, tt, d), lambda i, j, f: (i, j, 0)),
                  pl.BlockSpec((bt, N_ADA, d), lambda i, j, f: (i, 0, 0)),
                  pl.BlockSpec((1, d), lambda i, j, f: (0, 0)),
                  pl.BlockSpec((d, tf), lambda i, j, f: (0, f)),
                  pl.BlockSpec((d, tf), lambda i, j, f: (0, f + nf)),
                  pl.BlockSpec((tf, d), lambda i, j, f: (f, 0)),
                  pl.BlockSpec((1, d), lambda i, j, f: (0, 0))],
        out_specs=pl.BlockSpec((bt, tt, d), lambda i, j, f: (i, j, 0)),
        scratch_shapes=[pltpu.VMEM((bt * tt, d), bf16), pltpu.VMEM((bt * tt, d), f32)],
        compiler_params=_cp(("arbitrary", "arbitrary", "arbitrary")),
        name="ffn",
    )(x, ada, g.reshape(1, d), w_in16, w_in16, w_out16, g_final.reshape(1, d))


def _inproj_kernel(x_ref, ada_ref, g_ref, w_ref, o_ref, k16_ref, v16_ref, ki16_ref, k32_ref, v32_ref, ki32_ref, h_scr, *, k0):
    n = pl.program_id(2)
    bt, tt, d = x_ref.shape

    @pl.when(n == 0)
    def _():
        ada = ada_ref[...]
        h = _rms(x_ref[...], g_ref[...]) * (1.0 + ada[:, k0 + 1:k0 + 2, :]) + ada[:, k0:k0 + 1, :]
        h_scr[...] = h.reshape(bt * tt, d).astype(bf16)

    res = _dot(h_scr[...], w_ref[...])
    o_ref[...] = res.reshape(bt, tt, PROJ_TILE)

    @pl.when(n == OFF_K // PROJ_TILE)
    def _():
        k = res[:, :C_KV]
        v = res[:, C_KV:2 * C_KV]
        k16_ref[...] = k.astype(bf16).reshape(bt, tt, C_KV)
        v16_ref[...] = v.astype(bf16).reshape(bt, tt, C_KV)
        k32_ref[...] = k.reshape(bt, tt, C_KV_HEADS, C_HEAD_DIM)
        v32_ref[...] = v.reshape(bt, tt, C_KV_HEADS, C_HEAD_DIM)

    @pl.when(n == OFF_SMALL // PROJ_TILE)
    def _():
        lo = OFF_SMALL % PROJ_TILE + SMALL_KI
        ki = res[:, lo:lo + IDX_DIM]
        ki16_ref[...] = ki.astype(bf16).reshape(bt, tt, IDX_DIM)
        ki32_ref[...] = ki.reshape(bt, tt, IDX_DIM)


def _inproj(x, ada, g, w_proj16, *, k0):
    b, t, d = x.shape
    bt, tt = _token_tile(b, t, rows=1024)
    assert OFF_K % PROJ_TILE == 0 and OFF_V == OFF_K + C_KV and 2 * C_KV <= PROJ_TILE
    tok = lambda w: pl.BlockSpec((bt, tt, w), lambda i, j, n: (i, j, 0))
    heads = pl.BlockSpec((bt, tt, C_KV_HEADS, C_HEAD_DIM), lambda i, j, n: (i, j, 0, 0))
    kv32 = jax.ShapeDtypeStruct((b, t, C_KV_HEADS, C_HEAD_DIM), f32)
    return pl.pallas_call(
        functools.partial(_inproj_kernel, k0=k0),
        out_shape=(jax.ShapeDtypeStruct((b, t, PROJ_W), f32), jax.ShapeDtypeStruct((b, t, C_KV), bf16),
                   jax.ShapeDtypeStruct((b, t, C_KV), bf16), jax.ShapeDtypeStruct((b, t, IDX_DIM), bf16),
                   kv32, kv32, jax.ShapeDtypeStruct((b, t, IDX_DIM), f32)),
        grid=(b // bt, t // tt, PROJ_W // PROJ_TILE),
        in_specs=[pl.BlockSpec((bt, tt, d), lambda i, j, n: (i, j, 0)),
                  pl.BlockSpec((bt, N_ADA, d), lambda i, j, n: (i, 0, 0)),
                  pl.BlockSpec((1, d), lambda i, j, n: (0, 0)),
                  pl.BlockSpec((d, PROJ_TILE), lambda i, j, n: (0, n))],
        out_specs=(pl.BlockSpec((bt, tt, PROJ_TILE), lambda i, j, n: (i, j, n)), tok(C_KV), tok(C_KV), tok(IDX_DIM),
                   heads, heads, tok(IDX_DIM)),
        scratch_shapes=[pltpu.VMEM((bt * tt, d), bf16)],
        compiler_params=_cp(("arbitrary", "arbitrary", "arbitrary")),
        name="inproj",
    )(x, ada, g.reshape(1, d), w_proj16)


def _layout_w_in(w_in):
    d = w_in.shape[0]
    o = 0
    pa = w_in[:, o:o + A_COLS]; o += A_COLS
    z = w_in[:, o:o + SSM_INNER]; o += SSM_INNER
    xbc = w_in[:, o:o + CONV_DIM]; o += CONV_DIM
    dt = w_in[:, o:o + SSM_HEADS]; o += SSM_HEADS
    q = w_in[:, o:o + C_Q]; o += C_Q
    k = w_in[:, o:o + C_KV]; o += C_KV
    v = w_in[:, o:o + C_KV]; o += C_KV
    qi = w_in[:, o:o + C_QI]; o += C_QI
    ki = w_in[:, o:o + IDX_DIM]; o += IDX_DIM
    wi = w_in[:, o:o + IDX_HEADS]; o += IDX_HEADS
    gates = w_in[:, o:o + GATE_COLS]
    small = jnp.concatenate([dt, wi, jnp.zeros((d, SMALL_KI - SMALL_WI - IDX_HEADS), w_in.dtype), ki], axis=1)
    pad = jnp.zeros((d, PROJ_W - OFF_SMALL - LANES), w_in.dtype)
    return jnp.concatenate([gates, xbc, pa, z, q, k, v, qi, small, pad], axis=1)


def _mixa_kernel(pa_ref, ga_ref, ws_ref, bs_ref, oa_ref, va_ref):
    n = pa_ref.shape[1]
    ge = jax.nn.gelu(pa_ref[0])
    u = ge[:, :A_WIDTH]
    v = _rms(ge[:, A_WIDTH:], ga_ref[...])
    va_ref[0] = v
    vb = v.astype(bf16)
    tri = lax.broadcasted_iota(i32, (n, n), 0) >= lax.broadcasted_iota(i32, (n, n), 1)
    bs = bs_ref[...]
    for g in range(A_GROUPS):
        sl = slice(g * A_GROUP_DIM, (g + 1) * A_GROUP_DIM)
        w = jnp.where(tri, ws_ref[g], 0.0).astype(bf16)
        mixed = _dot(w, vb[:, sl]) + bs[:, g:g + 1]
        oa_ref[0, :, sl] = (u[:, sl] * mixed).astype(bf16)


def _mixer_a(proj, g_a, w_s, b_s):
    b, t, _ = proj.shape
    n = min(t, A_CHUNK)
    ws = w_s[:, :n, :n]
    bs_t = b_s[:, :n].T
    return pl.pallas_call(
        _mixa_kernel,
        out_shape=(jax.ShapeDtypeStruct((b, t, A_WIDTH), bf16), jax.ShapeDtypeStruct((b, t, A_WIDTH), f32)),
        grid=(b, t // n),
        in_specs=[pl.BlockSpec((1, n, A_COLS), lambda i, j: (i, j, OFF_PA // A_COLS)),
                  pl.BlockSpec((1, A_WIDTH), lambda i, j: (0, 0)),
                  pl.BlockSpec((A_GROUPS, n, n), lambda i, j: (0, 0, 0)),
                  pl.BlockSpec((n, A_GROUPS), lambda i, j: (0, 0))],
        out_specs=(pl.BlockSpec((1, n, A_WIDTH), lambda i, j: (i, j, 0)),
                   pl.BlockSpec((1, n, A_WIDTH), lambda i, j: (i, j, 0))),
        compiler_params=_cp(("arbitrary", "arbitrary")),
        name="mixer_a",
    )(proj, g_a.reshape(1, A_WIDTH), ws, bs_t)


CONV_PAD = 8


def _ssd_kernel(z_ref, xbc_ref, sm_ref, cinit_ref, h0_ref, wc_ref, bc_ref, dtb_ref, alog_ref, dsk_ref, gs_ref,
                ex_ref, ob_ref, hs_ref, xpad, ht):
    c = pl.program_id(1)
    q = z_ref.shape[1]
    gw = SSM_HPG * SSM_HEAD_DIM
    tail = CONV_W - 1

    @pl.when(c == 0)
    def _():
        xpad[CONV_PAD - tail:CONV_PAD, :] = cinit_ref[0]
        for g in range(SSM_GROUPS):
            ht[g] = h0_ref[0, g * gw:(g + 1) * gw, :].T

    xpad[CONV_PAD:CONV_PAD + q, :] = xbc_ref[0]
    wc = wc_ref[...]
    conv = bc_ref[...] + wc[0:1, :] * xpad[pl.ds(CONV_PAD - 3, q), :]
    for k in range(1, CONV_W):
        conv = conv + wc[k:k + 1, :] * xpad[pl.ds(CONV_PAD - 3 + k, q), :]
    nxt = xpad[CONV_PAD + q - tail:CONV_PAD + q, :]
    xpad[CONV_PAD - tail:CONV_PAD, :] = nxt
    conv = jax.nn.silu(conv)
    xs = conv[:, :SSM_INNER]
    bm = conv[:, SSM_INNER:SSM_INNER + SSM_GROUPS * SSM_STATE]
    cm = conv[:, SSM_INNER + SSM_GROUPS * SSM_STATE:]

    lane = lax.broadcasted_iota(i32, (q, LANES), 1)
    head_lane = lane < SSM_HEADS
    dt = jnp.where(head_lane, jax.nn.softplus(sm_ref[0] + dtb_ref[...]), 0.0)
    da = dt * (-jnp.exp(alog_ref[...]))
    tri = lax.broadcasted_iota(i32, (q, q), 0) >= lax.broadcasted_iota(i32, (q, q), 1)
    acs = _dot_zero_one_lhs(tri.astype(bf16), da, 3)
    acs_t = acs.T
    dt_t = dt.T
    acs_last = acs[q - 1:q, :]
    e_acs = jnp.where(head_lane, jnp.exp(acs), 0.0)
    decay = jnp.where(head_lane, jnp.exp(acs_last - acs) * dt, 0.0)
    ex = ex_ref[...]
    e_acs_x = _dot_zero_one_rhs(e_acs, ex, 3)
    decay_x = _dot_zero_one_rhs(decay, ex, 3)
    e_last_x = e_acs_x[q - 1:q, :]

    xs16 = xs.astype(bf16)
    xdec16 = (xs * decay_x).astype(bf16)
    y_parts = []
    for g in range(SSM_GROUPS):
        bg = bm[:, g * SSM_STATE:(g + 1) * SSM_STATE].astype(bf16)
        cg = cm[:, g * SSM_STATE:(g + 1) * SSM_STATE].astype(bf16)
        cb = _dot_nt(cg, bg)
        htg = ht[g]
        y_inter = _dot(cg, htg.astype(bf16))
        intra = []
        for r in range(SSM_HPG):
            hd = g * SSM_HPG + r
            seg = acs[:, hd:hd + 1] - acs_t[hd:hd + 1, :]
            m = cb * jnp.exp(jnp.where(tri, seg, -jnp.inf)) * dt_t[hd:hd + 1, :]
            intra.append(_dot(m.astype(bf16), xs16[:, hd * SSM_HEAD_DIM:(hd + 1) * SSM_HEAD_DIM]))
        y_parts.append(jnp.concatenate(intra, axis=1) + y_inter * e_acs_x[:, g * gw:(g + 1) * gw])
        ht[g] = htg * e_last_x[:, g * gw:(g + 1) * gw] + _dot_tn(bg, xdec16[:, g * gw:(g + 1) * gw])
    y = jnp.concatenate(y_parts, axis=1) + dsk_ref[...] * xs
    y = _rms(y * jax.nn.silu(z_ref[0]), gs_ref[...])
    ob_ref[0] = y.astype(bf16)

    @pl.when(c == pl.num_programs(1) - 1)
    def _():
        for g in range(SSM_GROUPS):
            hs_ref[0, g * gw:(g + 1) * gw, :] = ht[g].T


def _mixer_b(proj, conv_init, h0, w_conv, b_conv, dt_bias, a_log, d_skip, g_ssm):
    b, t, _ = proj.shape
    q = CHUNK
    pad_row = lambda v: jnp.zeros((1, LANES), f32).at[0, :SSM_HEADS].set(v)
    expand = (jnp.arange(LANES)[:, None] == (jnp.arange(SSM_INNER) // SSM_HEAD_DIM)[None, :]).astype(bf16)
    hrows = SSM_HEADS * SSM_HEAD_DIM
    return pl.pallas_call(
        _ssd_kernel,
        out_shape=(jax.ShapeDtypeStruct((b, t, SSM_INNER), bf16),
                   jax.ShapeDtypeStruct((b, hrows, SSM_STATE), f32)),
        grid=(b, t // q),
        in_specs=[pl.BlockSpec((1, q, SSM_INNER), lambda i, c: (i, c, OFF_Z // SSM_INNER)),
                  pl.BlockSpec((1, q, CONV_DIM), lambda i, c: (i, c, OFF_XBC // CONV_DIM)),
                  pl.BlockSpec((1, q, LANES), lambda i, c: (i, c, OFF_SMALL // LANES)),
                  pl.BlockSpec((1, CONV_W - 1, CONV_DIM), lambda i, c: (i, 0, 0)),
                  pl.BlockSpec((1, hrows, SSM_STATE), lambda i, c: (i, 0, 0)),
                  pl.BlockSpec((CONV_W, CONV_DIM), lambda i, c: (0, 0)),
                  pl.BlockSpec((1, CONV_DIM), lambda i, c: (0, 0)),
                  pl.BlockSpec((1, LANES), lambda i, c: (0, 0)),
                  pl.BlockSpec((1, LANES), lambda i, c: (0, 0)),
                  pl.BlockSpec((1, SSM_INNER), lambda i, c: (0, 0)),
                  pl.BlockSpec((1, SSM_INNER), lambda i, c: (0, 0)),
                  pl.BlockSpec((LANES, SSM_INNER), lambda i, c: (0, 0))],
        out_specs=(pl.BlockSpec((1, q, SSM_INNER), lambda i, c: (i, c, 0)),
                   pl.BlockSpec((1, hrows, SSM_STATE), lambda i, c: (i, 0, 0))),
        scratch_shapes=[pltpu.VMEM((CONV_PAD + q, CONV_DIM), f32),
                        pltpu.VMEM((SSM_GROUPS, SSM_STATE, SSM_HPG * SSM_HEAD_DIM), f32)],
        compiler_params=_cp(("arbitrary", "arbitrary")),
        name="mixer_b",
    )(proj, proj, proj, conv_init, h0.reshape(b, hrows, SSM_STATE), w_conv, b_conv.reshape(1, CONV_DIM),
      pad_row(dt_bias), pad_row(a_log), jnp.repeat(d_skip, SSM_HEAD_DIM).reshape(1, SSM_INNER),
      g_ssm.reshape(1, SSM_INNER), expand)


def _t5_bucket(rel):
    nb = N_BUCKETS // 2
    max_exact = nb // 2
    n = jnp.abs(rel)
    nf = jnp.maximum(n, 1).astype(jnp.float32)
    large = max_exact + (jnp.log(nf / max_exact) / math.log(MAX_DISTANCE / max_exact)
                         * (nb - max_exact)).astype(jnp.int32)
    large = jnp.minimum(large, nb - 1)
    return jnp.where(rel > 0, nb, 0) + jnp.where(n < max_exact, n, large)


def _bias_kernel(tab_ref, bk_ref, o_ref):
    h = pl.program_id(1)
    bk = bk_ref[0]
    far = tab_ref[FAR_BUCKET, h]
    acc = jnp.zeros(bk.shape, f32)
    for bucket in range(N_BUCKETS):
        acc = jnp.where(bk == bucket, tab_ref[bucket, h] - far, acc)
    o_ref[0, 0] = acc * LOG2E


def _bias_tables(rel_bias, qb, offsets):
    t = np.arange(qb)[:, None]
    cases = []
    for off in offsets:
        w = np.arange(ATT_KB)[None, :] - off
        valid = (w >= 0) & (w < BIAS_WIN)
        rel = w - BIAS_WIN + qb - t
        cases.append(jnp.where(jnp.asarray(valid), _t5_bucket(jnp.asarray(rel, dtype=jnp.int32)), -1))
    cases.append(jnp.full((qb, ATT_KB), -1, jnp.int32))
    buckets = jnp.stack(cases).astype(i32)
    nc = buckets.shape[0]
    return pl.pallas_call(
        _bias_kernel,
        out_shape=jax.ShapeDtypeStruct((nc, C_HEADS, qb, ATT_KB), f32),
        grid=(nc, C_HEADS),
        in_specs=[pl.BlockSpec(memory_space=pltpu.SMEM),
                  pl.BlockSpec((1, qb, ATT_KB), lambda c, h: (c, 0, 0))],
        out_specs=pl.BlockSpec((1, 1, qb, ATT_KB), lambda c, h: (c, h, 0, 0)),
        compiler_params=_cp(("arbitrary", "arbitrary")),
        name="t5_bias",
    )(rel_bias, buckets)


def _oddeven_merge_sort_pairs(n):
    pairs = []
    p = 1
    while p < n:
        k = p
        while k >= 1:
            for j in range(k % p, n - k, 2 * k):
                for i in range(min(k, n - j - k)):
                    if (i + j) // (2 * p) == (i + j + k) // (2 * p):
                        pairs.append((i + j, i + j + k))
            k //= 2
        p *= 2
    return pairs


def _compare_exchange(v, i, j):
    a, b = v[i], v[j]
    v[i] = jnp.maximum(a, b)
    v[j] = jnp.minimum(a, b)


def _sort_desc(v):
    for i, j in _oddeven_merge_sort_pairs(len(v)):
        _compare_exchange(v, i, j)


def _bitonic_merge_desc(v):
    d = len(v) // 2
    while d >= 1:
        for i in range(len(v)):
            if (i & d) == 0:
                _compare_exchange(v, i, i + d)
        d //= 2


def _index_kernel(qi_ref, sm_ref, ki_ref, o_ref, keys, top_scr, *, qoff, topk, idx_bits):
    j = pl.program_id(1)
    qb = qi_ref.shape[1]
    ltot = ki_ref.shape[1]
    kb = IDX_KB
    kmax = qoff + (j + 1) * qb
    nkb = lax.div(kmax + (kb - 1), kb)
    sub = kb // LANES

    qv = qi_ref[0].astype(bf16)
    stacked = jnp.concatenate([qv[:, h * IDX_DIM:(h + 1) * IDX_DIM] for h in range(IDX_HEADS)], axis=0)
    w = sm_ref[0][:, SMALL_WI:SMALL_WI + IDX_HEADS] * (IDX_HEADS ** -0.5) * (IDX_DIM ** -0.5)
    wb = [jnp.broadcast_to(w[:, h:h + 1], (qb, kb)) for h in range(IDX_HEADS)]
    qchunk = lax.shift_right_arithmetic(qoff + j * qb + lax.broadcasted_iota(i32, (qb, 1), 0), 6)
    lane_pos = lax.broadcasted_iota(i32, (1, kb), 1)

    def score_body(i, carry):
        off = pl.multiple_of(i * kb, kb)
        kblk = ki_ref[0, pl.ds(off, kb), :]
        acc = None
        for h in range(IDX_HEADS):
            term = jnp.maximum(_dot_nt(stacked[h * qb:(h + 1) * qb], kblk), 0.0) * wb[h]
            acc = term if acc is None else acc + term
        adm = lax.shift_right_arithmetic(off + lane_pos, 6) <= qchunk
        keys[:, pl.ds(off, kb)] = jnp.where(adm, jnp.where(acc == 0.0, 0.0, acc), -jnp.inf)
        return carry

    lax.fori_loop(0, nkb, score_body, 0)

    group = TOP_R * LANES
    ngrp = lax.div(kmax + (group - 1), group)

    def pad_body(i, carry):
        keys[:, pl.ds(pl.multiple_of(i * kb, kb), kb)] = jnp.full((qb, kb), -jnp.inf, f32)
        return carry

    lax.fori_loop(nkb, ngrp * (group // kb), pad_body, 0)
    top_scr[...] = jnp.full(top_scr.shape, -jnp.inf, f32)

    def top_body(gidx, carry):
        base = gidx * group
        for s in range(qb // SUBLANES):
            rs = slice(s * SUBLANES, (s + 1) * SUBLANES)
            v = [keys[rs, pl.ds(pl.multiple_of(base + i * LANES, LANES), LANES)] for i in range(TOP_R)]
            _sort_desc(v)
            merged = [jnp.maximum(top_scr[i, rs, :], v[TOP_R - 1 - i]) for i in range(TOP_R)]
            _bitonic_merge_desc(merged)
            for i in range(TOP_R):
                top_scr[i, rs, :] = merged[i]
        return carry

    lax.fori_loop(0, ngrp, top_body, 0)

    def flip(bits):
        return bits ^ (lax.shift_right_arithmetic(bits, 31) & INT_MAX)

    def to_key(x):
        k = flip(lax.bitcast_convert_type(x, i32))
        return jnp.where(k > 0, k - SUBNORMAL_SPAN, jnp.where(k < 0, k + SUBNORMAL_SPAN + 1, 0))

    def to_score(key):
        k = jnp.where(key > 0, key + SUBNORMAL_SPAN, jnp.where(key < 0, key - SUBNORMAL_SPAN - 1, 0))
        return lax.bitcast_convert_type(flip(k), f32)

    def count_top(thr):
        tb = jnp.broadcast_to(to_score(thr), (qb, LANES))
        acc = (top_scr[0] >= tb).astype(i32)
        for i in range(1, TOP_R):
            acc = acc + (top_scr[i] >= tb).astype(i32)
        return jnp.sum(acc, axis=1, keepdims=True)

    def count(pred):
        def body(i, acc):
            off = pl.multiple_of(i * kb, kb)
            blk = keys[:, pl.ds(off, kb)]
            m = pred(blk, off).astype(i32)
            for c in range(sub):
                acc = acc + m[:, c * LANES:(c + 1) * LANES]
            return acc
        acc = lax.fori_loop(0, nkb, body, jnp.zeros((qb, LANES), i32))
        return jnp.sum(acc, axis=1, keepdims=True)

    def count_ge(thr):
        tb = jnp.broadcast_to(to_score(thr), (qb, kb))
        return count(lambda blk, off: blk >= tb)

    assert topk <= 2 * LANES and TOP_R >= 2
    hi0 = to_key(jnp.max(top_scr[1 if topk > LANES else 0], axis=1, keepdims=True)) + 1
    lo0 = jnp.minimum(to_key(jnp.min(top_scr[1], axis=1, keepdims=True)), hi0 - 1)

    def search(count_fn, lo_start):
        def cond(c):
            it, open_rows, _, _ = c
            return (open_rows > 0) & (it < 40)

        def body(c):
            it, _, lo, hi = c
            mid = lax.shift_right_arithmetic(lo, 1) + lax.shift_right_arithmetic(hi, 1) + (lo & hi & 1)
            n = count_fn(mid)
            ge = n >= topk
            lo_n = jnp.where(ge, mid, lo)
            hi_n = jnp.where(n == topk, mid + 1, jnp.where(ge, hi, mid))
            open_rows = jnp.max((lo_n < hi_n - 1).astype(i32))
            return it + 1, open_rows, lo_n, hi_n

        open0 = jnp.max((lo_start < hi0 - 1).astype(i32))
        return lax.while_loop(cond, body, (jnp.int32(0), open0, lo_start, hi0))[2]

    thr_c = search(count_top, lo0)
    n_ge_c = count_top(thr_c)
    smallest = to_key(jnp.max(top_scr[TOP_R - 1], axis=1, keepdims=True))
    bad = (smallest >= thr_c) & (thr_c > NEG_INF_KEY)

    def full_search(_):
        t = search(count_ge, lo0)
        return t, count_ge(t)

    thr, n_ge = lax.cond(jnp.max(bad.astype(i32)) > 0, full_search, lambda _: (thr_c, n_ge_c), 0)
    tie = (n_ge > topk) & (thr > NEG_INF_KEY)

    def tie_cut(_):
        tb = jnp.broadcast_to(to_score(thr), (qb, kb))
        need = topk - count_ge(thr + 1)

        def body(_, c):
            lo, hi = c
            mid = lax.shift_right_arithmetic(lo + hi, 1)
            mb = jnp.broadcast_to(mid, (qb, kb))
            n = count(lambda blk, off: (blk == tb) & ((off + lane_pos) <= mb))
            ok = n >= need
            return jnp.where(ok, lo, mid), jnp.where(ok, mid, hi)

        lo_i = jnp.full((qb, 1), -1, i32)
        hi_i = jnp.full((qb, 1), 1, i32) * (nkb * kb - 1)
        _, cut = lax.fori_loop(0, idx_bits, body, (lo_i, hi_i))
        return jnp.where(tie, cut, INT_MAX)

    any_tie = jnp.max(tie.astype(i32)) > 0
    cut = lax.cond(any_tie, tie_cut, lambda _: jnp.full((qb, 1), INT_MAX, i32), 0)

    tb = jnp.broadcast_to(to_score(thr), (qb, kb))
    cb = jnp.broadcast_to(cut, (qb, kb))

    def out_body(i, carry):
        off = pl.multiple_of(i * kb, kb)
        blk = keys[:, pl.ds(off, kb)]
        sel = ((blk > tb) | ((blk == tb) & ((off + lane_pos) <= cb))) & (blk > -jnp.inf)
        o_ref[0, :, pl.ds(off, kb)] = jnp.where(sel, 0.0, NEG_BIG).astype(bf16)
        return carry

    lax.fori_loop(0, nkb, out_body, 0)

    def fill_body(i, carry):
        off = pl.multiple_of(i * kb, kb)
        o_ref[0, :, pl.ds(off, kb)] = jnp.full((qb, kb), NEG_BIG, bf16)
        return carry

    lax.fori_loop(nkb, ltot // kb, fill_body, 0)


def _dsa_mask(proj, ki16, *, qb, qoff, topk):
    b, t, _ = proj.shape
    ltot = ki16.shape[1]
    idx_bits = int(math.ceil(math.log2(ltot))) + 1
    kern = functools.partial(_index_kernel, qoff=qoff, topk=topk, idx_bits=idx_bits)
    return pl.pallas_call(
        kern,
        out_shape=jax.ShapeDtypeStruct((b, t, ltot), bf16),
        grid=(b, t // qb),
        in_specs=[pl.BlockSpec((1, qb, C_QI), lambda i, j: (i, j, OFF_QI // C_QI)),
                  pl.BlockSpec((1, qb, LANES), lambda i, j: (i, j, OFF_SMALL // LANES)),
                  pl.BlockSpec((1, ltot, IDX_DIM), lambda i, j: (i, 0, 0))],
        out_specs=pl.BlockSpec((1, qb, ltot), lambda i, j: (i, j, 0)),
        scratch_shapes=[pltpu.VMEM((qb, -(-ltot // (TOP_R * LANES)) * TOP_R * LANES), f32),
                        pltpu.VMEM((TOP_R, qb, LANES), f32)],
        compiler_params=_cp(("arbitrary", "arbitrary")),
        name="dsa_index",
    )(proj, proj, ki16)


def _attn_kernel(qidx_ref, kidx_ref, flag_ref, case_ref, q_ref, *refs, has_cache):
    if has_cache:
        (ck_ref, cv_ref, kn_ref, vn_ref, mb_ref, bias_ref, o_ref,
         q_scr, m_scr, l_scr, acc_scr, s_scr, p_scr, a_scr, mb_scr, k_ref, v_ref) = refs
    else:
        (k_ref, v_ref, mb_ref, bias_ref, o_ref,
         q_scr, m_scr, l_scr, acc_scr, s_scr, p_scr, a_scr, mb_scr) = refs
    s_id = pl.program_id(1)
    qb = q_ref.shape[1]
    kb = k_ref.shape[1]
    rows = C_GROUP * qb
    nt = kb // LANES
    flags = flag_ref[s_id]

    if has_cache:
        @pl.when((flags & 8) != 0)
        def _():
            k_ref[0] = kn_ref[0]
            v_ref[0] = vn_ref[0]

        @pl.when((flags & 8) == 0)
        def _():
            k_ref[0] = ck_ref[0, 0].reshape(kb, C_KV).astype(bf16)
            v_ref[0] = cv_ref[0, 0].reshape(kb, C_KV).astype(bf16)

    @pl.when((flags & 1) != 0)
    def _():
        qv = (q_ref[0] * (C_HEAD_DIM ** -0.5 * LOG2E)).astype(bf16)
        for g in range(C_KV_HEADS):
            for r in range(C_GROUP):
                hd = g * C_GROUP + r
                q_scr[g, r * qb:(r + 1) * qb, :] = qv[:, hd * C_HEAD_DIM:(hd + 1) * C_HEAD_DIM]
        m_scr[...] = jnp.full(m_scr.shape, NEG_BIG, f32)
        l_scr[...] = jnp.zeros_like(l_scr)
        acc_scr[...] = jnp.zeros_like(acc_scr)

    def visit(with_bias):
        kblk = k_ref[0]
        vblk = v_ref[0]
        mb_scr[...] = mb_ref[0].astype(f32)
        units = [(g, r) for g in range(C_KV_HEADS) for r in range(C_GROUP)]

        def qk(u):
            g, r = units[u]
            hs = slice(r * qb, (r + 1) * qb)
            s_scr[g, hs, :] = _dot_nt(q_scr[g, hs, :], kblk[:, g * C_HEAD_DIM:(g + 1) * C_HEAD_DIM])

        def soft(u):
            g, r = units[u]
            for c in range(qb // ATT_RC):
                q0 = c * ATT_RC
                rs = slice(r * qb + q0, r * qb + q0 + ATT_RC)
                s = s_scr[g, rs, :] + mb_scr[q0:q0 + ATT_RC, :]
                if with_bias:
                    s = s + bias_ref[0, g * C_GROUP + r, q0:q0 + ATT_RC, :]
                t = s[:, 0:LANES]
                for i in range(1, nt):
                    t = jnp.maximum(t, s[:, i * LANES:(i + 1) * LANES])
                m_old = m_scr[g, rs, :]
                m_new = jnp.maximum(m_old, jnp.max(t, axis=1, keepdims=True))
                alpha = jnp.exp2(m_old - m_new)
                p = jnp.exp2(s - jnp.concatenate([m_new] * nt, axis=1))
                psum = p[:, 0:LANES]
                for i in range(1, nt):
                    psum = psum + p[:, i * LANES:(i + 1) * LANES]
                l_scr[g, rs, :] = alpha * l_scr[g, rs, :] + psum
                m_scr[g, rs, :] = m_new
                a_scr[g, rs, :] = alpha
                p_scr[g, rs, :] = p.astype(bf16)

        def pv(u):
            g, r = units[u]
            hs = slice(r * qb, (r + 1) * qb)
            acc_scr[g, hs, :] = a_scr[g, hs, :] * acc_scr[g, hs, :] + _dot(
                p_scr[g, hs, :], vblk[:, g * C_HEAD_DIM:(g + 1) * C_HEAD_DIM])

        qk(0)
        for u in range(len(units)):
            if u + 1 < len(units):
                qk(u + 1)
            soft(u)
            pv(u)

    @pl.when((flags & 4) != 0)
    def _():
        visit(True)

    @pl.when((flags & 4) == 0)
    def _():
        visit(False)

    @pl.when((flags & 2) != 0)
    def _():
        for g in range(C_KV_HEADS):
            o = acc_scr[g] / jnp.sum(l_scr[g], axis=1, keepdims=True)
            for r in range(C_GROUP):
                hd = g * C_GROUP + r
                o_ref[0, :, hd * C_HEAD_DIM:(hd + 1) * C_HEAD_DIM] = o[r * qb:(r + 1) * qb, :].astype(bf16)


def _attn_schedule(nq, qb, qoff, offsets):
    qidx, kidx, flags, case = [], [], [], []
    zero_case = len(offsets)
    for j in range(nq):
        kmax = qoff + (j + 1) * qb
        nkb = -(-kmax // ATT_KB)
        for kb in range(nkb):
            off = kmax - BIAS_WIN - kb * ATT_KB
            qidx.append(j)
            kidx.append(kb)
            near = off in offsets
            flags.append((1 if kb == 0 else 0) | (2 if kb == nkb - 1 else 0) | (4 if near else 0)
                         | (8 if kb * ATT_KB >= qoff else 0))
            case.append(offsets.index(off) if near else zero_case)
            assert near == (-BIAS_WIN < off < ATT_KB)
    return [jnp.asarray(np.asarray(a, np.int32)) for a in (qidx, kidx, flags, case)]


def _dsa_attend(proj, k16, v16, maskbias, tables, *, qb, qoff, offsets, cache=None):
    b, t, _ = proj.shape
    nq = t // qb
    qidx, kidx, flags, case = _attn_schedule(nq, qb, qoff, offsets)
    ns = int(qidx.shape[0])
    rows = C_GROUP * qb
    kv_block = lambda f: pl.BlockSpec((1, ATT_KB, C_KV), f)
    if cache is None:
        kv_specs = [kv_block(lambda i, s, qi, ki, fl, ca: (i, ki[s], 0))] * 2
        kv_args = (k16, v16)
        kv_scratch = []
    else:
        layer = cache[2]
        ncache = cache[0].shape[2] // ATT_KB
        assert cache[0].shape[2] == qoff and qoff % ATT_KB == 0 and nq == 1 and k16.shape[1] == ATT_KB
        cache_block = pl.BlockSpec((1, 1, ATT_KB, C_KV_HEADS, C_HEAD_DIM),
                                   lambda i, s, qi, ki, fl, ca: (layer, i, jnp.minimum(ki[s], ncache - 1), 0, 0))
        kv_specs = [cache_block] * 2 + [kv_block(lambda i, s, qi, ki, fl, ca: (i, 0, 0))] * 2
        kv_args = (cache[0], cache[1], k16, v16)
        kv_scratch = [pltpu.VMEM((1, ATT_KB, C_KV), bf16)] * 2
    grid_spec = pltpu.PrefetchScalarGridSpec(
        num_scalar_prefetch=4,
        grid=(b, ns),
        in_specs=([pl.BlockSpec((1, qb, C_Q), lambda i, s, qi, ki, fl, ca: (i, qi[s], OFF_Q // C_Q))] + kv_specs
                  + [pl.BlockSpec((1, qb, ATT_KB), lambda i, s, qi, ki, fl, ca: (i, qi[s], ki[s])),
                     pl.BlockSpec((1, C_HEADS, qb, ATT_KB), lambda i, s, qi, ki, fl, ca: (ca[s], 0, 0, 0))]),
        out_specs=pl.BlockSpec((1, qb, C_WIDTH), lambda i, s, qi, ki, fl, ca: (i, qi[s], 0)),
        scratch_shapes=[pltpu.VMEM((C_KV_HEADS, rows, C_HEAD_DIM), bf16),
                        pltpu.VMEM((C_KV_HEADS, rows, LANES), f32),
                        pltpu.VMEM((C_KV_HEADS, rows, LANES), f32),
                        pltpu.VMEM((C_KV_HEADS, rows, C_HEAD_DIM), f32),
                        pltpu.VMEM((C_KV_HEADS, rows, ATT_KB), f32),
                        pltpu.VMEM((C_KV_HEADS, rows, ATT_KB), bf16),
                        pltpu.VMEM((C_KV_HEADS, rows, LANES), f32),
                        pltpu.VMEM((qb, ATT_KB), f32)] + kv_scratch)
    return pl.pallas_call(
        functools.partial(_attn_kernel, has_cache=cache is not None),
        out_shape=jax.ShapeDtypeStruct((b, t, C_WIDTH), bf16),
        grid_spec=grid_spec,
        compiler_params=_cp(("arbitrary", "arbitrary")),
        name="dsa_attend",
    )(qidx, kidx, flags, case, proj, *kv_args, maskbias, tables)


def _window_offsets(nq, qb, qoff):
    offs = set()
    for j in range(nq):
        kmax = qoff + (j + 1) * qb
        for kb in range(-(-kmax // ATT_KB)):
            off = kmax - BIAS_WIN - kb * ATT_KB
            if -BIAS_WIN < off < ATT_KB:
                offs.add(off)
    return sorted(offs)


def _mixer_c(proj, k_all16, v_all16, ki_all16, rel_bias, tables_memo, *, qb, qoff, true_len, cache=None):
    b, t, _ = proj.shape
    topk = min(TOPK_MAX, true_len // 4)
    assert BIAS_WIN - qb + 1 > MAX_DISTANCE
    offsets = _window_offsets(t // qb, qb, qoff)
    memo_key = (qb, qoff, t)
    if memo_key not in tables_memo:
        tables_memo[memo_key] = _bias_tables(rel_bias, qb, offsets)
    tables = tables_memo[memo_key]
    idx_qb = IDX_QB if t % IDX_QB == 0 else qb
    maskbias = _dsa_mask(proj, ki_all16, qb=idx_qb, qoff=qoff, topk=topk)
    return _dsa_attend(proj, k_all16, v_all16, maskbias, tables, qb=qb, qoff=qoff, offsets=offsets, cache=cache)


def _merge_kernel(x_ref, ada_ref, gt_ref, oa_ref, ob_ref, oc_ref, wa_ref, wb_ref, wc_ref, wo_ref, o_ref):
    bt, tt, d = x_ref.shape
    rows = bt * tt
    gates = jax.nn.sigmoid(gt_ref[...].reshape(rows, GATE_COLS))
    merged = (gates[:, 0:d] * _dot(oa_ref[...].reshape(rows, A_WIDTH), wa_ref[...])
              + gates[:, d:2 * d] * _dot(ob_ref[...].reshape(rows, SSM_INNER), wb_ref[...])
              + gates[:, 2 * d:3 * d] * _dot(oc_ref[...].reshape(rows, C_WIDTH), wc_ref[...]))
    mix = _dot(merged.astype(bf16), wo_ref[...]).reshape(bt, tt, d)
    o_ref[...] = x_ref[...] + ada_ref[...][:, 5:6, :] * mix


def _merge(x, ada, proj, o_a, o_b, o_c, wa16, wb16, wc16, wo16):
    b, t, d = x.shape
    bt, tt = _token_tile(b, t, rows=256)
    tok = lambda w: pl.BlockSpec((bt, tt, w), lambda i, j: (i, j, 0))
    full = lambda r, c: pl.BlockSpec((r, c), lambda i, j: (0, 0))
    return pl.pallas_call(
        _merge_kernel,
        out_shape=jax.ShapeDtypeStruct((b, t, d), f32),
        grid=(b // bt, t // tt),
        in_specs=[tok(d), pl.BlockSpec((bt, N_ADA, d), lambda i, j: (i, 0, 0)),
                  pl.BlockSpec((bt, tt, GATE_COLS), lambda i, j: (i, j, OFF_GATE // GATE_COLS)),
                  tok(A_WIDTH), tok(SSM_INNER), tok(C_WIDTH),
                  full(A_WIDTH, d), full(SSM_INNER, d), full(C_WIDTH, d), full(d, d)],
        out_specs=tok(d),
        compiler_params=_cp(("arbitrary", "arbitrary")),
        name="merge",
    )(x, ada, proj, o_a, o_b, o_c, wa16, wb16, wc16, wo16)


def _pad_keys(x, mult):
    pad = (-x.shape[1]) % mult
    if pad:
        x = jnp.concatenate([x, jnp.zeros((x.shape[0], pad) + x.shape[2:], x.dtype)], axis=1)
    return x


def _layer(x, ada, lp, rel_bias, tables_memo, past, g_final, last, layer):
    b, t, d = x.shape
    x = _ffn(x, ada, lp['g_ffn1'], lp['w_ffn1_in'], lp['w_ffn1_out'], g_final, k0=0, final_norm=False)
    proj, k16, v16, ki16, k_new, v_new, ki_new = _inproj(x, ada, lp['g_mix'], lp['w_proj'], k0=3)
    new_conv = proj[:, t - (CONV_W - 1):, OFF_XBC:OFF_XBC + CONV_DIM]

    o_a, va = _mixer_a(proj, lp['g_a'], lp['w_s'], lp['b_s'])

    if past is None:
        conv_init = jnp.zeros((b, CONV_W - 1, CONV_DIM), f32)
        h0 = jnp.zeros((b, SSM_HEADS, SSM_HEAD_DIM, SSM_STATE), f32)
    else:
        h0, conv_init = past[3], past[4]
    o_b, new_ssm = _mixer_b(proj, conv_init, h0, lp['w_conv'], lp['b_conv'], lp['dt_bias'], lp['a_log'],
                            lp['d_skip'], lp['g_ssm'])
    new_ssm = new_ssm.reshape(b, SSM_HEADS, SSM_HEAD_DIM, SSM_STATE)

    if past is None:
        k_all, v_all, ki_all, cache = k16, v16, ki16, None
        qb, qoff, true_len = (ATT_QB if t % ATT_QB == 0 else Q_BLOCK), 0, t
    else:
        plen = past[0].shape[2]
        if plen % ATT_KB == 0:
            k_all, v_all, cache = k16, v16, (past[0], past[1], layer)
        else:
            k_all = jnp.concatenate([past[0][layer].reshape(b, plen, C_KV).astype(bf16), k16], axis=1)
            v_all = jnp.concatenate([past[1][layer].reshape(b, plen, C_KV).astype(bf16), v16], axis=1)
            cache = None
        ki_all = jnp.concatenate([past[2].astype(bf16), ki16], axis=1)
        qb, qoff, true_len = t, plen, plen + t
    o_c = _mixer_c(proj, _pad_keys(k_all, ATT_KB), _pad_keys(v_all, ATT_KB), _pad_keys(ki_all, ATT_KB), rel_bias, tables_memo,
                   qb=qb, qoff=qoff, true_len=true_len, cache=cache)

    x = _merge(x, ada, proj, o_a, o_b, o_c, lp['w_br_a'], lp['w_br_b'], lp['w_br_c'], lp['w_out'])
    x = _ffn(x, ada, lp['g_ffn2'], lp['w_ffn2_in'], lp['w_ffn2_out'], g_final, k0=6, final_norm=last)
    states = (k_new, v_new, ki_new, new_ssm, new_conv, va)
    return x, states


def kernel(x_prompt, x_sample, c_prompt, c_sample, cache_k, cache_v, cache_kidx, state_ssm, state_conv, rel_bias, w_ada, b_ada, g_ffn1, w_ffn1_in, w_ffn1_out, g_mix, w_in, g_a, w_s, b_s, w_conv, b_conv, dt_bias, a_log, d_skip, g_ssm, w_br_a, w_br_b, w_br_c, w_out, g_ffn2, w_ffn2_in, w_ffn2_out, g_final):
    depth = w_ada.shape[0]
    bp, bs = c_prompt.shape[0], c_sample.shape[0]
    assert x_prompt.shape[1] % ATT_KB == 0 and x_prompt.shape[1] % Q_BLOCK == 0
    assert x_sample.shape[1] == CHUNK

    rows = -(-(bp + bs) // 8) * 8
    c_all = jnp.concatenate([c_prompt, c_sample, jnp.zeros((rows - bp - bs, D_MODEL), f32)], axis=0)
    ada_all = _ada_all(c_all, w_ada.astype(bf16), b_ada)
    ada_all = ada_all.reshape(depth, rows, N_ADA, D_MODEL)

    yp, ys = x_prompt, x_sample
    tables_memo = {}
    st_p, st_s = [], []
    for i in range(depth):
        lp = dict(g_ffn1=g_ffn1[i], w_ffn1_in=w_ffn1_in[i].astype(bf16), w_ffn1_out=w_ffn1_out[i].astype(bf16),
                  g_mix=g_mix[i], w_proj=_layout_w_in(w_in[i]).astype(bf16), g_a=g_a[i], w_s=w_s[i], b_s=b_s[i],
                  w_conv=w_conv[i], b_conv=b_conv[i], dt_bias=dt_bias[i], a_log=a_log[i], d_skip=d_skip[i],
                  g_ssm=g_ssm[i], w_br_a=w_br_a[i].astype(bf16), w_br_b=w_br_b[i].astype(bf16),
                  w_br_c=w_br_c[i].astype(bf16), w_out=w_out[i].astype(bf16), g_ffn2=g_ffn2[i],
                  w_ffn2_in=w_ffn2_in[i].astype(bf16), w_ffn2_out=w_ffn2_out[i].astype(bf16))
        last = i == depth - 1
        yp, sp = _layer(yp, ada_all[i, :bp], lp, rel_bias, tables_memo, None, g_final, last, i)
        ys, ss = _layer(ys, ada_all[i, bp:bp + bs], lp, rel_bias, tables_memo,
                        (cache_k, cache_v, cache_kidx[i], state_ssm[i], state_conv[i]), g_final, last, i)
        st_p.append(sp)
        st_s.append(ss)

    stack = lambda sts, k: jnp.stack([s[k] for s in sts])
    return (yp, ys,
            stack(st_p, 0), stack(st_p, 1), stack(st_p, 2), stack(st_p, 3), stack(st_p, 4),
            stack(st_s, 0), stack(st_s, 1), stack(st_s, 2), stack(st_s, 3), stack(st_s, 4), stack(st_s, 5))
```

```python
import functools
import math

import numpy as np
import jax
import jax.numpy as jnp
from jax import lax
from jax.experimental import pallas as pl
from jax.experimental.pallas import tpu as pltpu

f32 = jnp.float32
bf16 = jnp.bfloat16
i32 = jnp.int32

D_MODEL = 1024
CHUNK = 64
EPS = 1e-6
N_ADA = 9
D_FF = 2816
A_WIDTH = D_MODEL
A_GROUPS = 8
A_GROUP_DIM = A_WIDTH // A_GROUPS
A_CHUNK = 128
SSM_INNER = 2 * D_MODEL
SSM_HEAD_DIM = 64
SSM_HEADS = SSM_INNER // SSM_HEAD_DIM
SSM_GROUPS = 4
SSM_HPG = SSM_HEADS // SSM_GROUPS
SSM_STATE = 128
CONV_W = 4
CONV_DIM = SSM_INNER + 2 * SSM_GROUPS * SSM_STATE
C_HEADS = 8
C_KV_HEADS = 4
C_GROUP = C_HEADS // C_KV_HEADS
C_HEAD_DIM = 128
C_WIDTH = C_HEADS * C_HEAD_DIM
IDX_HEADS = 8
IDX_DIM = 64
TOPK_MAX = 256
Q_BLOCK = 128
N_BUCKETS = 32
MAX_DISTANCE = 128
A_COLS = 2 * A_WIDTH
C_Q = C_HEADS * C_HEAD_DIM
C_KV = C_KV_HEADS * C_HEAD_DIM
C_QI = IDX_HEADS * IDX_DIM
GATE_COLS = 3 * D_MODEL

LANES = 128
SUBLANES = 8
VMEM_LIMIT = 56 * 1024 * 1024

OFF_GATE = 0
OFF_XBC = OFF_GATE + GATE_COLS
OFF_PA = OFF_XBC + CONV_DIM
OFF_Z = OFF_PA + A_COLS
OFF_Q = OFF_Z + SSM_INNER
OFF_K = OFF_Q + C_Q
OFF_V = OFF_K + C_KV
OFF_QI = OFF_V + C_KV
OFF_SMALL = OFF_QI + C_QI
SMALL_WI = SSM_HEADS
SMALL_KI = LANES - IDX_DIM
PROJ_TILE = 1024
PROJ_W = ((OFF_SMALL + LANES + PROJ_TILE - 1) // PROJ_TILE) * PROJ_TILE

NEG_BIG = -1e30
SUBNORMAL_SPAN = 0x007FFFFF
NEG_INF_KEY = -2139095041 + SUBNORMAL_SPAN + 1
INT_MAX = 2147483647
ATT_KB = 1024
IDX_KB = 512
TOP_R = 16
IDX_QB = 256
IDX_RS = 32
ATT_RC = 32
LOG2E = math.log2(math.e)
ATT_QB = 256
BIAS_WIN = ATT_QB + MAX_DISTANCE
FAR_BUCKET = N_BUCKETS // 2 - 1


def _cp(sem, vmem=VMEM_LIMIT):
    return pltpu.CompilerParams(dimension_semantics=sem, vmem_limit_bytes=vmem)


def _rms(x, g):
    return x * lax.rsqrt(jnp.mean(x * x, axis=-1, keepdims=True) + EPS) * g


def _dot(a, b):
    return jnp.dot(a, b, preferred_element_type=f32)


def _dot_nt(a, b):
    return lax.dot_general(a, b, (((1,), (1,)), ((), ())), preferred_element_type=f32)


def _dot_tn(a, b):
    return lax.dot_general(a, b, (((0,), (0,)), ((), ())), preferred_element_type=f32)


def _bf16_pieces(x, pieces):
    out = []
    for _ in range(pieces):
        p = x.astype(bf16)
        out.append(p)
        x = x - p.astype(f32)
    return out


def _dot_zero_one_lhs(sel16, x, pieces):
    parts = [_dot(sel16, p) for p in _bf16_pieces(x, pieces)]
    return functools.reduce(lambda a, b: a + b, parts)


def _dot_zero_one_rhs(x, sel16, pieces):
    parts = [_dot(p, sel16) for p in _bf16_pieces(x, pieces)]
    return functools.reduce(lambda a, b: a + b, parts)


def _ada_kernel(c_ref, w_ref, b_ref, o_ref):
    s = jax.nn.silu(c_ref[...]).astype(bf16)
    o_ref[0] = _dot(s, w_ref[0]) + b_ref[0]


def _ada_all(c_all, w_ada16, b_ada):
    depth = w_ada16.shape[0]
    rows = c_all.shape[0]
    n_out = N_ADA * D_MODEL
    tn = 9 * LANES
    return pl.pallas_call(
        _ada_kernel,
        out_shape=jax.ShapeDtypeStruct((depth, rows, n_out), f32),
        grid=(depth, n_out // tn),
        in_specs=[pl.BlockSpec((rows, D_MODEL), lambda d, n: (0, 0)),
                  pl.BlockSpec((1, D_MODEL, tn), lambda d, n: (d, 0, n)),
                  pl.BlockSpec((1, 1, tn), lambda d, n: (d, 0, n))],
        out_specs=pl.BlockSpec((1, rows, tn), lambda d, n: (d, 0, n)),
        compiler_params=_cp(("arbitrary", "arbitrary")),
        name="ada",
    )(c_all, w_ada16, b_ada.reshape(depth, 1, n_out))


def _ffn_kernel(x_ref, ada_ref, g_ref, wg_ref, wu_ref, wo_ref, gf_ref, o_ref, h_scr, acc_scr, *, k0, final_norm):
    f = pl.program_id(2)
    bt, tt, d = x_ref.shape
    rows = bt * tt

    @pl.when(f == 0)
    def _():
        ada = ada_ref[...]
        h = _rms(x_ref[...], g_ref[...]) * (1.0 + ada[:, k0 + 1:k0 + 2, :]) + ada[:, k0:k0 + 1, :]
        h_scr[...] = h.reshape(rows, d).astype(bf16)
        acc_scr[...] = jnp.zeros_like(acc_scr)

    h = h_scr[...]
    gate = _dot(h, wg_ref[...])
    up = _dot(h, wu_ref[...])
    a = (jax.nn.silu(gate) * up).astype(bf16)
    acc_scr[...] += _dot(a, wo_ref[...])

    @pl.when(f == pl.num_programs(2) - 1)
    def _():
        ada = ada_ref[...]
        y = x_ref[...] + (0.5 * ada[:, k0 + 2:k0 + 3, :]) * acc_scr[...].reshape(bt, tt, d)
        if final_norm:
            y = _rms(y, gf_ref[...])
        o_ref[...] = y


def _token_tile(b, t, rows=512):
    if t >= rows:
        return 1, rows
    return min(b, rows // t), t


def _ffn(x, ada, g, w_in16, w_out16, g_final, *, k0, final_norm):
    b, t, d = x.shape
    bt, tt = _token_tile(b, t)
    tf = D_FF // 2
    nf = D_FF // tf
    kern = functools.partial(_ffn_kernel, k0=k0, final_norm=final_norm)
    return pl.pallas_call(
        kern,
        out_shape=jax.ShapeDtypeStruct((b, t, d), f32),
        grid=(b // bt, t // tt, nf),
        in_specs=[pl.BlockSpec((bt, tt, d), lambda i, j, f: (i, j, 0)),
                  pl.BlockSpec((bt, N_ADA, d), lambda i, j, f: (i, 0, 0)),
                  pl.BlockSpec((1, d), lambda i, j, f: (0, 0)),
                  pl.BlockSpec((d, tf), lambda i, j, f: (0, f)),
                  pl.BlockSpec((d, tf), lambda i, j, f: (0, f + nf)),
                  pl.BlockSpec((tf, d), lambda i, j, f: (f, 0)),
                  pl.BlockSpec((1, d), lambda i, j, f: (0, 0))],
        out_specs=pl.BlockSpec((bt, tt, d), lambda i, j, f: (i, j, 0)),
        scratch_shapes=[pltpu.VMEM((bt * tt, d), bf16), pltpu.VMEM((bt * tt, d), f32)],
        compiler_params=_cp(("arbitrary", "arbitrary", "arbitrary")),
        name="ffn",
    )(x, ada, g.reshape(1, d), w_in16, w_in16, w_out16, g_final.reshape(1, d))


def _inproj_kernel(x_ref, ada_ref, g_ref, w_ref, o_ref, k16_ref, v16_ref, ki16_ref, k32_ref, v32_ref, ki32_ref, h_scr, *, k0):
    n = pl.program_id(2)
    bt, tt, d = x_ref.shape

    @pl.when(n == 0)
    def _():
        ada = ada_ref[...]
        h = _rms(x_ref[...], g_ref[...]) * (1.0 + ada[:, k0 + 1:k0 + 2, :]) + ada[:, k0:k0 + 1, :]
        h_scr[...] = h.reshape(bt * tt, d).astype(bf16)

    res = _dot(h_scr[...], w_ref[...])
    o_ref[...] = res.reshape(bt, tt, PROJ_TILE)

    @pl.when(n == OFF_K // PROJ_TILE)
    def _():
        k = res[:, :C_KV]
        v = res[:, C_KV:2 * C_KV]
        k16_ref[...] = k.astype(bf16).reshape(bt, tt, C_KV)
        v16_ref[...] = v.astype(bf16).reshape(bt, tt, C_KV)
        k32_ref[...] = k.reshape(bt, tt, C_KV_HEADS, C_HEAD_DIM)
        v32_ref[...] = v.reshape(bt, tt, C_KV_HEADS, C_HEAD_DIM)

    @pl.when(n == OFF_SMALL // PROJ_TILE)
    def _():
        lo = OFF_SMALL % PROJ_TILE + SMALL_KI
        ki = res[:, lo:lo + IDX_DIM]
        ki16_ref[...] = ki.astype(bf16).reshape(bt, tt, IDX_DIM)
        ki32_ref[...] = ki.reshape(bt, tt, IDX_DIM)


def _inproj(x, ada, g, w_proj16, *, k0):
    b, t, d = x.shape
    bt, tt = _token_tile(b, t, rows=1024)
    assert OFF_K % PROJ_TILE == 0 and OFF_V == OFF_K + C_KV and 2 * C_KV <= PROJ_TILE
    tok = lambda w: pl.BlockSpec((bt, tt, w), lambda i, j, n: (i, j, 0))
    heads = pl.BlockSpec((bt, tt, C_KV_HEADS, C_HEAD_DIM), lambda i, j, n: (i, j, 0, 0))
    kv32 = jax.ShapeDtypeStruct((b, t, C_KV_HEADS, C_HEAD_DIM), f32)
    return pl.pallas_call(
        functools.partial(_inproj_kernel, k0=k0),
        out_shape=(jax.ShapeDtypeStruct((b, t, PROJ_W), f32), jax.ShapeDtypeStruct((b, t, C_KV), bf16),
                   jax.ShapeDtypeStruct((b, t, C_KV), bf16), jax.ShapeDtypeStruct((b, t, IDX_DIM), bf16),
                   kv32, kv32, jax.ShapeDtypeStruct((b, t, IDX_DIM), f32)),
        grid=(b // bt, t // tt, PROJ_W // PROJ_TILE),
        in_specs=[pl.BlockSpec((bt, tt, d), lambda i, j, n: (i, j, 0)),
                  pl.BlockSpec((bt, N_ADA, d), lambda i, j, n: (i, 0, 0)),
                  pl.BlockSpec((1, d), lambda i, j, n: (0, 0)),
                  pl.BlockSpec((d, PROJ_TILE), lambda i, j, n: (0, n))],
        out_specs=(pl.BlockSpec((bt, tt, PROJ_TILE), lambda i, j, n: (i, j, n)), tok(C_KV), tok(C_KV), tok(IDX_DIM),
                   heads, heads, tok(IDX_DIM)),
        scratch_shapes=[pltpu.VMEM((bt * tt, d), bf16)],
        compiler_params=_cp(("arbitrary", "arbitrary", "arbitrary")),
        name="inproj",
    )(x, ada, g.reshape(1, d), w_proj16)


def _layout_w_in(w_in):
    d = w_in.shape[0]
    o = 0
    pa = w_in[:, o:o + A_COLS]; o += A_COLS
    z = w_in[:, o:o + SSM_INNER]; o += SSM_INNER
    xbc = w_in[:, o:o + CONV_DIM]; o += CONV_DIM
    dt = w_in[:, o:o + SSM_HEADS]; o += SSM_HEADS
    q = w_in[:, o:o + C_Q]; o += C_Q
    k = w_in[:, o:o + C_KV]; o += C_KV
    v = w_in[:, o:o + C_KV]; o += C_KV
    qi = w_in[:, o:o + C_QI]; o += C_QI
    ki = w_in[:, o:o + IDX_DIM]; o += IDX_DIM
    wi = w_in[:, o:o + IDX_HEADS]; o += IDX_HEADS
    gates = w_in[:, o:o + GATE_COLS]
    small = jnp.concatenate([dt, wi, jnp.zeros((d, SMALL_KI - SMALL_WI - IDX_HEADS), w_in.dtype), ki], axis=1)
    pad = jnp.zeros((d, PROJ_W - OFF_SMALL - LANES), w_in.dtype)
    return jnp.concatenate([gates, xbc, pa, z, q, k, v, qi, small, pad], axis=1)


def _mixa_kernel(pa_ref, ga_ref, ws_ref, bs_ref, oa_ref, va_ref):
    n = pa_ref.shape[1]
    ge = jax.nn.gelu(pa_ref[0])
    u = ge[:, :A_WIDTH]
    v = _rms(ge[:, A_WIDTH:], ga_ref[...])
    va_ref[0] = v
    vb = v.astype(bf16)
    tri = lax.broadcasted_iota(i32, (n, n), 0) >= lax.broadcasted_iota(i32, (n, n), 1)
    bs = bs_ref[...]
    for g in range(A_GROUPS):
        sl = slice(g * A_GROUP_DIM, (g + 1) * A_GROUP_DIM)
        w = jnp.where(tri, ws_ref[g], 0.0).astype(bf16)
        mixed = _dot(w, vb[:, sl]) + bs[:, g:g + 1]
        oa_ref[0, :, sl] = (u[:, sl] * mixed).astype(bf16)


def _mixer_a(proj, g_a, w_s, b_s):
    b, t, _ = proj.shape
    n = min(t, A_CHUNK)
    ws = w_s[:, :n, :n]
    bs_t = b_s[:, :n].T
    return pl.pallas_call(
        _mixa_kernel,
        out_shape=(jax.ShapeDtypeStruct((b, t, A_WIDTH), bf16), jax.ShapeDtypeStruct((b, t, A_WIDTH), f32)),
        grid=(b, t // n),
        in_specs=[pl.BlockSpec((1, n, A_COLS), lambda i, j: (i, j, OFF_PA // A_COLS)),
                  pl.BlockSpec((1, A_WIDTH), lambda i, j: (0, 0)),
                  pl.BlockSpec((A_GROUPS, n, n), lambda i, j: (0, 0, 0)),
                  pl.BlockSpec((n, A_GROUPS), lambda i, j: (0, 0))],
        out_specs=(pl.BlockSpec((1, n, A_WIDTH), lambda i, j: (i, j, 0)),
                   pl.BlockSpec((1, n, A_WIDTH), lambda i, j: (i, j, 0))),
        compiler_params=_cp(("arbitrary", "arbitrary")),
        name="mixer_a",
    )(proj, g_a.reshape(1, A_WIDTH), ws, bs_t)


CONV_PAD = 8


def _ssd_kernel(z_ref, xbc_ref, sm_ref, cinit_ref, h0_ref, wc_ref, bc_ref, dtb_ref, alog_ref, dsk_ref, gs_ref,
                ex_ref, ob_ref, hs_ref, xpad, ht):
    c = pl.program_id(1)
    q = z_ref.shape[1]
    gw = SSM_HPG * SSM_HEAD_DIM
    tail = CONV_W - 1

    @pl.when(c == 0)
    def _():
        xpad[CONV_PAD - tail:CONV_PAD, :] = cinit_ref[0]
        for g in range(SSM_GROUPS):
            ht[g] = h0_ref[0, g * gw:(g + 1) * gw, :].T

    xpad[CONV_PAD:CONV_PAD + q, :] = xbc_ref[0]
    wc = wc_ref[...]
    conv = bc_ref[...] + wc[0:1, :] * xpad[pl.ds(CONV_PAD - 3, q), :]
    for k in range(1, CONV_W):
        conv = conv + wc[k:k + 1, :] * xpad[pl.ds(CONV_PAD - 3 + k, q), :]
    nxt = xpad[CONV_PAD + q - tail:CONV_PAD + q, :]
    xpad[CONV_PAD - tail:CONV_PAD, :] = nxt
    conv = jax.nn.silu(conv)
    xs = conv[:, :SSM_INNER]
    bm = conv[:, SSM_INNER:SSM_INNER + SSM_GROUPS * SSM_STATE]
    cm = conv[:, SSM_INNER + SSM_GROUPS * SSM_STATE:]

    lane = lax.broadcasted_iota(i32, (q, LANES), 1)
    head_lane = lane < SSM_HEADS
    dt = jnp.where(head_lane, jax.nn.softplus(sm_ref[0] + dtb_ref[...]), 0.0)
    da = dt * (-jnp.exp(alog_ref[...]))
    tri = lax.broadcasted_iota(i32, (q, q), 0) >= lax.broadcasted_iota(i32, (q, q), 1)
    acs = _dot_zero_one_lhs(tri.astype(bf16), da, 3)
    acs_t = acs.T
    dt_t = dt.T
    acs_last = acs[q - 1:q, :]
    e_acs = jnp.where(head_lane, jnp.exp(acs), 0.0)
    decay = jnp.where(head_lane, jnp.exp(acs_last - acs) * dt, 0.0)
    ex = ex_ref[...]
    e_acs_x = _dot_zero_one_rhs(e_acs, ex, 3)
    decay_x = _dot_zero_one_rhs(decay, ex, 3)
    e_last_x = e_acs_x[q - 1:q, :]

    xs16 = xs.astype(bf16)
    xdec16 = (xs * decay_x).astype(bf16)
    y_parts = []
    for g in range(SSM_GROUPS):
        bg = bm[:, g * SSM_STATE:(g + 1) * SSM_STATE].astype(bf16)
        cg = cm[:, g * SSM_STATE:(g + 1) * SSM_STATE].astype(bf16)
        cb = _dot_nt(cg, bg)
        htg = ht[g]
        y_inter = _dot(cg, htg.astype(bf16))
        intra = []
        for r in range(SSM_HPG):
            hd = g * SSM_HPG + r
            seg = acs[:, hd:hd + 1] - acs_t[hd:hd + 1, :]
            m = cb * jnp.exp(jnp.where(tri, seg, -jnp.inf)) * dt_t[hd:hd + 1, :]
            intra.append(_dot(m.astype(bf16), xs16[:, hd * SSM_HEAD_DIM:(hd + 1) * SSM_HEAD_DIM]))
        y_parts.append(jnp.concatenate(intra, axis=1) + y_inter * e_acs_x[:, g * gw:(g + 1) * gw])
        ht[g] = htg * e_last_x[:, g * gw:(g + 1) * gw] + _dot_tn(bg, xdec16[:, g * gw:(g + 1) * gw])
    y = jnp.concatenate(y_parts, axis=1) + dsk_ref[...] * xs
    y = _rms(y * jax.nn.silu(z_ref[0]), gs_ref[...])
    ob_ref[0] = y.astype(bf16)

    @pl.when(c == pl.num_programs(1) - 1)
    def _():
        for g in range(SSM_GROUPS):
            hs_ref[0, g * gw:(g + 1) * gw, :] = ht[g].T


def _mixer_b(proj, conv_init, h0, w_conv, b_conv, dt_bias, a_log, d_skip, g_ssm):
    b, t, _ = proj.shape
    q = CHUNK
    pad_row = lambda v: jnp.zeros((1, LANES), f32).at[0, :SSM_HEADS].set(v)
    expand = (jnp.arange(LANES)[:, None] == (jnp.arange(SSM_INNER) // SSM_HEAD_DIM)[None, :]).astype(bf16)
    hrows = SSM_HEADS * SSM_HEAD_DIM
    return pl.pallas_call(
        _ssd_kernel,
        out_shape=(jax.ShapeDtypeStruct((b, t, SSM_INNER), bf16),
                   jax.ShapeDtypeStruct((b, hrows, SSM_STATE), f32)),
        grid=(b, t // q),
        in_specs=[pl.BlockSpec((1, q, SSM_INNER), lambda i, c: (i, c, OFF_Z // SSM_INNER)),
                  pl.BlockSpec((1, q, CONV_DIM), lambda i, c: (i, c, OFF_XBC // CONV_DIM)),
                  pl.BlockSpec((1, q, LANES), lambda i, c: (i, c, OFF_SMALL // LANES)),
                  pl.BlockSpec((1, CONV_W - 1, CONV_DIM), lambda i, c: (i, 0, 0)),
                  pl.BlockSpec((1, hrows, SSM_STATE), lambda i, c: (i, 0, 0)),
                  pl.BlockSpec((CONV_W, CONV_DIM), lambda i, c: (0, 0)),
                  pl.BlockSpec((1, CONV_DIM), lambda i, c: (0, 0)),
                  pl.BlockSpec((1, LANES), lambda i, c: (0, 0)),
                  pl.BlockSpec((1, LANES), lambda i, c: (0, 0)),
                  pl.BlockSpec((1, SSM_INNER), lambda i, c: (0, 0)),
                  pl.BlockSpec((1, SSM_INNER), lambda i, c: (0, 0)),
                  pl.BlockSpec((LANES, SSM_INNER), lambda i, c: (0, 0))],
        out_specs=(pl.BlockSpec((1, q, SSM_INNER), lambda i, c: (i, c, 0)),
                   pl.BlockSpec((1, hrows, SSM_STATE), lambda i, c: (i, 0, 0))),
        scratch_shapes=[pltpu.VMEM((CONV_PAD + q, CONV_DIM), f32),
                        pltpu.VMEM((SSM_GROUPS, SSM_STATE, SSM_HPG * SSM_HEAD_DIM), f32)],
        compiler_params=_cp(("arbitrary", "arbitrary")),
        name="mixer_b",
    )(proj, proj, proj, conv_init, h0.reshape(b, hrows, SSM_STATE), w_conv, b_conv.reshape(1, CONV_DIM),
      pad_row(dt_bias), pad_row(a_log), jnp.repeat(d_skip, SSM_HEAD_DIM).reshape(1, SSM_INNER),
      g_ssm.reshape(1, SSM_INNER), expand)


def _t5_bucket(rel):
    nb = N_BUCKETS // 2
    max_exact = nb // 2
    n = jnp.abs(rel)
    nf = jnp.maximum(n, 1).astype(jnp.float32)
    large = max_exact + (jnp.log(nf / max_exact) / math.log(MAX_DISTANCE / max_exact)
                         * (nb - max_exact)).astype(jnp.int32)
    large = jnp.minimum(large, nb - 1)
    return jnp.where(rel > 0, nb, 0) + jnp.where(n < max_exact, n, large)


def _bias_kernel(tab_ref, bk_ref, o_ref):
    h = pl.program_id(1)
    bk = bk_ref[0]
    far = tab_ref[FAR_BUCKET, h]
    acc = jnp.zeros(bk.shape, f32)
    for bucket in range(N_BUCKETS):
        acc = jnp.where(bk == bucket, tab_ref[bucket, h] - far, acc)
    o_ref[0, 0] = acc * LOG2E


def _bias_tables(rel_bias, qb, offsets):
    t = np.arange(qb)[:, None]
    cases = []
    for off in offsets:
        w = np.arange(ATT_KB)[None, :] - off
        valid = (w >= 0) & (w < BIAS_WIN)
        rel = w - BIAS_WIN + qb - t
        cases.append(jnp.where(jnp.asarray(valid), _t5_bucket(jnp.asarray(rel, dtype=jnp.int32)), -1))
    cases.append(jnp.full((qb, ATT_KB), -1, jnp.int32))
    buckets = jnp.stack(cases).astype(i32)
    nc = buckets.shape[0]
    return pl.pallas_call(
        _bias_kernel,
        out_shape=jax.ShapeDtypeStruct((nc, C_HEADS, qb, ATT_KB), f32),
        grid=(nc, C_HEADS),
        in_specs=[pl.BlockSpec(memory_space=pltpu.SMEM),
                  pl.BlockSpec((1, qb, ATT_KB), lambda c, h: (c, 0, 0))],
        out_specs=pl.BlockSpec((1, 1, qb, ATT_KB), lambda c, h: (c, h, 0, 0)),
        compiler_params=_cp(("arbitrary", "arbitrary")),
        name="t5_bias",
    )(rel_bias, buckets)


def _oddeven_merge_sort_pairs(n):
    pairs = []
    p = 1
    while p < n:
        k = p
        while k >= 1:
            for j in range(k % p, n - k, 2 * k):
                for i in range(min(k, n - j - k)):
                    if (i + j) // (2 * p) == (i + j + k) // (2 * p):
                        pairs.append((i + j, i + j + k))
            k //= 2
        p *= 2
    return pairs


def _compare_exchange(v, i, j):
    a, b = v[i], v[j]
    v[i] = jnp.maximum(a, b)
    v[j] = jnp.minimum(a, b)


def _sort_desc(v):
    for i, j in _oddeven_merge_sort_pairs(len(v)):
        _compare_exchange(v, i, j)


def _bitonic_merge_desc(v):
    d = len(v) // 2
    while d >= 1:
        for i in range(len(v)):
            if (i & d) == 0:
                _compare_exchange(v, i, i + d)
        d //= 2


def _index_kernel(qi_ref, sm_ref, ki_ref, o_ref, keys, top_scr, *, qoff, topk, idx_bits):
    j = pl.program_id(1)
    qb = qi_ref.shape[1]
    ltot = ki_ref.shape[1]
    kb = IDX_KB
    kmax = qoff + (j + 1) * qb
    nkb = lax.div(kmax + (kb - 1), kb)
    sub = kb // LANES

    qv = qi_ref[0].astype(bf16)
    rs = IDX_RS
    stacked = jnp.concatenate([qv[s * rs:(s + 1) * rs, h * IDX_DIM:(h + 1) * IDX_DIM]
                               for s in range(qb // rs) for h in range(IDX_HEADS)], axis=0)
    w = sm_ref[0][:, SMALL_WI:SMALL_WI + IDX_HEADS] * (IDX_HEADS ** -0.5) * (IDX_DIM ** -0.5)
    wcol = [jnp.broadcast_to(w[:, h:h + 1], (qb, LANES)) for h in range(IDX_HEADS)]
    qchunk = lax.shift_right_arithmetic(qoff + j * qb + lax.broadcasted_iota(i32, (qb, 1), 0), 6)
    lane_pos = lax.broadcasted_iota(i32, (1, kb), 1)

    def score_body(i, carry):
        off = pl.multiple_of(i * kb, kb)
        prod = _dot_nt(stacked, ki_ref[0, pl.ds(off, kb), :])
        kchunk = lax.shift_right_arithmetic(off + lane_pos, 6)
        for s in range(qb // rs):
            rows = slice(s * rs, (s + 1) * rs)
            acc = None
            for h in range(IDX_HEADS):
                r0 = (s * IDX_HEADS + h) * rs
                term = jnp.maximum(prod[r0:r0 + rs], 0.0) * jnp.concatenate([wcol[h][rows]] * sub, axis=1)
                acc = term if acc is None else acc + term
            adm = kchunk <= qchunk[rows]
            keys[rows, pl.ds(off, kb)] = jnp.where(adm, jnp.where(acc == 0.0, 0.0, acc), -jnp.inf)
        return carry

    lax.fori_loop(0, nkb, score_body, 0)

    group = TOP_R * LANES
    ngrp = lax.div(kmax + (group - 1), group)

    def pad_body(i, carry):
        keys[:, pl.ds(pl.multiple_of(i * kb, kb), kb)] = jnp.full((qb, kb), -jnp.inf, f32)
        return carry

    lax.fori_loop(nkb, ngrp * (group // kb), pad_body, 0)
    top_scr[...] = jnp.full(top_scr.shape, -jnp.inf, f32)

    def top_body(gidx, carry):
        base = gidx * group
        for s in range(qb // SUBLANES):
            rs = slice(s * SUBLANES, (s + 1) * SUBLANES)
            v = [keys[rs, pl.ds(pl.multiple_of(base + i * LANES, LANES), LANES)] for i in range(TOP_R)]
            _sort_desc(v)
            merged = [jnp.maximum(top_scr[i, rs, :], v[TOP_R - 1 - i]) for i in range(TOP_R)]
            _bitonic_merge_desc(merged)
            for i in range(TOP_R):
                top_scr[i, rs, :] = merged[i]
        return carry

    lax.fori_loop(0, ngrp, top_body, 0)

    def flip(bits):
        return bits ^ (lax.shift_right_arithmetic(bits, 31) & INT_MAX)

    def to_key(x):
        k = flip(lax.bitcast_convert_type(x, i32))
        return jnp.where(k > 0, k - SUBNORMAL_SPAN, jnp.where(k < 0, k + SUBNORMAL_SPAN + 1, 0))

    def to_score(key):
        k = jnp.where(key > 0, key + SUBNORMAL_SPAN, jnp.where(key < 0, key - SUBNORMAL_SPAN - 1, 0))
        return lax.bitcast_convert_type(flip(k), f32)

    def count_top(thr):
        tb = jnp.broadcast_to(to_score(thr), (qb, LANES))
        acc = (top_scr[0] >= tb).astype(i32)
        for i in range(1, TOP_R):
            acc = acc + (top_scr[i] >= tb).astype(i32)
        return jnp.sum(acc, axis=1, keepdims=True)

    def count(pred):
        def body(i, acc):
            off = pl.multiple_of(i * kb, kb)
            blk = keys[:, pl.ds(off, kb)]
            m = pred(blk, off).astype(i32)
            for c in range(sub):
                acc = acc + m[:, c * LANES:(c + 1) * LANES]
            return acc
        acc = lax.fori_loop(0, nkb, body, jnp.zeros((qb, LANES), i32))
        return jnp.sum(acc, axis=1, keepdims=True)

    def count_ge(thr):
        tb = jnp.broadcast_to(to_score(thr), (qb, kb))
        return count(lambda blk, off: blk >= tb)

    assert topk <= 2 * LANES and TOP_R >= 2
    hi0 = to_key(jnp.max(top_scr[1 if topk > LANES else 0], axis=1, keepdims=True)) + 1
    lo0 = jnp.minimum(to_key(jnp.min(top_scr[1], axis=1, keepdims=True)), hi0 - 1)

    def search(count_fn, lo_start):
        def cond(c):
            it, open_rows, _, _ = c
            return (open_rows > 0) & (it < 40)

        def body(c):
            it, _, lo, hi = c
            mid = lax.shift_right_arithmetic(lo, 1) + lax.shift_right_arithmetic(hi, 1) + (lo & hi & 1)
            n = count_fn(mid)
            ge = n >= topk
            lo_n = jnp.where(ge, mid, lo)
            hi_n = jnp.where(n == topk, mid + 1, jnp.where(ge, hi, mid))
            open_rows = jnp.max((lo_n < hi_n - 1).astype(i32))
            return it + 1, open_rows, lo_n, hi_n

        open0 = jnp.max((lo_start < hi0 - 1).astype(i32))
        return lax.while_loop(cond, body, (jnp.int32(0), open0, lo_start, hi0))[2]

    thr_c = search(count_top, lo0)
    n_ge_c = count_top(thr_c)
    smallest = to_key(jnp.max(top_scr[TOP_R - 1], axis=1, keepdims=True))
    bad = (smallest >= thr_c) & (thr_c > NEG_INF_KEY)

    def full_search(_):
        t = search(count_ge, lo0)
        return t, count_ge(t)

    thr, n_ge = lax.cond(jnp.max(bad.astype(i32)) > 0, full_search, lambda _: (thr_c, n_ge_c), 0)
    tie = (n_ge > topk) & (thr > NEG_INF_KEY)

    def tie_cut(_):
        tb = jnp.broadcast_to(to_score(thr), (qb, kb))
        need = topk - count_ge(thr + 1)

        def body(_, c):
            lo, hi = c
            mid = lax.shift_right_arithmetic(lo + hi, 1)
            mb = jnp.broadcast_to(mid, (qb, kb))
            n = count(lambda blk, off: (blk == tb) & ((off + lane_pos) <= mb))
            ok = n >= need
            return jnp.where(ok, lo, mid), jnp.where(ok, mid, hi)

        lo_i = jnp.full((qb, 1), -1, i32)
        hi_i = jnp.full((qb, 1), 1, i32) * (nkb * kb - 1)
        _, cut = lax.fori_loop(0, idx_bits, body, (lo_i, hi_i))
        return jnp.where(tie, cut, INT_MAX)

    any_tie = jnp.max(tie.astype(i32)) > 0
    cut = lax.cond(any_tie, tie_cut, lambda _: jnp.full((qb, 1), INT_MAX, i32), 0)

    tb = jnp.broadcast_to(to_score(thr), (qb, kb))
    cb = jnp.broadcast_to(cut, (qb, kb))

    def out_body(i, carry):
        off = pl.multiple_of(i * kb, kb)
        blk = keys[:, pl.ds(off, kb)]
        sel = ((blk > tb) | ((blk == tb) & ((off + lane_pos) <= cb))) & (blk > -jnp.inf)
        o_ref[0, :, pl.ds(off, kb)] = jnp.where(sel, 0.0, NEG_BIG).astype(bf16)
        return carry

    lax.fori_loop(0, nkb, out_body, 0)

    def fill_body(i, carry):
        off = pl.multiple_of(i * kb, kb)
        o_ref[0, :, pl.ds(off, kb)] = jnp.full((qb, kb), NEG_BIG, bf16)
        return carry

    lax.fori_loop(nkb, ltot // kb, fill_body, 0)


def _dsa_mask(proj, ki16, *, qb, qoff, topk):
    b, t, _ = proj.shape
    ltot = ki16.shape[1]
    idx_bits = int(math.ceil(math.log2(ltot))) + 1
    kern = functools.partial(_index_kernel, qoff=qoff, topk=topk, idx_bits=idx_bits)
    return pl.pallas_call(
        kern,
        out_shape=jax.ShapeDtypeStruct((b, t, ltot), bf16),
        grid=(b, t // qb),
        in_specs=[pl.BlockSpec((1, qb, C_QI), lambda i, j: (i, j, OFF_QI // C_QI)),
                  pl.BlockSpec((1, qb, LANES), lambda i, j: (i, j, OFF_SMALL // LANES)),
                  pl.BlockSpec((1, ltot, IDX_DIM), lambda i, j: (i, 0, 0))],
        out_specs=pl.BlockSpec((1, qb, ltot), lambda i, j: (i, j, 0)),
        scratch_shapes=[pltpu.VMEM((qb, -(-ltot // (TOP_R * LANES)) * TOP_R * LANES), f32),
                        pltpu.VMEM((TOP_R, qb, LANES), f32)],
        compiler_params=_cp(("arbitrary", "arbitrary")),
        name="dsa_index",
    )(proj, proj, ki16)


def _attn_kernel(qidx_ref, kidx_ref, flag_ref, case_ref, q_ref, *refs, has_cache):
    if has_cache:
        (ck_ref, cv_ref, kn_ref, vn_ref, mb_ref, bias_ref, o_ref,
         q_scr, m_scr, l_scr, acc_scr, s_scr, p_scr, a_scr, mb_scr, k_ref, v_ref) = refs
    else:
        (k_ref, v_ref, mb_ref, bias_ref, o_ref,
         q_scr, m_scr, l_scr, acc_scr, s_scr, p_scr, a_scr, mb_scr) = refs
    s_id = pl.program_id(1)
    qb = q_ref.shape[1]
    kb = k_ref.shape[1]
    rows = C_GROUP * qb
    nt = kb // LANES
    flags = flag_ref[s_id]

    if has_cache:
        @pl.when((flags & 8) != 0)
        def _():
            k_ref[0] = kn_ref[0]
            v_ref[0] = vn_ref[0]

        @pl.when((flags & 8) == 0)
        def _():
            k_ref[0] = ck_ref[0, 0].reshape(kb, C_KV).astype(bf16)
            v_ref[0] = cv_ref[0, 0].reshape(kb, C_KV).astype(bf16)

    @pl.when((flags & 1) != 0)
    def _():
        qv = (q_ref[0] * (C_HEAD_DIM ** -0.5 * LOG2E)).astype(bf16)
        for g in range(C_KV_HEADS):
            for r in range(C_GROUP):
                hd = g * C_GROUP + r
                q_scr[g, r * qb:(r + 1) * qb, :] = qv[:, hd * C_HEAD_DIM:(hd + 1) * C_HEAD_DIM]
        m_scr[...] = jnp.full(m_scr.shape, NEG_BIG, f32)
        l_scr[...] = jnp.zeros_like(l_scr)
        acc_scr[...] = jnp.zeros_like(acc_scr)

    def visit(with_bias):
        kblk = k_ref[0]
        vblk = v_ref[0]
        mb_scr[...] = mb_ref[0].astype(f32)
        units = [(g, r) for g in range(C_KV_HEADS) for r in range(C_GROUP)]

        def qk(u):
            g, r = units[u]
            hs = slice(r * qb, (r + 1) * qb)
            s_scr[g, hs, :] = _dot_nt(q_scr[g, hs, :], kblk[:, g * C_HEAD_DIM:(g + 1) * C_HEAD_DIM])

        def soft(u):
            g, r = units[u]
            for c in range(qb // ATT_RC):
                q0 = c * ATT_RC
                rs = slice(r * qb + q0, r * qb + q0 + ATT_RC)
                s = s_scr[g, rs, :] + mb_scr[q0:q0 + ATT_RC, :]
                if with_bias:
                    s = s + bias_ref[0, g * C_GROUP + r, q0:q0 + ATT_RC, :]
                t = s[:, 0:LANES]
                for i in range(1, nt):
                    t = jnp.maximum(t, s[:, i * LANES:(i + 1) * LANES])
                m_old = m_scr[g, rs, :]
                m_new = jnp.maximum(m_old, jnp.max(t, axis=1, keepdims=True))
                alpha = jnp.exp2(m_old - m_new)
                p = jnp.exp2(s - jnp.concatenate([m_new] * nt, axis=1))
                psum = p[:, 0:LANES]
                for i in range(1, nt):
                    psum = psum + p[:, i * LANES:(i + 1) * LANES]
                l_scr[g, rs, :] = alpha * l_scr[g, rs, :] + psum
                m_scr[g, rs, :] = m_new
                a_scr[g, rs, :] = alpha
                p_scr[g, rs, :] = p.astype(bf16)

        def pv(u):
            g, r = units[u]
            hs = slice(r * qb, (r + 1) * qb)
            acc_scr[g, hs, :] = a_scr[g, hs, :] * acc_scr[g, hs, :] + _dot(
                p_scr[g, hs, :], vblk[:, g * C_HEAD_DIM:(g + 1) * C_HEAD_DIM])

        qk(0)
        for u in range(len(units)):
            if u + 1 < len(units):
                qk(u + 1)
            soft(u)
            pv(u)

    @pl.when((flags & 4) != 0)
    def _():
        visit(True)

    @pl.when((flags & 4) == 0)
    def _():
        visit(False)

    @pl.when((flags & 2) != 0)
    def _():
        for g in range(C_KV_HEADS):
            o = acc_scr[g] / jnp.sum(l_scr[g], axis=1, keepdims=True)
            for r in range(C_GROUP):
                hd = g * C_GROUP + r
                o_ref[0, :, hd * C_HEAD_DIM:(hd + 1) * C_HEAD_DIM] = o[r * qb:(r + 1) * qb, :].astype(bf16)


def _attn_schedule(nq, qb, qoff, offsets):
    qidx, kidx, flags, case = [], [], [], []
    zero_case = len(offsets)
    for j in range(nq):
        kmax = qoff + (j + 1) * qb
        nkb = -(-kmax // ATT_KB)
        for kb in range(nkb):
            off = kmax - BIAS_WIN - kb * ATT_KB
            qidx.append(j)
            kidx.append(kb)
            near = off in offsets
            flags.append((1 if kb == 0 else 0) | (2 if kb == nkb - 1 else 0) | (4 if near else 0)
                         | (8 if kb * ATT_KB >= qoff else 0))
            case.append(offsets.index(off) if near else zero_case)
            assert near == (-BIAS_WIN < off < ATT_KB)
    return [jnp.asarray(np.asarray(a, np.int32)) for a in (qidx, kidx, flags, case)]


def _dsa_attend(proj, k16, v16, maskbias, tables, *, qb, qoff, offsets, cache=None):
    b, t, _ = proj.shape
    nq = t // qb
    qidx, kidx, flags, case = _attn_schedule(nq, qb, qoff, offsets)
    ns = int(qidx.shape[0])
    rows = C_GROUP * qb
    kv_block = lambda f: pl.BlockSpec((1, ATT_KB, C_KV), f)
    if cache is None:
        kv_specs = [kv_block(lambda i, s, qi, ki, fl, ca: (i, ki[s], 0))] * 2
        kv_args = (k16, v16)
        kv_scratch = []
    else:
        layer = cache[2]
        ncache = cache[0].shape[2] // ATT_KB
        assert cache[0].shape[2] == qoff and qoff % ATT_KB == 0 and nq == 1 and k16.shape[1] == ATT_KB
        cache_block = pl.BlockSpec((1, 1, ATT_KB, C_KV_HEADS, C_HEAD_DIM),
                                   lambda i, s, qi, ki, fl, ca: (layer, i, jnp.minimum(ki[s], ncache - 1), 0, 0))
        kv_specs = [cache_block] * 2 + [kv_block(lambda i, s, qi, ki, fl, ca: (i, 0, 0))] * 2
        kv_args = (cache[0], cache[1], k16, v16)
        kv_scratch = [pltpu.VMEM((1, ATT_KB, C_KV), bf16)] * 2
    grid_spec = pltpu.PrefetchScalarGridSpec(
        num_scalar_prefetch=4,
        grid=(b, ns),
        in_specs=([pl.BlockSpec((1, qb, C_Q), lambda i, s, qi, ki, fl, ca: (i, qi[s], OFF_Q // C_Q))] + kv_specs
                  + [pl.BlockSpec((1, qb, ATT_KB), lambda i, s, qi, ki, fl, ca: (i, qi[s], ki[s])),
                     pl.BlockSpec((1, C_HEADS, qb, ATT_KB), lambda i, s, qi, ki, fl, ca: (ca[s], 0, 0, 0))]),
        out_specs=pl.BlockSpec((1, qb, C_WIDTH), lambda i, s, qi, ki, fl, ca: (i, qi[s], 0)),
        scratch_shapes=[pltpu.VMEM((C_KV_HEADS, rows, C_HEAD_DIM), bf16),
                        pltpu.VMEM((C_KV_HEADS, rows, LANES), f32),
                        pltpu.VMEM((C_KV_HEADS, rows, LANES), f32),
                        pltpu.VMEM((C_KV_HEADS, rows, C_HEAD_DIM), f32),
                        pltpu.VMEM((C_KV_HEADS, rows, ATT_KB), f32),
                        pltpu.VMEM((C_KV_HEADS, rows, ATT_KB), bf16),
                        pltpu.VMEM((C_KV_HEADS, rows, LANES), f32),
                        pltpu.VMEM((qb, ATT_KB), f32)] + kv_scratch)
    return pl.pallas_call(
        functools.partial(_attn_kernel, has_cache=cache is not None),
        out_shape=jax.ShapeDtypeStruct((b, t, C_WIDTH), bf16),
        grid_spec=grid_spec,
        compiler_params=_cp(("arbitrary", "arbitrary")),
        name="dsa_attend",
    )(qidx, kidx, flags, case, proj, *kv_args, maskbias, tables)


def _window_offsets(nq, qb, qoff):
    offs = set()
    for j in range(nq):
        kmax = qoff + (j + 1) * qb
        for kb in range(-(-kmax // ATT_KB)):
            off = kmax - BIAS_WIN - kb * ATT_KB
            if -BIAS_WIN < off < ATT_KB:
                offs.add(off)
    return sorted(offs)


def _mixer_c(proj, k_all16, v_all16, ki_all16, rel_bias, tables_memo, *, qb, qoff, true_len, cache=None):
    b, t, _ = proj.shape
    topk = min(TOPK_MAX, true_len // 4)
    assert BIAS_WIN - qb + 1 > MAX_DISTANCE
    offsets = _window_offsets(t // qb, qb, qoff)
    memo_key = (qb, qoff, t)
    if memo_key not in tables_memo:
        tables_memo[memo_key] = _bias_tables(rel_bias, qb, offsets)
    tables = tables_memo[memo_key]
    idx_qb = IDX_QB if t % IDX_QB == 0 else qb
    maskbias = _dsa_mask(proj, ki_all16, qb=idx_qb, qoff=qoff, topk=topk)
    return _dsa_attend(proj, k_all16, v_all16, maskbias, tables, qb=qb, qoff=qoff, offsets=offsets, cache=cache)


def _merge_kernel(x_ref, ada_ref, gt_ref, oa_ref, ob_ref, oc_ref, wa_ref, wb_ref, wc_ref, wo_ref, o_ref):
    bt, tt, d = x_ref.shape
    rows = bt * tt
    gates = jax.nn.sigmoid(gt_ref[...].reshape(rows, GATE_COLS))
    merged = (gates[:, 0:d] * _dot(oa_ref[...].reshape(rows, A_WIDTH), wa_ref[...])
              + gates[:, d:2 * d] * _dot(ob_ref[...].reshape(rows, SSM_INNER), wb_ref[...])
              + gates[:, 2 * d:3 * d] * _dot(oc_ref[...].reshape(rows, C_WIDTH), wc_ref[...]))
    mix = _dot(merged.astype(bf16), wo_ref[...]).reshape(bt, tt, d)
    o_ref[...] = x_ref[...] + ada_ref[...][:, 5:6, :] * mix


def _merge(x, ada, proj, o_a, o_b, o_c, wa16, wb16, wc16, wo16):
    b, t, d = x.shape
    bt, tt = _token_tile(b, t, rows=256)
    tok = lambda w: pl.BlockSpec((bt, tt, w), lambda i, j: (i, j, 0))
    full = lambda r, c: pl.BlockSpec((r, c), lambda i, j: (0, 0))
    return pl.pallas_call(
        _merge_kernel,
        out_shape=jax.ShapeDtypeStruct((b, t, d), f32),
        grid=(b // bt, t // tt),
        in_specs=[tok(d), pl.BlockSpec((bt, N_ADA, d), lambda i, j: (i, 0, 0)),
                  pl.BlockSpec((bt, tt, GATE_COLS), lambda i, j: (i, j, OFF_GATE // GATE_COLS)),
                  tok(A_WIDTH), tok(SSM_INNER), tok(C_WIDTH),
                  full(A_WIDTH, d), full(SSM_INNER, d), full(C_WIDTH, d), full(d, d)],
        out_specs=tok(d),
        compiler_params=_cp(("arbitrary", "arbitrary")),
        name="merge",
    )(x, ada, proj, o_a, o_b, o_c, wa16, wb16, wc16, wo16)


def _pad_keys(x, mult):
    pad = (-x.shape[1]) % mult
    if pad:
        x = jnp.concatenate([x, jnp.zeros((x.shape[0], pad) + x.shape[2:], x.dtype)], axis=1)
    return x


def _layer(x, ada, lp, rel_bias, tables_memo, past, g_final, last, layer):
    b, t, d = x.shape
    x = _ffn(x, ada, lp['g_ffn1'], lp['w_ffn1_in'], lp['w_ffn1_out'], g_final, k0=0, final_norm=False)
    proj, k16, v16, ki16, k_new, v_new, ki_new = _inproj(x, ada, lp['g_mix'], lp['w_proj'], k0=3)
    new_conv = proj[:, t - (CONV_W - 1):, OFF_XBC:OFF_XBC + CONV_DIM]

    o_a, va = _mixer_a(proj, lp['g_a'], lp['w_s'], lp['b_s'])

    if past is None:
        conv_init = jnp.zeros((b, CONV_W - 1, CONV_DIM), f32)
        h0 = jnp.zeros((b, SSM_HEADS, SSM_HEAD_DIM, SSM_STATE), f32)
    else:
        h0, conv_init = past[3], past[4]
    o_b, new_ssm = _mixer_b(proj, conv_init, h0, lp['w_conv'], lp['b_conv'], lp['dt_bias'], lp['a_log'],
                            lp['d_skip'], lp['g_ssm'])
    new_ssm = new_ssm.reshape(b, SSM_HEADS, SSM_HEAD_DIM, SSM_STATE)

    if past is None:
        k_all, v_all, ki_all, cache = k16, v16, ki16, None
        qb, qoff, true_len = (ATT_QB if t % ATT_QB == 0 else Q_BLOCK), 0, t
    else:
        plen = past[0].shape[2]
        if plen % ATT_KB == 0:
            k_all, v_all, cache = k16, v16, (past[0], past[1], layer)
        else:
            k_all = jnp.concatenate([past[0][layer].reshape(b, plen, C_KV).astype(bf16), k16], axis=1)
            v_all = jnp.concatenate([past[1][layer].reshape(b, plen, C_KV).astype(bf16), v16], axis=1)
            cache = None
        ki_all = jnp.concatenate([past[2].astype(bf16), ki16], axis=1)
        qb, qoff, true_len = t, plen, plen + t
    o_c = _mixer_c(proj, _pad_keys(k_all, ATT_KB), _pad_keys(v_all, ATT_KB), _pad_keys(ki_all, ATT_KB), rel_bias, tables_memo,
                   qb=qb, qoff=qoff, true_len=true_len, cache=cache)

    x = _merge(x, ada, proj, o_a, o_b, o_c, lp['w_br_a'], lp['w_br_b'], lp['w_br_c'], lp['w_out'])
    x = _ffn(x, ada, lp['g_ffn2'], lp['w_ffn2_in'], lp['w_ffn2_out'], g_final, k0=6, final_norm=last)
    states = (k_new, v_new, ki_new, new_ssm, new_conv, va)
    return x, states


def kernel(x_prompt, x_sample, c_prompt, c_sample, cache_k, cache_v, cache_kidx, state_ssm, state_conv, rel_bias, w_ada, b_ada, g_ffn1, w_ffn1_in, w_ffn1_out, g_mix, w_in, g_a, w_s, b_s, w_conv, b_conv, dt_bias, a_log, d_skip, g_ssm, w_br_a, w_br_b, w_br_c, w_out, g_ffn2, w_ffn2_in, w_ffn2_out, g_final):
    depth = w_ada.shape[0]
    bp, bs = c_prompt.shape[0], c_sample.shape[0]
    assert x_prompt.shape[1] % ATT_KB == 0 and x_prompt.shape[1] % Q_BLOCK == 0
    assert x_sample.shape[1] == CHUNK

    rows = -(-(bp + bs) // 8) * 8
    c_all = jnp.concatenate([c_prompt, c_sample, jnp.zeros((rows - bp - bs, D_MODEL), f32)], axis=0)
    ada_all = _ada_all(c_all, w_ada.astype(bf16), b_ada)
    ada_all = ada_all.reshape(depth, rows, N_ADA, D_MODEL)

    yp, ys = x_prompt, x_sample
    tables_memo = {}
    st_p, st_s = [], []
    for i in range(depth):
        lp = dict(g_ffn1=g_ffn1[i], w_ffn1_in=w_ffn1_in[i].astype(bf16), w_ffn1_out=w_ffn1_out[i].astype(bf16),
                  g_mix=g_mix[i], w_proj=_layout_w_in(w_in[i]).astype(bf16), g_a=g_a[i], w_s=w_s[i], b_s=b_s[i],
                  w_conv=w_conv[i], b_conv=b_conv[i], dt_bias=dt_bias[i], a_log=a_log[i], d_skip=d_skip[i],
                  g_ssm=g_ssm[i], w_br_a=w_br_a[i].astype(bf16), w_br_b=w_br_b[i].astype(bf16),
                  w_br_c=w_br_c[i].astype(bf16), w_out=w_out[i].astype(bf16), g_ffn2=g_ffn2[i],
                  w_ffn2_in=w_ffn2_in[i].astype(bf16), w_ffn2_out=w_ffn2_out[i].astype(bf16))
        last = i == depth - 1
        yp, sp = _layer(yp, ada_all[i, :bp], lp, rel_bias, tables_memo, None, g_final, last, i)
        ys, ss = _layer(ys, ada_all[i, bp:bp + bs], lp, rel_bias, tables_memo,
                        (cache_k, cache_v, cache_kidx[i], state_ssm[i], state_conv[i]), g_final, last, i)
        st_p.append(sp)
        st_s.append(ss)

    stack = lambda sts, k: jnp.stack([s[k] for s in sts])
    return (yp, ys,
            stack(st_p, 0), stack(st_p, 1), stack(st_p, 2), stack(st_p, 3), stack(st_p, 4),
            stack(st_s, 0), stack(st_s, 1), stack(st_s, 2), stack(st_s, 3), stack(st_s, 4), stack(st_s, 5))
```

```python
import functools
import math

import numpy as np
import jax
import jax.numpy as jnp
from jax import lax
from jax.experimental import pallas as pl
from jax.experimental.pallas import tpu as pltpu

f32 = jnp.float32
bf16 = jnp.bfloat16
i32 = jnp.int32

D_MODEL = 1024
CHUNK = 64
EPS = 1e-6
N_ADA = 9
D_FF = 2816
A_WIDTH = D_MODEL
A_GROUPS = 8
A_GROUP_DIM = A_WIDTH // A_GROUPS
A_CHUNK = 128
SSM_INNER = 2 * D_MODEL
SSM_HEAD_DIM = 64
SSM_HEADS = SSM_INNER // SSM_HEAD_DIM
SSM_GROUPS = 4
SSM_HPG = SSM_HEADS // SSM_GROUPS
SSM_STATE = 128
CONV_W = 4
CONV_DIM = SSM_INNER + 2 * SSM_GROUPS * SSM_STATE
C_HEADS = 8
C_KV_HEADS = 4
C_GROUP = C_HEADS // C_KV_HEADS
C_HEAD_DIM = 128
C_WIDTH = C_HEADS * C_HEAD_DIM
IDX_HEADS = 8
IDX_DIM = 64
TOPK_MAX = 256
Q_BLOCK = 128
N_BUCKETS = 32
MAX_DISTANCE = 128
A_COLS = 2 * A_WIDTH
C_Q = C_HEADS * C_HEAD_DIM
C_KV = C_KV_HEADS * C_HEAD_DIM
C_QI = IDX_HEADS * IDX_DIM
GATE_COLS = 3 * D_MODEL

LANES = 128
SUBLANES = 8
VMEM_LIMIT = 56 * 1024 * 1024

OFF_GATE = 0
OFF_XBC = OFF_GATE + GATE_COLS
OFF_PA = OFF_XBC + CONV_DIM
OFF_Z = OFF_PA + A_COLS
OFF_Q = OFF_Z + SSM_INNER
OFF_K = OFF_Q + C_Q
OFF_V = OFF_K + C_KV
OFF_QI = OFF_V + C_KV
OFF_SMALL = OFF_QI + C_QI
SMALL_WI = SSM_HEADS
SMALL_KI = LANES - IDX_DIM
PROJ_TILE = 1024
PROJ_W = ((OFF_SMALL + LANES + PROJ_TILE - 1) // PROJ_TILE) * PROJ_TILE

NEG_BIG = -1e30
SUBNORMAL_SPAN = 0x007FFFFF
NEG_INF_KEY = -2139095041 + SUBNORMAL_SPAN + 1
INT_MAX = 2147483647
ATT_KB = 1024
IDX_KB = 512
TOP_R = 16
IDX_QB = 256
IDX_RS = 32
ATT_RC = 32
LOG2E = math.log2(math.e)
ATT_QB = 256
BIAS_WIN = ATT_QB + MAX_DISTANCE
FAR_BUCKET = N_BUCKETS // 2 - 1


def _cp(sem, vmem=VMEM_LIMIT):
    return pltpu.CompilerParams(dimension_semantics=sem, vmem_limit_bytes=vmem)


def _rms(x, g):
    return x * lax.rsqrt(jnp.mean(x * x, axis=-1, keepdims=True) + EPS) * g


def _dot(a, b):
    return jnp.dot(a, b, preferred_element_type=f32)


def _dot_nt(a, b):
    return lax.dot_general(a, b, (((1,), (1,)), ((), ())), preferred_element_type=f32)


def _dot_tn(a, b):
    return lax.dot_general(a, b, (((0,), (0,)), ((), ())), preferred_element_type=f32)


def _bf16_pieces(x, pieces):
    out = []
    for _ in range(pieces):
        p = x.astype(bf16)
        out.append(p)
        x = x - p.astype(f32)
    return out


def _dot_zero_one_lhs(sel16, x, pieces):
    parts = [_dot(sel16, p) for p in _bf16_pieces(x, pieces)]
    return functools.reduce(lambda a, b: a + b, parts)


def _dot_zero_one_rhs(x, sel16, pieces):
    parts = [_dot(p, sel16) for p in _bf16_pieces(x, pieces)]
    return functools.reduce(lambda a, b: a + b, parts)


def _ada_kernel(c_ref, w_ref, b_ref, o_ref):
    s = jax.nn.silu(c_ref[...]).astype(bf16)
    o_ref[0] = _dot(s, w_ref[0]) + b_ref[0]


def _ada_all(c_all, w_ada16, b_ada):
    depth = w_ada16.shape[0]
    rows = c_all.shape[0]
    n_out = N_ADA * D_MODEL
    tn = 9 * LANES
    return pl.pallas_call(
        _ada_kernel,
        out_shape=jax.ShapeDtypeStruct((depth, rows, n_out), f32),
        grid=(depth, n_out // tn),
        in_specs=[pl.BlockSpec((rows, D_MODEL), lambda d, n: (0, 0)),
                  pl.BlockSpec((1, D_MODEL, tn), lambda d, n: (d, 0, n)),
                  pl.BlockSpec((1, 1, tn), lambda d, n: (d, 0, n))],
        out_specs=pl.BlockSpec((1, rows, tn), lambda d, n: (d, 0, n)),
        compiler_params=_cp(("arbitrary", "arbitrary")),
        name="ada",
    )(c_all, w_ada16, b_ada.reshape(depth, 1, n_out))


def _ffn_kernel(x_ref, ada_ref, g_ref, wg_ref, wu_ref, wo_ref, gf_ref, o_ref, h_scr, acc_scr, *, k0, final_norm):
    f = pl.program_id(2)
    bt, tt, d = x_ref.shape
    rows = bt * tt

    @pl.when(f == 0)
    def _():
        ada = ada_ref[...]
        h = _rms(x_ref[...], g_ref[...]) * (1.0 + ada[:, k0 + 1:k0 + 2, :]) + ada[:, k0:k0 + 1, :]
        h_scr[...] = h.reshape(rows, d).astype(bf16)
        acc_scr[...] = jnp.zeros_like(acc_scr)

    h = h_scr[...]
    gate = _dot(h, wg_ref[...])
    up = _dot(h, wu_ref[...])
    a = (jax.nn.silu(gate) * up).astype(bf16)
    acc_scr[...] += _dot(a, wo_ref[...])

    @pl.when(f == pl.num_programs(2) - 1)
    def _():
        ada = ada_ref[...]
        y = x_ref[...] + (0.5 * ada[:, k0 + 2:k0 + 3, :]) * acc_scr[...].reshape(bt, tt, d)
        if final_norm:
            y = _rms(y, gf_ref[...])
        o_ref[...] = y


def _token_tile(b, t, rows=512):
    if t >= rows:
        return 1, rows
    return min(b, rows // t), t


def _ffn(x, ada, g, w_in16, w_out16, g_final, *, k0, final_norm):
    b, t, d = x.shape
    bt, tt = _token_tile(b, t)
    tf = D_FF // 2
    nf = D_FF // tf
    kern = functools.partial(_ffn_kernel, k0=k0, final_norm=final_norm)
    return pl.pallas_call(
        kern,
        out_shape=jax.ShapeDtypeStruct((b, t, d), f32),
        grid=(b // bt, t // tt, nf),
        in_specs=[pl.BlockSpec((bt, tt, d), lambda i, j, f: (i, j, 0)),
                  pl.BlockSpec((bt, N_ADA, d), lambda i, j, f: (i, 0, 0)),
                  pl.BlockSpec((1, d), lambda i, j, f: (0, 0)),
                  pl.BlockSpec((d, tf), lambda i, j, f: (0, f)),
                  pl.BlockSpec((d, tf), lambda i, j, f: (0, f + nf)),
                  pl.BlockSpec((tf, d), lambda i, j, f: (f, 0)),
                  pl.BlockSpec((1, d), lambda i, j, f: (0, 0))],
        out_specs=pl.BlockSpec((bt, tt, d), lambda i, j, f: (i, j, 0)),
        scratch_shapes=[pltpu.VMEM((bt * tt, d), bf16), pltpu.VMEM((bt * tt, d), f32)],
        compiler_params=_cp(("arbitrary", "arbitrary", "arbitrary")),
        name="ffn",
    )(x, ada, g.reshape(1, d), w_in16, w_in16, w_out16, g_final.reshape(1, d))


def _inproj_kernel(x_ref, ada_ref, g_ref, w_ref, o_ref, k16_ref, v16_ref, ki16_ref, k32_ref, v32_ref, ki32_ref, h_scr, *, k0):
    n = pl.program_id(2)
    bt, tt, d = x_ref.shape

    @pl.when(n == 0)
    def _():
        ada = ada_ref[...]
        h = _rms(x_ref[...], g_ref[...]) * (1.0 + ada[:, k0 + 1:k0 + 2, :]) + ada[:, k0:k0 + 1, :]
        h_scr[...] = h.reshape(bt * tt, d).astype(bf16)

    res = _dot(h_scr[...], w_ref[...])
    o_ref[...] = res.reshape(bt, tt, PROJ_TILE)

    @pl.when(n == OFF_K // PROJ_TILE)
    def _():
        k = res[:, :C_KV]
        v = res[:, C_KV:2 * C_KV]
        k16_ref[...] = k.astype(bf16).reshape(bt, tt, C_KV)
        v16_ref[...] = v.astype(bf16).reshape(bt, tt, C_KV)
        k32_ref[...] = k.reshape(bt, tt, C_KV_HEADS, C_HEAD_DIM)
        v32_ref[...] = v.reshape(bt, tt, C_KV_HEADS, C_HEAD_DIM)

    @pl.when(n == OFF_SMALL // PROJ_TILE)
    def _():
        lo = OFF_SMALL % PROJ_TILE + SMALL_KI
        ki = res[:, lo:lo + IDX_DIM]
        ki16_ref[...] = ki.astype(bf16).reshape(bt, tt, IDX_DIM)
        ki32_ref[...] = ki.reshape(bt, tt, IDX_DIM)


def _inproj(x, ada, g, w_proj16, *, k0):
    b, t, d = x.shape
    bt, tt = _token_tile(b, t, rows=1024)
    assert OFF_K % PROJ_TILE == 0 and OFF_V == OFF_K + C_KV and 2 * C_KV <= PROJ_TILE
    tok = lambda w: pl.BlockSpec((bt, tt, w), lambda i, j, n: (i, j, 0))
    heads = pl.BlockSpec((bt, tt, C_KV_HEADS, C_HEAD_DIM), lambda i, j, n: (i, j, 0, 0))
    kv32 = jax.ShapeDtypeStruct((b, t, C_KV_HEADS, C_HEAD_DIM), f32)
    return pl.pallas_call(
        functools.partial(_inproj_kernel, k0=k0),
        out_shape=(jax.ShapeDtypeStruct((b, t, PROJ_W), f32), jax.ShapeDtypeStruct((b, t, C_KV), bf16),
                   jax.ShapeDtypeStruct((b, t, C_KV), bf16), jax.ShapeDtypeStruct((b, t, IDX_DIM), bf16),
                   kv32, kv32, jax.ShapeDtypeStruct((b, t, IDX_DIM), f32)),
        grid=(b // bt, t // tt, PROJ_W // PROJ_TILE),
        in_specs=[pl.BlockSpec((bt, tt, d), lambda i, j, n: (i, j, 0)),
                  pl.BlockSpec((bt, N_ADA, d), lambda i, j, n: (i, 0, 0)),
                  pl.BlockSpec((1, d), lambda i, j, n: (0, 0)),
                  pl.BlockSpec((d, PROJ_TILE), lambda i, j, n: (0, n))],
        out_specs=(pl.BlockSpec((bt, tt, PROJ_TILE), lambda i, j, n: (i, j, n)), tok(C_KV), tok(C_KV), tok(IDX_DIM),
                   heads, heads, tok(IDX_DIM)),
        scratch_shapes=[pltpu.VMEM((bt * tt, d), bf16)],
        compiler_params=_cp(("arbitrary", "arbitrary", "arbitrary")),
        name="inproj",
    )(x, ada, g.reshape(1, d), w_proj16)


def _layout_w_in(w_in):
    d = w_in.shape[0]
    o = 0
    pa = w_in[:, o:o + A_COLS]; o += A_COLS
    z = w_in[:, o:o + SSM_INNER]; o += SSM_INNER
    xbc = w_in[:, o:o + CONV_DIM]; o += CONV_DIM
    dt = w_in[:, o:o + SSM_HEADS]; o += SSM_HEADS
    q = w_in[:, o:o + C_Q]; o += C_Q
    k = w_in[:, o:o + C_KV]; o += C_KV
    v = w_in[:, o:o + C_KV]; o += C_KV
    qi = w_in[:, o:o + C_QI]; o += C_QI
    ki = w_in[:, o:o + IDX_DIM]; o += IDX_DIM
    wi = w_in[:, o:o + IDX_HEADS]; o += IDX_HEADS
    gates = w_in[:, o:o + GATE_COLS]
    small = jnp.concatenate([dt, wi, jnp.zeros((d, SMALL_KI - SMALL_WI - IDX_HEADS), w_in.dtype), ki], axis=1)
    pad = jnp.zeros((d, PROJ_W - OFF_SMALL - LANES), w_in.dtype)
    return jnp.concatenate([gates, xbc, pa, z, q, k, v, qi, small, pad], axis=1)


def _mixa_kernel(pa_ref, ga_ref, ws_ref, bs_ref, oa_ref, va_ref):
    n = pa_ref.shape[1]
    ge = jax.nn.gelu(pa_ref[0])
    u = ge[:, :A_WIDTH]
    v = _rms(ge[:, A_WIDTH:], ga_ref[...])
    va_ref[0] = v
    vb = v.astype(bf16)
    tri = lax.broadcasted_iota(i32, (n, n), 0) >= lax.broadcasted_iota(i32, (n, n), 1)
    bs = bs_ref[...]
    for g in range(A_GROUPS):
        sl = slice(g * A_GROUP_DIM, (g + 1) * A_GROUP_DIM)
        w = jnp.where(tri, ws_ref[g], 0.0).astype(bf16)
        mixed = _dot(w, vb[:, sl]) + bs[:, g:g + 1]
        oa_ref[0, :, sl] = (u[:, sl] * mixed).astype(bf16)


def _mixer_a(proj, g_a, w_s, b_s):
    b, t, _ = proj.shape
    n = min(t, A_CHUNK)
    ws = w_s[:, :n, :n]
    bs_t = b_s[:, :n].T
    return pl.pallas_call(
        _mixa_kernel,
        out_shape=(jax.ShapeDtypeStruct((b, t, A_WIDTH), bf16), jax.ShapeDtypeStruct((b, t, A_WIDTH), f32)),
        grid=(b, t // n),
        in_specs=[pl.BlockSpec((1, n, A_COLS), lambda i, j: (i, j, OFF_PA // A_COLS)),
                  pl.BlockSpec((1, A_WIDTH), lambda i, j: (0, 0)),
                  pl.BlockSpec((A_GROUPS, n, n), lambda i, j: (0, 0, 0)),
                  pl.BlockSpec((n, A_GROUPS), lambda i, j: (0, 0))],
        out_specs=(pl.BlockSpec((1, n, A_WIDTH), lambda i, j: (i, j, 0)),
                   pl.BlockSpec((1, n, A_WIDTH), lambda i, j: (i, j, 0))),
        compiler_params=_cp(("arbitrary", "arbitrary")),
        name="mixer_a",
    )(proj, g_a.reshape(1, A_WIDTH), ws, bs_t)


CONV_PAD = 8


def _ssd_kernel(z_ref, xbc_ref, sm_ref, cinit_ref, h0_ref, wc_ref, bc_ref, dtb_ref, alog_ref, dsk_ref, gs_ref,
                ex_ref, ob_ref, hs_ref, xpad, ht):
    c = pl.program_id(1)
    q = z_ref.shape[1]
    gw = SSM_HPG * SSM_HEAD_DIM
    tail = CONV_W - 1

    @pl.when(c == 0)
    def _():
        xpad[CONV_PAD - tail:CONV_PAD, :] = cinit_ref[0]
        for g in range(SSM_GROUPS):
            ht[g] = h0_ref[0, g * gw:(g + 1) * gw, :].T

    xpad[CONV_PAD:CONV_PAD + q, :] = xbc_ref[0]
    wc = wc_ref[...]
    conv = bc_ref[...] + wc[0:1, :] * xpad[pl.ds(CONV_PAD - 3, q), :]
    for k in range(1, CONV_W):
        conv = conv + wc[k:k + 1, :] * xpad[pl.ds(CONV_PAD - 3 + k, q), :]
    nxt = xpad[CONV_PAD + q - tail:CONV_PAD + q, :]
    xpad[CONV_PAD - tail:CONV_PAD, :] = nxt
    conv = jax.nn.silu(conv)
    xs = conv[:, :SSM_INNER]
    bm = conv[:, SSM_INNER:SSM_INNER + SSM_GROUPS * SSM_STATE]
    cm = conv[:, SSM_INNER + SSM_GROUPS * SSM_STATE:]

    lane = lax.broadcasted_iota(i32, (q, LANES), 1)
    head_lane = lane < SSM_HEADS
    dt = jnp.where(head_lane, jax.nn.softplus(sm_ref[0] + dtb_ref[...]), 0.0)
    da = dt * (-jnp.exp(alog_ref[...]))
    tri = lax.broadcasted_iota(i32, (q, q), 0) >= lax.broadcasted_iota(i32, (q, q), 1)
    acs = _dot_zero_one_lhs(tri.astype(bf16), da, 3)
    acs_t = acs.T
    dt_t = dt.T
    acs_last = acs[q - 1:q, :]
    e_acs = jnp.where(head_lane, jnp.exp(acs), 0.0)
    decay = jnp.where(head_lane, jnp.exp(acs_last - acs) * dt, 0.0)
    ex = ex_ref[...]
    e_acs_x = _dot_zero_one_rhs(e_acs, ex, 3)
    decay_x = _dot_zero_one_rhs(decay, ex, 3)
    e_last_x = e_acs_x[q - 1:q, :]

    xs16 = xs.astype(bf16)
    xdec16 = (xs * decay_x).astype(bf16)
    y_parts = []
    for g in range(SSM_GROUPS):
        bg = bm[:, g * SSM_STATE:(g + 1) * SSM_STATE].astype(bf16)
        cg = cm[:, g * SSM_STATE:(g + 1) * SSM_STATE].astype(bf16)
        cb = _dot_nt(cg, bg)
        htg = ht[g]
        y_inter = _dot(cg, htg.astype(bf16))
        intra = []
        for r in range(SSM_HPG):
            hd = g * SSM_HPG + r
            seg = acs[:, hd:hd + 1] - acs_t[hd:hd + 1, :]
            m = cb * jnp.exp(jnp.where(tri, seg, -jnp.inf)) * dt_t[hd:hd + 1, :]
            intra.append(_dot(m.astype(bf16), xs16[:, hd * SSM_HEAD_DIM:(hd + 1) * SSM_HEAD_DIM]))
        y_parts.append(jnp.concatenate(intra, axis=1) + y_inter * e_acs_x[:, g * gw:(g + 1) * gw])
        ht[g] = htg * e_last_x[:, g * gw:(g + 1) * gw] + _dot_tn(bg, xdec16[:, g * gw:(g + 1) * gw])
    y = jnp.concatenate(y_parts, axis=1) + dsk_ref[...] * xs
    y = _rms(y * jax.nn.silu(z_ref[0]), gs_ref[...])
    ob_ref[0] = y.astype(bf16)

    @pl.when(c == pl.num_programs(1) - 1)
    def _():
        for g in range(SSM_GROUPS):
            hs_ref[0, g * gw:(g + 1) * gw, :] = ht[g].T


def _mixer_b(proj, conv_init, h0, w_conv, b_conv, dt_bias, a_log, d_skip, g_ssm):
    b, t, _ = proj.shape
    q = CHUNK
    pad_row = lambda v: jnp.zeros((1, LANES), f32).at[0, :SSM_HEADS].set(v)
    expand = (jnp.arange(LANES)[:, None] == (jnp.arange(SSM_INNER) // SSM_HEAD_DIM)[None, :]).astype(bf16)
    hrows = SSM_HEADS * SSM_HEAD_DIM
    return pl.pallas_call(
        _ssd_kernel,
        out_shape=(jax.ShapeDtypeStruct((b, t, SSM_INNER), bf16),
                   jax.ShapeDtypeStruct((b, hrows, SSM_STATE), f32)),
        grid=(b, t // q),
        in_specs=[pl.BlockSpec((1, q, SSM_INNER), lambda i, c: (i, c, OFF_Z // SSM_INNER)),
                  pl.BlockSpec((1, q, CONV_DIM), lambda i, c: (i, c, OFF_XBC // CONV_DIM)),
                  pl.BlockSpec((1, q, LANES), lambda i, c: (i, c, OFF_SMALL // LANES)),
                  pl.BlockSpec((1, CONV_W - 1, CONV_DIM), lambda i, c: (i, 0, 0)),
                  pl.BlockSpec((1, hrows, SSM_STATE), lambda i, c: (i, 0, 0)),
                  pl.BlockSpec((CONV_W, CONV_DIM), lambda i, c: (0, 0)),
                  pl.BlockSpec((1, CONV_DIM), lambda i, c: (0, 0)),
                  pl.BlockSpec((1, LANES), lambda i, c: (0, 0)),
                  pl.BlockSpec((1, LANES), lambda i, c: (0, 0)),
                  pl.BlockSpec((1, SSM_INNER), lambda i, c: (0, 0)),
                  pl.BlockSpec((1, SSM_INNER), lambda i, c: (0, 0)),
                  pl.BlockSpec((LANES, SSM_INNER), lambda i, c: (0, 0))],
        out_specs=(pl.BlockSpec((1, q, SSM_INNER), lambda i, c: (i, c, 0)),
                   pl.BlockSpec((1, hrows, SSM_STATE), lambda i, c: (i, 0, 0))),
        scratch_shapes=[pltpu.VMEM((CONV_PAD + q, CONV_DIM), f32),
                        pltpu.VMEM((SSM_GROUPS, SSM_STATE, SSM_HPG * SSM_HEAD_DIM), f32)],
        compiler_params=_cp(("arbitrary", "arbitrary")),
        name="mixer_b",
    )(proj, proj, proj, conv_init, h0.reshape(b, hrows, SSM_STATE), w_conv, b_conv.reshape(1, CONV_DIM),
      pad_row(dt_bias), pad_row(a_log), jnp.repeat(d_skip, SSM_HEAD_DIM).reshape(1, SSM_INNER),
      g_ssm.reshape(1, SSM_INNER), expand)


def _t5_bucket(rel):
    nb = N_BUCKETS // 2
    max_exact = nb // 2
    n = jnp.abs(rel)
    nf = jnp.maximum(n, 1).astype(jnp.float32)
    large = max_exact + (jnp.log(nf / max_exact) / math.log(MAX_DISTANCE / max_exact)
                         * (nb - max_exact)).astype(jnp.int32)
    large = jnp.minimum(large, nb - 1)
    return jnp.where(rel > 0, nb, 0) + jnp.where(n < max_exact, n, large)


def _bias_kernel(tab_ref, bk_ref, o_ref):
    h = pl.program_id(1)
    bk = bk_ref[0]
    far = tab_ref[FAR_BUCKET, h]
    acc = jnp.zeros(bk.shape, f32)
    for bucket in range(N_BUCKETS):
        acc = jnp.where(bk == bucket, tab_ref[bucket, h] - far, acc)
    o_ref[0, 0] = acc * LOG2E


def _bias_tables(rel_bias, qb, offsets):
    t = np.arange(qb)[:, None]
    cases = []
    for off in offsets:
        w = np.arange(ATT_KB)[None, :] - off
        valid = (w >= 0) & (w < BIAS_WIN)
        rel = w - BIAS_WIN + qb - t
        cases.append(jnp.where(jnp.asarray(valid), _t5_bucket(jnp.asarray(rel, dtype=jnp.int32)), -1))
    cases.append(jnp.full((qb, ATT_KB), -1, jnp.int32))
    buckets = jnp.stack(cases).astype(i32)
    nc = buckets.shape[0]
    return pl.pallas_call(
        _bias_kernel,
        out_shape=jax.ShapeDtypeStruct((nc, C_HEADS, qb, ATT_KB), f32),
        grid=(nc, C_HEADS),
        in_specs=[pl.BlockSpec(memory_space=pltpu.SMEM),
                  pl.BlockSpec((1, qb, ATT_KB), lambda c, h: (c, 0, 0))],
        out_specs=pl.BlockSpec((1, 1, qb, ATT_KB), lambda c, h: (c, h, 0, 0)),
        compiler_params=_cp(("arbitrary", "arbitrary")),
        name="t5_bias",
    )(rel_bias, buckets)


def _oddeven_merge_sort_pairs(n):
    pairs = []
    p = 1
    while p < n:
        k = p
        while k >= 1:
            for j in range(k % p, n - k, 2 * k):
                for i in range(min(k, n - j - k)):
                    if (i + j) // (2 * p) == (i + j + k) // (2 * p):
                        pairs.append((i + j, i + j + k))
            k //= 2
        p *= 2
    return pairs


def _compare_exchange(v, i, j):
    a, b = v[i], v[j]
    v[i] = jnp.maximum(a, b)
    v[j] = jnp.minimum(a, b)


def _sort_desc(v):
    for i, j in _oddeven_merge_sort_pairs(len(v)):
        _compare_exchange(v, i, j)


def _bitonic_merge_desc(v):
    d = len(v) // 2
    while d >= 1:
        for i in range(len(v)):
            if (i & d) == 0:
                _compare_exchange(v, i, i + d)
        d //= 2


def _index_kernel(qi_ref, sm_ref, ki_ref, o_ref, keys, top_scr, *, qoff, topk):
    j = pl.program_id(1)
    qb = qi_ref.shape[1]
    ltot = ki_ref.shape[1]
    kb = IDX_KB
    kmax = qoff + (j + 1) * qb
    nkb = lax.div(kmax + (kb - 1), kb)
    sub = kb // LANES

    qv = qi_ref[0].astype(bf16)
    rs = IDX_RS
    stacked = jnp.concatenate([qv[s * rs:(s + 1) * rs, h * IDX_DIM:(h + 1) * IDX_DIM]
                               for s in range(qb // rs) for h in range(IDX_HEADS)], axis=0)
    w = sm_ref[0][:, SMALL_WI:SMALL_WI + IDX_HEADS] * (IDX_HEADS ** -0.5) * (IDX_DIM ** -0.5)
    wcol = [jnp.broadcast_to(w[:, h:h + 1], (qb, LANES)) for h in range(IDX_HEADS)]
    qchunk = lax.shift_right_arithmetic(qoff + j * qb + lax.broadcasted_iota(i32, (qb, 1), 0), 6)
    lane_pos = lax.broadcasted_iota(i32, (1, kb), 1)

    def score_body(i, carry):
        off = pl.multiple_of(i * kb, kb)
        prod = _dot_nt(stacked, ki_ref[0, pl.ds(off, kb), :])
        kchunk = lax.shift_right_arithmetic(off + lane_pos, 6)
        for s in range(qb // rs):
            rows = slice(s * rs, (s + 1) * rs)
            acc = None
            for h in range(IDX_HEADS):
                r0 = (s * IDX_HEADS + h) * rs
                term = jnp.maximum(prod[r0:r0 + rs], 0.0) * jnp.concatenate([wcol[h][rows]] * sub, axis=1)
                acc = term if acc is None else acc + term
            adm = kchunk <= qchunk[rows]
            keys[rows, pl.ds(off, kb)] = jnp.where(adm, jnp.where(acc == 0.0, 0.0, acc), -jnp.inf)
        return carry

    lax.fori_loop(0, nkb, score_body, 0)

    group = TOP_R * LANES
    ngrp = lax.div(kmax + (group - 1), group)

    def pad_body(i, carry):
        keys[:, pl.ds(pl.multiple_of(i * kb, kb), kb)] = jnp.full((qb, kb), -jnp.inf, f32)
        return carry

    lax.fori_loop(nkb, ngrp * (group // kb), pad_body, 0)
    top_scr[...] = jnp.full(top_scr.shape, -jnp.inf, f32)

    def top_body(gidx, carry):
        base = gidx * group
        for s in range(qb // SUBLANES):
            rs = slice(s * SUBLANES, (s + 1) * SUBLANES)
            v = [keys[rs, pl.ds(pl.multiple_of(base + i * LANES, LANES), LANES)] for i in range(TOP_R)]
            _sort_desc(v)
            merged = [jnp.maximum(top_scr[i, rs, :], v[TOP_R - 1 - i]) for i in range(TOP_R)]
            _bitonic_merge_desc(merged)
            for i in range(TOP_R):
                top_scr[i, rs, :] = merged[i]
        return carry

    lax.fori_loop(0, ngrp, top_body, 0)

    def flip(bits):
        return bits ^ (lax.shift_right_arithmetic(bits, 31) & INT_MAX)

    def to_key(x):
        k = flip(lax.bitcast_convert_type(x, i32))
        return jnp.where(k > 0, k - SUBNORMAL_SPAN, jnp.where(k < 0, k + SUBNORMAL_SPAN + 1, 0))

    def to_score(key):
        k = jnp.where(key > 0, key + SUBNORMAL_SPAN, jnp.where(key < 0, key - SUBNORMAL_SPAN - 1, 0))
        return lax.bitcast_convert_type(flip(k), f32)

    def count_top(thr):
        tb = jnp.broadcast_to(to_score(thr), (qb, LANES))
        acc = (top_scr[0] >= tb).astype(i32)
        for i in range(1, TOP_R):
            acc = acc + (top_scr[i] >= tb).astype(i32)
        return jnp.sum(acc, axis=1, keepdims=True)

    def count(pred):
        def body(i, acc):
            off = pl.multiple_of(i * kb, kb)
            blk = keys[:, pl.ds(off, kb)]
            m = pred(blk, off).astype(i32)
            for c in range(sub):
                acc = acc + m[:, c * LANES:(c + 1) * LANES]
            return acc
        acc = lax.fori_loop(0, nkb, body, jnp.zeros((qb, LANES), i32))
        return jnp.sum(acc, axis=1, keepdims=True)

    def count_ge(thr):
        tb = jnp.broadcast_to(to_score(thr), (qb, kb))
        return count(lambda blk, off: blk >= tb)

    assert topk <= 2 * LANES and TOP_R >= 2
    hi0 = to_key(jnp.max(top_scr[1 if topk > LANES else 0], axis=1, keepdims=True)) + 1
    lo0 = jnp.minimum(to_key(jnp.min(top_scr[1], axis=1, keepdims=True)), hi0 - 1)

    def search(count_fn, lo_start, hi_start):
        def cond(c):
            it, open_rows, _, _ = c
            return (open_rows > 0) & (it < 40)

        def body(c):
            it, _, lo, hi = c
            mid = lax.shift_right_arithmetic(lo, 1) + lax.shift_right_arithmetic(hi, 1) + (lo & hi & 1)
            n = count_fn(mid)
            ge = n >= topk
            lo_n = jnp.where(ge, mid, lo)
            hi_n = jnp.where(n == topk, mid + 1, jnp.where(ge, hi, mid))
            open_rows = jnp.max((lo_n < hi_n - 1).astype(i32))
            return it + 1, open_rows, lo_n, hi_n

        open0 = jnp.max((lo_start < hi_start - 1).astype(i32))
        return lax.while_loop(cond, body, (jnp.int32(0), open0, lo_start, hi_start))[2]

    n_zero = count_top(jnp.zeros((qb, 1), i32))
    n_pos = count_top(jnp.ones((qb, 1), i32))
    lo1 = jnp.where(n_pos >= topk, jnp.maximum(lo0, 1), jnp.where(n_zero >= topk, 0, lo0))
    hi1 = jnp.where(n_pos >= topk, hi0, jnp.where(n_zero >= topk, 1, jnp.minimum(hi0, 0)))
    thr_c = search(count_top, lo1, hi1)
    n_ge_c = count_top(thr_c)
    smallest = to_key(jnp.max(top_scr[TOP_R - 1], axis=1, keepdims=True))
    bad = (smallest >= thr_c) & (thr_c > NEG_INF_KEY)

    def full_search(_):
        t = search(count_ge, lo0, hi0)
        return t, count_ge(t)

    thr, n_ge = lax.cond(jnp.max(bad.astype(i32)) > 0, full_search, lambda _: (thr_c, n_ge_c), 0)
    tie = (n_ge > topk) & (thr > NEG_INF_KEY)
    tb = jnp.broadcast_to(to_score(thr), (qb, kb))

    def write_mask(i, sel):
        o_ref[0, :, pl.ds(pl.multiple_of(i * kb, kb), kb)] = jnp.where(sel, 0.0, NEG_BIG).astype(bf16)

    def out_plain(_):
        def body(i, carry):
            blk = keys[:, pl.ds(pl.multiple_of(i * kb, kb), kb)]
            write_mask(i, (blk >= tb) & (blk > -jnp.inf))
            return carry
        lax.fori_loop(0, nkb, body, 0)
        return 0

    def out_tied(_):
        need = jnp.where(tie, topk - count_ge(thr + 1), INT_MAX).astype(f32)
        tri = (lax.broadcasted_iota(i32, (kb, kb), 0) <= lax.broadcasted_iota(i32, (kb, kb), 1)).astype(bf16)

        def body(i, seen):
            blk = keys[:, pl.ds(pl.multiple_of(i * kb, kb), kb)]
            eq = blk == tb
            prefix = _dot(jnp.where(eq, 1.0, 0.0).astype(bf16), tri)
            take = eq & (prefix + seen <= need)
            write_mask(i, ((blk > tb) | take) & (blk > -jnp.inf))
            return seen + prefix[:, kb - 1:kb]

        lax.fori_loop(0, nkb, body, jnp.zeros((qb, 1), f32))
        return 0

    lax.cond(jnp.max(tie.astype(i32)) > 0, out_tied, out_plain, 0)

    def fill_body(i, carry):
        off = pl.multiple_of(i * kb, kb)
        o_ref[0, :, pl.ds(off, kb)] = jnp.full((qb, kb), NEG_BIG, bf16)
        return carry

    lax.fori_loop(nkb, ltot // kb, fill_body, 0)


def _dsa_mask(proj, ki16, *, qb, qoff, topk):
    b, t, _ = proj.shape
    ltot = ki16.shape[1]
    kern = functools.partial(_index_kernel, qoff=qoff, topk=topk)
    return pl.pallas_call(
        kern,
        out_shape=jax.ShapeDtypeStruct((b, t, ltot), bf16),
        grid=(b, t // qb),
        in_specs=[pl.BlockSpec((1, qb, C_QI), lambda i, j: (i, j, OFF_QI // C_QI)),
                  pl.BlockSpec((1, qb, LANES), lambda i, j: (i, j, OFF_SMALL // LANES)),
                  pl.BlockSpec((1, ltot, IDX_DIM), lambda i, j: (i, 0, 0))],
        out_specs=pl.BlockSpec((1, qb, ltot), lambda i, j: (i, j, 0)),
        scratch_shapes=[pltpu.VMEM((qb, -(-ltot // (TOP_R * LANES)) * TOP_R * LANES), f32),
                        pltpu.VMEM((TOP_R, qb, LANES), f32)],
        compiler_params=_cp(("arbitrary", "arbitrary")),
        name="dsa_index",
    )(proj, proj, ki16)


def _attn_kernel(qidx_ref, kidx_ref, flag_ref, case_ref, q_ref, *refs, has_cache):
    if has_cache:
        (ck_ref, cv_ref, kn_ref, vn_ref, mb_ref, bias_ref, o_ref,
         q_scr, m_scr, l_scr, acc_scr, s_scr, p_scr, a_scr, mb_scr, k_ref, v_ref) = refs
    else:
        (k_ref, v_ref, mb_ref, bias_ref, o_ref,
         q_scr, m_scr, l_scr, acc_scr, s_scr, p_scr, a_scr, mb_scr) = refs
    s_id = pl.program_id(1)
    qb = q_ref.shape[1]
    kb = k_ref.shape[1]
    rows = C_GROUP * qb
    nt = kb // LANES
    flags = flag_ref[s_id]

    if has_cache:
        @pl.when((flags & 8) != 0)
        def _():
            k_ref[0] = kn_ref[0]
            v_ref[0] = vn_ref[0]

        @pl.when((flags & 8) == 0)
        def _():
            k_ref[0] = ck_ref[0, 0].reshape(kb, C_KV).astype(bf16)
            v_ref[0] = cv_ref[0, 0].reshape(kb, C_KV).astype(bf16)

    @pl.when((flags & 1) != 0)
    def _():
        qv = (q_ref[0] * (C_HEAD_DIM ** -0.5 * LOG2E)).astype(bf16)
        for g in range(C_KV_HEADS):
            for r in range(C_GROUP):
                hd = g * C_GROUP + r
                q_scr[g, r * qb:(r + 1) * qb, :] = qv[:, hd * C_HEAD_DIM:(hd + 1) * C_HEAD_DIM]
        m_scr[...] = jnp.full(m_scr.shape, NEG_BIG, f32)
        l_scr[...] = jnp.zeros_like(l_scr)
        acc_scr[...] = jnp.zeros_like(acc_scr)

    def visit(with_bias):
        kblk = k_ref[0]
        vblk = v_ref[0]
        mb_scr[...] = mb_ref[0].astype(f32)
        units = [(g, r) for g in range(C_KV_HEADS) for r in range(C_GROUP)]

        def qk(u):
            g, r = units[u]
            hs = slice(r * qb, (r + 1) * qb)
            s_scr[g, hs, :] = _dot_nt(q_scr[g, hs, :], kblk[:, g * C_HEAD_DIM:(g + 1) * C_HEAD_DIM])

        def soft(u):
            g, r = units[u]
            for c in range(qb // ATT_RC):
                q0 = c * ATT_RC
                rs = slice(r * qb + q0, r * qb + q0 + ATT_RC)
                s = s_scr[g, rs, :] + mb_scr[q0:q0 + ATT_RC, :]
                if with_bias:
                    s = s + bias_ref[0, g * C_GROUP + r, q0:q0 + ATT_RC, :]
                t = s[:, 0:LANES]
                for i in range(1, nt):
                    t = jnp.maximum(t, s[:, i * LANES:(i + 1) * LANES])
                m_old = m_scr[g, rs, :]
                m_new = jnp.maximum(m_old, jnp.max(t, axis=1, keepdims=True))
                alpha = jnp.exp2(m_old - m_new)
                p = jnp.exp2(s - jnp.concatenate([m_new] * nt, axis=1))
                psum = p[:, 0:LANES]
                for i in range(1, nt):
                    psum = psum + p[:, i * LANES:(i + 1) * LANES]
                l_scr[g, rs, :] = alpha * l_scr[g, rs, :] + psum
                m_scr[g, rs, :] = m_new
                a_scr[g, rs, :] = alpha
                p_scr[g, rs, :] = p.astype(bf16)

        def pv(u):
            g, r = units[u]
            hs = slice(r * qb, (r + 1) * qb)
            acc_scr[g, hs, :] = a_scr[g, hs, :] * acc_scr[g, hs, :] + _dot(
                p_scr[g, hs, :], vblk[:, g * C_HEAD_DIM:(g + 1) * C_HEAD_DIM])

        qk(0)
        for u in range(len(units)):
            if u + 1 < len(units):
                qk(u + 1)
            soft(u)
            pv(u)

    @pl.when((flags & 4) != 0)
    def _():
        visit(True)

    @pl.when((flags & 4) == 0)
    def _():
        visit(False)

    @pl.when((flags & 2) != 0)
    def _():
        for g in range(C_KV_HEADS):
            o = acc_scr[g] / jnp.sum(l_scr[g], axis=1, keepdims=True)
            for r in range(C_GROUP):
                hd = g * C_GROUP + r
                o_ref[0, :, hd * C_HEAD_DIM:(hd + 1) * C_HEAD_DIM] = o[r * qb:(r + 1) * qb, :].astype(bf16)


def _attn_schedule(nq, qb, qoff, offsets):
    qidx, kidx, flags, case = [], [], [], []
    zero_case = len(offsets)
    for j in range(nq):
        kmax = qoff + (j + 1) * qb
        nkb = -(-kmax // ATT_KB)
        for kb in range(nkb):
            off = kmax - BIAS_WIN - kb * ATT_KB
            qidx.append(j)
            kidx.append(kb)
            near = off in offsets
            flags.append((1 if kb == 0 else 0) | (2 if kb == nkb - 1 else 0) | (4 if near else 0)
                         | (8 if kb * ATT_KB >= qoff else 0))
            case.append(offsets.index(off) if near else zero_case)
            assert near == (-BIAS_WIN < off < ATT_KB)
    return [jnp.asarray(np.asarray(a, np.int32)) for a in (qidx, kidx, flags, case)]


def _dsa_attend(proj, k16, v16, maskbias, tables, *, qb, qoff, offsets, cache=None):
    b, t, _ = proj.shape
    nq = t // qb
    qidx, kidx, flags, case = _attn_schedule(nq, qb, qoff, offsets)
    ns = int(qidx.shape[0])
    rows = C_GROUP * qb
    kv_block = lambda f: pl.BlockSpec((1, ATT_KB, C_KV), f)
    if cache is None:
        kv_specs = [kv_block(lambda i, s, qi, ki, fl, ca: (i, ki[s], 0))] * 2
        kv_args = (k16, v16)
        kv_scratch = []
    else:
        layer = cache[2]
        ncache = cache[0].shape[2] // ATT_KB
        assert cache[0].shape[2] == qoff and qoff % ATT_KB == 0 and nq == 1 and k16.shape[1] == ATT_KB
        cache_block = pl.BlockSpec((1, 1, ATT_KB, C_KV_HEADS, C_HEAD_DIM),
                                   lambda i, s, qi, ki, fl, ca: (layer, i, jnp.minimum(ki[s], ncache - 1), 0, 0))
        kv_specs = [cache_block] * 2 + [kv_block(lambda i, s, qi, ki, fl, ca: (i, 0, 0))] * 2
        kv_args = (cache[0], cache[1], k16, v16)
        kv_scratch = [pltpu.VMEM((1, ATT_KB, C_KV), bf16)] * 2
    grid_spec = pltpu.PrefetchScalarGridSpec(
        num_scalar_prefetch=4,
        grid=(b, ns),
        in_specs=([pl.BlockSpec((1, qb, C_Q), lambda i, s, qi, ki, fl, ca: (i, qi[s], OFF_Q // C_Q))] + kv_specs
                  + [pl.BlockSpec((1, qb, ATT_KB), lambda i, s, qi, ki, fl, ca: (i, qi[s], ki[s])),
                     pl.BlockSpec((1, C_HEADS, qb, ATT_KB), lambda i, s, qi, ki, fl, ca: (ca[s], 0, 0, 0))]),
        out_specs=pl.BlockSpec((1, qb, C_WIDTH), lambda i, s, qi, ki, fl, ca: (i, qi[s], 0)),
        scratch_shapes=[pltpu.VMEM((C_KV_HEADS, rows, C_HEAD_DIM), bf16),
                        pltpu.VMEM((C_KV_HEADS, rows, LANES), f32),
                        pltpu.VMEM((C_KV_HEADS, rows, LANES), f32),
                        pltpu.VMEM((C_KV_HEADS, rows, C_HEAD_DIM), f32),
                        pltpu.VMEM((C_KV_HEADS, rows, ATT_KB), f32),
                        pltpu.VMEM((C_KV_HEADS, rows, ATT_KB), bf16),
                        pltpu.VMEM((C_KV_HEADS, rows, LANES), f32),
                        pltpu.VMEM((qb, ATT_KB), f32)] + kv_scratch)
    return pl.pallas_call(
        functools.partial(_attn_kernel, has_cache=cache is not None),
        out_shape=jax.ShapeDtypeStruct((b, t, C_WIDTH), bf16),
        grid_spec=grid_spec,
        compiler_params=_cp(("arbitrary", "arbitrary")),
        name="dsa_attend",
    )(qidx, kidx, flags, case, proj, *kv_args, maskbias, tables)


def _window_offsets(nq, qb, qoff):
    offs = set()
    for j in range(nq):
        kmax = qoff + (j + 1) * qb
        for kb in range(-(-kmax // ATT_KB)):
            off = kmax - BIAS_WIN - kb * ATT_KB
            if -BIAS_WIN < off < ATT_KB:
                offs.add(off)
    return sorted(offs)


def _mixer_c(proj, k_all16, v_all16, ki_all16, rel_bias, tables_memo, *, qb, qoff, true_len, cache=None):
    b, t, _ = proj.shape
    topk = min(TOPK_MAX, true_len // 4)
    assert BIAS_WIN - qb + 1 > MAX_DISTANCE
    offsets = _window_offsets(t // qb, qb, qoff)
    memo_key = (qb, qoff, t)
    if memo_key not in tables_memo:
        tables_memo[memo_key] = _bias_tables(rel_bias, qb, offsets)
    tables = tables_memo[memo_key]
    idx_qb = IDX_QB if t % IDX_QB == 0 else qb
    maskbias = _dsa_mask(proj, ki_all16, qb=idx_qb, qoff=qoff, topk=topk)
    return _dsa_attend(proj, k_all16, v_all16, maskbias, tables, qb=qb, qoff=qoff, offsets=offsets, cache=cache)


def _merge_kernel(x_ref, ada_ref, gt_ref, oa_ref, ob_ref, oc_ref, wa_ref, wb_ref, wc_ref, wo_ref, o_ref):
    bt, tt, d = x_ref.shape
    rows = bt * tt
    gates = jax.nn.sigmoid(gt_ref[...].reshape(rows, GATE_COLS))
    merged = (gates[:, 0:d] * _dot(oa_ref[...].reshape(rows, A_WIDTH), wa_ref[...])
              + gates[:, d:2 * d] * _dot(ob_ref[...].reshape(rows, SSM_INNER), wb_ref[...])
              + gates[:, 2 * d:3 * d] * _dot(oc_ref[...].reshape(rows, C_WIDTH), wc_ref[...]))
    mix = _dot(merged.astype(bf16), wo_ref[...]).reshape(bt, tt, d)
    o_ref[...] = x_ref[...] + ada_ref[...][:, 5:6, :] * mix


def _merge(x, ada, proj, o_a, o_b, o_c, wa16, wb16, wc16, wo16):
    b, t, d = x.shape
    bt, tt = _token_tile(b, t, rows=256)
    tok = lambda w: pl.BlockSpec((bt, tt, w), lambda i, j: (i, j, 0))
    full = lambda r, c: pl.BlockSpec((r, c), lambda i, j: (0, 0))
    return pl.pallas_call(
        _merge_kernel,
        out_shape=jax.ShapeDtypeStruct((b, t, d), f32),
        grid=(b // bt, t // tt),
        in_specs=[tok(d), pl.BlockSpec((bt, N_ADA, d), lambda i, j: (i, 0, 0)),
                  pl.BlockSpec((bt, tt, GATE_COLS), lambda i, j: (i, j, OFF_GATE // GATE_COLS)),
                  tok(A_WIDTH), tok(SSM_INNER), tok(C_WIDTH),
                  full(A_WIDTH, d), full(SSM_INNER, d), full(C_WIDTH, d), full(d, d)],
        out_specs=tok(d),
        compiler_params=_cp(("arbitrary", "arbitrary")),
        name="merge",
    )(x, ada, proj, o_a, o_b, o_c, wa16, wb16, wc16, wo16)


def _pad_keys(x, mult):
    pad = (-x.shape[1]) % mult
    if pad:
        x = jnp.concatenate([x, jnp.zeros((x.shape[0], pad) + x.shape[2:], x.dtype)], axis=1)
    return x


def _layer(x, ada, lp, rel_bias, tables_memo, past, g_final, last, layer):
    b, t, d = x.shape
    x = _ffn(x, ada, lp['g_ffn1'], lp['w_ffn1_in'], lp['w_ffn1_out'], g_final, k0=0, final_norm=False)
    proj, k16, v16, ki16, k_new, v_new, ki_new = _inproj(x, ada, lp['g_mix'], lp['w_proj'], k0=3)
    new_conv = proj[:, t - (CONV_W - 1):, OFF_XBC:OFF_XBC + CONV_DIM]

    o_a, va = _mixer_a(proj, lp['g_a'], lp['w_s'], lp['b_s'])

    if past is None:
        conv_init = jnp.zeros((b, CONV_W - 1, CONV_DIM), f32)
        h0 = jnp.zeros((b, SSM_HEADS, SSM_HEAD_DIM, SSM_STATE), f32)
    else:
        h0, conv_init = past[3], past[4]
    o_b, new_ssm = _mixer_b(proj, conv_init, h0, lp['w_conv'], lp['b_conv'], lp['dt_bias'], lp['a_log'],
                            lp['d_skip'], lp['g_ssm'])
    new_ssm = new_ssm.reshape(b, SSM_HEADS, SSM_HEAD_DIM, SSM_STATE)

    if past is None:
        k_all, v_all, ki_all, cache = k16, v16, ki16, None
        qb, qoff, true_len = (ATT_QB if t % ATT_QB == 0 else Q_BLOCK), 0, t
    else:
        plen = past[0].shape[2]
        if plen % ATT_KB == 0:
            k_all, v_all, cache = k16, v16, (past[0], past[1], layer)
        else:
            k_all = jnp.concatenate([past[0][layer].reshape(b, plen, C_KV).astype(bf16), k16], axis=1)
            v_all = jnp.concatenate([past[1][layer].reshape(b, plen, C_KV).astype(bf16), v16], axis=1)
            cache = None
        ki_all = jnp.concatenate([past[2].astype(bf16), ki16], axis=1)
        qb, qoff, true_len = t, plen, plen + t
    o_c = _mixer_c(proj, _pad_keys(k_all, ATT_KB), _pad_keys(v_all, ATT_KB), _pad_keys(ki_all, ATT_KB), rel_bias, tables_memo,
                   qb=qb, qoff=qoff, true_len=true_len, cache=cache)

    x = _merge(x, ada, proj, o_a, o_b, o_c, lp['w_br_a'], lp['w_br_b'], lp['w_br_c'], lp['w_out'])
    x = _ffn(x, ada, lp['g_ffn2'], lp['w_ffn2_in'], lp['w_ffn2_out'], g_final, k0=6, final_norm=last)
    states = (k_new, v_new, ki_new, new_ssm, new_conv, va)
    return x, states


def kernel(x_prompt, x_sample, c_prompt, c_sample, cache_k, cache_v, cache_kidx, state_ssm, state_conv, rel_bias, w_ada, b_ada, g_ffn1, w_ffn1_in, w_ffn1_out, g_mix, w_in, g_a, w_s, b_s, w_conv, b_conv, dt_bias, a_log, d_skip, g_ssm, w_br_a, w_br_b, w_br_c, w_out, g_ffn2, w_ffn2_in, w_ffn2_out, g_final):
    depth = w_ada.shape[0]
    bp, bs = c_prompt.shape[0], c_sample.shape[0]
    assert x_prompt.shape[1] % ATT_KB == 0 and x_prompt.shape[1] % Q_BLOCK == 0
    assert x_sample.shape[1] == CHUNK

    rows = -(-(bp + bs) // 8) * 8
    c_all = jnp.concatenate([c_prompt, c_sample, jnp.zeros((rows - bp - bs, D_MODEL), f32)], axis=0)
    ada_all = _ada_all(c_all, w_ada.astype(bf16), b_ada)
    ada_all = ada_all.reshape(depth, rows, N_ADA, D_MODEL)

    yp, ys = x_prompt, x_sample
    tables_memo = {}
    st_p, st_s = [], []
    for i in range(depth):
        lp = dict(g_ffn1=g_ffn1[i], w_ffn1_in=w_ffn1_in[i].astype(bf16), w_ffn1_out=w_ffn1_out[i].astype(bf16),
                  g_mix=g_mix[i], w_proj=_layout_w_in(w_in[i]).astype(bf16), g_a=g_a[i], w_s=w_s[i], b_s=b_s[i],
                  w_conv=w_conv[i], b_conv=b_conv[i], dt_bias=dt_bias[i], a_log=a_log[i], d_skip=d_skip[i],
                  g_ssm=g_ssm[i], w_br_a=w_br_a[i].astype(bf16), w_br_b=w_br_b[i].astype(bf16),
                  w_br_c=w_br_c[i].astype(bf16), w_out=w_out[i].astype(bf16), g_ffn2=g_ffn2[i],
                  w_ffn2_in=w_ffn2_in[i].astype(bf16), w_ffn2_out=w_ffn2_out[i].astype(bf16))
        last = i == depth - 1
        yp, sp = _layer(yp, ada_all[i, :bp], lp, rel_bias, tables_memo, None, g_final, last, i)
        ys, ss = _layer(ys, ada_all[i, bp:bp + bs], lp, rel_bias, tables_memo,
                        (cache_k, cache_v, cache_kidx[i], state_ssm[i], state_conv[i]), g_final, last, i)
        st_p.append(sp)
        st_s.append(ss)

    stack = lambda sts, k: jnp.stack([s[k] for s in sts])
    return (yp, ys,
            stack(st_p, 0), stack(st_p, 1), stack(st_p, 2), stack(st_p, 3), stack(st_p, 4),
            stack(st_s, 0), stack(st_s, 1), stack(st_s, 2), stack(st_s, 3), stack(st_s, 4), stack(st_s, 5))
```

```python
import functools
import math

import numpy as np
import jax
import jax.numpy as jnp
from jax import lax
from jax.experimental import pallas as pl
from jax.experimental.pallas import tpu as pltpu

f32 = jnp.float32
bf16 = jnp.bfloat16
i32 = jnp.int32

D_MODEL = 1024
CHUNK = 64
EPS = 1e-6
N_ADA = 9
D_FF = 2816
A_WIDTH = D_MODEL
A_GROUPS = 8
A_GROUP_DIM = A_WIDTH // A_GROUPS
A_CHUNK = 128
SSM_INNER = 2 * D_MODEL
SSM_HEAD_DIM = 64
SSM_HEADS = SSM_INNER // SSM_HEAD_DIM
SSM_GROUPS = 4
SSM_HPG = SSM_HEADS // SSM_GROUPS
SSM_STATE = 128
CONV_W = 4
CONV_DIM = SSM_INNER + 2 * SSM_GROUPS * SSM_STATE
C_HEADS = 8
C_KV_HEADS = 4
C_GROUP = C_HEADS // C_KV_HEADS
C_HEAD_DIM = 128
C_WIDTH = C_HEADS * C_HEAD_DIM
IDX_HEADS = 8
IDX_DIM = 64
TOPK_MAX = 256
Q_BLOCK = 128
N_BUCKETS = 32
MAX_DISTANCE = 128
A_COLS = 2 * A_WIDTH
C_Q = C_HEADS * C_HEAD_DIM
C_KV = C_KV_HEADS * C_HEAD_DIM
C_QI = IDX_HEADS * IDX_DIM
GATE_COLS = 3 * D_MODEL

LANES = 128
SUBLANES = 8
VMEM_LIMIT = 56 * 1024 * 1024

OFF_GATE = 0
OFF_XBC = OFF_GATE + GATE_COLS
OFF_PA = OFF_XBC + CONV_DIM
OFF_Z = OFF_PA + A_COLS
OFF_Q = OFF_Z + SSM_INNER
OFF_K = OFF_Q + C_Q
OFF_V = OFF_K + C_KV
OFF_QI = OFF_V + C_KV
OFF_SMALL = OFF_QI + C_QI
SMALL_WI = SSM_HEADS
SMALL_KI = LANES - IDX_DIM
PROJ_TILE = 1024
PROJ_W = ((OFF_SMALL + LANES + PROJ_TILE - 1) // PROJ_TILE) * PROJ_TILE

NEG_BIG = -1e30
SUBNORMAL_SPAN = 0x007FFFFF
NEG_INF_KEY = -2139095041 + SUBNORMAL_SPAN + 1
INT_MAX = 2147483647
ATT_KB = 1024
IDX_KB = 512
TOP_R = 16
IDX_QB = 256
IDX_RS = 32
ATT_RC = 32
LOG2E = math.log2(math.e)
ATT_QB = 256
BIAS_WIN = ATT_QB + MAX_DISTANCE
FAR_BUCKET = N_BUCKETS // 2 - 1


def _cp(sem, vmem=VMEM_LIMIT):
    return pltpu.CompilerParams(dimension_semantics=sem, vmem_limit_bytes=vmem)


def _rms(x, g):
    return x * lax.rsqrt(jnp.mean(x * x, axis=-1, keepdims=True) + EPS) * g


def _dot(a, b):
    return jnp.dot(a, b, preferred_element_type=f32)


def _dot_nt(a, b):
    return lax.dot_general(a, b, (((1,), (1,)), ((), ())), preferred_element_type=f32)


def _dot_tn(a, b):
    return lax.dot_general(a, b, (((0,), (0,)), ((), ())), preferred_element_type=f32)


def _bf16_pieces(x, pieces):
    out = []
    for _ in range(pieces):
        p = x.astype(bf16)
        out.append(p)
        x = x - p.astype(f32)
    return out


def _dot_zero_one_lhs(sel16, x, pieces):
    parts = [_dot(sel16, p) for p in _bf16_pieces(x, pieces)]
    return functools.reduce(lambda a, b: a + b, parts)


def _dot_zero_one_rhs(x, sel16, pieces):
    parts = [_dot(p, sel16) for p in _bf16_pieces(x, pieces)]
    return functools.reduce(lambda a, b: a + b, parts)


def _ada_kernel(c_ref, w_ref, b_ref, o_ref):
    s = jax.nn.silu(c_ref[...]).astype(bf16)
    o_ref[0] = _dot(s, w_ref[0]) + b_ref[0]


def _ada_all(c_all, w_ada16, b_ada):
    depth = w_ada16.shape[0]
    rows = c_all.shape[0]
    n_out = N_ADA * D_MODEL
    tn = 9 * LANES
    return pl.pallas_call(
        _ada_kernel,
        out_shape=jax.ShapeDtypeStruct((depth, rows, n_out), f32),
        grid=(depth, n_out // tn),
        in_specs=[pl.BlockSpec((rows, D_MODEL), lambda d, n: (0, 0)),
                  pl.BlockSpec((1, D_MODEL, tn), lambda d, n: (d, 0, n)),
                  pl.BlockSpec((1, 1, tn), lambda d, n: (d, 0, n))],
        out_specs=pl.BlockSpec((1, rows, tn), lambda d, n: (d, 0, n)),
        compiler_params=_cp(("arbitrary", "arbitrary")),
        name="ada",
    )(c_all, w_ada16, b_ada.reshape(depth, 1, n_out))


def _ffn_kernel(x_ref, ada_ref, g_ref, wg_ref, wu_ref, wo_ref, gf_ref, o_ref, h_scr, acc_scr, *, k0, final_norm):
    f = pl.program_id(2)
    bt, tt, d = x_ref.shape
    rows = bt * tt

    @pl.when(f == 0)
    def _():
        ada = ada_ref[...]
        h = _rms(x_ref[...], g_ref[...]) * (1.0 + ada[:, k0 + 1:k0 + 2, :]) + ada[:, k0:k0 + 1, :]
        h_scr[...] = h.reshape(rows, d).astype(bf16)
        acc_scr[...] = jnp.zeros_like(acc_scr)

    h = h_scr[...]
    gate = _dot(h, wg_ref[...])
    up = _dot(h, wu_ref[...])
    a = (jax.nn.silu(gate) * up).astype(bf16)
    acc_scr[...] += _dot(a, wo_ref[...])

    @pl.when(f == pl.num_programs(2) - 1)
    def _():
        ada = ada_ref[...]
        y = x_ref[...] + (0.5 * ada[:, k0 + 2:k0 + 3, :]) * acc_scr[...].reshape(bt, tt, d)
        if final_norm:
            y = _rms(y, gf_ref[...])
        o_ref[...] = y


def _token_tile(b, t, rows=512):
    if t >= rows:
        return 1, rows
    return min(b, rows // t), t


def _ffn(x, ada, g, w_in16, w_out16, g_final, *, k0, final_norm):
    b, t, d = x.shape
    bt, tt = _token_tile(b, t)
    tf = D_FF // 2
    nf = D_FF // tf
    kern = functools.partial(_ffn_kernel, k0=k0, final_norm=final_norm)
    return pl.pallas_call(
        kern,
        out_shape=jax.ShapeDtypeStruct((b, t, d), f32),
        grid=(b // bt, t // tt, nf),
        in_specs=[pl.BlockSpec((bt, tt, d), lambda i, j, f: (i, j, 0)),
                  pl.BlockSpec((bt, N_ADA, d), lambda i, j, f: (i, 0, 0)),
                  pl.BlockSpec((1, d), lambda i, j, f: (0, 0)),
                  pl.BlockSpec((d, tf), lambda i, j, f: (0, f)),
                  pl.BlockSpec((d, tf), lambda i, j, f: (0, f + nf)),
                  pl.BlockSpec((tf, d), lambda i, j, f: (f, 0)),
                  pl.BlockSpec((1, d), lambda i, j, f: (0, 0))],
        out_specs=pl.BlockSpec((bt, tt, d), lambda i, j, f: (i, j, 0)),
        scratch_shapes=[pltpu.VMEM((bt * tt, d), bf16), pltpu.VMEM((bt * tt, d), f32)],
        compiler_params=_cp(("arbitrary", "arbitrary", "arbitrary")),
        name="ffn",
    )(x, ada, g.reshape(1, d), w_in16, w_in16, w_out16, g_final.reshape(1, d))


def _inproj_kernel(x_ref, ada_ref, g_ref, w_ref, o_ref, k16_ref, v16_ref, ki16_ref, k32_ref, v32_ref, ki32_ref, h_scr, *, k0):
    n = pl.program_id(2)
    bt, tt, d = x_ref.shape

    @pl.when(n == 0)
    def _():
        ada = ada_ref[...]
        h = _rms(x_ref[...], g_ref[...]) * (1.0 + ada[:, k0 + 1:k0 + 2, :]) + ada[:, k0:k0 + 1, :]
        h_scr[...] = h.reshape(bt * tt, d).astype(bf16)

    res = _dot(h_scr[...], w_ref[...])
    o_ref[...] = res.reshape(bt, tt, PROJ_TILE)

    @pl.when(n == OFF_K // PROJ_TILE)
    def _():
        k = res[:, :C_KV]
        v = res[:, C_KV:2 * C_KV]
        k16_ref[...] = k.astype(bf16).reshape(bt, tt, C_KV)
        v16_ref[...] = v.astype(bf16).reshape(bt, tt, C_KV)
        k32_ref[...] = k.reshape(bt, tt, C_KV_HEADS, C_HEAD_DIM)
        v32_ref[...] = v.reshape(bt, tt, C_KV_HEADS, C_HEAD_DIM)

    @pl.when(n == OFF_SMALL // PROJ_TILE)
    def _():
        lo = OFF_SMALL % PROJ_TILE + SMALL_KI
        ki = res[:, lo:lo + IDX_DIM]
        ki16_ref[...] = ki.astype(bf16).reshape(bt, tt, IDX_DIM)
        ki32_ref[...] = ki.reshape(bt, tt, IDX_DIM)


def _inproj(x, ada, g, w_proj16, *, k0):
    b, t, d = x.shape
    bt, tt = _token_tile(b, t, rows=1024)
    assert OFF_K % PROJ_TILE == 0 and OFF_V == OFF_K + C_KV and 2 * C_KV <= PROJ_TILE
    tok = lambda w: pl.BlockSpec((bt, tt, w), lambda i, j, n: (i, j, 0))
    heads = pl.BlockSpec((bt, tt, C_KV_HEADS, C_HEAD_DIM), lambda i, j, n: (i, j, 0, 0))
    kv32 = jax.ShapeDtypeStruct((b, t, C_KV_HEADS, C_HEAD_DIM), f32)
    return pl.pallas_call(
        functools.partial(_inproj_kernel, k0=k0),
        out_shape=(jax.ShapeDtypeStruct((b, t, PROJ_W), f32), jax.ShapeDtypeStruct((b, t, C_KV), bf16),
                   jax.ShapeDtypeStruct((b, t, C_KV), bf16), jax.ShapeDtypeStruct((b, t, IDX_DIM), bf16),
                   kv32, kv32, jax.ShapeDtypeStruct((b, t, IDX_DIM), f32)),
        grid=(b // bt, t // tt, PROJ_W // PROJ_TILE),
        in_specs=[pl.BlockSpec((bt, tt, d), lambda i, j, n: (i, j, 0)),
                  pl.BlockSpec((bt, N_ADA, d), lambda i, j, n: (i, 0, 0)),
                  pl.BlockSpec((1, d), lambda i, j, n: (0, 0)),
                  pl.BlockSpec((d, PROJ_TILE), lambda i, j, n: (0, n))],
        out_specs=(pl.BlockSpec((bt, tt, PROJ_TILE), lambda i, j, n: (i, j, n)), tok(C_KV), tok(C_KV), tok(IDX_DIM),
                   heads, heads, tok(IDX_DIM)),
        scratch_shapes=[pltpu.VMEM((bt * tt, d), bf16)],
        compiler_params=_cp(("arbitrary", "arbitrary", "arbitrary")),
        name="inproj",
    )(x, ada, g.reshape(1, d), w_proj16)


def _layout_w_in(w_in):
    d = w_in.shape[0]
    o = 0
    pa = w_in[:, o:o + A_COLS]; o += A_COLS
    z = w_in[:, o:o + SSM_INNER]; o += SSM_INNER
    xbc = w_in[:, o:o + CONV_DIM]; o += CONV_DIM
    dt = w_in[:, o:o + SSM_HEADS]; o += SSM_HEADS
    q = w_in[:, o:o + C_Q]; o += C_Q
    k = w_in[:, o:o + C_KV]; o += C_KV
    v = w_in[:, o:o + C_KV]; o += C_KV
    qi = w_in[:, o:o + C_QI]; o += C_QI
    ki = w_in[:, o:o + IDX_DIM]; o += IDX_DIM
    wi = w_in[:, o:o + IDX_HEADS]; o += IDX_HEADS
    gates = w_in[:, o:o + GATE_COLS]
    small = jnp.concatenate([dt, wi, jnp.zeros((d, SMALL_KI - SMALL_WI - IDX_HEADS), w_in.dtype), ki], axis=1)
    pad = jnp.zeros((d, PROJ_W - OFF_SMALL - LANES), w_in.dtype)
    return jnp.concatenate([gates, xbc, pa, z, q, k, v, qi, small, pad], axis=1)


def _mixa_kernel(pa_ref, ga_ref, ws_ref, bs_ref, oa_ref, va_ref):
    n = pa_ref.shape[1]
    ge = jax.nn.gelu(pa_ref[0])
    u = ge[:, :A_WIDTH]
    v = _rms(ge[:, A_WIDTH:], ga_ref[...])
    va_ref[0] = v
    vb = v.astype(bf16)
    tri = lax.broadcasted_iota(i32, (n, n), 0) >= lax.broadcasted_iota(i32, (n, n), 1)
    bs = bs_ref[...]
    for g in range(A_GROUPS):
        sl = slice(g * A_GROUP_DIM, (g + 1) * A_GROUP_DIM)
        w = jnp.where(tri, ws_ref[g], 0.0).astype(bf16)
        mixed = _dot(w, vb[:, sl]) + bs[:, g:g + 1]
        oa_ref[0, :, sl] = (u[:, sl] * mixed).astype(bf16)


def _mixer_a(proj, g_a, w_s, b_s):
    b, t, _ = proj.shape
    n = min(t, A_CHUNK)
    ws = w_s[:, :n, :n]
    bs_t = b_s[:, :n].T
    return pl.pallas_call(
        _mixa_kernel,
        out_shape=(jax.ShapeDtypeStruct((b, t, A_WIDTH), bf16), jax.ShapeDtypeStruct((b, t, A_WIDTH), f32)),
        grid=(b, t // n),
        in_specs=[pl.BlockSpec((1, n, A_COLS), lambda i, j: (i, j, OFF_PA // A_COLS)),
                  pl.BlockSpec((1, A_WIDTH), lambda i, j: (0, 0)),
                  pl.BlockSpec((A_GROUPS, n, n), lambda i, j: (0, 0, 0)),
                  pl.BlockSpec((n, A_GROUPS), lambda i, j: (0, 0))],
        out_specs=(pl.BlockSpec((1, n, A_WIDTH), lambda i, j: (i, j, 0)),
                   pl.BlockSpec((1, n, A_WIDTH), lambda i, j: (i, j, 0))),
        compiler_params=_cp(("arbitrary", "arbitrary")),
        name="mixer_a",
    )(proj, g_a.reshape(1, A_WIDTH), ws, bs_t)


CONV_PAD = 8
SSD_CHUNKS_PER_STEP = 2


def _ssd_kernel(z_ref, xbc_ref, sm_ref, cinit_ref, h0_ref, wc_ref, bc_ref, dtb_ref, alog_ref, dsk_ref, gs_ref,
                ex_ref, ob_ref, hs_ref, xpad, ht):
    c = pl.program_id(1)
    q = CHUNK
    gw = SSM_HPG * SSM_HEAD_DIM
    tail = CONV_W - 1

    @pl.when(c == 0)
    def _():
        xpad[CONV_PAD - tail:CONV_PAD, :] = cinit_ref[0]
        for g in range(SSM_GROUPS):
            ht[g] = h0_ref[0, g * gw:(g + 1) * gw, :].T

    wc = wc_ref[...]
    ex = ex_ref[...]
    lane = lax.broadcasted_iota(i32, (q, LANES), 1)
    head_lane = lane < SSM_HEADS
    tri = lax.broadcasted_iota(i32, (q, q), 0) >= lax.broadcasted_iota(i32, (q, q), 1)

    def chunk(r0):
        rows = slice(r0, r0 + q)
        xpad[CONV_PAD:CONV_PAD + q, :] = xbc_ref[0, rows, :]
        conv = bc_ref[...] + wc[0:1, :] * xpad[pl.ds(CONV_PAD - 3, q), :]
        for k in range(1, CONV_W):
            conv = conv + wc[k:k + 1, :] * xpad[pl.ds(CONV_PAD - 3 + k, q), :]
        nxt = xpad[CONV_PAD + q - tail:CONV_PAD + q, :]
        xpad[CONV_PAD - tail:CONV_PAD, :] = nxt
        conv = jax.nn.silu(conv)
        xs = conv[:, :SSM_INNER]
        bm = conv[:, SSM_INNER:SSM_INNER + SSM_GROUPS * SSM_STATE]
        cm = conv[:, SSM_INNER + SSM_GROUPS * SSM_STATE:]

        dt = jnp.where(head_lane, jax.nn.softplus(sm_ref[0, rows, :] + dtb_ref[...]), 0.0)
        da = dt * (-jnp.exp(alog_ref[...]))
        acs = _dot_zero_one_lhs(tri.astype(bf16), da, 3)
        acs_t = acs.T
        dt_t = dt.T
        acs_last = acs[q - 1:q, :]
        e_acs = jnp.where(head_lane, jnp.exp(acs), 0.0)
        decay = jnp.where(head_lane, jnp.exp(acs_last - acs) * dt, 0.0)
        e_acs_x = _dot_zero_one_rhs(e_acs, ex, 3)
        decay_x = _dot_zero_one_rhs(decay, ex, 3)
        e_last_x = e_acs_x[q - 1:q, :]

        xs16 = xs.astype(bf16)
        xdec16 = (xs * decay_x).astype(bf16)
        y_parts = []
        for g in range(SSM_GROUPS):
            bg = bm[:, g * SSM_STATE:(g + 1) * SSM_STATE].astype(bf16)
            cg = cm[:, g * SSM_STATE:(g + 1) * SSM_STATE].astype(bf16)
            cb = _dot_nt(cg, bg)
            htg = ht[g]
            y_inter = _dot(cg, htg.astype(bf16))
            intra = []
            for r in range(SSM_HPG):
                hd = g * SSM_HPG + r
                seg = acs[:, hd:hd + 1] - acs_t[hd:hd + 1, :]
                m = cb * jnp.exp(jnp.where(tri, seg, -jnp.inf)) * dt_t[hd:hd + 1, :]
                intra.append(_dot(m.astype(bf16), xs16[:, hd * SSM_HEAD_DIM:(hd + 1) * SSM_HEAD_DIM]))
            y_parts.append(jnp.concatenate(intra, axis=1) + y_inter * e_acs_x[:, g * gw:(g + 1) * gw])
            ht[g] = htg * e_last_x[:, g * gw:(g + 1) * gw] + _dot_tn(bg, xdec16[:, g * gw:(g + 1) * gw])
        y = jnp.concatenate(y_parts, axis=1) + dsk_ref[...] * xs
        y = _rms(y * jax.nn.silu(z_ref[0, rows, :]), gs_ref[...])
        ob_ref[0, rows, :] = y.astype(bf16)

    for sub in range(z_ref.shape[1] // q):
        chunk(sub * q)


    @pl.when(c == pl.num_programs(1) - 1)
    def _():
        for g in range(SSM_GROUPS):
            hs_ref[0, g * gw:(g + 1) * gw, :] = ht[g].T


def _mixer_b(proj, conv_init, h0, w_conv, b_conv, dt_bias, a_log, d_skip, g_ssm):
    b, t, _ = proj.shape
    q = CHUNK * (SSD_CHUNKS_PER_STEP if t % (CHUNK * SSD_CHUNKS_PER_STEP) == 0 else 1)
    pad_row = lambda v: jnp.zeros((1, LANES), f32).at[0, :SSM_HEADS].set(v)
    expand = (jnp.arange(LANES)[:, None] == (jnp.arange(SSM_INNER) // SSM_HEAD_DIM)[None, :]).astype(bf16)
    hrows = SSM_HEADS * SSM_HEAD_DIM
    return pl.pallas_call(
        _ssd_kernel,
        out_shape=(jax.ShapeDtypeStruct((b, t, SSM_INNER), bf16),
                   jax.ShapeDtypeStruct((b, hrows, SSM_STATE), f32)),
        grid=(b, t // q),
        in_specs=[pl.BlockSpec((1, q, SSM_INNER), lambda i, c: (i, c, OFF_Z // SSM_INNER)),
                  pl.BlockSpec((1, q, CONV_DIM), lambda i, c: (i, c, OFF_XBC // CONV_DIM)),
                  pl.BlockSpec((1, q, LANES), lambda i, c: (i, c, OFF_SMALL // LANES)),
                  pl.BlockSpec((1, CONV_W - 1, CONV_DIM), lambda i, c: (i, 0, 0)),
                  pl.BlockSpec((1, hrows, SSM_STATE), lambda i, c: (i, 0, 0)),
                  pl.BlockSpec((CONV_W, CONV_DIM), lambda i, c: (0, 0)),
                  pl.BlockSpec((1, CONV_DIM), lambda i, c: (0, 0)),
                  pl.BlockSpec((1, LANES), lambda i, c: (0, 0)),
                  pl.BlockSpec((1, LANES), lambda i, c: (0, 0)),
                  pl.BlockSpec((1, SSM_INNER), lambda i, c: (0, 0)),
                  pl.BlockSpec((1, SSM_INNER), lambda i, c: (0, 0)),
                  pl.BlockSpec((LANES, SSM_INNER), lambda i, c: (0, 0))],
        out_specs=(pl.BlockSpec((1, q, SSM_INNER), lambda i, c: (i, c, 0)),
                   pl.BlockSpec((1, hrows, SSM_STATE), lambda i, c: (i, 0, 0))),
        scratch_shapes=[pltpu.VMEM((CONV_PAD + CHUNK, CONV_DIM), f32),
                        pltpu.VMEM((SSM_GROUPS, SSM_STATE, SSM_HPG * SSM_HEAD_DIM), f32)],
        compiler_params=_cp(("arbitrary", "arbitrary")),
        name="mixer_b",
    )(proj, proj, proj, conv_init, h0.reshape(b, hrows, SSM_STATE), w_conv, b_conv.reshape(1, CONV_DIM),
      pad_row(dt_bias), pad_row(a_log), jnp.repeat(d_skip, SSM_HEAD_DIM).reshape(1, SSM_INNER),
      g_ssm.reshape(1, SSM_INNER), expand)


def _t5_bucket(rel):
    nb = N_BUCKETS // 2
    max_exact = nb // 2
    n = jnp.abs(rel)
    nf = jnp.maximum(n, 1).astype(jnp.float32)
    large = max_exact + (jnp.log(nf / max_exact) / math.log(MAX_DISTANCE / max_exact)
                         * (nb - max_exact)).astype(jnp.int32)
    large = jnp.minimum(large, nb - 1)
    return jnp.where(rel > 0, nb, 0) + jnp.where(n < max_exact, n, large)


def _bias_kernel(tab_ref, bk_ref, o_ref):
    h = pl.program_id(1)
    bk = bk_ref[0]
    far = tab_ref[FAR_BUCKET, h]
    acc = jnp.zeros(bk.shape, f32)
    for bucket in range(N_BUCKETS):
        acc = jnp.where(bk == bucket, tab_ref[bucket, h] - far, acc)
    o_ref[0, 0] = acc * LOG2E


def _bias_tables(rel_bias, qb, offsets):
    t = np.arange(qb)[:, None]
    cases = []
    for off in offsets:
        w = np.arange(ATT_KB)[None, :] - off
        valid = (w >= 0) & (w < BIAS_WIN)
        rel = w - BIAS_WIN + qb - t
        cases.append(jnp.where(jnp.asarray(valid), _t5_bucket(jnp.asarray(rel, dtype=jnp.int32)), -1))
    cases.append(jnp.full((qb, ATT_KB), -1, jnp.int32))
    buckets = jnp.stack(cases).astype(i32)
    nc = buckets.shape[0]
    return pl.pallas_call(
        _bias_kernel,
        out_shape=jax.ShapeDtypeStruct((nc, C_HEADS, qb, ATT_KB), f32),
        grid=(nc, C_HEADS),
        in_specs=[pl.BlockSpec(memory_space=pltpu.SMEM),
                  pl.BlockSpec((1, qb, ATT_KB), lambda c, h: (c, 0, 0))],
        out_specs=pl.BlockSpec((1, 1, qb, ATT_KB), lambda c, h: (c, h, 0, 0)),
        compiler_params=_cp(("arbitrary", "arbitrary")),
        name="t5_bias",
    )(rel_bias, buckets)


def _oddeven_merge_sort_pairs(n):
    pairs = []
    p = 1
    while p < n:
        k = p
        while k >= 1:
            for j in range(k % p, n - k, 2 * k):
                for i in range(min(k, n - j - k)):
                    if (i + j) // (2 * p) == (i + j + k) // (2 * p):
                        pairs.append((i + j, i + j + k))
            k //= 2
        p *= 2
    return pairs


def _compare_exchange(v, i, j):
    a, b = v[i], v[j]
    v[i] = jnp.maximum(a, b)
    v[j] = jnp.minimum(a, b)


def _sort_desc(v):
    for i, j in _oddeven_merge_sort_pairs(len(v)):
        _compare_exchange(v, i, j)


def _bitonic_merge_desc(v):
    d = len(v) // 2
    while d >= 1:
        for i in range(len(v)):
            if (i & d) == 0:
                _compare_exchange(v, i, i + d)
        d //= 2


def _index_kernel(qi_ref, sm_ref, ki_ref, o_ref, keys, top_scr, *, qoff, topk):
    j = pl.program_id(1)
    qb = qi_ref.shape[1]
    ltot = ki_ref.shape[1]
    kb = IDX_KB
    kmax = qoff + (j + 1) * qb
    nkb = lax.div(kmax + (kb - 1), kb)
    sub = kb // LANES

    qv = qi_ref[0].astype(bf16)
    rs = IDX_RS
    stacked = jnp.concatenate([qv[s * rs:(s + 1) * rs, h * IDX_DIM:(h + 1) * IDX_DIM]
                               for s in range(qb // rs) for h in range(IDX_HEADS)], axis=0)
    w = sm_ref[0][:, SMALL_WI:SMALL_WI + IDX_HEADS] * (IDX_HEADS ** -0.5) * (IDX_DIM ** -0.5)
    wcol = [jnp.broadcast_to(w[:, h:h + 1], (qb, LANES)) for h in range(IDX_HEADS)]
    qchunk = lax.shift_right_arithmetic(qoff + j * qb + lax.broadcasted_iota(i32, (qb, 1), 0), 6)
    lane_pos = lax.broadcasted_iota(i32, (1, kb), 1)

    def score_body(i, carry):
        off = pl.multiple_of(i * kb, kb)
        prod = _dot_nt(stacked, ki_ref[0, pl.ds(off, kb), :])
        kchunk = lax.shift_right_arithmetic(off + lane_pos, 6)
        for s in range(qb // rs):
            rows = slice(s * rs, (s + 1) * rs)
            acc = None
            for h in range(IDX_HEADS):
                r0 = (s * IDX_HEADS + h) * rs
                term = jnp.maximum(prod[r0:r0 + rs], 0.0) * jnp.concatenate([wcol[h][rows]] * sub, axis=1)
                acc = term if acc is None else acc + term
            adm = kchunk <= qchunk[rows]
            keys[rows, pl.ds(off, kb)] = jnp.where(adm, jnp.where(acc == 0.0, 0.0, acc), -jnp.inf)
        return carry

    lax.fori_loop(0, nkb, score_body, 0)

    group = TOP_R * LANES
    ngrp = lax.div(kmax + (group - 1), group)

    def pad_body(i, carry):
        keys[:, pl.ds(pl.multiple_of(i * kb, kb), kb)] = jnp.full((qb, kb), -jnp.inf, f32)
        return carry

    lax.fori_loop(nkb, ngrp * (group // kb), pad_body, 0)
    top_scr[...] = jnp.full(top_scr.shape, -jnp.inf, f32)

    def top_body(gidx, carry):
        base = gidx * group
        for s in range(qb // SUBLANES):
            rs = slice(s * SUBLANES, (s + 1) * SUBLANES)
            v = [keys[rs, pl.ds(pl.multiple_of(base + i * LANES, LANES), LANES)] for i in range(TOP_R)]
            _sort_desc(v)
            merged = [jnp.maximum(top_scr[i, rs, :], v[TOP_R - 1 - i]) for i in range(TOP_R)]
            _bitonic_merge_desc(merged)
            for i in range(TOP_R):
                top_scr[i, rs, :] = merged[i]
        return carry

    lax.fori_loop(0, ngrp, top_body, 0)

    def flip(bits):
        return bits ^ (lax.shift_right_arithmetic(bits, 31) & INT_MAX)

    def to_key(x):
        k = flip(lax.bitcast_convert_type(x, i32))
        return jnp.where(k > 0, k - SUBNORMAL_SPAN, jnp.where(k < 0, k + SUBNORMAL_SPAN + 1, 0))

    def to_score(key):
        k = jnp.where(key > 0, key + SUBNORMAL_SPAN, jnp.where(key < 0, key - SUBNORMAL_SPAN - 1, 0))
        return lax.bitcast_convert_type(flip(k), f32)

    def count_top(thr):
        tb = jnp.broadcast_to(to_score(thr), (qb, LANES))
        acc = (top_scr[0] >= tb).astype(i32)
        for i in range(1, TOP_R):
            acc = acc + (top_scr[i] >= tb).astype(i32)
        return jnp.sum(acc, axis=1, keepdims=True)

    def count(pred):
        def body(i, acc):
            off = pl.multiple_of(i * kb, kb)
            blk = keys[:, pl.ds(off, kb)]
            m = pred(blk, off).astype(i32)
            for c in range(sub):
                acc = acc + m[:, c * LANES:(c + 1) * LANES]
            return acc
        acc = lax.fori_loop(0, nkb, body, jnp.zeros((qb, LANES), i32))
        return jnp.sum(acc, axis=1, keepdims=True)

    def count_ge(thr):
        tb = jnp.broadcast_to(to_score(thr), (qb, kb))
        return count(lambda blk, off: blk >= tb)

    assert topk <= 2 * LANES and TOP_R >= 2
    hi0 = to_key(jnp.max(top_scr[1 if topk > LANES else 0], axis=1, keepdims=True)) + 1
    lo0 = jnp.minimum(to_key(jnp.min(top_scr[1], axis=1, keepdims=True)), hi0 - 1)

    def search(count_fn, lo_start, hi_start):
        def cond(c):
            it, open_rows, _, _ = c
            return (open_rows > 0) & (it < 40)

        def body(c):
            it, _, lo, hi = c
            mid = lax.shift_right_arithmetic(lo, 1) + lax.shift_right_arithmetic(hi, 1) + (lo & hi & 1)
            n = count_fn(mid)
            ge = n >= topk
            lo_n = jnp.where(ge, mid, lo)
            hi_n = jnp.where(n == topk, mid + 1, jnp.where(ge, hi, mid))
            open_rows = jnp.max((lo_n < hi_n - 1).astype(i32))
            return it + 1, open_rows, lo_n, hi_n

        open0 = jnp.max((lo_start < hi_start - 1).astype(i32))
        return lax.while_loop(cond, body, (jnp.int32(0), open0, lo_start, hi_start))[2]

    n_zero = count_top(jnp.zeros((qb, 1), i32))
    n_pos = count_top(jnp.ones((qb, 1), i32))
    lo1 = jnp.where(n_pos >= topk, jnp.maximum(lo0, 1), jnp.where(n_zero >= topk, 0, lo0))
    hi1 = jnp.where(n_pos >= topk, hi0, jnp.where(n_zero >= topk, 1, jnp.minimum(hi0, 0)))
    thr_c = search(count_top, lo1, hi1)
    n_ge_c = count_top(thr_c)
    smallest = to_key(jnp.max(top_scr[TOP_R - 1], axis=1, keepdims=True))
    bad = (smallest >= thr_c) & (thr_c > NEG_INF_KEY)

    def full_search(_):
        t = search(count_ge, lo0, hi0)
        return t, count_ge(t)

    thr, n_ge = lax.cond(jnp.max(bad.astype(i32)) > 0, full_search, lambda _: (thr_c, n_ge_c), 0)
    tie = (n_ge > topk) & (thr > NEG_INF_KEY)
    tb = jnp.broadcast_to(to_score(thr), (qb, kb))

    def write_mask(i, sel):
        o_ref[0, :, pl.ds(pl.multiple_of(i * kb, kb), kb)] = jnp.where(sel, 0.0, NEG_BIG).astype(bf16)

    def out_plain(_):
        def body(i, carry):
            blk = keys[:, pl.ds(pl.multiple_of(i * kb, kb), kb)]
            write_mask(i, (blk >= tb) & (blk > -jnp.inf))
            return carry
        lax.fori_loop(0, nkb, body, 0)
        return 0

    def out_tied(_):
        need = jnp.where(tie, topk - count_ge(thr + 1), INT_MAX).astype(f32)
        tri = (lax.broadcasted_iota(i32, (kb, kb), 0) <= lax.broadcasted_iota(i32, (kb, kb), 1)).astype(bf16)

        def body(i, seen):
            blk = keys[:, pl.ds(pl.multiple_of(i * kb, kb), kb)]
            eq = blk == tb
            prefix = _dot(jnp.where(eq, 1.0, 0.0).astype(bf16), tri)
            take = eq & (prefix + seen <= need)
            write_mask(i, ((blk > tb) | take) & (blk > -jnp.inf))
            return seen + prefix[:, kb - 1:kb]

        lax.fori_loop(0, nkb, body, jnp.zeros((qb, 1), f32))
        return 0

    lax.cond(jnp.max(tie.astype(i32)) > 0, out_tied, out_plain, 0)

    def fill_body(i, carry):
        off = pl.multiple_of(i * kb, kb)
        o_ref[0, :, pl.ds(off, kb)] = jnp.full((qb, kb), NEG_BIG, bf16)
        return carry

    lax.fori_loop(nkb, ltot // kb, fill_body, 0)


def _dsa_mask(proj, ki16, *, qb, qoff, topk):
    b, t, _ = proj.shape
    ltot = ki16.shape[1]
    kern = functools.partial(_index_kernel, qoff=qoff, topk=topk)
    return pl.pallas_call(
        kern,
        out_shape=jax.ShapeDtypeStruct((b, t, ltot), bf16),
        grid=(b, t // qb),
        in_specs=[pl.BlockSpec((1, qb, C_QI), lambda i, j: (i, j, OFF_QI // C_QI)),
                  pl.BlockSpec((1, qb, LANES), lambda i, j: (i, j, OFF_SMALL // LANES)),
                  pl.BlockSpec((1, ltot, IDX_DIM), lambda i, j: (i, 0, 0))],
        out_specs=pl.BlockSpec((1, qb, ltot), lambda i, j: (i, j, 0)),
        scratch_shapes=[pltpu.VMEM((qb, -(-ltot // (TOP_R * LANES)) * TOP_R * LANES), f32),
                        pltpu.VMEM((TOP_R, qb, LANES), f32)],
        compiler_params=_cp(("arbitrary", "arbitrary")),
        name="dsa_index",
    )(proj, proj, ki16)


def _attn_kernel(qidx_ref, kidx_ref, flag_ref, case_ref, q_ref, *refs, has_cache):
    if has_cache:
        (ck_ref, cv_ref, kn_ref, vn_ref, mb_ref, bias_ref, o_ref,
         q_scr, m_scr, l_scr, acc_scr, s_scr, p_scr, a_scr, mb_scr, k_ref, v_ref) = refs
    else:
        (k_ref, v_ref, mb_ref, bias_ref, o_ref,
         q_scr, m_scr, l_scr, acc_scr, s_scr, p_scr, a_scr, mb_scr) = refs
    s_id = pl.program_id(1)
    qb = q_ref.shape[1]
    kb = k_ref.shape[1]
    rows = C_GROUP * qb
    nt = kb // LANES
    flags = flag_ref[s_id]

    if has_cache:
        @pl.when((flags & 8) != 0)
        def _():
            k_ref[0] = kn_ref[0]
            v_ref[0] = vn_ref[0]

        @pl.when((flags & 8) == 0)
        def _():
            k_ref[0] = ck_ref[0, 0].reshape(kb, C_KV).astype(bf16)
            v_ref[0] = cv_ref[0, 0].reshape(kb, C_KV).astype(bf16)

    @pl.when((flags & 1) != 0)
    def _():
        qv = (q_ref[0] * (C_HEAD_DIM ** -0.5 * LOG2E)).astype(bf16)
        for g in range(C_KV_HEADS):
            for r in range(C_GROUP):
                hd = g * C_GROUP + r
                q_scr[g, r * qb:(r + 1) * qb, :] = qv[:, hd * C_HEAD_DIM:(hd + 1) * C_HEAD_DIM]
        m_scr[...] = jnp.full(m_scr.shape, NEG_BIG, f32)
        l_scr[...] = jnp.zeros_like(l_scr)
        acc_scr[...] = jnp.zeros_like(acc_scr)

    def visit(with_bias):
        kblk = k_ref[0]
        vblk = v_ref[0]
        mb_scr[...] = mb_ref[0].astype(f32)
        units = [(g, r) for g in range(C_KV_HEADS) for r in range(C_GROUP)]

        def qk(u):
            g, r = units[u]
            hs = slice(r * qb, (r + 1) * qb)
            s_scr[g, hs, :] = _dot_nt(q_scr[g, hs, :], kblk[:, g * C_HEAD_DIM:(g + 1) * C_HEAD_DIM])

        def soft(u):
            g, r = units[u]
            for c in range(qb // ATT_RC):
                q0 = c * ATT_RC
                rs = slice(r * qb + q0, r * qb + q0 + ATT_RC)
                s = s_scr[g, rs, :] + mb_scr[q0:q0 + ATT_RC, :]
                if with_bias:
                    s = s + bias_ref[0, g * C_GROUP + r, q0:q0 + ATT_RC, :]
                t = s[:, 0:LANES]
                for i in range(1, nt):
                    t = jnp.maximum(t, s[:, i * LANES:(i + 1) * LANES])
                m_old = m_scr[g, rs, :]
                m_new = jnp.maximum(m_old, jnp.max(t, axis=1, keepdims=True))
                alpha = jnp.exp2(m_old - m_new)
                p = jnp.exp2(s - jnp.concatenate([m_new] * nt, axis=1))
                psum = p[:, 0:LANES]
                for i in range(1, nt):
                    psum = psum + p[:, i * LANES:(i + 1) * LANES]
                l_scr[g, rs, :] = alpha * l_scr[g, rs, :] + psum
                m_scr[g, rs, :] = m_new
                a_scr[g, rs, :] = alpha
                p_scr[g, rs, :] = p.astype(bf16)

        def pv(u):
            g, r = units[u]
            hs = slice(r * qb, (r + 1) * qb)
            acc_scr[g, hs, :] = a_scr[g, hs, :] * acc_scr[g, hs, :] + _dot(
                p_scr[g, hs, :], vblk[:, g * C_HEAD_DIM:(g + 1) * C_HEAD_DIM])

        qk(0)
        for u in range(len(units)):
            if u + 1 < len(units):
                qk(u + 1)
            soft(u)
            pv(u)

    @pl.when((flags & 4) != 0)
    def _():
        visit(True)

    @pl.when((flags & 4) == 0)
    def _():
        visit(False)

    @pl.when((flags & 2) != 0)
    def _():
        for g in range(C_KV_HEADS):
            o = acc_scr[g] / jnp.sum(l_scr[g], axis=1, keepdims=True)
            for r in range(C_GROUP):
                hd = g * C_GROUP + r
                o_ref[0, :, hd * C_HEAD_DIM:(hd + 1) * C_HEAD_DIM] = o[r * qb:(r + 1) * qb, :].astype(bf16)


def _attn_schedule(nq, qb, qoff, offsets):
    qidx, kidx, flags, case = [], [], [], []
    zero_case = len(offsets)
    for j in range(nq):
        kmax = qoff + (j + 1) * qb
        nkb = -(-kmax // ATT_KB)
        for kb in range(nkb):
            off = kmax - BIAS_WIN - kb * ATT_KB
            qidx.append(j)
            kidx.append(kb)
            near = off in offsets
            flags.append((1 if kb == 0 else 0) | (2 if kb == nkb - 1 else 0) | (4 if near else 0)
                         | (8 if kb * ATT_KB >= qoff else 0))
            case.append(offsets.index(off) if near else zero_case)
            assert near == (-BIAS_WIN < off < ATT_KB)
    return [jnp.asarray(np.asarray(a, np.int32)) for a in (qidx, kidx, flags, case)]


def _dsa_attend(proj, k16, v16, maskbias, tables, *, qb, qoff, offsets, cache=None):
    b, t, _ = proj.shape
    nq = t // qb
    qidx, kidx, flags, case = _attn_schedule(nq, qb, qoff, offsets)
    ns = int(qidx.shape[0])
    rows = C_GROUP * qb
    kv_block = lambda f: pl.BlockSpec((1, ATT_KB, C_KV), f)
    if cache is None:
        kv_specs = [kv_block(lambda i, s, qi, ki, fl, ca: (i, ki[s], 0))] * 2
        kv_args = (k16, v16)
        kv_scratch = []
    else:
        layer = cache[2]
        ncache = cache[0].shape[2] // ATT_KB
        assert cache[0].shape[2] == qoff and qoff % ATT_KB == 0 and nq == 1 and k16.shape[1] == ATT_KB
        cache_block = pl.BlockSpec((1, 1, ATT_KB, C_KV_HEADS, C_HEAD_DIM),
                                   lambda i, s, qi, ki, fl, ca: (layer, i, jnp.minimum(ki[s], ncache - 1), 0, 0))
        kv_specs = [cache_block] * 2 + [kv_block(lambda i, s, qi, ki, fl, ca: (i, 0, 0))] * 2
        kv_args = (cache[0], cache[1], k16, v16)
        kv_scratch = [pltpu.VMEM((1, ATT_KB, C_KV), bf16)] * 2
    grid_spec = pltpu.PrefetchScalarGridSpec(
        num_scalar_prefetch=4,
        grid=(b, ns),
        in_specs=([pl.BlockSpec((1, qb, C_Q), lambda i, s, qi, ki, fl, ca: (i, qi[s], OFF_Q // C_Q))] + kv_specs
                  + [pl.BlockSpec((1, qb, ATT_KB), lambda i, s, qi, ki, fl, ca: (i, qi[s], ki[s])),
                     pl.BlockSpec((1, C_HEADS, qb, ATT_KB), lambda i, s, qi, ki, fl, ca: (ca[s], 0, 0, 0))]),
        out_specs=pl.BlockSpec((1, qb, C_WIDTH), lambda i, s, qi, ki, fl, ca: (i, qi[s], 0)),
        scratch_shapes=[pltpu.VMEM((C_KV_HEADS, rows, C_HEAD_DIM), bf16),
                        pltpu.VMEM((C_KV_HEADS, rows, LANES), f32),
                        pltpu.VMEM((C_KV_HEADS, rows, LANES), f32),
                        pltpu.VMEM((C_KV_HEADS, rows, C_HEAD_DIM), f32),
                        pltpu.VMEM((C_KV_HEADS, rows, ATT_KB), f32),
                        pltpu.VMEM((C_KV_HEADS, rows, ATT_KB), bf16),
                        pltpu.VMEM((C_KV_HEADS, rows, LANES), f32),
                        pltpu.VMEM((qb, ATT_KB), f32)] + kv_scratch)
    return pl.pallas_call(
        functools.partial(_attn_kernel, has_cache=cache is not None),
        out_shape=jax.ShapeDtypeStruct((b, t, C_WIDTH), bf16),
        grid_spec=grid_spec,
        compiler_params=_cp(("arbitrary", "arbitrary")),
        name="dsa_attend",
    )(qidx, kidx, flags, case, proj, *kv_args, maskbias, tables)


def _window_offsets(nq, qb, qoff):
    offs = set()
    for j in range(nq):
        kmax = qoff + (j + 1) * qb
        for kb in range(-(-kmax // ATT_KB)):
            off = kmax - BIAS_WIN - kb * ATT_KB
            if -BIAS_WIN < off < ATT_KB:
                offs.add(off)
    return sorted(offs)


def _mixer_c(proj, k_all16, v_all16, ki_all16, rel_bias, tables_memo, *, qb, qoff, true_len, cache=None):
    b, t, _ = proj.shape
    topk = min(TOPK_MAX, true_len // 4)
    assert BIAS_WIN - qb + 1 > MAX_DISTANCE
    offsets = _window_offsets(t // qb, qb, qoff)
    memo_key = (qb, qoff, t)
    if memo_key not in tables_memo:
        tables_memo[memo_key] = _bias_tables(rel_bias, qb, offsets)
    tables = tables_memo[memo_key]
    idx_qb = IDX_QB if t % IDX_QB == 0 else qb
    maskbias = _dsa_mask(proj, ki_all16, qb=idx_qb, qoff=qoff, topk=topk)
    return _dsa_attend(proj, k_all16, v_all16, maskbias, tables, qb=qb, qoff=qoff, offsets=offsets, cache=cache)


def _merge_kernel(x_ref, ada_ref, gt_ref, oa_ref, ob_ref, oc_ref, wa_ref, wb_ref, wc_ref, wo_ref, o_ref):
    bt, tt, d = x_ref.shape
    rows = bt * tt
    gates = jax.nn.sigmoid(gt_ref[...].reshape(rows, GATE_COLS))
    merged = (gates[:, 0:d] * _dot(oa_ref[...].reshape(rows, A_WIDTH), wa_ref[...])
              + gates[:, d:2 * d] * _dot(ob_ref[...].reshape(rows, SSM_INNER), wb_ref[...])
              + gates[:, 2 * d:3 * d] * _dot(oc_ref[...].reshape(rows, C_WIDTH), wc_ref[...]))
    mix = _dot(merged.astype(bf16), wo_ref[...]).reshape(bt, tt, d)
    o_ref[...] = x_ref[...] + ada_ref[...][:, 5:6, :] * mix


def _merge(x, ada, proj, o_a, o_b, o_c, wa16, wb16, wc16, wo16):
    b, t, d = x.shape
    bt, tt = _token_tile(b, t, rows=256)
    tok = lambda w: pl.BlockSpec((bt, tt, w), lambda i, j: (i, j, 0))
    full = lambda r, c: pl.BlockSpec((r, c), lambda i, j: (0, 0))
    return pl.pallas_call(
        _merge_kernel,
        out_shape=jax.ShapeDtypeStruct((b, t, d), f32),
        grid=(b // bt, t // tt),
        in_specs=[tok(d), pl.BlockSpec((bt, N_ADA, d), lambda i, j: (i, 0, 0)),
                  pl.BlockSpec((bt, tt, GATE_COLS), lambda i, j: (i, j, OFF_GATE // GATE_COLS)),
                  tok(A_WIDTH), tok(SSM_INNER), tok(C_WIDTH),
                  full(A_WIDTH, d), full(SSM_INNER, d), full(C_WIDTH, d), full(d, d)],
        out_specs=tok(d),
        compiler_params=_cp(("arbitrary", "arbitrary")),
        name="merge",
    )(x, ada, proj, o_a, o_b, o_c, wa16, wb16, wc16, wo16)


def _pad_keys(x, mult):
    pad = (-x.shape[1]) % mult
    if pad:
        x = jnp.concatenate([x, jnp.zeros((x.shape[0], pad) + x.shape[2:], x.dtype)], axis=1)
    return x


def _layer(x, ada, lp, rel_bias, tables_memo, past, g_final, last, layer):
    b, t, d = x.shape
    x = _ffn(x, ada, lp['g_ffn1'], lp['w_ffn1_in'], lp['w_ffn1_out'], g_final, k0=0, final_norm=False)
    proj, k16, v16, ki16, k_new, v_new, ki_new = _inproj(x, ada, lp['g_mix'], lp['w_proj'], k0=3)
    new_conv = proj[:, t - (CONV_W - 1):, OFF_XBC:OFF_XBC + CONV_DIM]

    o_a, va = _mixer_a(proj, lp['g_a'], lp['w_s'], lp['b_s'])

    if past is None:
        conv_init = jnp.zeros((b, CONV_W - 1, CONV_DIM), f32)
        h0 = jnp.zeros((b, SSM_HEADS, SSM_HEAD_DIM, SSM_STATE), f32)
    else:
        h0, conv_init = past[3], past[4]
    o_b, new_ssm = _mixer_b(proj, conv_init, h0, lp['w_conv'], lp['b_conv'], lp['dt_bias'], lp['a_log'],
                            lp['d_skip'], lp['g_ssm'])
    new_ssm = new_ssm.reshape(b, SSM_HEADS, SSM_HEAD_DIM, SSM_STATE)

    if past is None:
        k_all, v_all, ki_all, cache = k16, v16, ki16, None
        qb, qoff, true_len = (ATT_QB if t % ATT_QB == 0 else Q_BLOCK), 0, t
    else:
        plen = past[0].shape[2]
        if plen % ATT_KB == 0:
            k_all, v_all, cache = k16, v16, (past[0], past[1], layer)
        else:
            k_all = jnp.concatenate([past[0][layer].reshape(b, plen, C_KV).astype(bf16), k16], axis=1)
            v_all = jnp.concatenate([past[1][layer].reshape(b, plen, C_KV).astype(bf16), v16], axis=1)
            cache = None
        ki_all = jnp.concatenate([past[2].astype(bf16), ki16], axis=1)
        qb, qoff, true_len = t, plen, plen + t
    o_c = _mixer_c(proj, _pad_keys(k_all, ATT_KB), _pad_keys(v_all, ATT_KB), _pad_keys(ki_all, ATT_KB), rel_bias, tables_memo,
                   qb=qb, qoff=qoff, true_len=true_len, cache=cache)

    x = _merge(x, ada, proj, o_a, o_b, o_c, lp['w_br_a'], lp['w_br_b'], lp['w_br_c'], lp['w_out'])
    x = _ffn(x, ada, lp['g_ffn2'], lp['w_ffn2_in'], lp['w_ffn2_out'], g_final, k0=6, final_norm=last)
    states = (k_new, v_new, ki_new, new_ssm, new_conv, va)
    return x, states


def kernel(x_prompt, x_sample, c_prompt, c_sample, cache_k, cache_v, cache_kidx, state_ssm, state_conv, rel_bias, w_ada, b_ada, g_ffn1, w_ffn1_in, w_ffn1_out, g_mix, w_in, g_a, w_s, b_s, w_conv, b_conv, dt_bias, a_log, d_skip, g_ssm, w_br_a, w_br_b, w_br_c, w_out, g_ffn2, w_ffn2_in, w_ffn2_out, g_final):
    depth = w_ada.shape[0]
    bp, bs = c_prompt.shape[0], c_sample.shape[0]
    assert x_prompt.shape[1] % ATT_KB == 0 and x_prompt.shape[1] % Q_BLOCK == 0
    assert x_sample.shape[1] == CHUNK

    rows = -(-(bp + bs) // 8) * 8
    c_all = jnp.concatenate([c_prompt, c_sample, jnp.zeros((rows - bp - bs, D_MODEL), f32)], axis=0)
    ada_all = _ada_all(c_all, w_ada.astype(bf16), b_ada)
    ada_all = ada_all.reshape(depth, rows, N_ADA, D_MODEL)

    yp, ys = x_prompt, x_sample
    tables_memo = {}
    st_p, st_s = [], []
    for i in range(depth):
        lp = dict(g_ffn1=g_ffn1[i], w_ffn1_in=w_ffn1_in[i].astype(bf16), w_ffn1_out=w_ffn1_out[i].astype(bf16),
                  g_mix=g_mix[i], w_proj=_layout_w_in(w_in[i]).astype(bf16), g_a=g_a[i], w_s=w_s[i], b_s=b_s[i],
                  w_conv=w_conv[i], b_conv=b_conv[i], dt_bias=dt_bias[i], a_log=a_log[i], d_skip=d_skip[i],
                  g_ssm=g_ssm[i], w_br_a=w_br_a[i].astype(bf16), w_br_b=w_br_b[i].astype(bf16),
                  w_br_c=w_br_c[i].astype(bf16), w_out=w_out[i].astype(bf16), g_ffn2=g_ffn2[i],
                  w_ffn2_in=w_ffn2_in[i].astype(bf16), w_ffn2_out=w_ffn2_out[i].astype(bf16))
        last = i == depth - 1
        yp, sp = _layer(yp, ada_all[i, :bp], lp, rel_bias, tables_memo, None, g_final, last, i)
        ys, ss = _layer(ys, ada_all[i, bp:bp + bs], lp, rel_bias, tables_memo,
                        (cache_k, cache_v, cache_kidx[i], state_ssm[i], state_conv[i]), g_final, last, i)
        st_p.append(sp)
        st_s.append(ss)

    stack = lambda sts, k: jnp.stack([s[k] for s in sts])
    return (yp, ys,
            stack(st_p, 0), stack(st_p, 1), stack(st_p, 2), stack(st_p, 3), stack(st_p, 4),
            stack(st_s, 0), stack(st_s, 1), stack(st_s, 2), stack(st_s, 3), stack(st_s, 4), stack(st_s, 5))
```

```python
import functools
import math

import numpy as np
import jax
import jax.numpy as jnp
from jax import lax
from jax.experimental import pallas as pl
from jax.experimental.pallas import tpu as pltpu

f32 = jnp.float32
bf16 = jnp.bfloat16
i32 = jnp.int32

D_MODEL = 1024
CHUNK = 64
EPS = 1e-6
N_ADA = 9
D_FF = 2816
A_WIDTH = D_MODEL
A_GROUPS = 8
A_GROUP_DIM = A_WIDTH // A_GROUPS
A_CHUNK = 128
SSM_INNER = 2 * D_MODEL
SSM_HEAD_DIM = 64
SSM_HEADS = SSM_INNER // SSM_HEAD_DIM
SSM_GROUPS = 4
SSM_HPG = SSM_HEADS // SSM_GROUPS
SSM_STATE = 128
CONV_W = 4
CONV_DIM = SSM_INNER + 2 * SSM_GROUPS * SSM_STATE
C_HEADS = 8
C_KV_HEADS = 4
C_GROUP = C_HEADS // C_KV_HEADS
C_HEAD_DIM = 128
C_WIDTH = C_HEADS * C_HEAD_DIM
IDX_HEADS = 8
IDX_DIM = 64
TOPK_MAX = 256
Q_BLOCK = 128
N_BUCKETS = 32
MAX_DISTANCE = 128
A_COLS = 2 * A_WIDTH
C_Q = C_HEADS * C_HEAD_DIM
C_KV = C_KV_HEADS * C_HEAD_DIM
C_QI = IDX_HEADS * IDX_DIM
GATE_COLS = 3 * D_MODEL

LANES = 128
SUBLANES = 8
VMEM_LIMIT = 56 * 1024 * 1024

OFF_GATE = 0
OFF_XBC = OFF_GATE + GATE_COLS
OFF_PA = OFF_XBC + CONV_DIM
OFF_Z = OFF_PA + A_COLS
OFF_Q = OFF_Z + SSM_INNER
OFF_K = OFF_Q + C_Q
OFF_V = OFF_K + C_KV
OFF_QI = OFF_V + C_KV
OFF_SMALL = OFF_QI + C_QI
SMALL_WI = SSM_HEADS
SMALL_KI = LANES - IDX_DIM
PROJ_TILE = 1024
PROJ_W = ((OFF_SMALL + LANES + PROJ_TILE - 1) // PROJ_TILE) * PROJ_TILE

NEG_BIG = -1e30
SUBNORMAL_SPAN = 0x007FFFFF
NEG_INF_KEY = -2139095041 + SUBNORMAL_SPAN + 1
INT_MAX = 2147483647
ATT_KB = 1024
IDX_KB = 512
TOP_R = 16
IDX_QB = 256
IDX_RS = 32
ATT_RC = 32
LOG2E = math.log2(math.e)
ATT_QB = 256
BIAS_WIN = ATT_QB + MAX_DISTANCE
FAR_BUCKET = N_BUCKETS // 2 - 1


def _cp(sem, vmem=VMEM_LIMIT):
    return pltpu.CompilerParams(dimension_semantics=sem, vmem_limit_bytes=vmem)


def _rms(x, g):
    return x * lax.rsqrt(jnp.mean(x * x, axis=-1, keepdims=True) + EPS) * g


def _dot(a, b):
    return jnp.dot(a, b, preferred_element_type=f32)


def _dot_nt(a, b):
    return lax.dot_general(a, b, (((1,), (1,)), ((), ())), preferred_element_type=f32)


def _dot_tn(a, b):
    return lax.dot_general(a, b, (((0,), (0,)), ((), ())), preferred_element_type=f32)


def _bf16_pieces(x, pieces):
    out = []
    for _ in range(pieces):
        p = x.astype(bf16)
        out.append(p)
        x = x - p.astype(f32)
    return out


def _dot_zero_one_lhs(sel16, x, pieces):
    parts = [_dot(sel16, p) for p in _bf16_pieces(x, pieces)]
    return functools.reduce(lambda a, b: a + b, parts)


def _dot_zero_one_rhs(x, sel16, pieces):
    parts = [_dot(p, sel16) for p in _bf16_pieces(x, pieces)]
    return functools.reduce(lambda a, b: a + b, parts)


def _ada_kernel(c_ref, w_ref, b_ref, o_ref):
    s = jax.nn.silu(c_ref[...]).astype(bf16)
    o_ref[0] = _dot(s, w_ref[0]) + b_ref[0]


def _ada_all(c_all, w_ada16, b_ada):
    depth = w_ada16.shape[0]
    rows = c_all.shape[0]
    n_out = N_ADA * D_MODEL
    tn = 9 * LANES
    return pl.pallas_call(
        _ada_kernel,
        out_shape=jax.ShapeDtypeStruct((depth, rows, n_out), f32),
        grid=(depth, n_out // tn),
        in_specs=[pl.BlockSpec((rows, D_MODEL), lambda d, n: (0, 0)),
                  pl.BlockSpec((1, D_MODEL, tn), lambda d, n: (d, 0, n)),
                  pl.BlockSpec((1, 1, tn), lambda d, n: (d, 0, n))],
        out_specs=pl.BlockSpec((1, rows, tn), lambda d, n: (d, 0, n)),
        compiler_params=_cp(("arbitrary", "arbitrary")),
        name="ada",
    )(c_all, w_ada16, b_ada.reshape(depth, 1, n_out))


def _ffn_kernel(x_ref, ada_ref, g_ref, wg_ref, wu_ref, wo_ref, gf_ref, o_ref, h_scr, acc_scr, *, k0, final_norm):
    f = pl.program_id(2)
    bt, tt, d = x_ref.shape
    rows = bt * tt

    @pl.when(f == 0)
    def _():
        ada = ada_ref[...]
        h = _rms(x_ref[...], g_ref[...]) * (1.0 + ada[:, k0 + 1:k0 + 2, :]) + ada[:, k0:k0 + 1, :]
        h_scr[...] = h.reshape(rows, d).astype(bf16)
        acc_scr[...] = jnp.zeros_like(acc_scr)

    h = h_scr[...]
    gate = _dot(h, wg_ref[...])
    up = _dot(h, wu_ref[...])
    a = (jax.nn.silu(gate) * up).astype(bf16)
    acc_scr[...] += _dot(a, wo_ref[...])

    @pl.when(f == pl.num_programs(2) - 1)
    def _():
        ada = ada_ref[...]
        y = x_ref[...] + (0.5 * ada[:, k0 + 2:k0 + 3, :]) * acc_scr[...].reshape(bt, tt, d)
        if final_norm:
            y = _rms(y, gf_ref[...])
        o_ref[...] = y


def _token_tile(b, t, rows=512):
    if t >= rows:
        return 1, rows
    return min(b, rows // t), t


def _ffn(x, ada, g, w_in16, w_out16, g_final, *, k0, final_norm):
    b, t, d = x.shape
    bt, tt = _token_tile(b, t)
    tf = D_FF // 2
    nf = D_FF // tf
    kern = functools.partial(_ffn_kernel, k0=k0, final_norm=final_norm)
    return pl.pallas_call(
        kern,
        out_shape=jax.ShapeDtypeStruct((b, t, d), f32),
        grid=(b // bt, t // tt, nf),
        in_specs=[pl.BlockSpec((bt, tt, d), lambda i, j, f: (i, j, 0)),
                  pl.BlockSpec((bt, N_ADA, d), lambda i, j, f: (i, 0, 0)),
                  pl.BlockSpec((1, d), lambda i, j, f: (0, 0)),
                  pl.BlockSpec((d, tf), lambda i, j, f: (0, f)),
                  pl.BlockSpec((d, tf), lambda i, j, f: (0, f + nf)),
                  pl.BlockSpec((tf, d), lambda i, j, f: (f, 0)),
                  pl.BlockSpec((1, d), lambda i, j, f: (0, 0))],
        out_specs=pl.BlockSpec((bt, tt, d), lambda i, j, f: (i, j, 0)),
        scratch_shapes=[pltpu.VMEM((bt * tt, d), bf16), pltpu.VMEM((bt * tt, d), f32)],
        compiler_params=_cp(("arbitrary", "arbitrary", "arbitrary")),
        name="ffn",
    )(x, ada, g.reshape(1, d), w_in16, w_in16, w_out16, g_final.reshape(1, d))


def _inproj_kernel(x_ref, ada_ref, g_ref, w_ref, o_ref, k16_ref, v16_ref, ki16_ref, k32_ref, v32_ref, ki32_ref, h_scr, *, k0):
    n = pl.program_id(2)
    bt, tt, d = x_ref.shape

    @pl.when(n == 0)
    def _():
        ada = ada_ref[...]
        h = _rms(x_ref[...], g_ref[...]) * (1.0 + ada[:, k0 + 1:k0 + 2, :]) + ada[:, k0:k0 + 1, :]
        h_scr[...] = h.reshape(bt * tt, d).astype(bf16)

    res = _dot(h_scr[...], w_ref[...])
    o_ref[...] = res.reshape(bt, tt, PROJ_TILE)

    @pl.when(n == OFF_K // PROJ_TILE)
    def _():
        k = res[:, :C_KV]
        v = res[:, C_KV:2 * C_KV]
        k16_ref[...] = k.astype(bf16).reshape(bt, tt, C_KV)
        v16_ref[...] = v.astype(bf16).reshape(bt, tt, C_KV)
        k32_ref[...] = k.reshape(bt, tt, C_KV_HEADS, C_HEAD_DIM)
        v32_ref[...] = v.reshape(bt, tt, C_KV_HEADS, C_HEAD_DIM)

    @pl.when(n == OFF_SMALL // PROJ_TILE)
    def _():
        lo = OFF_SMALL % PROJ_TILE + SMALL_KI
        ki = res[:, lo:lo + IDX_DIM]
        ki16_ref[...] = ki.astype(bf16).reshape(bt, tt, IDX_DIM)
        ki32_ref[...] = ki.reshape(bt, tt, IDX_DIM)


def _inproj(x, ada, g, w_proj16, *, k0):
    b, t, d = x.shape
    bt, tt = _token_tile(b, t, rows=1024)
    assert OFF_K % PROJ_TILE == 0 and OFF_V == OFF_K + C_KV and 2 * C_KV <= PROJ_TILE
    tok = lambda w: pl.BlockSpec((bt, tt, w), lambda i, j, n: (i, j, 0))
    heads = pl.BlockSpec((bt, tt, C_KV_HEADS, C_HEAD_DIM), lambda i, j, n: (i, j, 0, 0))
    kv32 = jax.ShapeDtypeStruct((b, t, C_KV_HEADS, C_HEAD_DIM), f32)
    return pl.pallas_call(
        functools.partial(_inproj_kernel, k0=k0),
        out_shape=(jax.ShapeDtypeStruct((b, t, PROJ_W), f32), jax.ShapeDtypeStruct((b, t, C_KV), bf16),
                   jax.ShapeDtypeStruct((b, t, C_KV), bf16), jax.ShapeDtypeStruct((b, t, IDX_DIM), bf16),
                   kv32, kv32, jax.ShapeDtypeStruct((b, t, IDX_DIM), f32)),
        grid=(b // bt, t // tt, PROJ_W // PROJ_TILE),
        in_specs=[pl.BlockSpec((bt, tt, d), lambda i, j, n: (i, j, 0)),
                  pl.BlockSpec((bt, N_ADA, d), lambda i, j, n: (i, 0, 0)),
                  pl.BlockSpec((1, d), lambda i, j, n: (0, 0)),
                  pl.BlockSpec((d, PROJ_TILE), lambda i, j, n: (0, n))],
        out_specs=(pl.BlockSpec((bt, tt, PROJ_TILE), lambda i, j, n: (i, j, n)), tok(C_KV), tok(C_KV), tok(IDX_DIM),
                   heads, heads, tok(IDX_DIM)),
        scratch_shapes=[pltpu.VMEM((bt * tt, d), bf16)],
        compiler_params=_cp(("arbitrary", "arbitrary", "arbitrary")),
        name="inproj",
    )(x, ada, g.reshape(1, d), w_proj16)


def _layout_w_in(w_in):
    d = w_in.shape[0]
    o = 0
    pa = w_in[:, o:o + A_COLS]; o += A_COLS
    z = w_in[:, o:o + SSM_INNER]; o += SSM_INNER
    xbc = w_in[:, o:o + CONV_DIM]; o += CONV_DIM
    dt = w_in[:, o:o + SSM_HEADS]; o += SSM_HEADS
    q = w_in[:, o:o + C_Q]; o += C_Q
    k = w_in[:, o:o + C_KV]; o += C_KV
    v = w_in[:, o:o + C_KV]; o += C_KV
    qi = w_in[:, o:o + C_QI]; o += C_QI
    ki = w_in[:, o:o + IDX_DIM]; o += IDX_DIM
    wi = w_in[:, o:o + IDX_HEADS]; o += IDX_HEADS
    gates = w_in[:, o:o + GATE_COLS]
    small = jnp.concatenate([dt, wi, jnp.zeros((d, SMALL_KI - SMALL_WI - IDX_HEADS), w_in.dtype), ki], axis=1)
    pad = jnp.zeros((d, PROJ_W - OFF_SMALL - LANES), w_in.dtype)
    return jnp.concatenate([gates, xbc, pa, z, q, k, v, qi, small, pad], axis=1)


def _mixa_kernel(pa_ref, ga_ref, ws_ref, bs_ref, oa_ref, va_ref):
    n = ws_ref.shape[1]
    tri = lax.broadcasted_iota(i32, (n, n), 0) >= lax.broadcasted_iota(i32, (n, n), 1)
    bs = bs_ref[...]
    ws = [jnp.where(tri, ws_ref[g], 0.0).astype(bf16) for g in range(A_GROUPS)]
    for c in range(pa_ref.shape[1] // n):
        rows = slice(c * n, (c + 1) * n)
        ge = jax.nn.gelu(pa_ref[0, rows, :])
        u = ge[:, :A_WIDTH]
        v = _rms(ge[:, A_WIDTH:], ga_ref[...])
        va_ref[0, rows, :] = v
        vb = v.astype(bf16)
        for g in range(A_GROUPS):
            sl = slice(g * A_GROUP_DIM, (g + 1) * A_GROUP_DIM)
            mixed = _dot(ws[g], vb[:, sl]) + bs[:, g:g + 1]
            oa_ref[0, rows, sl] = (u[:, sl] * mixed).astype(bf16)


def _mixer_a(proj, g_a, w_s, b_s):
    b, t, _ = proj.shape
    n = min(t, A_CHUNK)
    ws = w_s[:, :n, :n]
    bs_t = b_s[:, :n].T
    tt = n * (MIXA_CHUNKS_PER_STEP if t % (n * MIXA_CHUNKS_PER_STEP) == 0 else 1)
    return pl.pallas_call(
        _mixa_kernel,
        out_shape=(jax.ShapeDtypeStruct((b, t, A_WIDTH), bf16), jax.ShapeDtypeStruct((b, t, A_WIDTH), f32)),
        grid=(b, t // tt),
        in_specs=[pl.BlockSpec((1, tt, A_COLS), lambda i, j: (i, j, OFF_PA // A_COLS)),
                  pl.BlockSpec((1, A_WIDTH), lambda i, j: (0, 0)),
                  pl.BlockSpec((A_GROUPS, n, n), lambda i, j: (0, 0, 0)),
                  pl.BlockSpec((n, A_GROUPS), lambda i, j: (0, 0))],
        out_specs=(pl.BlockSpec((1, tt, A_WIDTH), lambda i, j: (i, j, 0)),
                   pl.BlockSpec((1, tt, A_WIDTH), lambda i, j: (i, j, 0))),
        compiler_params=_cp(("arbitrary", "arbitrary")),
        name="mixer_a",
    )(proj, g_a.reshape(1, A_WIDTH), ws, bs_t)


CONV_PAD = 8
SSD_CHUNKS_PER_STEP = 4
MIXA_CHUNKS_PER_STEP = 4


def _ssd_kernel(z_ref, xbc_ref, sm_ref, cinit_ref, h0_ref, wc_ref, bc_ref, dtb_ref, alog_ref, dsk_ref, gs_ref,
                ex_ref, ob_ref, hs_ref, xpad, ht):
    c = pl.program_id(1)
    q = CHUNK
    gw = SSM_HPG * SSM_HEAD_DIM
    tail = CONV_W - 1

    @pl.when(c == 0)
    def _():
        xpad[CONV_PAD - tail:CONV_PAD, :] = cinit_ref[0]
        for g in range(SSM_GROUPS):
            ht[g] = h0_ref[0, g * gw:(g + 1) * gw, :].T

    wc = wc_ref[...]
    ex = ex_ref[...]
    lane = lax.broadcasted_iota(i32, (q, LANES), 1)
    head_lane = lane < SSM_HEADS
    tri = lax.broadcasted_iota(i32, (q, q), 0) >= lax.broadcasted_iota(i32, (q, q), 1)

    def chunk(r0):
        rows = slice(r0, r0 + q)
        xpad[CONV_PAD:CONV_PAD + q, :] = xbc_ref[0, rows, :]
        conv = bc_ref[...] + wc[0:1, :] * xpad[pl.ds(CONV_PAD - 3, q), :]
        for k in range(1, CONV_W):
            conv = conv + wc[k:k + 1, :] * xpad[pl.ds(CONV_PAD - 3 + k, q), :]
        nxt = xpad[CONV_PAD + q - tail:CONV_PAD + q, :]
        xpad[CONV_PAD - tail:CONV_PAD, :] = nxt
        conv = jax.nn.silu(conv)
        xs = conv[:, :SSM_INNER]
        bm = conv[:, SSM_INNER:SSM_INNER + SSM_GROUPS * SSM_STATE]
        cm = conv[:, SSM_INNER + SSM_GROUPS * SSM_STATE:]

        dt = jnp.where(head_lane, jax.nn.softplus(sm_ref[0, rows, :] + dtb_ref[...]), 0.0)
        da = dt * (-jnp.exp(alog_ref[...]))
        acs = _dot_zero_one_lhs(tri.astype(bf16), da, 3)
        acs_t = acs.T
        dt_t = dt.T
        acs_last = acs[q - 1:q, :]
        e_acs = jnp.where(head_lane, jnp.exp(acs), 0.0)
        decay = jnp.where(head_lane, jnp.exp(acs_last - acs) * dt, 0.0)
        e_acs_x = _dot_zero_one_rhs(e_acs, ex, 3)
        decay_x = _dot_zero_one_rhs(decay, ex, 3)
        e_last_x = e_acs_x[q - 1:q, :]

        xs16 = xs.astype(bf16)
        xdec16 = (xs * decay_x).astype(bf16)
        y_parts = []
        for g in range(SSM_GROUPS):
            bg = bm[:, g * SSM_STATE:(g + 1) * SSM_STATE].astype(bf16)
            cg = cm[:, g * SSM_STATE:(g + 1) * SSM_STATE].astype(bf16)
            cb = _dot_nt(cg, bg)
            htg = ht[g]
            y_inter = _dot(cg, htg.astype(bf16))
            intra = []
            for r in range(SSM_HPG):
                hd = g * SSM_HPG + r
                seg = acs[:, hd:hd + 1] - acs_t[hd:hd + 1, :]
                m = cb * jnp.exp(jnp.where(tri, seg, -jnp.inf)) * dt_t[hd:hd + 1, :]
                intra.append(_dot(m.astype(bf16), xs16[:, hd * SSM_HEAD_DIM:(hd + 1) * SSM_HEAD_DIM]))
            y_parts.append(jnp.concatenate(intra, axis=1) + y_inter * e_acs_x[:, g * gw:(g + 1) * gw])
            ht[g] = htg * e_last_x[:, g * gw:(g + 1) * gw] + _dot_tn(bg, xdec16[:, g * gw:(g + 1) * gw])
        y = jnp.concatenate(y_parts, axis=1) + dsk_ref[...] * xs
        y = _rms(y * jax.nn.silu(z_ref[0, rows, :]), gs_ref[...])
        ob_ref[0, rows, :] = y.astype(bf16)

    for sub in range(z_ref.shape[1] // q):
        chunk(sub * q)


    @pl.when(c == pl.num_programs(1) - 1)
    def _():
        for g in range(SSM_GROUPS):
            hs_ref[0, g * gw:(g + 1) * gw, :] = ht[g].T


def _mixer_b(proj, conv_init, h0, w_conv, b_conv, dt_bias, a_log, d_skip, g_ssm):
    b, t, _ = proj.shape
    q = CHUNK * (SSD_CHUNKS_PER_STEP if t % (CHUNK * SSD_CHUNKS_PER_STEP) == 0 else 1)
    pad_row = lambda v: jnp.zeros((1, LANES), f32).at[0, :SSM_HEADS].set(v)
    expand = (jnp.arange(LANES)[:, None] == (jnp.arange(SSM_INNER) // SSM_HEAD_DIM)[None, :]).astype(bf16)
    hrows = SSM_HEADS * SSM_HEAD_DIM
    return pl.pallas_call(
        _ssd_kernel,
        out_shape=(jax.ShapeDtypeStruct((b, t, SSM_INNER), bf16),
                   jax.ShapeDtypeStruct((b, hrows, SSM_STATE), f32)),
        grid=(b, t // q),
        in_specs=[pl.BlockSpec((1, q, SSM_INNER), lambda i, c: (i, c, OFF_Z // SSM_INNER)),
                  pl.BlockSpec((1, q, CONV_DIM), lambda i, c: (i, c, OFF_XBC // CONV_DIM)),
                  pl.BlockSpec((1, q, LANES), lambda i, c: (i, c, OFF_SMALL // LANES)),
                  pl.BlockSpec((1, CONV_W - 1, CONV_DIM), lambda i, c: (i, 0, 0)),
                  pl.BlockSpec((1, hrows, SSM_STATE), lambda i, c: (i, 0, 0)),
                  pl.BlockSpec((CONV_W, CONV_DIM), lambda i, c: (0, 0)),
                  pl.BlockSpec((1, CONV_DIM), lambda i, c: (0, 0)),
                  pl.BlockSpec((1, LANES), lambda i, c: (0, 0)),
                  pl.BlockSpec((1, LANES), lambda i, c: (0, 0)),
                  pl.BlockSpec((1, SSM_INNER), lambda i, c: (0, 0)),
                  pl.BlockSpec((1, SSM_INNER), lambda i, c: (0, 0)),
                  pl.BlockSpec((LANES, SSM_INNER), lambda i, c: (0, 0))],
        out_specs=(pl.BlockSpec((1, q, SSM_INNER), lambda i, c: (i, c, 0)),
                   pl.BlockSpec((1, hrows, SSM_STATE), lambda i, c: (i, 0, 0))),
        scratch_shapes=[pltpu.VMEM((CONV_PAD + CHUNK, CONV_DIM), f32),
                        pltpu.VMEM((SSM_GROUPS, SSM_STATE, SSM_HPG * SSM_HEAD_DIM), f32)],
        compiler_params=_cp(("arbitrary", "arbitrary")),
        name="mixer_b",
    )(proj, proj, proj, conv_init, h0.reshape(b, hrows, SSM_STATE), w_conv, b_conv.reshape(1, CONV_DIM),
      pad_row(dt_bias), pad_row(a_log), jnp.repeat(d_skip, SSM_HEAD_DIM).reshape(1, SSM_INNER),
      g_ssm.reshape(1, SSM_INNER), expand)


def _t5_bucket(rel):
    nb = N_BUCKETS // 2
    max_exact = nb // 2
    n = jnp.abs(rel)
    nf = jnp.maximum(n, 1).astype(jnp.float32)
    large = max_exact + (jnp.log(nf / max_exact) / math.log(MAX_DISTANCE / max_exact)
                         * (nb - max_exact)).astype(jnp.int32)
    large = jnp.minimum(large, nb - 1)
    return jnp.where(rel > 0, nb, 0) + jnp.where(n < max_exact, n, large)


def _bias_kernel(tab_ref, bk_ref, o_ref):
    h = pl.program_id(1)
    bk = bk_ref[0]
    far = tab_ref[FAR_BUCKET, h]
    acc = jnp.zeros(bk.shape, f32)
    for bucket in range(N_BUCKETS):
        acc = jnp.where(bk == bucket, tab_ref[bucket, h] - far, acc)
    o_ref[0, 0] = acc * LOG2E


def _bias_tables(rel_bias, qb, offsets):
    t = np.arange(qb)[:, None]
    cases = []
    for off in offsets:
        w = np.arange(ATT_KB)[None, :] - off
        valid = (w >= 0) & (w < BIAS_WIN)
        rel = w - BIAS_WIN + qb - t
        cases.append(jnp.where(jnp.asarray(valid), _t5_bucket(jnp.asarray(rel, dtype=jnp.int32)), -1))
    cases.append(jnp.full((qb, ATT_KB), -1, jnp.int32))
    buckets = jnp.stack(cases).astype(i32)
    nc = buckets.shape[0]
    return pl.pallas_call(
        _bias_kernel,
        out_shape=jax.ShapeDtypeStruct((nc, C_HEADS, qb, ATT_KB), f32),
        grid=(nc, C_HEADS),
        in_specs=[pl.BlockSpec(memory_space=pltpu.SMEM),
                  pl.BlockSpec((1, qb, ATT_KB), lambda c, h: (c, 0, 0))],
        out_specs=pl.BlockSpec((1, 1, qb, ATT_KB), lambda c, h: (c, h, 0, 0)),
        compiler_params=_cp(("arbitrary", "arbitrary")),
        name="t5_bias",
    )(rel_bias, buckets)


def _oddeven_merge_sort_pairs(n):
    pairs = []
    p = 1
    while p < n:
        k = p
        while k >= 1:
            for j in range(k % p, n - k, 2 * k):
                for i in range(min(k, n - j - k)):
                    if (i + j) // (2 * p) == (i + j + k) // (2 * p):
                        pairs.append((i + j, i + j + k))
            k //= 2
        p *= 2
    return pairs


def _compare_exchange(v, i, j):
    a, b = v[i], v[j]
    v[i] = jnp.maximum(a, b)
    v[j] = jnp.minimum(a, b)


def _sort_desc(v):
    for i, j in _oddeven_merge_sort_pairs(len(v)):
        _compare_exchange(v, i, j)


def _bitonic_merge_desc(v):
    d = len(v) // 2
    while d >= 1:
        for i in range(len(v)):
            if (i & d) == 0:
                _compare_exchange(v, i, i + d)
        d //= 2


def _index_kernel(qi_ref, sm_ref, ki_ref, o_ref, keys, top_scr, *, qoff, topk):
    j = pl.program_id(1)
    qb = qi_ref.shape[1]
    ltot = ki_ref.shape[1]
    kb = IDX_KB
    kmax = qoff + (j + 1) * qb
    nkb = lax.div(kmax + (kb - 1), kb)
    sub = kb // LANES

    qv = qi_ref[0].astype(bf16)
    rs = IDX_RS
    stacked = jnp.concatenate([qv[s * rs:(s + 1) * rs, h * IDX_DIM:(h + 1) * IDX_DIM]
                               for s in range(qb // rs) for h in range(IDX_HEADS)], axis=0)
    w = sm_ref[0][:, SMALL_WI:SMALL_WI + IDX_HEADS] * (IDX_HEADS ** -0.5) * (IDX_DIM ** -0.5)
    wcol = [jnp.broadcast_to(w[:, h:h + 1], (qb, LANES)) for h in range(IDX_HEADS)]
    qchunk = lax.shift_right_arithmetic(qoff + j * qb + lax.broadcasted_iota(i32, (qb, 1), 0), 6)
    lane_pos = lax.broadcasted_iota(i32, (1, kb), 1)

    def score_body(i, carry):
        off = pl.multiple_of(i * kb, kb)
        prod = _dot_nt(stacked, ki_ref[0, pl.ds(off, kb), :])
        kchunk = lax.shift_right_arithmetic(off + lane_pos, 6)
        for s in range(qb // rs):
            rows = slice(s * rs, (s + 1) * rs)
            acc = None
            for h in range(IDX_HEADS):
                r0 = (s * IDX_HEADS + h) * rs
                term = jnp.maximum(prod[r0:r0 + rs], 0.0) * jnp.concatenate([wcol[h][rows]] * sub, axis=1)
                acc = term if acc is None else acc + term
            adm = kchunk <= qchunk[rows]
            keys[rows, pl.ds(off, kb)] = jnp.where(adm, jnp.where(acc == 0.0, 0.0, acc), -jnp.inf)
        return carry

    lax.fori_loop(0, nkb, score_body, 0)

    group = TOP_R * LANES
    ngrp = lax.div(kmax + (group - 1), group)

    def pad_body(i, carry):
        keys[:, pl.ds(pl.multiple_of(i * kb, kb), kb)] = jnp.full((qb, kb), -jnp.inf, f32)
        return carry

    lax.fori_loop(nkb, ngrp * (group // kb), pad_body, 0)
    top_scr[...] = jnp.full(top_scr.shape, -jnp.inf, f32)

    def top_body(gidx, carry):
        base = gidx * group
        for s in range(qb // SUBLANES):
            rs = slice(s * SUBLANES, (s + 1) * SUBLANES)
            v = [keys[rs, pl.ds(pl.multiple_of(base + i * LANES, LANES), LANES)] for i in range(TOP_R)]
            _sort_desc(v)
            merged = [jnp.maximum(top_scr[i, rs, :], v[TOP_R - 1 - i]) for i in range(TOP_R)]
            _bitonic_merge_desc(merged)
            for i in range(TOP_R):
                top_scr[i, rs, :] = merged[i]
        return carry

    lax.fori_loop(0, ngrp, top_body, 0)

    def flip(bits):
        return bits ^ (lax.shift_right_arithmetic(bits, 31) & INT_MAX)

    def to_key(x):
        k = flip(lax.bitcast_convert_type(x, i32))
        return jnp.where(k > 0, k - SUBNORMAL_SPAN, jnp.where(k < 0, k + SUBNORMAL_SPAN + 1, 0))

    def to_score(key):
        k = jnp.where(key > 0, key + SUBNORMAL_SPAN, jnp.where(key < 0, key - SUBNORMAL_SPAN - 1, 0))
        return lax.bitcast_convert_type(flip(k), f32)

    def count_top(thr):
        tb = jnp.broadcast_to(to_score(thr), (qb, LANES))
        acc = (top_scr[0] >= tb).astype(i32)
        for i in range(1, TOP_R):
            acc = acc + (top_scr[i] >= tb).astype(i32)
        return jnp.sum(acc, axis=1, keepdims=True)

    def count(pred):
        def body(i, acc):
            off = pl.multiple_of(i * kb, kb)
            blk = keys[:, pl.ds(off, kb)]
            m = pred(blk, off).astype(i32)
            for c in range(sub):
                acc = acc + m[:, c * LANES:(c + 1) * LANES]
            return acc
        acc = lax.fori_loop(0, nkb, body, jnp.zeros((qb, LANES), i32))
        return jnp.sum(acc, axis=1, keepdims=True)

    def count_ge(thr):
        tb = jnp.broadcast_to(to_score(thr), (qb, kb))
        return count(lambda blk, off: blk >= tb)

    assert topk <= 2 * LANES and TOP_R >= 2
    hi0 = to_key(jnp.max(top_scr[1 if topk > LANES else 0], axis=1, keepdims=True)) + 1
    lo0 = jnp.minimum(to_key(jnp.min(top_scr[1], axis=1, keepdims=True)), hi0 - 1)

    def search(count_fn, lo_start, hi_start):
        def cond(c):
            it, open_rows, _, _ = c
            return (open_rows > 0) & (it < 40)

        def body(c):
            it, _, lo, hi = c
            mid = lax.shift_right_arithmetic(lo, 1) + lax.shift_right_arithmetic(hi, 1) + (lo & hi & 1)
            n = count_fn(mid)
            ge = n >= topk
            lo_n = jnp.where(ge, mid, lo)
            hi_n = jnp.where(n == topk, mid + 1, jnp.where(ge, hi, mid))
            open_rows = jnp.max((lo_n < hi_n - 1).astype(i32))
            return it + 1, open_rows, lo_n, hi_n

        open0 = jnp.max((lo_start < hi_start - 1).astype(i32))
        return lax.while_loop(cond, body, (jnp.int32(0), open0, lo_start, hi_start))[2]

    n_zero = count_top(jnp.zeros((qb, 1), i32))
    n_pos = count_top(jnp.ones((qb, 1), i32))
    lo1 = jnp.where(n_pos >= topk, jnp.maximum(lo0, 1), jnp.where(n_zero >= topk, 0, lo0))
    hi1 = jnp.where(n_pos >= topk, hi0, jnp.where(n_zero >= topk, 1, jnp.minimum(hi0, 0)))
    thr_c = search(count_top, lo1, hi1)
    n_ge_c = count_top(thr_c)
    smallest = to_key(jnp.max(top_scr[TOP_R - 1], axis=1, keepdims=True))
    bad = (smallest >= thr_c) & (thr_c > NEG_INF_KEY)

    def full_search(_):
        t = search(count_ge, lo0, hi0)
        return t, count_ge(t)

    thr, n_ge = lax.cond(jnp.max(bad.astype(i32)) > 0, full_search, lambda _: (thr_c, n_ge_c), 0)
    tie = (n_ge > topk) & (thr > NEG_INF_KEY)
    tb = jnp.broadcast_to(to_score(thr), (qb, kb))

    def write_mask(i, sel):
        o_ref[0, :, pl.ds(pl.multiple_of(i * kb, kb), kb)] = jnp.where(sel, 0.0, NEG_BIG).astype(bf16)

    def out_plain(_):
        def body(i, carry):
            blk = keys[:, pl.ds(pl.multiple_of(i * kb, kb), kb)]
            write_mask(i, (blk >= tb) & (blk > -jnp.inf))
            return carry
        lax.fori_loop(0, nkb, body, 0)
        return 0

    def out_tied(_):
        need = jnp.where(tie, topk - count_ge(thr + 1), INT_MAX).astype(f32)
        tri = (lax.broadcasted_iota(i32, (kb, kb), 0) <= lax.broadcasted_iota(i32, (kb, kb), 1)).astype(bf16)

        def body(i, seen):
            blk = keys[:, pl.ds(pl.multiple_of(i * kb, kb), kb)]
            eq = blk == tb
            prefix = _dot(jnp.where(eq, 1.0, 0.0).astype(bf16), tri)
            take = eq & (prefix + seen <= need)
            write_mask(i, ((blk > tb) | take) & (blk > -jnp.inf))
            return seen + prefix[:, kb - 1:kb]

        lax.fori_loop(0, nkb, body, jnp.zeros((qb, 1), f32))
        return 0

    lax.cond(jnp.max(tie.astype(i32)) > 0, out_tied, out_plain, 0)

    def fill_body(i, carry):
        off = pl.multiple_of(i * kb, kb)
        o_ref[0, :, pl.ds(off, kb)] = jnp.full((qb, kb), NEG_BIG, bf16)
        return carry

    lax.fori_loop(nkb, ltot // kb, fill_body, 0)


def _dsa_mask(proj, ki16, *, qb, qoff, topk):
    b, t, _ = proj.shape
    ltot = ki16.shape[1]
    kern = functools.partial(_index_kernel, qoff=qoff, topk=topk)
    return pl.pallas_call(
        kern,
        out_shape=jax.ShapeDtypeStruct((b, t, ltot), bf16),
        grid=(b, t // qb),
        in_specs=[pl.BlockSpec((1, qb, C_QI), lambda i, j: (i, j, OFF_QI // C_QI)),
                  pl.BlockSpec((1, qb, LANES), lambda i, j: (i, j, OFF_SMALL // LANES)),
                  pl.BlockSpec((1, ltot, IDX_DIM), lambda i, j: (i, 0, 0))],
        out_specs=pl.BlockSpec((1, qb, ltot), lambda i, j: (i, j, 0)),
        scratch_shapes=[pltpu.VMEM((qb, -(-ltot // (TOP_R * LANES)) * TOP_R * LANES), f32),
                        pltpu.VMEM((TOP_R, qb, LANES), f32)],
        compiler_params=_cp(("arbitrary", "arbitrary")),
        name="dsa_index",
    )(proj, proj, ki16)


def _attn_kernel(qidx_ref, kidx_ref, flag_ref, case_ref, q_ref, *refs, has_cache):
    if has_cache:
        (ck_ref, cv_ref, kn_ref, vn_ref, mb_ref, bias_ref, o_ref,
         q_scr, m_scr, l_scr, acc_scr, s_scr, p_scr, a_scr, mb_scr, k_ref, v_ref) = refs
    else:
        (k_ref, v_ref, mb_ref, bias_ref, o_ref,
         q_scr, m_scr, l_scr, acc_scr, s_scr, p_scr, a_scr, mb_scr) = refs
    s_id = pl.program_id(1)
    qb = q_ref.shape[1]
    kb = k_ref.shape[1]
    rows = C_GROUP * qb
    nt = kb // LANES
    flags = flag_ref[s_id]

    if has_cache:
        @pl.when((flags & 8) != 0)
        def _():
            k_ref[0] = kn_ref[0]
            v_ref[0] = vn_ref[0]

        @pl.when((flags & 8) == 0)
        def _():
            k_ref[0] = ck_ref[0, 0].reshape(kb, C_KV).astype(bf16)
            v_ref[0] = cv_ref[0, 0].reshape(kb, C_KV).astype(bf16)

    @pl.when((flags & 1) != 0)
    def _():
        qv = (q_ref[0] * (C_HEAD_DIM ** -0.5 * LOG2E)).astype(bf16)
        for g in range(C_KV_HEADS):
            for r in range(C_GROUP):
                hd = g * C_GROUP + r
                q_scr[g, r * qb:(r + 1) * qb, :] = qv[:, hd * C_HEAD_DIM:(hd + 1) * C_HEAD_DIM]
        m_scr[...] = jnp.full(m_scr.shape, NEG_BIG, f32)
        l_scr[...] = jnp.zeros_like(l_scr)
        acc_scr[...] = jnp.zeros_like(acc_scr)

    def visit(with_bias):
        kblk = k_ref[0]
        vblk = v_ref[0]
        mb_scr[...] = mb_ref[0].astype(f32)
        units = [(g, r) for g in range(C_KV_HEADS) for r in range(C_GROUP)]

        def qk(u):
            g, r = units[u]
            hs = slice(r * qb, (r + 1) * qb)
            s_scr[g, hs, :] = _dot_nt(q_scr[g, hs, :], kblk[:, g * C_HEAD_DIM:(g + 1) * C_HEAD_DIM])

        def soft(u):
            g, r = units[u]
            for c in range(qb // ATT_RC):
                q0 = c * ATT_RC
                rs = slice(r * qb + q0, r * qb + q0 + ATT_RC)
                s = s_scr[g, rs, :] + mb_scr[q0:q0 + ATT_RC, :]
                if with_bias:
                    s = s + bias_ref[0, g * C_GROUP + r, q0:q0 + ATT_RC, :]
                t = s[:, 0:LANES]
                for i in range(1, nt):
                    t = jnp.maximum(t, s[:, i * LANES:(i + 1) * LANES])
                m_old = m_scr[g, rs, :]
                m_new = jnp.maximum(m_old, jnp.max(t, axis=1, keepdims=True))
                alpha = jnp.exp2(m_old - m_new)
                p = jnp.exp2(s - jnp.concatenate([m_new] * nt, axis=1))
                psum = p[:, 0:LANES]
                for i in range(1, nt):
                    psum = psum + p[:, i * LANES:(i + 1) * LANES]
                l_scr[g, rs, :] = alpha * l_scr[g, rs, :] + psum
                m_scr[g, rs, :] = m_new
                a_scr[g, rs, :] = alpha
                p_scr[g, rs, :] = p.astype(bf16)

        def pv(u):
            g, r = units[u]
            hs = slice(r * qb, (r + 1) * qb)
            acc_scr[g, hs, :] = a_scr[g, hs, :] * acc_scr[g, hs, :] + _dot(
                p_scr[g, hs, :], vblk[:, g * C_HEAD_DIM:(g + 1) * C_HEAD_DIM])

        qk(0)
        for u in range(len(units)):
            if u + 1 < len(units):
                qk(u + 1)
            soft(u)
            pv(u)

    @pl.when((flags & 4) != 0)
    def _():
        visit(True)

    @pl.when((flags & 4) == 0)
    def _():
        visit(False)

    @pl.when((flags & 2) != 0)
    def _():
        for g in range(C_KV_HEADS):
            o = acc_scr[g] / jnp.sum(l_scr[g], axis=1, keepdims=True)
            for r in range(C_GROUP):
                hd = g * C_GROUP + r
                o_ref[0, :, hd * C_HEAD_DIM:(hd + 1) * C_HEAD_DIM] = o[r * qb:(r + 1) * qb, :].astype(bf16)


def _attn_schedule(nq, qb, qoff, offsets):
    qidx, kidx, flags, case = [], [], [], []
    zero_case = len(offsets)
    for j in range(nq):
        kmax = qoff + (j + 1) * qb
        nkb = -(-kmax // ATT_KB)
        for kb in range(nkb):
            off = kmax - BIAS_WIN - kb * ATT_KB
            qidx.append(j)
            kidx.append(kb)
            near = off in offsets
            flags.append((1 if kb == 0 else 0) | (2 if kb == nkb - 1 else 0) | (4 if near else 0)
                         | (8 if kb * ATT_KB >= qoff else 0))
            case.append(offsets.index(off) if near else zero_case)
            assert near == (-BIAS_WIN < off < ATT_KB)
    return [jnp.asarray(np.asarray(a, np.int32)) for a in (qidx, kidx, flags, case)]


def _dsa_attend(proj, k16, v16, maskbias, tables, *, qb, qoff, offsets, cache=None):
    b, t, _ = proj.shape
    nq = t // qb
    qidx, kidx, flags, case = _attn_schedule(nq, qb, qoff, offsets)
    ns = int(qidx.shape[0])
    rows = C_GROUP * qb
    kv_block = lambda f: pl.BlockSpec((1, ATT_KB, C_KV), f)
    if cache is None:
        kv_specs = [kv_block(lambda i, s, qi, ki, fl, ca: (i, ki[s], 0))] * 2
        kv_args = (k16, v16)
        kv_scratch = []
    else:
        layer = cache[2]
        ncache = cache[0].shape[2] // ATT_KB
        assert cache[0].shape[2] == qoff and qoff % ATT_KB == 0 and nq == 1 and k16.shape[1] == ATT_KB
        cache_block = pl.BlockSpec((1, 1, ATT_KB, C_KV_HEADS, C_HEAD_DIM),
                                   lambda i, s, qi, ki, fl, ca: (layer, i, jnp.minimum(ki[s], ncache - 1), 0, 0))
        kv_specs = [cache_block] * 2 + [kv_block(lambda i, s, qi, ki, fl, ca: (i, 0, 0))] * 2
        kv_args = (cache[0], cache[1], k16, v16)
        kv_scratch = [pltpu.VMEM((1, ATT_KB, C_KV), bf16)] * 2
    grid_spec = pltpu.PrefetchScalarGridSpec(
        num_scalar_prefetch=4,
        grid=(b, ns),
        in_specs=([pl.BlockSpec((1, qb, C_Q), lambda i, s, qi, ki, fl, ca: (i, qi[s], OFF_Q // C_Q))] + kv_specs
                  + [pl.BlockSpec((1, qb, ATT_KB), lambda i, s, qi, ki, fl, ca: (i, qi[s], ki[s])),
                     pl.BlockSpec((1, C_HEADS, qb, ATT_KB), lambda i, s, qi, ki, fl, ca: (ca[s], 0, 0, 0))]),
        out_specs=pl.BlockSpec((1, qb, C_WIDTH), lambda i, s, qi, ki, fl, ca: (i, qi[s], 0)),
        scratch_shapes=[pltpu.VMEM((C_KV_HEADS, rows, C_HEAD_DIM), bf16),
                        pltpu.VMEM((C_KV_HEADS, rows, LANES), f32),
                        pltpu.VMEM((C_KV_HEADS, rows, LANES), f32),
                        pltpu.VMEM((C_KV_HEADS, rows, C_HEAD_DIM), f32),
                        pltpu.VMEM((C_KV_HEADS, rows, ATT_KB), f32),
                        pltpu.VMEM((C_KV_HEADS, rows, ATT_KB), bf16),
                        pltpu.VMEM((C_KV_HEADS, rows, LANES), f32),
                        pltpu.VMEM((qb, ATT_KB), f32)] + kv_scratch)
    return pl.pallas_call(
        functools.partial(_attn_kernel, has_cache=cache is not None),
        out_shape=jax.ShapeDtypeStruct((b, t, C_WIDTH), bf16),
        grid_spec=grid_spec,
        compiler_params=_cp(("arbitrary", "arbitrary")),
        name="dsa_attend",
    )(qidx, kidx, flags, case, proj, *kv_args, maskbias, tables)


def _window_offsets(nq, qb, qoff):
    offs = set()
    for j in range(nq):
        kmax = qoff + (j + 1) * qb
        for kb in range(-(-kmax // ATT_KB)):
            off = kmax - BIAS_WIN - kb * ATT_KB
            if -BIAS_WIN < off < ATT_KB:
                offs.add(off)
    return sorted(offs)


def _mixer_c(proj, k_all16, v_all16, ki_all16, rel_bias, tables_memo, *, qb, qoff, true_len, cache=None):
    b, t, _ = proj.shape
    topk = min(TOPK_MAX, true_len // 4)
    assert BIAS_WIN - qb + 1 > MAX_DISTANCE
    offsets = _window_offsets(t // qb, qb, qoff)
    memo_key = (qb, qoff, t)
    if memo_key not in tables_memo:
        tables_memo[memo_key] = _bias_tables(rel_bias, qb, offsets)
    tables = tables_memo[memo_key]
    idx_qb = IDX_QB if t % IDX_QB == 0 else qb
    maskbias = _dsa_mask(proj, ki_all16, qb=idx_qb, qoff=qoff, topk=topk)
    return _dsa_attend(proj, k_all16, v_all16, maskbias, tables, qb=qb, qoff=qoff, offsets=offsets, cache=cache)


def _merge_kernel(x_ref, ada_ref, gt_ref, oa_ref, ob_ref, oc_ref, wa_ref, wb_ref, wc_ref, wo_ref, o_ref):
    bt, tt, d = x_ref.shape
    rows = bt * tt
    gates = jax.nn.sigmoid(gt_ref[...].reshape(rows, GATE_COLS))
    merged = (gates[:, 0:d] * _dot(oa_ref[...].reshape(rows, A_WIDTH), wa_ref[...])
              + gates[:, d:2 * d] * _dot(ob_ref[...].reshape(rows, SSM_INNER), wb_ref[...])
              + gates[:, 2 * d:3 * d] * _dot(oc_ref[...].reshape(rows, C_WIDTH), wc_ref[...]))
    mix = _dot(merged.astype(bf16), wo_ref[...]).reshape(bt, tt, d)
    o_ref[...] = x_ref[...] + ada_ref[...][:, 5:6, :] * mix


def _merge(x, ada, proj, o_a, o_b, o_c, wa16, wb16, wc16, wo16):
    b, t, d = x.shape
    bt, tt = _token_tile(b, t, rows=256)
    tok = lambda w: pl.BlockSpec((bt, tt, w), lambda i, j: (i, j, 0))
    full = lambda r, c: pl.BlockSpec((r, c), lambda i, j: (0, 0))
    return pl.pallas_call(
        _merge_kernel,
        out_shape=jax.ShapeDtypeStruct((b, t, d), f32),
        grid=(b // bt, t // tt),
        in_specs=[tok(d), pl.BlockSpec((bt, N_ADA, d), lambda i, j: (i, 0, 0)),
                  pl.BlockSpec((bt, tt, GATE_COLS), lambda i, j: (i, j, OFF_GATE // GATE_COLS)),
                  tok(A_WIDTH), tok(SSM_INNER), tok(C_WIDTH),
                  full(A_WIDTH, d), full(SSM_INNER, d), full(C_WIDTH, d), full(d, d)],
        out_specs=tok(d),
        compiler_params=_cp(("arbitrary", "arbitrary")),
        name="merge",
    )(x, ada, proj, o_a, o_b, o_c, wa16, wb16, wc16, wo16)


def _pad_keys(x, mult):
    pad = (-x.shape[1]) % mult
    if pad:
        x = jnp.concatenate([x, jnp.zeros((x.shape[0], pad) + x.shape[2:], x.dtype)], axis=1)
    return x


def _layer(x, ada, lp, rel_bias, tables_memo, past, g_final, last, layer):
    b, t, d = x.shape
    x = _ffn(x, ada, lp['g_ffn1'], lp['w_ffn1_in'], lp['w_ffn1_out'], g_final, k0=0, final_norm=False)
    proj, k16, v16, ki16, k_new, v_new, ki_new = _inproj(x, ada, lp['g_mix'], lp['w_proj'], k0=3)
    new_conv = proj[:, t - (CONV_W - 1):, OFF_XBC:OFF_XBC + CONV_DIM]

    o_a, va = _mixer_a(proj, lp['g_a'], lp['w_s'], lp['b_s'])

    if past is None:
        conv_init = jnp.zeros((b, CONV_W - 1, CONV_DIM), f32)
        h0 = jnp.zeros((b, SSM_HEADS, SSM_HEAD_DIM, SSM_STATE), f32)
    else:
        h0, conv_init = past[3], past[4]
    o_b, new_ssm = _mixer_b(proj, conv_init, h0, lp['w_conv'], lp['b_conv'], lp['dt_bias'], lp['a_log'],
                            lp['d_skip'], lp['g_ssm'])
    new_ssm = new_ssm.reshape(b, SSM_HEADS, SSM_HEAD_DIM, SSM_STATE)

    if past is None:
        k_all, v_all, ki_all, cache = k16, v16, ki16, None
        qb, qoff, true_len = (ATT_QB if t % ATT_QB == 0 else Q_BLOCK), 0, t
    else:
        plen = past[0].shape[2]
        if plen % ATT_KB == 0:
            k_all, v_all, cache = k16, v16, (past[0], past[1], layer)
        else:
            k_all = jnp.concatenate([past[0][layer].reshape(b, plen, C_KV).astype(bf16), k16], axis=1)
            v_all = jnp.concatenate([past[1][layer].reshape(b, plen, C_KV).astype(bf16), v16], axis=1)
            cache = None
        ki_all = jnp.concatenate([past[2].astype(bf16), ki16], axis=1)
        qb, qoff, true_len = t, plen, plen + t
    o_c = _mixer_c(proj, _pad_keys(k_all, ATT_KB), _pad_keys(v_all, ATT_KB), _pad_keys(ki_all, ATT_KB), rel_bias, tables_memo,
                   qb=qb, qoff=qoff, true_len=true_len, cache=cache)

    x = _merge(x, ada, proj, o_a, o_b, o_c, lp['w_br_a'], lp['w_br_b'], lp['w_br_c'], lp['w_out'])
    x = _ffn(x, ada, lp['g_ffn2'], lp['w_ffn2_in'], lp['w_ffn2_out'], g_final, k0=6, final_norm=last)
    states = (k_new, v_new, ki_new, new_ssm, new_conv, va)
    return x, states


def kernel(x_prompt, x_sample, c_prompt, c_sample, cache_k, cache_v, cache_kidx, state_ssm, state_conv, rel_bias, w_ada, b_ada, g_ffn1, w_ffn1_in, w_ffn1_out, g_mix, w_in, g_a, w_s, b_s, w_conv, b_conv, dt_bias, a_log, d_skip, g_ssm, w_br_a, w_br_b, w_br_c, w_out, g_ffn2, w_ffn2_in, w_ffn2_out, g_final):
    depth = w_ada.shape[0]
    bp, bs = c_prompt.shape[0], c_sample.shape[0]
    assert x_prompt.shape[1] % ATT_KB == 0 and x_prompt.shape[1] % Q_BLOCK == 0
    assert x_sample.shape[1] == CHUNK

    rows = -(-(bp + bs) // 8) * 8
    c_all = jnp.concatenate([c_prompt, c_sample, jnp.zeros((rows - bp - bs, D_MODEL), f32)], axis=0)
    ada_all = _ada_all(c_all, w_ada.astype(bf16), b_ada)
    ada_all = ada_all.reshape(depth, rows, N_ADA, D_MODEL)

    yp, ys = x_prompt, x_sample
    tables_memo = {}
    st_p, st_s = [], []
    for i in range(depth):
        lp = dict(g_ffn1=g_ffn1[i], w_ffn1_in=w_ffn1_in[i].astype(bf16), w_ffn1_out=w_ffn1_out[i].astype(bf16),
                  g_mix=g_mix[i], w_proj=_layout_w_in(w_in[i]).astype(bf16), g_a=g_a[i], w_s=w_s[i], b_s=b_s[i],
                  w_conv=w_conv[i], b_conv=b_conv[i], dt_bias=dt_bias[i], a_log=a_log[i], d_skip=d_skip[i],
                  g_ssm=g_ssm[i], w_br_a=w_br_a[i].astype(bf16), w_br_b=w_br_b[i].astype(bf16),
                  w_br_c=w_br_c[i].astype(bf16), w_out=w_out[i].astype(bf16), g_ffn2=g_ffn2[i],
                  w_ffn2_in=w_ffn2_in[i].astype(bf16), w_ffn2_out=w_ffn2_out[i].astype(bf16))
        last = i == depth - 1
        yp, sp = _layer(yp, ada_all[i, :bp], lp, rel_bias, tables_memo, None, g_final, last, i)
        ys, ss = _layer(ys, ada_all[i, bp:bp + bs], lp, rel_bias, tables_memo,
                        (cache_k, cache_v, cache_kidx[i], state_ssm[i], state_conv[i]), g_final, last, i)
        st_p.append(sp)
        st_s.append(ss)

    stack = lambda sts, k: jnp.stack([s[k] for s in sts])
    return (yp, ys,
            stack(st_p, 0), stack(st_p, 1), stack(st_p, 2), stack(st_p, 3), stack(st_p, 4),
            stack(st_s, 0), stack(st_s, 1), stack(st_s, 2), stack(st_s, 3), stack(st_s, 4), stack(st_s, 5))
```
